```python
import math
import jax, jax.numpy as jnp
from jax import lax
import numpy as np

D_MODEL = 1024
BATCH = 16
SEQ = 2048
DEPTH = 2
DEC_BATCH = 8
DEC_SEQ = 2048
PAST_LEN = 128

HY_WIDTH = D_MODEL // 2
HY_ORDER = 2
FILT_EMB = 33
FILT_BANDS = (FILT_EMB - 1) // 2
FILT_HIDDEN = 64
DECAY_TARGET = 1e-2
FAST_DECAY_PCT = 0.3
SLOW_DECAY_PCT = 1.5
N_DIR = 2
HEAD_DIM = 64
N_Q_HEADS = (D_MODEL // 2) // HEAD_DIM
N_KV_HEADS = 2
GQA_GROUP = N_Q_HEADS // N_KV_HEADS
WINDOW = 128
BLOCK = 128
ROPE_THETA = 10000.0
ATTN_WIDTH = N_Q_HEADS * HEAD_DIM
KV_WIDTH = N_KV_HEADS * HEAD_DIM
HY_COLS = (HY_ORDER + 1) * HY_WIDTH
IN_COLS = HY_COLS + ATTN_WIDTH + 2 * KV_WIDTH + 2 * D_MODEL
SPLITS = (HY_COLS, HY_COLS + ATTN_WIDTH, HY_COLS + ATTN_WIDTH + KV_WIDTH,
          HY_COLS + ATTN_WIDTH + 2 * KV_WIDTH, HY_COLS + ATTN_WIDTH + 2 * KV_WIDTH + D_MODEL)
PEER_HEADS = 8
N_KEYS = 128
N_EXPERTS = N_KEYS * N_KEYS
PEER_TOPK = 16
PEER_QDIM = 256
PEER_HALF = PEER_QDIM // 2
PEER_CHUNK = 128
EPS = 1e-6

kernel_name = "hybrid_hyena_swa_peer_encoder"


def rmsnorm(x, gain):
    x32 = x.astype(jnp.float32)
    y = x32 * lax.rsqrt(jnp.mean(x32 * x32, axis=-1, keepdims=True) + EPS)
    return y.astype(x.dtype) * gain


def rope(x):
    L = x.shape[1]
    inv = ROPE_THETA ** (-jnp.arange(0, HEAD_DIM, 2, dtype=jnp.float32) / HEAD_DIM)
    ang = jnp.arange(L, dtype=jnp.float32)[:, None] * inv[None, :]
    cos = jnp.cos(ang)[None, :, None, :]
    sin = jnp.sin(ang)[None, :, None, :]
    x32 = x.astype(jnp.float32)
    x1, x2 = x32[..., :HEAD_DIM // 2], x32[..., HEAD_DIM // 2:]
    return jnp.concatenate([x1 * cos - x2 * sin, x2 * cos + x1 * sin], axis=-1).astype(x.dtype)


def implicit_filters(L, w1, b1, freq, w2, b2, w3):
    f32 = jnp.float32
    t = jnp.linspace(0.0, 1.0, L, dtype=f32)[:, None]
    w = 2.0 * math.pi * jnp.arange(L, dtype=f32)[:, None] / L
    bands = jnp.linspace(1e-4, FILT_BANDS - 1, FILT_BANDS, dtype=f32)[None, :]
    z = jnp.concatenate([t, jnp.cos(bands * w), -jnp.sin(bands * w)], axis=-1)
    fr = freq.astype(f32)
    a = jnp.sin(fr * (z @ w1.astype(f32) + b1.astype(f32)))
    a = jnp.sin(fr * (a @ w2.astype(f32) + b2.astype(f32)))
    hf = (a @ w3.astype(f32)).reshape(L, HY_ORDER, N_DIR, HY_WIDTH)
    max_decay = math.log(DECAY_TARGET) / FAST_DECAY_PCT
    min_decay = math.log(DECAY_TARGET) / SLOW_DECAY_PCT
    deltas = jnp.linspace(min_decay, max_decay, HY_WIDTH, dtype=f32)
    hf = hf * jnp.exp(-t * jnp.abs(deltas))[:, None, None, :]
    h_full = jnp.concatenate([hf[:, :, 0], jnp.zeros((1, HY_ORDER, HY_WIDTH), f32), hf[:0:-1, :, 1]], axis=0)
    return jnp.fft.rfft(h_full, axis=0)


def fftconv(u, Hf, bias):
    L = u.shape[1]
    u32 = u.astype(jnp.float32)
    U = jnp.fft.rfft(u32, n=2 * L, axis=1)
    y = jnp.fft.irfft(U * Hf[None], n=2 * L, axis=1)[:, :L]
    return (y + u32 * bias.astype(jnp.float32)).astype(u.dtype)


def window_attention(q, k, v, sink):
    B, L = q.shape[0], q.shape[1]
    nb = L // BLOCK
    qb = q.reshape(B, nb, BLOCK, N_KV_HEADS, GQA_GROUP, HEAD_DIM)
    pad = ((0, 0), (BLOCK, BLOCK), (0, 0), (0, 0))
    kp = jnp.pad(k, pad).reshape(B, nb + 2, BLOCK, N_KV_HEADS, HEAD_DIM)
    vp = jnp.pad(v, pad).reshape(B, nb + 2, BLOCK, N_KV_HEADS, HEAD_DIM)
    kb = jnp.concatenate([kp[:, :-2], kp[:, 1:-1], kp[:, 2:]], axis=2)
    vb = jnp.concatenate([vp[:, :-2], vp[:, 1:-1], vp[:, 2:]], axis=2)
    s = jnp.einsum('bnqkgd,bnjkd->bnkgqj', qb, kb).astype(jnp.float32) * (HEAD_DIM ** -0.5)
    blk = jnp.arange(nb)[:, None, None] * BLOCK
    qpos = blk + jnp.arange(BLOCK)[None, :, None]
    kpos = blk - BLOCK + jnp.arange(3 * BLOCK)[None, None, :]
    valid = (jnp.abs(kpos - qpos) <= WINDOW) & (kpos >= 0) & (kpos < L)
    s = jnp.where(valid[None, :, None, None], s, -jnp.inf)
    sk = sink.astype(jnp.float32).reshape(N_KV_HEADS, GQA_GROUP)[None, None, :, :, None, None]
    m = jnp.maximum(jnp.max(s, axis=-1, keepdims=True), sk)
    p = jnp.exp(s - m)
    p = p / (jnp.sum(p, axis=-1, keepdims=True) + jnp.exp(sk - m))
    o = jnp.einsum('bnkgqj,bnjkd->bnqkgd', p.astype(v.dtype), vb)
    return o.reshape(B, L, ATTN_WIDTH)


def token_mixer(h, w_in, conv_w, conv_b, f_w1, f_b1, f_freq, f_w2, f_b2, f_w3, f_bias,
                q_gain, k_gain, sink, w_pa, w_pb, w_out):
    B, L, _ = h.shape
    z = h @ w_in
    hy, q, k, v, ga, gb = jnp.split(z, SPLITS, axis=-1)
    hp = jnp.pad(hy, ((0, 0), (1, 1), (0, 0)))
    hy = hp[:, :-2] * conv_w[0] + hp[:, 1:-1] * conv_w[1] + hp[:, 2:] * conv_w[2] + conv_b
    v0, x1, x2 = jnp.split(hy, HY_ORDER + 1, axis=-1)
    Hf = implicit_filters(L, f_w1, f_b1, f_freq, f_w2, f_b2, f_w3)
    zz = x1 * fftconv(v0, Hf[:, 0], f_bias[0])
    ya = x2 * fftconv(zz, Hf[:, 1], f_bias[1])
    q = rope(rmsnorm(q.reshape(B, L, N_Q_HEADS, HEAD_DIM), q_gain))
    k = rope(rmsnorm(k.reshape(B, L, N_KV_HEADS, HEAD_DIM), k_gain))
    v = v.reshape(B, L, N_KV_HEADS, HEAD_DIM)
    yb = window_attention(q, k, v, sink)
    merged = jax.nn.sigmoid(ga) * (ya @ w_pa) + jax.nn.sigmoid(gb) * (yb @ w_pb)
    return merged @ w_out


def peer(h, wq, k1, k2, u_tab, v_tab):
    B, L, D = h.shape
    T = B * L
    ht = h.reshape(T, D)
    q = (ht @ wq).reshape(T, PEER_HEADS, PEER_QDIM)
    s1 = jnp.einsum('thd,nd->thn', q[..., :PEER_HALF], k1).astype(jnp.float32)
    s2 = jnp.einsum('thd,nd->thn', q[..., PEER_HALF:], k2).astype(jnp.float32)
    v1, i1 = lax.top_k(s1, PEER_TOPK)
    v2, i2 = lax.top_k(s2, PEER_TOPK)
    cand = (v1[..., :, None] + v2[..., None, :]).reshape(T, PEER_HEADS, PEER_TOPK * PEER_TOPK)
    vs, ic = lax.top_k(cand, PEER_TOPK)
    e1 = jnp.take_along_axis(i1, ic // PEER_TOPK, axis=-1)
    e2 = jnp.take_along_axis(i2, ic % PEER_TOPK, axis=-1)
    idx = e1 * N_KEYS + e2
    g = jax.nn.softmax(vs, axis=-1).astype(h.dtype)
    nc = T // PEER_CHUNK

    def chunk(args):
        xc, ec, gc = args
        a = jnp.einsum('td,thkd->thk', xc, u_tab[ec])
        w = gc * jax.nn.gelu(a, approximate=False)
        return jnp.einsum('thk,thkd->td', w, v_tab[ec])

    y = lax.map(chunk, (ht.reshape(nc, PEER_CHUNK, D),
                        idx.reshape(nc, PEER_CHUNK, PEER_HEADS, PEER_TOPK),
                        g.reshape(nc, PEER_CHUNK, PEER_HEADS, PEER_TOPK)))
    return y.reshape(B, L, D)


def trunk(x, c, w_mod, b_mod, g_norm1, g_norm2, w_in, conv_w, conv_b, f_w1, f_b1, f_freq, f_w2, f_b2, f_w3,
          f_bias, q_gain, k_gain, sink, w_pa, w_pb, w_out, peer_wq, peer_k1, peer_k2, peer_u, peer_v):
    for l in range(DEPTH):
        mod = jax.nn.silu(c) @ w_mod[l] + b_mod[l]
        sh1, sc1, gt1, sh2, sc2, gt2 = [m[:, None, :] for m in jnp.split(mod, 6, axis=-1)]
        h = rmsnorm(x, g_norm1[l]) * (1.0 + sc1) + sh1
        x = x + gt1 * token_mixer(h, w_in[l], conv_w[l], conv_b[l], f_w1[l], f_b1[l], f_freq[l], f_w2[l],
                                  f_b2[l], f_w3[l], f_bias[l], q_gain[l], k_gain[l], sink[l],
                                  w_pa[l], w_pb[l], w_out[l])
        h = rmsnorm(x, g_norm2[l]) * (1.0 + sc2) + sh2
        x = x + gt2 * peer(h, peer_wq[l], peer_k1[l], peer_k2[l], peer_u[l], peer_v[l])
    return x


def setup_inputs(seed: int = 0) -> dict:
    key = jax.random.key(seed)
    ks = iter(jax.random.split(key, 32))
    f32 = jnp.float32

    def nrm(shape, scale):
        return jax.random.normal(next(ks), shape, f32) * scale

    def gain(shape):
        return 1.0 + nrm(shape, 0.02)

    D = D_MODEL
    return {
        'x_prompt': nrm((BATCH, SEQ, D), 1.0),
        'x_sample': nrm((DEC_BATCH, DEC_SEQ, D), 1.0),
        'c_prompt': nrm((BATCH, D), 1.0),
        'c_sample': nrm((DEC_BATCH, D), 1.0),
        'w_mod': nrm((DEPTH, D, 6 * D), 0.5 * D ** -0.5),
        'b_mod': nrm((DEPTH, 6 * D), 0.02),
        'g_norm1': gain((DEPTH, D)),
        'g_norm2': gain((DEPTH, D)),
        'w_in': nrm((DEPTH, D, IN_COLS), D ** -0.5),
        'conv_w': nrm((DEPTH, 3, HY_COLS), 3 ** -0.5),
        'conv_b': nrm((DEPTH, HY_COLS), 0.02),
        'f_w1': nrm((DEPTH, FILT_EMB, FILT_HIDDEN), FILT_EMB ** -0.5),
        'f_b1': nrm((DEPTH, FILT_HIDDEN), 0.02),
        'f_freq': gain((DEPTH, FILT_HIDDEN)),
        'f_w2': nrm((DEPTH, FILT_HIDDEN, FILT_HIDDEN), FILT_HIDDEN ** -0.5),
        'f_b2': nrm((DEPTH, FILT_HIDDEN), 0.02),
        'f_w3': nrm((DEPTH, FILT_HIDDEN, HY_ORDER * N_DIR * HY_WIDTH), 0.02),
        'f_bias': nrm((DEPTH, HY_ORDER, HY_WIDTH), 0.5),
        'q_gain': gain((DEPTH, HEAD_DIM)),
        'k_gain': gain((DEPTH, HEAD_DIM)),
        'sink': nrm((DEPTH, N_Q_HEADS), 0.5),
        'w_pa': nrm((DEPTH, HY_WIDTH, D), HY_WIDTH ** -0.5),
        'w_pb': nrm((DEPTH, ATTN_WIDTH, D), ATTN_WIDTH ** -0.5),
        'w_out': nrm((DEPTH, D, D), D ** -0.5),
        'peer_wq': nrm((DEPTH, D, PEER_HEADS * PEER_QDIM), D ** -0.5),
        'peer_k1': nrm((DEPTH, N_KEYS, PEER_HALF), PEER_HALF ** -0.5),
        'peer_k2': nrm((DEPTH, N_KEYS, PEER_HALF), PEER_HALF ** -0.5),
        'peer_u': nrm((DEPTH, N_EXPERTS, D), D ** -0.5),
        'peer_v': nrm((DEPTH, N_EXPERTS, D), PEER_HEADS ** -0.5),
    }


def reference(x_prompt, x_sample, c_prompt, c_sample, w_mod, b_mod, g_norm1, g_norm2, w_in, conv_w, conv_b,
              f_w1, f_b1, f_freq, f_w2, f_b2, f_w3, f_bias, q_gain, k_gain, sink, w_pa, w_pb, w_out,
              peer_wq, peer_k1, peer_k2, peer_u, peer_v):
    y_prompt = trunk(x_prompt, c_prompt, w_mod, b_mod, g_norm1, g_norm2, w_in, conv_w, conv_b, f_w1, f_b1,
                     f_freq, f_w2, f_b2, f_w3, f_bias, q_gain, k_gain, sink, w_pa, w_pb, w_out,
                     peer_wq, peer_k1, peer_k2, peer_u, peer_v)
    y_sample = trunk(x_sample, c_sample, w_mod, b_mod, g_norm1, g_norm2, w_in, conv_w, conv_b, f_w1, f_b1,
                     f_freq, f_w2, f_b2, f_w3, f_bias, q_gain, k_gain, sink, w_pa, w_pb, w_out,
                     peer_wq, peer_k1, peer_k2, peer_u, peer_v)
    return (y_prompt, y_sample)
```

```python
import functools
import math

import jax
import jax.numpy as jnp
import numpy as np
from jax import lax
from jax.experimental import pallas as pl
from jax.experimental.pallas import tpu as pltpu

F32 = jnp.float32
BF16 = jnp.bfloat16

EPS = 1e-6
HEAD_DIM = 64
N_Q_HEADS = 8
N_KV_HEADS = 2
WINDOW = 128
BLOCK = 128
ROPE_THETA = 10000.0
HY_ORDER = 2
N_DIR = 2
FILT_BANDS = 16
DECAY_TARGET = 1e-2
FAST_DECAY_PCT = 0.3
SLOW_DECAY_PCT = 1.5
PEER_HEADS = 8
N_KEYS = 128
PEER_TOPK = 16
PEER_QDIM = 256
LANES = 128
SUBLANES = 8
VMEM_LIMIT = 56 * 1024 * 1024

_STAIR = [(i, j) for i in range(PEER_TOPK) for j in range(PEER_TOPK) if (i + 1) * (j + 1) <= PEER_TOPK]
_STAIR_ROWS = -(-len(_STAIR) // SUBLANES) * SUBLANES


def _cparams(sem, vmem=VMEM_LIMIT):
    return pltpu.CompilerParams(dimension_semantics=sem, vmem_limit_bytes=vmem)


def _dot(a, b):
    return jnp.dot(a, b, preferred_element_type=F32)


def _dot_nt(a, b):
    return lax.dot_general(a, b, (((1,), (1,)), ((), ())), preferred_element_type=F32)


def _dot_hi(a, b):
    return jnp.dot(a, b, preferred_element_type=F32, precision=lax.Precision.HIGHEST)


def _sigmoid(x):
    return 1.0 / (1.0 + jnp.exp(-x))


def _mod_kernel(c_ref, w_ref, b_ref, o_ref):
    c = c_ref[...]
    s = c * _sigmoid(c)
    o_ref[0] = _dot(s.astype(BF16), w_ref[0]) + b_ref[0]


def _modulation(c, w_mod_bf, b_mod):
    depth, d, n6 = w_mod_bf.shape
    bsz = c.shape[0]
    tn = 1536
    return pl.pallas_call(
        _mod_kernel,
        grid=(depth, n6 // tn),
        in_specs=[pl.BlockSpec((bsz, d), lambda l, j: (0, 0)),
                  pl.BlockSpec((1, d, tn), lambda l, j: (l, 0, j)),
                  pl.BlockSpec((1, 1, tn), lambda l, j: (l, 0, j))],
        out_specs=pl.BlockSpec((1, bsz, tn), lambda l, j: (l, 0, j)),
        out_shape=jax.ShapeDtypeStruct((depth, bsz, n6), F32),
        compiler_params=_cparams(("arbitrary", "arbitrary")),
        name="adaln_mod",
    )(c, w_mod_bf, b_mod.reshape(depth, 1, n6))


def _norm_mod(x, gain, shift, scale):
    y = x * lax.rsqrt(jnp.mean(x * x, axis=-1, keepdims=True) + EPS)
    return (y * gain) * (1.0 + scale) + shift


def _inproj_kernel(x_ref, mod_ref, g_ref, w_ref, hy_ref, q_ref, k_ref, v_ref, ga_ref, gb_ref, *, splits):
    m = mod_ref[0]
    h = _norm_mod(x_ref[...], g_ref[...], m[0:1], m[1:2])
    z = _dot(h.astype(BF16), w_ref[...])
    outs = (hy_ref, q_ref, k_ref, v_ref, ga_ref, gb_ref)
    lo = 0
    for ref, hi in zip(outs, splits):
        ref[...] = z[:, lo:hi]
        lo = hi


def _inproj(x2, mod3, gain, w_in_bf, seq, widths, tm=256):
    t, d = x2.shape
    ncols = w_in_bf.shape[1]
    per = seq // tm
    splits = tuple(int(s) for s in np.cumsum(widths))
    return pl.pallas_call(
        functools.partial(_inproj_kernel, splits=splits),
        grid=(t // tm,),
        in_specs=[pl.BlockSpec((tm, d), lambda i: (i, 0)),
                  pl.BlockSpec((1, 6, d), lambda i: (i // per, 0, 0)),
                  pl.BlockSpec((1, d), lambda i: (0, 0)),
                  pl.BlockSpec((d, ncols), lambda i: (0, 0))],
        out_specs=[pl.BlockSpec((tm, w), lambda i: (i, 0)) for w in widths],
        out_shape=[jax.ShapeDtypeStruct((t, w), F32) for w in widths],
        compiler_params=_cparams(("arbitrary",)),
        name="inproj",
    )(x2, mod3, gain.reshape(1, d), w_in_bf)


def _filter_kernel(z_ref, w1_ref, b1_ref, fr_ref, w2_ref, b2_ref, w3_ref, ad_ref, o_ref):
    z = z_ref[0]
    fr = fr_ref[...]
    a = jnp.sin(fr * (_dot_hi(z, w1_ref[...]) + b1_ref[...]))
    a = jnp.sin(fr * (_dot_hi(a, w2_ref[...]) + b2_ref[...]))
    h = _dot_hi(a, w3_ref[0])
    h = h * jnp.exp(-z[:, 0:1] * ad_ref[...])
    row = lax.broadcasted_iota(jnp.int32, h.shape, 0)
    dead = (pl.program_id(0) == 1) & (pl.program_id(1) == 0) & (row == 0)
    o_ref[0] = jnp.where(dead, 0.0, h)


def _filters_time(seq, f_w1, f_b1, f_freq, f_w2, f_b2, f_w3, tm=512):
    hidden = f_w1.shape[1]
    cw = f_w3.shape[1] // (HY_ORDER * N_DIR)
    t = jnp.linspace(0.0, 1.0, seq, dtype=F32)[:, None]
    w = 2.0 * math.pi * jnp.arange(seq, dtype=F32)[:, None] / seq
    bands = jnp.linspace(1e-4, FILT_BANDS - 1, FILT_BANDS, dtype=F32)[None, :]
    z = jnp.concatenate([t, jnp.cos(bands * w), -jnp.sin(bands * w)], axis=-1)
    emb = z.shape[1]
    z = jnp.pad(z, ((0, 0), (0, LANES - emb)))
    zcat = jnp.stack([z, jnp.concatenate([z[:1], z[:0:-1]], axis=0)], axis=0)
    w1p = jnp.pad(f_w1, ((0, LANES - emb), (0, 0)))
    w3d = f_w3.reshape(hidden, HY_ORDER, N_DIR, cw).transpose(2, 0, 1, 3).reshape(N_DIR, hidden, HY_ORDER * cw)
    max_decay = math.log(DECAY_TARGET) / FAST_DECAY_PCT
    min_decay = math.log(DECAY_TARGET) / SLOW_DECAY_PCT
    ad = jnp.abs(jnp.linspace(min_decay, max_decay, cw, dtype=F32))
    ad = jnp.tile(ad, HY_ORDER)[None, :]
    oc = HY_ORDER * cw
    return pl.pallas_call(
        _filter_kernel,
        grid=(N_DIR, seq // tm),
        in_specs=[pl.BlockSpec((1, tm, LANES), lambda g, r: (g, r, 0)),
                  pl.BlockSpec((LANES, hidden), lambda g, r: (0, 0)),
                  pl.BlockSpec((1, hidden), lambda g, r: (0, 0)),
                  pl.BlockSpec((1, hidden), lambda g, r: (0, 0)),
                  pl.BlockSpec((hidden, hidden), lambda g, r: (0, 0)),
                  pl.BlockSpec((1, hidden), lambda g, r: (0, 0)),
                  pl.BlockSpec((1, hidden, oc), lambda g, r: (g, 0, 0)),
                  pl.BlockSpec((1, oc), lambda g, r: (0, 0))],
        out_specs=pl.BlockSpec((1, tm, oc), lambda g, r: (g, r, 0)),
        out_shape=jax.ShapeDtypeStruct((N_DIR, seq, oc), F32),
        compiler_params=_cparams(("arbitrary", "arbitrary")),
        name="hyena_filter_mlp",
    )(zcat, w1p, f_b1[None, :], f_freq[None, :], f_w2, f_b2[None, :], w3d, ad)


def _dft_tables(seq):
    n2 = 2 * seq
    f = jnp.arange(seq, dtype=jnp.int32)
    ft = (f[:, None] * f[None, :]) % n2
    ang = ft.astype(F32) * (2.0 * math.pi / n2)
    return jnp.cos(ang).astype(BF16), jnp.sin(ang).astype(BF16)


def _spec_kernel(fc_ref, fs_ref, h_ref, a_ref, b_ref, nyq_ref, *, seq, tf):
    hlo = h_ref[0]
    hhi = h_ref[1]
    hlo_b = hlo.astype(BF16)
    hhi_b = hhi.astype(BF16)
    f = pl.program_id(1) * tf + lax.broadcasted_iota(jnp.int32, (tf, 1), 0)
    sgn = jnp.where(f % 2 == 0, 1.0, -1.0)
    fc = fc_ref[...]
    fs = fs_ref[...]
    hr = _dot(fc, hlo_b) + sgn * _dot(fc, hhi_b)
    hs = _dot(fs, hlo_b) + sgn * _dot(fs, hhi_b)
    w = jnp.where(f == 0, 1.0, 2.0) * (1.0 / (2 * seq))
    a_ref[...] = w * hr
    b_ref[...] = -(w * hs)
    t = lax.broadcasted_iota(jnp.int32, (seq, 1), 0)
    alt = jnp.where(t % 2 == 0, 1.0, -1.0)
    nyq = jnp.sum(alt * (hlo + hhi), axis=0, keepdims=True) * (1.0 / (2 * seq))
    nyq_ref[...] = jnp.broadcast_to(nyq, nyq_ref.shape)


def _filter_spectra(fc, fs, hcat, tf=512, tcol=512):
    seq = fc.shape[0]
    oc = hcat.shape[2]
    return pl.pallas_call(
        functools.partial(_spec_kernel, seq=seq, tf=tf),
        grid=(oc // tcol, seq // tf),
        in_specs=[pl.BlockSpec((tf, seq), lambda j, i: (i, 0)),
                  pl.BlockSpec((tf, seq), lambda j, i: (i, 0)),
                  pl.BlockSpec((2, seq, tcol), lambda j, i: (0, 0, j))],
        out_specs=[pl.BlockSpec((tf, tcol), lambda j, i: (i, j)),
                   pl.BlockSpec((tf, tcol), lambda j, i: (i, j)),
                   pl.BlockSpec((SUBLANES, tcol), lambda j, i: (0, j))],
        out_shape=[jax.ShapeDtypeStruct((seq, oc), F32),
                   jax.ShapeDtypeStruct((seq, oc), F32),
                   jax.ShapeDtypeStruct((SUBLANES, oc), F32)],
        compiler_params=_cparams(("arbitrary", "arbitrary")),
        name="hyena_filter_spectra",
    )(fc, fs, hcat)


def _shortconv(x, w_ref, b_ref, seq):
    row = lax.broadcasted_iota(jnp.int32, (seq, 1), 0)
    xm = jnp.where(row == 0, 0.0, pltpu.roll(x, 1, 0))
    xp = jnp.where(row == seq - 1, 0.0, pltpu.roll(x, seq - 1, 0))
    return xm * w_ref[0:1, :] + x * w_ref[1:2, :] + xp * w_ref[2:3, :] + b_ref[...]


def _conv_kernel(fc_ref, fs_ref, u_ref, g_ref, a_ref, b_ref, nyq_ref, bias_ref,
                 cwu_ref, cbu_ref, cwg_ref, cbg_ref, o_ref, acc_ref, *, seq, fb, conv_u):
    u = u_ref[...]
    if conv_u:
        u = _shortconv(u, cwu_ref, cbu_ref, seq)
    gate = _shortconv(g_ref[...], cwg_ref, cbg_ref, seq)
    ub = u.astype(BF16)
    for c in range(seq // fb):
        rows = slice(c * fb, (c + 1) * fb)
        ur = _dot(fc_ref[rows, :], ub)
        us = _dot(fs_ref[rows, :], ub)
        a = a_ref[rows, :]
        b = b_ref[rows, :]
        qr = (ur * a + us * b).astype(BF16)
        qi = (us * a - ur * b).astype(BF16)
        part = _dot(fc_ref[:, rows], qr) + _dot(fs_ref[:, rows], qi)
        if c == 0:
            acc_ref[...] = part
        else:
            acc_ref[...] += part
    t = lax.broadcasted_iota(jnp.int32, (seq, 1), 0)
    alt = jnp.where(t % 2 == 0, 1.0, -1.0)
    unyq = jnp.sum(alt * u, axis=0, keepdims=True)
    y = acc_ref[...] + alt * (unyq * nyq_ref[0:1, :]) + bias_ref[...] * u
    o_ref[...] = gate * y


def _long_conv(fc, fs, u_src, u_blk0, g_src, g_blk0, spec_a, spec_b, nyq, s_blk0, bias, cw, cb,
               cu_blk0, cg_blk0, bsz, seq, conv_u, tc=256, fb=512):
    nct = 512 // tc
    t = bsz * seq
    const = lambda j, b: (0, 0)
    return pl.pallas_call(
        functools.partial(_conv_kernel, seq=seq, fb=fb, conv_u=conv_u),
        grid=(nct, bsz),
        in_specs=[pl.BlockSpec(memory_space=pltpu.VMEM),
                  pl.BlockSpec(memory_space=pltpu.VMEM),
                  pl.BlockSpec((seq, tc), lambda j, b: (b, u_blk0 + j)),
                  pl.BlockSpec((seq, tc), lambda j, b: (b, g_blk0 + j)),
                  pl.BlockSpec((seq, tc), lambda j, b: (0, s_blk0 + j)),
                  pl.BlockSpec((seq, tc), lambda j, b: (0, s_blk0 + j)),
                  pl.BlockSpec((SUBLANES, tc), lambda j, b: (0, s_blk0 + j)),
                  pl.BlockSpec((1, tc), lambda j, b: (0, j)),
                  pl.BlockSpec((3, tc), lambda j, b: (0, cu_blk0 + j)),
                  pl.BlockSpec((1, tc), lambda j, b: (0, cu_blk0 + j)),
                  pl.BlockSpec((3, tc), lambda j, b: (0, cg_blk0 + j)),
                  pl.BlockSpec((1, tc), lambda j, b: (0, cg_blk0 + j))],
        out_specs=pl.BlockSpec((seq, tc), lambda j, b: (b, j)),
        out_shape=jax.ShapeDtypeStruct((t, 512), F32),
        scratch_shapes=[pltpu.VMEM((seq, tc), F32)],
        compiler_params=_cparams(("arbitrary", "arbitrary")),
        name="hyena_long_conv_u" if conv_u else "hyena_long_conv",
    )(fc, fs, u_src, g_src, spec_a, spec_b, nyq, bias, cw, cb, cw, cb)


def _attn_kernel(q_ref, k_ref, v_ref, cos_ref, sin_ref, qg_ref, kg_ref, sink_ref, o_ref,
                 qn_ref, km_ref, vm_ref, *, seq):
    lane = lax.broadcasted_iota(jnp.int32, (1, LANES), 1)
    r = lax.broadcasted_iota(jnp.int32, (LANES, LANES), 0) // HEAD_DIM
    c = lax.broadcasted_iota(jnp.int32, (LANES, LANES), 1) // HEAD_DIM
    bd = jnp.where(r == c, 1.0, 0.0).astype(BF16)
    first_half = (lane % HEAD_DIM) < (HEAD_DIM // 2)
    cos = cos_ref[...]
    sin = sin_ref[...]

    def norm_rope(x, gain):
        sq = x * x
        hi = sq.astype(BF16)
        lo = (sq - hi.astype(F32)).astype(BF16)
        ss = _dot(hi, bd) + _dot(lo, bd)
        y = (x * lax.rsqrt(ss * (1.0 / HEAD_DIM) + EPS)) * gain
        partner = jnp.where(first_half, pltpu.roll(y, LANES - HEAD_DIM // 2, 1), pltpu.roll(y, HEAD_DIM // 2, 1))
        return y * cos + partner * sin

    for p in range(N_Q_HEADS // 2):
        cols = slice(p * LANES, (p + 1) * LANES)
        qn_ref[:, cols] = norm_rope(q_ref[:, cols], qg_ref[...]).astype(BF16)
    kn = norm_rope(k_ref[...], kg_ref[...])
    left = lane < HEAD_DIM
    for src_ref, dst_ref in ((None, km_ref), (v_ref, vm_ref)):
        val = kn if src_ref is None else src_ref[...]
        rolled = pltpu.roll(val, HEAD_DIM, 1)
        dst_ref[0] = jnp.where(left, val, 0.0).astype(BF16)
        dst_ref[1] = jnp.where(left, 0.0, rolled).astype(BF16)
        dst_ref[2] = jnp.where(left, rolled, 0.0).astype(BF16)
        dst_ref[3] = jnp.where(left, 0.0, val).astype(BF16)

    span = 3 * BLOCK
    scale = HEAD_DIM ** -0.5
    ii = lax.broadcasted_iota(jnp.int32, (BLOCK, span), 0)
    jj = lax.broadcasted_iota(jnp.int32, (BLOCK, span), 1)

    def block(n, carry):
        q0 = pl.multiple_of(n * BLOCK, BLOCK)
        start = pl.multiple_of(jnp.clip((n - 1) * BLOCK, 0, seq - span), BLOCK)
        valid = jnp.abs((start - q0) + jj - ii) <= WINDOW
        for p in range(N_Q_HEADS // 2):
            cols = slice(p * LANES, (p + 1) * LANES)
            kv = (2 * p) // (N_Q_HEADS // N_KV_HEADS)
            qp = qn_ref[pl.ds(q0, BLOCK), cols]
            o = jnp.zeros((BLOCK, LANES), F32)
            for a in range(2):
                h = 2 * p + a
                kb = km_ref[2 * kv + a, pl.ds(start, span), :]
                s = _dot_nt(qp, kb) * scale
                s = jnp.where(valid, s, -jnp.inf)
                sk = sink_ref[h:h + 1, 0:1]
                m = jnp.maximum(jnp.max(s, axis=-1, keepdims=True), sk)
                e = jnp.exp(s - m)
                den = jnp.sum(e, axis=-1, keepdims=True) + jnp.exp(sk - m)
                pn = (e / den).astype(BF16)
                o = o + _dot(pn, vm_ref[2 * kv + a, pl.ds(start, span), :])
            o_ref[pl.ds(q0, BLOCK), cols] = o
        return carry

    lax.fori_loop(0, seq // BLOCK, block, 0)


def _attention(q, k, v, q_gain, k_gain, sink, bsz, seq):
    inv = ROPE_THETA ** (-jnp.arange(0, HEAD_DIM, 2, dtype=F32) / HEAD_DIM)
    ang = jnp.arange(seq, dtype=F32)[:, None] * inv[None, :]
    cos = jnp.tile(jnp.cos(ang), (1, LANES // (HEAD_DIM // 2)))
    sn = jnp.sin(ang)
    sin = jnp.tile(jnp.concatenate([-sn, sn], axis=1), (1, LANES // HEAD_DIM))
    qg = jnp.tile(q_gain, LANES // HEAD_DIM)[None, :]
    kg = jnp.tile(k_gain, LANES // HEAD_DIM)[None, :]
    sinkb = jnp.broadcast_to(sink[:, None], (N_Q_HEADS, LANES))
    aw = N_Q_HEADS * HEAD_DIM
    kw = N_KV_HEADS * HEAD_DIM
    const = lambda b: (0, 0)
    return pl.pallas_call(
        functools.partial(_attn_kernel, seq=seq),
        grid=(bsz,),
        in_specs=[pl.BlockSpec((seq, aw), lambda b: (b, 0)),
                  pl.BlockSpec((seq, kw), lambda b: (b, 0)),
                  pl.BlockSpec((seq, kw), lambda b: (b, 0)),
                  pl.BlockSpec((seq, LANES), const),
                  pl.BlockSpec((seq, LANES), const),
                  pl.BlockSpec((1, LANES), const),
                  pl.BlockSpec((1, LANES), const),
                  pl.BlockSpec((N_Q_HEADS, LANES), const)],
        out_specs=pl.BlockSpec((seq, aw), lambda b: (b, 0)),
        out_shape=jax.ShapeDtypeStruct((bsz * seq, aw), F32),
        scratch_shapes=[pltpu.VMEM((seq, aw), BF16),
                        pltpu.VMEM((4, seq, LANES), BF16),
                        pltpu.VMEM((4, seq, LANES), BF16)],
        compiler_params=_cparams(("arbitrary",)),
        name="window_attention",
    )(q, k, v, cos, sin, qg, kg, sinkb)


def _merge_kernel(x_ref, mod_ref, ya_ref, yb_ref, ga_ref, gb_ref, wpa_ref, wpb_ref, wo_ref, o_ref):
    pa = _dot(ya_ref[...].astype(BF16), wpa_ref[...])
    pb = _dot(yb_ref[...].astype(BF16), wpb_ref[...])
    merged = _sigmoid(ga_ref[...]) * pa + _sigmoid(gb_ref[...]) * pb
    out = _dot(merged.astype(BF16), wo_ref[...])
    o_ref[...] = x_ref[...] + mod_ref[0][2:3] * out


def _merge(x2, mod3, ya, yb, ga, gb, w_pa_bf, w_pb_bf, w_out_bf, seq, tm=512):
    t, d = x2.shape
    per = seq // tm
    row = lambda w: pl.BlockSpec((tm, w), lambda i: (i, 0))
    full = lambda a: pl.BlockSpec(a.shape, lambda i: (0, 0))
    return pl.pallas_call(
        _merge_kernel,
        grid=(t // tm,),
        in_specs=[row(d), pl.BlockSpec((1, 6, d), lambda i: (i // per, 0, 0)),
                  row(ya.shape[1]), row(yb.shape[1]), row(d), row(d),
                  full(w_pa_bf), full(w_pb_bf), full(w_out_bf)],
        out_specs=row(d),
        out_shape=jax.ShapeDtypeStruct((t, d), F32),
        compiler_params=_cparams(("arbitrary",)),
        name="merge_outproj",
    )(x2, mod3, ya, yb, ga, gb, w_pa_bf, w_pb_bf, w_out_bf)


def _topk_rows(s, k, val_ref, idx_ref):
    n = s.shape[0]
    row = lax.broadcasted_iota(jnp.int32, s.shape, 0)
    for i in range(k):
        m = jnp.max(s, axis=0, keepdims=True)
        first = jnp.min(jnp.where(s == m, row, n), axis=0, keepdims=True)
        val_ref[i:i + 1, :] = m
        idx_ref[i:i + 1, :] = first
        s = jnp.where(row == first, -jnp.inf, s)


def _peer_score_kernel(x_ref, mod_ref, g_ref, wq_ref, k1_ref, k2_ref, h_ref, idx_ref, gate_ref,
                       v1_ref, i1_ref, v2_ref, i2_ref, cv_ref, ce_ref, it_ref, gt_ref, *, tm):
    m = mod_ref[0]
    h = _norm_mod(x_ref[...], g_ref[...], m[3:4], m[4:5])
    h_ref[...] = h
    q = _dot(h.astype(BF16), wq_ref[...]).astype(BF16)
    half = PEER_QDIM // 2
    neg = jnp.full((1, tm), -jnp.inf, F32)
    for hd in range(PEER_HEADS):
        q1 = q[:, hd * PEER_QDIM: hd * PEER_QDIM + half]
        q2 = q[:, hd * PEER_QDIM + half: (hd + 1) * PEER_QDIM]
        _topk_rows(_dot_nt(k1_ref[...], q1), PEER_TOPK, v1_ref, i1_ref)
        _topk_rows(_dot_nt(k2_ref[...], q2), PEER_TOPK, v2_ref, i2_ref)
        for r, (i, j) in enumerate(_STAIR):
            cv_ref[r:r + 1, :] = v1_ref[i:i + 1, :] + v2_ref[j:j + 1, :]
            ce_ref[r:r + 1, :] = i1_ref[i:i + 1, :] * N_KEYS + i2_ref[j:j + 1, :]
        for r in range(len(_STAIR), _STAIR_ROWS):
            cv_ref[r:r + 1, :] = neg
            ce_ref[r:r + 1, :] = jnp.zeros((1, tm), jnp.int32)
        cand = cv_ref[...]
        ce = ce_ref[...]
        row = lax.broadcasted_iota(jnp.int32, cand.shape, 0)
        vals = []
        for kk in range(PEER_TOPK):
            mx = jnp.max(cand, axis=0, keepdims=True)
            first = jnp.min(jnp.where(cand == mx, row, _STAIR_ROWS), axis=0, keepdims=True)
            sel = row == first
            it_ref[hd * PEER_TOPK + kk: hd * PEER_TOPK + kk + 1, :] = jnp.sum(jnp.where(sel, ce, 0), axis=0, keepdims=True)
            vals.append(mx)
            cand = jnp.where(sel, -jnp.inf, cand)
        ex = [jnp.exp(v - vals[0]) for v in vals]
        tot = ex[0]
        for e in ex[1:]:
            tot = tot + e
        for kk in range(PEER_TOPK):
            gt_ref[hd * PEER_TOPK + kk: hd * PEER_TOPK + kk + 1, :] = ex[kk] / tot
    idx_ref[...] = it_ref[...].T
    gate_ref[...] = gt_ref[...].T


def _peer_score(x2, mod3, gain, wq_bf, k1_bf, k2_bf, seq, tm=256):
    t, d = x2.shape
    per = seq // tm
    slots = PEER_HEADS * PEER_TOPK
    full = lambda a: pl.BlockSpec(a.shape, lambda i: (0, 0))
    return pl.pallas_call(
        functools.partial(_peer_score_kernel, tm=tm),
        grid=(t // tm,),
        in_specs=[pl.BlockSpec((tm, d), lambda i: (i, 0)),
                  pl.BlockSpec((1, 6, d), lambda i: (i // per, 0, 0)),
                  pl.BlockSpec((1, d), lambda i: (0, 0)),
                  full(wq_bf), full(k1_bf), full(k2_bf)],
        out_specs=[pl.BlockSpec((tm, d), lambda i: (i, 0)),
                   pl.BlockSpec((tm, slots), lambda i: (i, 0)),
                   pl.BlockSpec((tm, slots), lambda i: (i, 0))],
        out_shape=[jax.ShapeDtypeStruct((t, d), F32),
                   jax.ShapeDtypeStruct((t, slots), jnp.int32),
                   jax.ShapeDtypeStruct((t, slots), F32)],
        scratch_shapes=[pltpu.VMEM((PEER_TOPK, tm), F32), pltpu.VMEM((PEER_TOPK, tm), jnp.int32),
                        pltpu.VMEM((PEER_TOPK, tm), F32), pltpu.VMEM((PEER_TOPK, tm), jnp.int32),
                        pltpu.VMEM((_STAIR_ROWS, tm), F32), pltpu.VMEM((_STAIR_ROWS, tm), jnp.int32),
                        pltpu.VMEM((slots, tm), jnp.int32), pltpu.VMEM((slots, tm), F32)],
        compiler_params=_cparams(("arbitrary",)),
        name="peer_score_topk",
    )(x2, mod3, gain.reshape(1, d), wq_bf, k1_bf, k2_bf)


def _peer_gather_kernel(idx_ref, idxn_ref, tab_ref, h_ref, gate_ref, x_ref, mod_ref, o_ref,
                        buf_ref, sem_ref, *, tc, d, nsteps):
    slots = PEER_HEADS * PEER_TOPK
    step = pl.program_id(0)
    cur = step % 2

    def row_copy(row, dst_slot, dst_row):
        return pltpu.make_async_copy(tab_ref.at[pl.ds(row, 1), :],
                                     buf_ref.at[dst_slot, pl.ds(dst_row, 1), :],
                                     sem_ref.at[dst_slot])

    def issue(ids_ref, dst_slot):
        def tok(t, carry):
            for j in range(slots):
                row_copy(ids_ref[t, j], dst_slot, t * slots + j).start()
            return carry
        lax.fori_loop(0, tc, tok, 0)

    @pl.when(step == 0)
    def _():
        issue(idx_ref, 0)

    @pl.when(step + 1 < nsteps)
    def _():
        issue(idxn_ref, 1 - cur)

    pltpu.make_async_copy(tab_ref.at[pl.ds(0, tc * slots), :], buf_ref.at[cur], sem_ref.at[cur]).wait()

    eye = (lax.broadcasted_iota(jnp.int32, (slots, slots), 0) ==
           lax.broadcasted_iota(jnp.int32, (slots, slots), 1))
    gt2 = mod_ref[0][5:6]

    def tok(t, carry):
        base = pl.multiple_of(t * slots, slots)
        u = buf_ref[cur, pl.ds(base, slots), 0:d]
        prod = u * h_ref[pl.ds(t, 1), :]
        part = prod[:, 0:LANES]
        for c in range(1, d // LANES):
            part = part + prod[:, c * LANES:(c + 1) * LANES]
        a = jnp.sum(part, axis=1, keepdims=True)
        grow = jnp.broadcast_to(gate_ref[pl.ds(t, 1), :], (slots, slots))
        gcol = jnp.sum(jnp.where(eye, grow, 0.0), axis=1, keepdims=True)
        w = gcol * (0.5 * a * (1.0 + lax.erf(a * (2.0 ** -0.5))))
        v = buf_ref[cur, pl.ds(base, slots), d:2 * d]
        y = jnp.sum(v * w, axis=0, keepdims=True)
        o_ref[pl.ds(t, 1), :] = x_ref[pl.ds(t, 1), :] + gt2 * y
        return carry

    lax.fori_loop(0, tc, tok, 0)


def _peer_gather(x2, mod3, h2, idx, gate, table, seq, tc=8):
    t, d = x2.shape
    slots = idx.shape[1]
    nsteps = t // tc
    per = seq // tc
    row = lambda w: pl.BlockSpec((tc, w), lambda i: (i, 0))
    return pl.pallas_call(
        functools.partial(_peer_gather_kernel, tc=tc, d=d, nsteps=nsteps),
        grid=(nsteps,),
        in_specs=[pl.BlockSpec((tc, slots), lambda i: (i, 0), memory_space=pltpu.SMEM),
                  pl.BlockSpec((tc, slots), lambda i: (jnp.minimum(i + 1, nsteps - 1), 0), memory_space=pltpu.SMEM),
                  pl.BlockSpec(memory_space=pl.ANY),
                  row(d), row(slots), row(d),
                  pl.BlockSpec((1, 6, d), lambda i: (i // per, 0, 0))],
        out_specs=row(d),
        out_shape=jax.ShapeDtypeStruct((t, d), F32),
        scratch_shapes=[pltpu.VMEM((2, tc * slots, 2 * d), F32), pltpu.SemaphoreType.DMA((2,))],
        compiler_params=_cparams(("arbitrary",)),
        name="peer_gather_mix",
    )(idx, idx, table, h2, gate, x2, mod3)


def kernel(x_prompt, x_sample, c_prompt, c_sample, w_mod, b_mod, g_norm1, g_norm2, w_in, conv_w, conv_b, f_w1, f_b1, f_freq, f_w2, f_b2, f_w3, f_bias, q_gain, k_gain, sink, w_pa, w_pb, w_out, peer_wq, peer_k1, peer_k2, peer_u, peer_v):
    depth = w_mod.shape[0]
    bp, seq, d = x_prompt.shape
    bs = x_sample.shape[0]
    assert x_sample.shape[1] == seq
    bsz = bp + bs
    x = jnp.concatenate([x_prompt, x_sample], axis=0).reshape(bsz * seq, d)
    c = jnp.concatenate([c_prompt, c_sample], axis=0)

    hyw = w_pa.shape[1]
    aw = w_pb.shape[1]
    kw = N_KV_HEADS * HEAD_DIM
    widths = (HY_ORDER + 1) * hyw, aw, kw, kw, d, d
    cblk = hyw // 256

    mod = _modulation(c, w_mod.astype(BF16), b_mod)
    fc, fs = _dft_tables(seq)

    for l in range(depth):
        mod3 = mod[l].reshape(bsz, 6, d)
        hy, q, k, v, ga, gb = _inproj(x, mod3, g_norm1[l], w_in[l].astype(BF16), seq, widths)
        hcat = _filters_time(seq, f_w1[l], f_b1[l], f_freq[l], f_w2[l], f_b2[l], f_w3[l])
        spec_a, spec_b, nyq = _filter_spectra(fc, fs, hcat)
        cw, cb = conv_w[l], conv_b[l][None, :]
        zz = _long_conv(fc, fs, hy, 0, hy, cblk, spec_a, spec_b, nyq, 0, f_bias[l][0:1], cw, cb,
                        0, cblk, bsz, seq, conv_u=True)
        ya = _long_conv(fc, fs, zz, 0, hy, 2 * cblk, spec_a, spec_b, nyq, cblk, f_bias[l][1:2], cw, cb,
                        0, 2 * cblk, bsz, seq, conv_u=False)
        yb = _attention(q, k, v, q_gain[l], k_gain[l], sink[l], bsz, seq)
        x = _merge(x, mod3, ya, yb, ga, gb, w_pa[l].astype(BF16), w_pb[l].astype(BF16), w_out[l].astype(BF16), seq)
        h2, idx, gate = _peer_score(x, mod3, g_norm2[l], peer_wq[l].astype(BF16),
                                    peer_k1[l].astype(BF16), peer_k2[l].astype(BF16), seq)
        table = jnp.concatenate([peer_u[l], peer_v[l]], axis=1)
        x = _peer_gather(x, mod3, h2, idx, gate, table, seq)

    x = x.reshape(bsz, seq, d)
    return x[:bp], x[bp:]
```

```python
import functools
import math

import jax
import jax.numpy as jnp
import numpy as np
from jax import lax
from jax.experimental import pallas as pl
from jax.experimental.pallas import tpu as pltpu

F32 = jnp.float32
BF16 = jnp.bfloat16

EPS = 1e-6
HEAD_DIM = 64
N_Q_HEADS = 8
N_KV_HEADS = 2
WINDOW = 128
BLOCK = 128
ROPE_THETA = 10000.0
HY_ORDER = 2
N_DIR = 2
FILT_BANDS = 16
DECAY_TARGET = 1e-2
FAST_DECAY_PCT = 0.3
SLOW_DECAY_PCT = 1.5
PEER_HEADS = 8
N_KEYS = 128
PEER_TOPK = 16
PEER_QDIM = 256
LANES = 128
SUBLANES = 8
VMEM_LIMIT = 56 * 1024 * 1024
GATHER_RING = 4
GATHER_AHEAD = 2

_STAIR = [(i, j) for i in range(PEER_TOPK) for j in range(PEER_TOPK) if (i + 1) * (j + 1) <= PEER_TOPK]
_STAIR_ROWS = -(-len(_STAIR) // SUBLANES) * SUBLANES


def _cparams(sem, vmem=VMEM_LIMIT):
    return pltpu.CompilerParams(dimension_semantics=sem, vmem_limit_bytes=vmem)


def _dot(a, b):
    return jnp.dot(a, b, preferred_element_type=F32)


def _dot_nt(a, b):
    return lax.dot_general(a, b, (((1,), (1,)), ((), ())), preferred_element_type=F32)


def _dot_hi(a, b):
    return jnp.dot(a, b, preferred_element_type=F32, precision=lax.Precision.HIGHEST)


def _sigmoid(x):
    return 1.0 / (1.0 + jnp.exp(-x))


def _mod_kernel(c_ref, w_ref, b_ref, o_ref):
    c = c_ref[...]
    s = c * _sigmoid(c)
    o_ref[0] = _dot(s.astype(BF16), w_ref[0]) + b_ref[0]


def _modulation(c, w_mod_bf, b_mod):
    depth, d, n6 = w_mod_bf.shape
    bsz = c.shape[0]
    tn = 1536
    return pl.pallas_call(
        _mod_kernel,
        grid=(depth, n6 // tn),
        in_specs=[pl.BlockSpec((bsz, d), lambda l, j: (0, 0)),
                  pl.BlockSpec((1, d, tn), lambda l, j: (l, 0, j)),
                  pl.BlockSpec((1, 1, tn), lambda l, j: (l, 0, j))],
        out_specs=pl.BlockSpec((1, bsz, tn), lambda l, j: (l, 0, j)),
        out_shape=jax.ShapeDtypeStruct((depth, bsz, n6), F32),
        compiler_params=_cparams(("arbitrary", "arbitrary")),
        name="adaln_mod",
    )(c, w_mod_bf, b_mod.reshape(depth, 1, n6))


def _norm_mod(x, gain, shift, scale):
    y = x * lax.rsqrt(jnp.mean(x * x, axis=-1, keepdims=True) + EPS)
    return (y * gain) * (1.0 + scale) + shift


def _inproj_kernel(x_ref, mod_ref, g_ref, w_ref, hy_ref, q_ref, k_ref, v_ref, ga_ref, gb_ref, *, splits):
    m = mod_ref[0]
    h = _norm_mod(x_ref[...], g_ref[...], m[0:1], m[1:2])
    z = _dot(h.astype(BF16), w_ref[...])
    outs = (hy_ref, q_ref, k_ref, v_ref, ga_ref, gb_ref)
    lo = 0
    for ref, hi in zip(outs, splits):
        ref[...] = z[:, lo:hi]
        lo = hi


def _inproj(x2, mod3, gain, w_in_bf, seq, widths, tm=256):
    t, d = x2.shape
    ncols = w_in_bf.shape[1]
    per = seq // tm
    splits = tuple(int(s) for s in np.cumsum(widths))
    return pl.pallas_call(
        functools.partial(_inproj_kernel, splits=splits),
        grid=(t // tm,),
        in_specs=[pl.BlockSpec((tm, d), lambda i: (i, 0)),
                  pl.BlockSpec((1, 6, d), lambda i: (i // per, 0, 0)),
                  pl.BlockSpec((1, d), lambda i: (0, 0)),
                  pl.BlockSpec((d, ncols), lambda i: (0, 0))],
        out_specs=[pl.BlockSpec((tm, w), lambda i: (i, 0)) for w in widths],
        out_shape=[jax.ShapeDtypeStruct((t, w), F32) for w in widths],
        compiler_params=_cparams(("arbitrary",)),
        name="inproj",
    )(x2, mod3, gain.reshape(1, d), w_in_bf)


def _filter_kernel(z_ref, w1_ref, b1_ref, fr_ref, w2_ref, b2_ref, w3_ref, ad_ref, o_ref):
    z = z_ref[0]
    fr = fr_ref[...]
    a = jnp.sin(fr * (_dot_hi(z, w1_ref[...]) + b1_ref[...]))
    a = jnp.sin(fr * (_dot_hi(a, w2_ref[...]) + b2_ref[...]))
    h = _dot_hi(a, w3_ref[0])
    h = h * jnp.exp(-z[:, 0:1] * ad_ref[...])
    row = lax.broadcasted_iota(jnp.int32, h.shape, 0)
    dead = (pl.program_id(0) == 1) & (pl.program_id(1) == 0) & (row == 0)
    o_ref[0] = jnp.where(dead, 0.0, h)


def _filters_time(seq, f_w1, f_b1, f_freq, f_w2, f_b2, f_w3, tm=512):
    hidden = f_w1.shape[1]
    cw = f_w3.shape[1] // (HY_ORDER * N_DIR)
    t = jnp.linspace(0.0, 1.0, seq, dtype=F32)[:, None]
    w = 2.0 * math.pi * jnp.arange(seq, dtype=F32)[:, None] / seq
    bands = jnp.linspace(1e-4, FILT_BANDS - 1, FILT_BANDS, dtype=F32)[None, :]
    z = jnp.concatenate([t, jnp.cos(bands * w), -jnp.sin(bands * w)], axis=-1)
    emb = z.shape[1]
    z = jnp.pad(z, ((0, 0), (0, LANES - emb)))
    zcat = jnp.stack([z, jnp.concatenate([z[:1], z[:0:-1]], axis=0)], axis=0)
    w1p = jnp.pad(f_w1, ((0, LANES - emb), (0, 0)))
    w3d = f_w3.reshape(hidden, HY_ORDER, N_DIR, cw).transpose(2, 0, 1, 3).reshape(N_DIR, hidden, HY_ORDER * cw)
    max_decay = math.log(DECAY_TARGET) / FAST_DECAY_PCT
    min_decay = math.log(DECAY_TARGET) / SLOW_DECAY_PCT
    ad = jnp.abs(jnp.linspace(min_decay, max_decay, cw, dtype=F32))
    ad = jnp.tile(ad, HY_ORDER)[None, :]
    oc = HY_ORDER * cw
    return pl.pallas_call(
        _filter_kernel,
        grid=(N_DIR, seq // tm),
        in_specs=[pl.BlockSpec((1, tm, LANES), lambda g, r: (g, r, 0)),
                  pl.BlockSpec((LANES, hidden), lambda g, r: (0, 0)),
                  pl.BlockSpec((1, hidden), lambda g, r: (0, 0)),
                  pl.BlockSpec((1, hidden), lambda g, r: (0, 0)),
                  pl.BlockSpec((hidden, hidden), lambda g, r: (0, 0)),
                  pl.BlockSpec((1, hidden), lambda g, r: (0, 0)),
                  pl.BlockSpec((1, hidden, oc), lambda g, r: (g, 0, 0)),
                  pl.BlockSpec((1, oc), lambda g, r: (0, 0))],
        out_specs=pl.BlockSpec((1, tm, oc), lambda g, r: (g, r, 0)),
        out_shape=jax.ShapeDtypeStruct((N_DIR, seq, oc), F32),
        compiler_params=_cparams(("arbitrary", "arbitrary")),
        name="hyena_filter_mlp",
    )(zcat, w1p, f_b1[None, :], f_freq[None, :], f_w2, f_b2[None, :], w3d, ad)


def _dft_tables(seq):
    n2 = 2 * seq
    f = jnp.arange(seq, dtype=jnp.int32)
    ft = (f[:, None] * f[None, :]) % n2
    ang = ft.astype(F32) * (2.0 * math.pi / n2)
    return jnp.cos(ang).astype(BF16), jnp.sin(ang).astype(BF16)


def _spec_kernel(fc_ref, fs_ref, h_ref, a_ref, b_ref, nyq_ref, *, seq, tf):
    hlo = h_ref[0]
    hhi = h_ref[1]
    hlo_b = hlo.astype(BF16)
    hhi_b = hhi.astype(BF16)
    f = pl.program_id(1) * tf + lax.broadcasted_iota(jnp.int32, (tf, 1), 0)
    sgn = jnp.where(f % 2 == 0, 1.0, -1.0)
    fc = fc_ref[...]
    fs = fs_ref[...]
    hr = _dot(fc, hlo_b) + sgn * _dot(fc, hhi_b)
    hs = _dot(fs, hlo_b) + sgn * _dot(fs, hhi_b)
    w = jnp.where(f == 0, 1.0, 2.0) * (1.0 / (2 * seq))
    a_ref[...] = w * hr
    b_ref[...] = -(w * hs)
    t = lax.broadcasted_iota(jnp.int32, (seq, 1), 0)
    alt = jnp.where(t % 2 == 0, 1.0, -1.0)
    nyq = jnp.sum(alt * (hlo + hhi), axis=0, keepdims=True) * (1.0 / (2 * seq))
    nyq_ref[...] = jnp.broadcast_to(nyq, nyq_ref.shape)


def _filter_spectra(fc, fs, hcat, tf=512, tcol=512):
    seq = fc.shape[0]
    oc = hcat.shape[2]
    return pl.pallas_call(
        functools.partial(_spec_kernel, seq=seq, tf=tf),
        grid=(oc // tcol, seq // tf),
        in_specs=[pl.BlockSpec((tf, seq), lambda j, i: (i, 0)),
                  pl.BlockSpec((tf, seq), lambda j, i: (i, 0)),
                  pl.BlockSpec((2, seq, tcol), lambda j, i: (0, 0, j))],
        out_specs=[pl.BlockSpec((tf, tcol), lambda j, i: (i, j)),
                   pl.BlockSpec((tf, tcol), lambda j, i: (i, j)),
                   pl.BlockSpec((SUBLANES, tcol), lambda j, i: (0, j))],
        out_shape=[jax.ShapeDtypeStruct((seq, oc), F32),
                   jax.ShapeDtypeStruct((seq, oc), F32),
                   jax.ShapeDtypeStruct((SUBLANES, oc), F32)],
        compiler_params=_cparams(("arbitrary", "arbitrary")),
        name="hyena_filter_spectra",
    )(fc, fs, hcat)


def _shortconv(x, w_ref, b_ref, seq):
    row = lax.broadcasted_iota(jnp.int32, (seq, 1), 0)
    xm = jnp.where(row == 0, 0.0, pltpu.roll(x, 1, 0))
    xp = jnp.where(row == seq - 1, 0.0, pltpu.roll(x, seq - 1, 0))
    return xm * w_ref[0:1, :] + x * w_ref[1:2, :] + xp * w_ref[2:3, :] + b_ref[...]


def _conv_kernel(fc_ref, fs_ref, u_ref, g_ref, a_ref, b_ref, nyq_ref, bias_ref,
                 cwu_ref, cbu_ref, cwg_ref, cbg_ref, o_ref, acc_ref, *, seq, fb, conv_u):
    u = u_ref[...]
    if conv_u:
        u = _shortconv(u, cwu_ref, cbu_ref, seq)
    gate = _shortconv(g_ref[...], cwg_ref, cbg_ref, seq)
    ub = u.astype(BF16)
    for c in range(seq // fb):
        rows = slice(c * fb, (c + 1) * fb)
        ur = _dot(fc_ref[rows, :], ub)
        us = _dot(fs_ref[rows, :], ub)
        a = a_ref[rows, :]
        b = b_ref[rows, :]
        qr = (ur * a + us * b).astype(BF16)
        qi = (us * a - ur * b).astype(BF16)
        part = _dot(fc_ref[:, rows], qr) + _dot(fs_ref[:, rows], qi)
        if c == 0:
            acc_ref[...] = part
        else:
            acc_ref[...] += part
    t = lax.broadcasted_iota(jnp.int32, (seq, 1), 0)
    alt = jnp.where(t % 2 == 0, 1.0, -1.0)
    unyq = jnp.sum(alt * u, axis=0, keepdims=True)
    y = acc_ref[...] + alt * (unyq * nyq_ref[0:1, :]) + bias_ref[...] * u
    o_ref[...] = gate * y


def _long_conv(fc, fs, u_src, u_blk0, g_src, g_blk0, spec_a, spec_b, nyq, s_blk0, bias, cw, cb,
               cu_blk0, cg_blk0, bsz, seq, conv_u, tc=256, fb=512):
    nct = 512 // tc
    t = bsz * seq
    const = lambda j, b: (0, 0)
    return pl.pallas_call(
        functools.partial(_conv_kernel, seq=seq, fb=fb, conv_u=conv_u),
        grid=(nct, bsz),
        in_specs=[pl.BlockSpec(memory_space=pltpu.VMEM),
                  pl.BlockSpec(memory_space=pltpu.VMEM),
                  pl.BlockSpec((seq, tc), lambda j, b: (b, u_blk0 + j)),
                  pl.BlockSpec((seq, tc), lambda j, b: (b, g_blk0 + j)),
                  pl.BlockSpec((seq, tc), lambda j, b: (0, s_blk0 + j)),
                  pl.BlockSpec((seq, tc), lambda j, b: (0, s_blk0 + j)),
                  pl.BlockSpec((SUBLANES, tc), lambda j, b: (0, s_blk0 + j)),
                  pl.BlockSpec((1, tc), lambda j, b: (0, j)),
                  pl.BlockSpec((3, tc), lambda j, b: (0, cu_blk0 + j)),
                  pl.BlockSpec((1, tc), lambda j, b: (0, cu_blk0 + j)),
                  pl.BlockSpec((3, tc), lambda j, b: (0, cg_blk0 + j)),
                  pl.BlockSpec((1, tc), lambda j, b: (0, cg_blk0 + j))],
        out_specs=pl.BlockSpec((seq, tc), lambda j, b: (b, j)),
        out_shape=jax.ShapeDtypeStruct((t, 512), F32),
        scratch_shapes=[pltpu.VMEM((seq, tc), F32)],
        compiler_params=_cparams(("arbitrary", "arbitrary")),
        name="hyena_long_conv_u" if conv_u else "hyena_long_conv",
    )(fc, fs, u_src, g_src, spec_a, spec_b, nyq, bias, cw, cb, cw, cb)


def _attn_kernel(q_ref, k_ref, v_ref, cos_ref, sin_ref, qg_ref, kg_ref, sink_ref, o_ref,
                 qn_ref, km_ref, vm_ref, *, seq):
    lane = lax.broadcasted_iota(jnp.int32, (1, LANES), 1)
    r = lax.broadcasted_iota(jnp.int32, (LANES, LANES), 0) // HEAD_DIM
    c = lax.broadcasted_iota(jnp.int32, (LANES, LANES), 1) // HEAD_DIM
    bd = jnp.where(r == c, 1.0, 0.0).astype(BF16)
    first_half = (lane % HEAD_DIM) < (HEAD_DIM // 2)
    cos = cos_ref[...]
    sin = sin_ref[...]

    def norm_rope(x, gain):
        sq = x * x
        hi = sq.astype(BF16)
        lo = (sq - hi.astype(F32)).astype(BF16)
        ss = _dot(hi, bd) + _dot(lo, bd)
        y = (x * lax.rsqrt(ss * (1.0 / HEAD_DIM) + EPS)) * gain
        partner = jnp.where(first_half, pltpu.roll(y, LANES - HEAD_DIM // 2, 1), pltpu.roll(y, HEAD_DIM // 2, 1))
        return y * cos + partner * sin

    for p in range(N_Q_HEADS // 2):
        cols = slice(p * LANES, (p + 1) * LANES)
        qn_ref[:, cols] = norm_rope(q_ref[:, cols], qg_ref[...]).astype(BF16)
    kn = norm_rope(k_ref[...], kg_ref[...])
    left = lane < HEAD_DIM
    for src_ref, dst_ref in ((None, km_ref), (v_ref, vm_ref)):
        val = kn if src_ref is None else src_ref[...]
        rolled = pltpu.roll(val, HEAD_DIM, 1)
        dst_ref[0] = jnp.where(left, val, 0.0).astype(BF16)
        dst_ref[1] = jnp.where(left, 0.0, rolled).astype(BF16)
        dst_ref[2] = jnp.where(left, rolled, 0.0).astype(BF16)
        dst_ref[3] = jnp.where(left, 0.0, val).astype(BF16)

    span = 3 * BLOCK
    scale = HEAD_DIM ** -0.5
    ii = lax.broadcasted_iota(jnp.int32, (BLOCK, span), 0)
    jj = lax.broadcasted_iota(jnp.int32, (BLOCK, span), 1)

    def block(n, carry):
        q0 = pl.multiple_of(n * BLOCK, BLOCK)
        start = pl.multiple_of(jnp.clip((n - 1) * BLOCK, 0, seq - span), BLOCK)
        valid = jnp.abs((start - q0) + jj - ii) <= WINDOW
        for p in range(N_Q_HEADS // 2):
            cols = slice(p * LANES, (p + 1) * LANES)
            kv = (2 * p) // (N_Q_HEADS // N_KV_HEADS)
            qp = qn_ref[pl.ds(q0, BLOCK), cols]
            o = jnp.zeros((BLOCK, LANES), F32)
            for a in range(2):
                h = 2 * p + a
                kb = km_ref[2 * kv + a, pl.ds(start, span), :]
                s = _dot_nt(qp, kb) * scale
                s = jnp.where(valid, s, -jnp.inf)
                sk = sink_ref[h:h + 1, 0:1]
                m = jnp.maximum(jnp.max(s, axis=-1, keepdims=True), sk)
                e = jnp.exp(s - m)
                den = jnp.sum(e, axis=-1, keepdims=True) + jnp.exp(sk - m)
                pn = (e / den).astype(BF16)
                o = o + _dot(pn, vm_ref[2 * kv + a, pl.ds(start, span), :])
            o_ref[pl.ds(q0, BLOCK), cols] = o
        return carry

    lax.fori_loop(0, seq // BLOCK, block, 0)


def _attention(q, k, v, q_gain, k_gain, sink, bsz, seq):
    inv = ROPE_THETA ** (-jnp.arange(0, HEAD_DIM, 2, dtype=F32) / HEAD_DIM)
    ang = jnp.arange(seq, dtype=F32)[:, None] * inv[None, :]
    cos = jnp.tile(jnp.cos(ang), (1, LANES // (HEAD_DIM // 2)))
    sn = jnp.sin(ang)
    sin = jnp.tile(jnp.concatenate([-sn, sn], axis=1), (1, LANES // HEAD_DIM))
    qg = jnp.tile(q_gain, LANES // HEAD_DIM)[None, :]
    kg = jnp.tile(k_gain, LANES // HEAD_DIM)[None, :]
    sinkb = jnp.broadcast_to(sink[:, None], (N_Q_HEADS, LANES))
    aw = N_Q_HEADS * HEAD_DIM
    kw = N_KV_HEADS * HEAD_DIM
    const = lambda b: (0, 0)
    return pl.pallas_call(
        functools.partial(_attn_kernel, seq=seq),
        grid=(bsz,),
        in_specs=[pl.BlockSpec((seq, aw), lambda b: (b, 0)),
                  pl.BlockSpec((seq, kw), lambda b: (b, 0)),
                  pl.BlockSpec((seq, kw), lambda b: (b, 0)),
                  pl.BlockSpec((seq, LANES), const),
                  pl.BlockSpec((seq, LANES), const),
                  pl.BlockSpec((1, LANES), const),
                  pl.BlockSpec((1, LANES), const),
                  pl.BlockSpec((N_Q_HEADS, LANES), const)],
        out_specs=pl.BlockSpec((seq, aw), lambda b: (b, 0)),
        out_shape=jax.ShapeDtypeStruct((bsz * seq, aw), F32),
        scratch_shapes=[pltpu.VMEM((seq, aw), BF16),
                        pltpu.VMEM((4, seq, LANES), BF16),
                        pltpu.VMEM((4, seq, LANES), BF16)],
        compiler_params=_cparams(("arbitrary",)),
        name="window_attention",
    )(q, k, v, cos, sin, qg, kg, sinkb)


def _merge_kernel(x_ref, mod_ref, ya_ref, yb_ref, ga_ref, gb_ref, wpa_ref, wpb_ref, wo_ref, o_ref):
    pa = _dot(ya_ref[...].astype(BF16), wpa_ref[...])
    pb = _dot(yb_ref[...].astype(BF16), wpb_ref[...])
    merged = _sigmoid(ga_ref[...]) * pa + _sigmoid(gb_ref[...]) * pb
    out = _dot(merged.astype(BF16), wo_ref[...])
    o_ref[...] = x_ref[...] + mod_ref[0][2:3] * out


def _merge(x2, mod3, ya, yb, ga, gb, w_pa_bf, w_pb_bf, w_out_bf, seq, tm=512):
    t, d = x2.shape
    per = seq // tm
    row = lambda w: pl.BlockSpec((tm, w), lambda i: (i, 0))
    full = lambda a: pl.BlockSpec(a.shape, lambda i: (0, 0))
    return pl.pallas_call(
        _merge_kernel,
        grid=(t // tm,),
        in_specs=[row(d), pl.BlockSpec((1, 6, d), lambda i: (i // per, 0, 0)),
                  row(ya.shape[1]), row(yb.shape[1]), row(d), row(d),
                  full(w_pa_bf), full(w_pb_bf), full(w_out_bf)],
        out_specs=row(d),
        out_shape=jax.ShapeDtypeStruct((t, d), F32),
        compiler_params=_cparams(("arbitrary",)),
        name="merge_outproj",
    )(x2, mod3, ya, yb, ga, gb, w_pa_bf, w_pb_bf, w_out_bf)


def _topk_rows(s, k, val_ref, idx_ref):
    n = s.shape[0]
    row = lax.broadcasted_iota(jnp.int32, s.shape, 0)
    for i in range(k):
        m = jnp.max(s, axis=0, keepdims=True)
        first = jnp.min(jnp.where(s == m, row, n), axis=0, keepdims=True)
        val_ref[i:i + 1, :] = m
        idx_ref[i:i + 1, :] = first
        s = jnp.where(row == first, -jnp.inf, s)


def _peer_score_kernel(x_ref, mod_ref, g_ref, wq_ref, k1_ref, k2_ref, h_ref, idx_ref, gate_ref,
                       v1_ref, i1_ref, v2_ref, i2_ref, cv_ref, ce_ref, it_ref, gt_ref, *, tm):
    m = mod_ref[0]
    h = _norm_mod(x_ref[...], g_ref[...], m[3:4], m[4:5])
    h_ref[...] = h
    q = _dot(h.astype(BF16), wq_ref[...]).astype(BF16)
    half = PEER_QDIM // 2
    neg = jnp.full((1, tm), -jnp.inf, F32)
    for hd in range(PEER_HEADS):
        q1 = q[:, hd * PEER_QDIM: hd * PEER_QDIM + half]
        q2 = q[:, hd * PEER_QDIM + half: (hd + 1) * PEER_QDIM]
        _topk_rows(_dot_nt(k1_ref[...], q1), PEER_TOPK, v1_ref, i1_ref)
        _topk_rows(_dot_nt(k2_ref[...], q2), PEER_TOPK, v2_ref, i2_ref)
        for r, (i, j) in enumerate(_STAIR):
            cv_ref[r:r + 1, :] = v1_ref[i:i + 1, :] + v2_ref[j:j + 1, :]
            ce_ref[r:r + 1, :] = i1_ref[i:i + 1, :] * N_KEYS + i2_ref[j:j + 1, :]
        for r in range(len(_STAIR), _STAIR_ROWS):
            cv_ref[r:r + 1, :] = neg
            ce_ref[r:r + 1, :] = jnp.zeros((1, tm), jnp.int32)
        cand = cv_ref[...]
        ce = ce_ref[...]
        row = lax.broadcasted_iota(jnp.int32, cand.shape, 0)
        vals = []
        for kk in range(PEER_TOPK):
            mx = jnp.max(cand, axis=0, keepdims=True)
            first = jnp.min(jnp.where(cand == mx, row, _STAIR_ROWS), axis=0, keepdims=True)
            sel = row == first
            it_ref[hd * PEER_TOPK + kk: hd * PEER_TOPK + kk + 1, :] = jnp.sum(jnp.where(sel, ce, 0), axis=0, keepdims=True)
            vals.append(mx)
            cand = jnp.where(sel, -jnp.inf, cand)
        ex = [jnp.exp(v - vals[0]) for v in vals]
        tot = ex[0]
        for e in ex[1:]:
            tot = tot + e
        for kk in range(PEER_TOPK):
            gt_ref[hd * PEER_TOPK + kk: hd * PEER_TOPK + kk + 1, :] = ex[kk] / tot
    idx_ref[...] = it_ref[...].T
    gate_ref[...] = gt_ref[...].T


def _peer_score(x2, mod3, gain, wq_bf, k1_bf, k2_bf, seq, tm=256):
    t, d = x2.shape
    per = seq // tm
    slots = PEER_HEADS * PEER_TOPK
    full = lambda a: pl.BlockSpec(a.shape, lambda i: (0, 0))
    return pl.pallas_call(
        functools.partial(_peer_score_kernel, tm=tm),
        grid=(t // tm,),
        in_specs=[pl.BlockSpec((tm, d), lambda i: (i, 0)),
                  pl.BlockSpec((1, 6, d), lambda i: (i // per, 0, 0)),
                  pl.BlockSpec((1, d), lambda i: (0, 0)),
                  full(wq_bf), full(k1_bf), full(k2_bf)],
        out_specs=[pl.BlockSpec((tm, d), lambda i: (i, 0)),
                   pl.BlockSpec((tm, slots), lambda i: (i, 0)),
                   pl.BlockSpec((tm, slots), lambda i: (i, 0))],
        out_shape=[jax.ShapeDtypeStruct((t, d), F32),
                   jax.ShapeDtypeStruct((t, slots), jnp.int32),
                   jax.ShapeDtypeStruct((t, slots), F32)],
        scratch_shapes=[pltpu.VMEM((PEER_TOPK, tm), F32), pltpu.VMEM((PEER_TOPK, tm), jnp.int32),
                        pltpu.VMEM((PEER_TOPK, tm), F32), pltpu.VMEM((PEER_TOPK, tm), jnp.int32),
                        pltpu.VMEM((_STAIR_ROWS, tm), F32), pltpu.VMEM((_STAIR_ROWS, tm), jnp.int32),
                        pltpu.VMEM((slots, tm), jnp.int32), pltpu.VMEM((slots, tm), F32)],
        compiler_params=_cparams(("arbitrary",)),
        name="peer_score_topk",
    )(x2, mod3, gain.reshape(1, d), wq_bf, k1_bf, k2_bf)


def _peer_gather_kernel(idx_ref, idxn_ref, tab_ref, h_ref, gate_ref, x_ref, mod_ref, o_ref,
                        buf_ref, sem_ref, *, tc, d, nsteps):
    slots = PEER_HEADS * PEER_TOPK
    step = pl.program_id(0)

    def issue_token(chunk, t):
        ids_ref, c = (idx_ref, chunk) if chunk < GATHER_RING else (idxn_ref, chunk - GATHER_RING)
        for j in range(slots):
            pltpu.make_async_copy(tab_ref.at[ids_ref[c * tc + t, j]],
                                  buf_ref.at[c, t, pl.ds(j, 1), :],
                                  sem_ref.at[c]).start(priority=j % 2)

    def wait(slot):
        pltpu.make_async_copy(buf_ref.at[slot], buf_ref.at[slot], sem_ref.at[slot]).wait()

    eye = (lax.broadcasted_iota(jnp.int32, (slots, slots), 0) ==
           lax.broadcasted_iota(jnp.int32, (slots, slots), 1))
    gt2 = mod_ref[0][5:6]

    def mix_token(slot, t):
        r = slot * tc + t
        u = buf_ref[slot, t, :, 0:d]
        prod = u * h_ref[r:r + 1, :]
        part = prod[:, 0:LANES]
        for c in range(1, d // LANES):
            part = part + prod[:, c * LANES:(c + 1) * LANES]
        a = jnp.sum(part, axis=1, keepdims=True)
        grow = jnp.broadcast_to(gate_ref[r:r + 1, :], (slots, slots))
        gcol = jnp.sum(jnp.where(eye, grow, 0.0), axis=1, keepdims=True)
        w = gcol * (0.5 * a * (1.0 + lax.erf(a * (2.0 ** -0.5))))
        v = buf_ref[slot, t, :, d:2 * d]
        y = jnp.sum(v * w, axis=0, keepdims=True)
        o_ref[r:r + 1, :] = x_ref[r:r + 1, :] + gt2 * y

    @pl.when(step == 0)
    def _():
        for c in range(GATHER_AHEAD):
            for t in range(tc):
                issue_token(c, t)

    for p in range(GATHER_RING):
        wait(p)
        for t in range(tc):
            issue_token(p + GATHER_AHEAD, t)
            mix_token(p, t)

    @pl.when(step == nsteps - 1)
    def _():
        for c in range(GATHER_AHEAD):
            wait(c)


def _peer_gather(x2, mod3, h2, idx, gate, table, seq, tc=4):
    t, d = x2.shape
    slots = idx.shape[1]
    tb = GATHER_RING * tc
    nsteps = t // tb
    per = seq // tb
    row = lambda w: pl.BlockSpec((tb, w), lambda i: (i, 0))
    return pl.pallas_call(
        functools.partial(_peer_gather_kernel, tc=tc, d=d, nsteps=nsteps),
        grid=(nsteps,),
        in_specs=[pl.BlockSpec((tb, slots), lambda i: (i, 0), memory_space=pltpu.SMEM),
                  pl.BlockSpec((tb, slots), lambda i: (jnp.minimum(i + 1, nsteps - 1), 0), memory_space=pltpu.SMEM),
                  pl.BlockSpec(memory_space=pl.ANY),
                  row(d), row(slots), row(d),
                  pl.BlockSpec((1, 6, d), lambda i: (i // per, 0, 0))],
        out_specs=row(d),
        out_shape=jax.ShapeDtypeStruct((t, d), F32),
        scratch_shapes=[pltpu.VMEM((GATHER_RING, tc, slots, 2 * d), F32), pltpu.SemaphoreType.DMA((GATHER_RING,))],
        compiler_params=_cparams(("arbitrary",)),
        name="peer_gather_mix",
    )(idx, idx, table.reshape(table.shape[0], 1, 2 * d), h2, gate, x2, mod3)


def kernel(x_prompt, x_sample, c_prompt, c_sample, w_mod, b_mod, g_norm1, g_norm2, w_in, conv_w, conv_b, f_w1, f_b1, f_freq, f_w2, f_b2, f_w3, f_bias, q_gain, k_gain, sink, w_pa, w_pb, w_out, peer_wq, peer_k1, peer_k2, peer_u, peer_v):
    depth = w_mod.shape[0]
    bp, seq, d = x_prompt.shape
    bs = x_sample.shape[0]
    assert x_sample.shape[1] == seq
    bsz = bp + bs
    x = jnp.concatenate([x_prompt, x_sample], axis=0).reshape(bsz * seq, d)
    c = jnp.concatenate([c_prompt, c_sample], axis=0)

    hyw = w_pa.shape[1]
    aw = w_pb.shape[1]
    kw = N_KV_HEADS * HEAD_DIM
    widths = (HY_ORDER + 1) * hyw, aw, kw, kw, d, d
    cblk = hyw // 256

    mod = _modulation(c, w_mod.astype(BF16), b_mod)
    fc, fs = _dft_tables(seq)

    for l in range(depth):
        mod3 = mod[l].reshape(bsz, 6, d)
        hy, q, k, v, ga, gb = _inproj(x, mod3, g_norm1[l], w_in[l].astype(BF16), seq, widths)
        hcat = _filters_time(seq, f_w1[l], f_b1[l], f_freq[l], f_w2[l], f_b2[l], f_w3[l])
        spec_a, spec_b, nyq = _filter_spectra(fc, fs, hcat)
        cw, cb = conv_w[l], conv_b[l][None, :]
        zz = _long_conv(fc, fs, hy, 0, hy, cblk, spec_a, spec_b, nyq, 0, f_bias[l][0:1], cw, cb,
                        0, cblk, bsz, seq, conv_u=True)
        ya = _long_conv(fc, fs, zz, 0, hy, 2 * cblk, spec_a, spec_b, nyq, cblk, f_bias[l][1:2], cw, cb,
                        0, 2 * cblk, bsz, seq, conv_u=False)
        yb = _attention(q, k, v, q_gain[l], k_gain[l], sink[l], bsz, seq)
        x = _merge(x, mod3, ya, yb, ga, gb, w_pa[l].astype(BF16), w_pb[l].astype(BF16), w_out[l].astype(BF16), seq)
        h2, idx, gate = _peer_score(x, mod3, g_norm2[l], peer_wq[l].astype(BF16),
                                    peer_k1[l].astype(BF16), peer_k2[l].astype(BF16), seq)
        table = jnp.concatenate([peer_u[l], peer_v[l]], axis=1)
        x = _peer_gather(x, mod3, h2, idx, gate, table, seq)

    x = x.reshape(bsz, seq, d)
    return x[:bp], x[bp:]
```

```python
import functools
import math

import jax
import jax.numpy as jnp
import numpy as np
from jax import lax
from jax.experimental import pallas as pl
from jax.experimental.pallas import tpu as pltpu

F32 = jnp.float32
BF16 = jnp.bfloat16

EPS = 1e-6
HEAD_DIM = 64
N_Q_HEADS = 8
N_KV_HEADS = 2
WINDOW = 128
BLOCK = 128
ROPE_THETA = 10000.0
HY_ORDER = 2
N_DIR = 2
FILT_BANDS = 16
DECAY_TARGET = 1e-2
FAST_DECAY_PCT = 0.3
SLOW_DECAY_PCT = 1.5
PEER_HEADS = 8
N_KEYS = 128
PEER_TOPK = 16
PEER_QDIM = 256
LANES = 128
SUBLANES = 8
VMEM_LIMIT = 56 * 1024 * 1024
GATHER_RING = 4
GATHER_AHEAD = 2

_STAIR = [(i, j) for i in range(PEER_TOPK) for j in range(PEER_TOPK) if (i + 1) * (j + 1) <= PEER_TOPK]
_STAIR_ROWS = -(-len(_STAIR) // SUBLANES) * SUBLANES


def _cparams(sem, vmem=VMEM_LIMIT):
    return pltpu.CompilerParams(dimension_semantics=sem, vmem_limit_bytes=vmem)


def _dot(a, b):
    return jnp.dot(a, b, preferred_element_type=F32)


def _dot_nt(a, b):
    return lax.dot_general(a, b, (((1,), (1,)), ((), ())), preferred_element_type=F32)


def _dot_hi(a, b):
    return jnp.dot(a, b, preferred_element_type=F32, precision=lax.Precision.HIGHEST)


def _sigmoid(x):
    return 1.0 / (1.0 + jnp.exp(-x))


def _mod_kernel(c_ref, w_ref, b_ref, o_ref):
    c = c_ref[...]
    s = c * _sigmoid(c)
    o_ref[0] = _dot(s.astype(BF16), w_ref[0]) + b_ref[0]


def _modulation(c, w_mod_bf, b_mod):
    depth, d, n6 = w_mod_bf.shape
    bsz = c.shape[0]
    tn = 1536
    return pl.pallas_call(
        _mod_kernel,
        grid=(depth, n6 // tn),
        in_specs=[pl.BlockSpec((bsz, d), lambda l, j: (0, 0)),
                  pl.BlockSpec((1, d, tn), lambda l, j: (l, 0, j)),
                  pl.BlockSpec((1, 1, tn), lambda l, j: (l, 0, j))],
        out_specs=pl.BlockSpec((1, bsz, tn), lambda l, j: (l, 0, j)),
        out_shape=jax.ShapeDtypeStruct((depth, bsz, n6), F32),
        compiler_params=_cparams(("arbitrary", "arbitrary")),
        name="adaln_mod",
    )(c, w_mod_bf, b_mod.reshape(depth, 1, n6))


def _norm_mod(x, gain, shift, scale):
    y = x * lax.rsqrt(jnp.mean(x * x, axis=-1, keepdims=True) + EPS)
    return (y * gain) * (1.0 + scale) + shift


def _inproj_kernel(x_ref, mod_ref, g_ref, w_ref, hy_ref, q_ref, k_ref, v_ref, ga_ref, gb_ref, *, splits):
    m = mod_ref[0]
    h = _norm_mod(x_ref[...], g_ref[...], m[0:1], m[1:2])
    z = _dot(h.astype(BF16), w_ref[...])
    outs = (hy_ref, q_ref, k_ref, v_ref, ga_ref, gb_ref)
    lo = 0
    for ref, hi in zip(outs, splits):
        ref[...] = z[:, lo:hi]
        lo = hi


def _inproj(x2, mod3, gain, w_in_bf, seq, widths, tm=256):
    t, d = x2.shape
    ncols = w_in_bf.shape[1]
    per = seq // tm
    splits = tuple(int(s) for s in np.cumsum(widths))
    return pl.pallas_call(
        functools.partial(_inproj_kernel, splits=splits),
        grid=(t // tm,),
        in_specs=[pl.BlockSpec((tm, d), lambda i: (i, 0)),
                  pl.BlockSpec((1, 6, d), lambda i: (i // per, 0, 0)),
                  pl.BlockSpec((1, d), lambda i: (0, 0)),
                  pl.BlockSpec((d, ncols), lambda i: (0, 0))],
        out_specs=[pl.BlockSpec((tm, w), lambda i: (i, 0)) for w in widths],
        out_shape=[jax.ShapeDtypeStruct((t, w), F32) for w in widths],
        compiler_params=_cparams(("arbitrary",)),
        name="inproj",
    )(x2, mod3, gain.reshape(1, d), w_in_bf)


def _filter_kernel(z_ref, w1_ref, b1_ref, fr_ref, w2_ref, b2_ref, w3_ref, ad_ref, o_ref):
    z = z_ref[0]
    fr = fr_ref[...]
    a = jnp.sin(fr * (_dot_hi(z, w1_ref[...]) + b1_ref[...]))
    a = jnp.sin(fr * (_dot_hi(a, w2_ref[...]) + b2_ref[...]))
    h = _dot_hi(a, w3_ref[0])
    h = h * jnp.exp(-z[:, 0:1] * ad_ref[...])
    row = lax.broadcasted_iota(jnp.int32, h.shape, 0)
    dead = (pl.program_id(0) == 1) & (pl.program_id(1) == 0) & (row == 0)
    o_ref[0] = jnp.where(dead, 0.0, h)


def _filters_time(seq, f_w1, f_b1, f_freq, f_w2, f_b2, f_w3, tm=512):
    hidden = f_w1.shape[1]
    cw = f_w3.shape[1] // (HY_ORDER * N_DIR)
    t = jnp.linspace(0.0, 1.0, seq, dtype=F32)[:, None]
    w = 2.0 * math.pi * jnp.arange(seq, dtype=F32)[:, None] / seq
    bands = jnp.linspace(1e-4, FILT_BANDS - 1, FILT_BANDS, dtype=F32)[None, :]
    z = jnp.concatenate([t, jnp.cos(bands * w), -jnp.sin(bands * w)], axis=-1)
    emb = z.shape[1]
    z = jnp.pad(z, ((0, 0), (0, LANES - emb)))
    zcat = jnp.stack([z, jnp.concatenate([z[:1], z[:0:-1]], axis=0)], axis=0)
    w1p = jnp.pad(f_w1, ((0, LANES - emb), (0, 0)))
    w3d = f_w3.reshape(hidden, HY_ORDER, N_DIR, cw).transpose(2, 0, 1, 3).reshape(N_DIR, hidden, HY_ORDER * cw)
    max_decay = math.log(DECAY_TARGET) / FAST_DECAY_PCT
    min_decay = math.log(DECAY_TARGET) / SLOW_DECAY_PCT
    ad = jnp.abs(jnp.linspace(min_decay, max_decay, cw, dtype=F32))
    ad = jnp.tile(ad, HY_ORDER)[None, :]
    oc = HY_ORDER * cw
    return pl.pallas_call(
        _filter_kernel,
        grid=(N_DIR, seq // tm),
        in_specs=[pl.BlockSpec((1, tm, LANES), lambda g, r: (g, r, 0)),
                  pl.BlockSpec((LANES, hidden), lambda g, r: (0, 0)),
                  pl.BlockSpec((1, hidden), lambda g, r: (0, 0)),
                  pl.BlockSpec((1, hidden), lambda g, r: (0, 0)),
                  pl.BlockSpec((hidden, hidden), lambda g, r: (0, 0)),
                  pl.BlockSpec((1, hidden), lambda g, r: (0, 0)),
                  pl.BlockSpec((1, hidden, oc), lambda g, r: (g, 0, 0)),
                  pl.BlockSpec((1, oc), lambda g, r: (0, 0))],
        out_specs=pl.BlockSpec((1, tm, oc), lambda g, r: (g, r, 0)),
        out_shape=jax.ShapeDtypeStruct((N_DIR, seq, oc), F32),
        compiler_params=_cparams(("arbitrary", "arbitrary")),
        name="hyena_filter_mlp",
    )(zcat, w1p, f_b1[None, :], f_freq[None, :], f_w2, f_b2[None, :], w3d, ad)


def _dft_tables(seq):
    n2 = 2 * seq
    f = jnp.arange(seq, dtype=jnp.int32)
    ft = (f[:, None] * f[None, :]) % n2
    ang = ft.astype(F32) * (2.0 * math.pi / n2)
    return jnp.cos(ang).astype(BF16), jnp.sin(ang).astype(BF16)


def _spec_kernel(fc_ref, fs_ref, h_ref, a_ref, b_ref, nyq_ref, *, seq, tf):
    hlo = h_ref[0]
    hhi = h_ref[1]
    hlo_b = hlo.astype(BF16)
    hhi_b = hhi.astype(BF16)
    f = pl.program_id(1) * tf + lax.broadcasted_iota(jnp.int32, (tf, 1), 0)
    sgn = jnp.where(f % 2 == 0, 1.0, -1.0)
    fc = fc_ref[...]
    fs = fs_ref[...]
    hr = _dot(fc, hlo_b) + sgn * _dot(fc, hhi_b)
    hs = _dot(fs, hlo_b) + sgn * _dot(fs, hhi_b)
    w = jnp.where(f == 0, 1.0, 2.0) * (1.0 / (2 * seq))
    a_ref[...] = w * hr
    b_ref[...] = -(w * hs)
    t = lax.broadcasted_iota(jnp.int32, (seq, 1), 0)
    alt = jnp.where(t % 2 == 0, 1.0, -1.0)
    nyq = jnp.sum(alt * (hlo + hhi), axis=0, keepdims=True) * (1.0 / (2 * seq))
    nyq_ref[...] = jnp.broadcast_to(nyq, nyq_ref.shape)


def _filter_spectra(fc, fs, hcat, tf=512, tcol=512):
    seq = fc.shape[0]
    oc = hcat.shape[2]
    return pl.pallas_call(
        functools.partial(_spec_kernel, seq=seq, tf=tf),
        grid=(oc // tcol, seq // tf),
        in_specs=[pl.BlockSpec((tf, seq), lambda j, i: (i, 0)),
                  pl.BlockSpec((tf, seq), lambda j, i: (i, 0)),
                  pl.BlockSpec((2, seq, tcol), lambda j, i: (0, 0, j))],
        out_specs=[pl.BlockSpec((tf, tcol), lambda j, i: (i, j)),
                   pl.BlockSpec((tf, tcol), lambda j, i: (i, j)),
                   pl.BlockSpec((SUBLANES, tcol), lambda j, i: (0, j))],
        out_shape=[jax.ShapeDtypeStruct((seq, oc), F32),
                   jax.ShapeDtypeStruct((seq, oc), F32),
                   jax.ShapeDtypeStruct((SUBLANES, oc), F32)],
        compiler_params=_cparams(("arbitrary", "arbitrary")),
        name="hyena_filter_spectra",
    )(fc, fs, hcat)


def _shortconv(x, w_ref, b_ref, seq):
    row = lax.broadcasted_iota(jnp.int32, (seq, 1), 0)
    xm = jnp.where(row == 0, 0.0, pltpu.roll(x, 1, 0))
    xp = jnp.where(row == seq - 1, 0.0, pltpu.roll(x, seq - 1, 0))
    return xm * w_ref[0:1, :] + x * w_ref[1:2, :] + xp * w_ref[2:3, :] + b_ref[...]


def _conv_kernel(fc_ref, fs_ref, u_ref, g_ref, a_ref, b_ref, nyq_ref, bias_ref,
                 cwu_ref, cbu_ref, cwg_ref, cbg_ref, o_ref, acc_ref, *, seq, fb, conv_u):
    u = u_ref[...]
    if conv_u:
        u = _shortconv(u, cwu_ref, cbu_ref, seq)
    gate = _shortconv(g_ref[...], cwg_ref, cbg_ref, seq)
    ub = u.astype(BF16)
    for c in range(seq // fb):
        rows = slice(c * fb, (c + 1) * fb)
        ur = _dot(fc_ref[rows, :], ub)
        us = _dot(fs_ref[rows, :], ub)
        a = a_ref[rows, :]
        b = b_ref[rows, :]
        qr = (ur * a + us * b).astype(BF16)
        qi = (us * a - ur * b).astype(BF16)
        part = _dot(fc_ref[:, rows], qr) + _dot(fs_ref[:, rows], qi)
        if c == 0:
            acc_ref[...] = part
        else:
            acc_ref[...] += part
    t = lax.broadcasted_iota(jnp.int32, (seq, 1), 0)
    alt = jnp.where(t % 2 == 0, 1.0, -1.0)
    unyq = jnp.sum(alt * u, axis=0, keepdims=True)
    y = acc_ref[...] + alt * (unyq * nyq_ref[0:1, :]) + bias_ref[...] * u
    o_ref[...] = gate * y


def _long_conv(fc, fs, u_src, u_blk0, g_src, g_blk0, spec_a, spec_b, nyq, s_blk0, bias, cw, cb,
               cu_blk0, cg_blk0, bsz, seq, conv_u, tc=256, fb=512):
    nct = 512 // tc
    t = bsz * seq
    const = lambda j, b: (0, 0)
    return pl.pallas_call(
        functools.partial(_conv_kernel, seq=seq, fb=fb, conv_u=conv_u),
        grid=(nct, bsz),
        in_specs=[pl.BlockSpec(memory_space=pltpu.VMEM),
                  pl.BlockSpec(memory_space=pltpu.VMEM),
                  pl.BlockSpec((seq, tc), lambda j, b: (b, u_blk0 + j)),
                  pl.BlockSpec((seq, tc), lambda j, b: (b, g_blk0 + j)),
                  pl.BlockSpec((seq, tc), lambda j, b: (0, s_blk0 + j)),
                  pl.BlockSpec((seq, tc), lambda j, b: (0, s_blk0 + j)),
                  pl.BlockSpec((SUBLANES, tc), lambda j, b: (0, s_blk0 + j)),
                  pl.BlockSpec((1, tc), lambda j, b: (0, j)),
                  pl.BlockSpec((3, tc), lambda j, b: (0, cu_blk0 + j)),
                  pl.BlockSpec((1, tc), lambda j, b: (0, cu_blk0 + j)),
                  pl.BlockSpec((3, tc), lambda j, b: (0, cg_blk0 + j)),
                  pl.BlockSpec((1, tc), lambda j, b: (0, cg_blk0 + j))],
        out_specs=pl.BlockSpec((seq, tc), lambda j, b: (b, j)),
        out_shape=jax.ShapeDtypeStruct((t, 512), F32),
        scratch_shapes=[pltpu.VMEM((seq, tc), F32)],
        compiler_params=_cparams(("arbitrary", "arbitrary")),
        name="hyena_long_conv_u" if conv_u else "hyena_long_conv",
    )(fc, fs, u_src, g_src, spec_a, spec_b, nyq, bias, cw, cb, cw, cb)


def _attn_kernel(q_ref, k_ref, v_ref, cos_ref, sin_ref, qg_ref, kg_ref, sink_ref, o_ref,
                 qn_ref, km_ref, vm_ref, *, seq):
    lane = lax.broadcasted_iota(jnp.int32, (1, LANES), 1)
    r = lax.broadcasted_iota(jnp.int32, (LANES, LANES), 0) // HEAD_DIM
    c = lax.broadcasted_iota(jnp.int32, (LANES, LANES), 1) // HEAD_DIM
    bd = jnp.where(r == c, 1.0, 0.0).astype(BF16)
    first_half = (lane % HEAD_DIM) < (HEAD_DIM // 2)
    cos = cos_ref[...]
    sin = sin_ref[...]

    def norm_rope(x, gain):
        sq = x * x
        hi = sq.astype(BF16)
        lo = (sq - hi.astype(F32)).astype(BF16)
        ss = _dot(hi, bd) + _dot(lo, bd)
        y = (x * lax.rsqrt(ss * (1.0 / HEAD_DIM) + EPS)) * gain
        partner = jnp.where(first_half, pltpu.roll(y, LANES - HEAD_DIM // 2, 1), pltpu.roll(y, HEAD_DIM // 2, 1))
        return y * cos + partner * sin

    for p in range(N_Q_HEADS // 2):
        cols = slice(p * LANES, (p + 1) * LANES)
        qn_ref[:, cols] = norm_rope(q_ref[:, cols], qg_ref[...]).astype(BF16)
    kn = norm_rope(k_ref[...], kg_ref[...])
    left = lane < HEAD_DIM
    for src_ref, dst_ref in ((None, km_ref), (v_ref, vm_ref)):
        val = kn if src_ref is None else src_ref[...]
        rolled = pltpu.roll(val, HEAD_DIM, 1)
        dst_ref[0] = jnp.where(left, val, 0.0).astype(BF16)
        dst_ref[1] = jnp.where(left, 0.0, rolled).astype(BF16)
        dst_ref[2] = jnp.where(left, rolled, 0.0).astype(BF16)
        dst_ref[3] = jnp.where(left, 0.0, val).astype(BF16)

    span = 3 * BLOCK
    scale = HEAD_DIM ** -0.5
    ii = lax.broadcasted_iota(jnp.int32, (BLOCK, span), 0)
    jj = lax.broadcasted_iota(jnp.int32, (BLOCK, span), 1)

    def block(n, carry):
        q0 = pl.multiple_of(n * BLOCK, BLOCK)
        start = pl.multiple_of(jnp.clip((n - 1) * BLOCK, 0, seq - span), BLOCK)
        valid = jnp.abs((start - q0) + jj - ii) <= WINDOW
        for p in range(N_Q_HEADS // 2):
            cols = slice(p * LANES, (p + 1) * LANES)
            kv = (2 * p) // (N_Q_HEADS // N_KV_HEADS)
            qp = qn_ref[pl.ds(q0, BLOCK), cols]
            o = jnp.zeros((BLOCK, LANES), F32)
            for a in range(2):
                h = 2 * p + a
                kb = km_ref[2 * kv + a, pl.ds(start, span), :]
                s = _dot_nt(qp, kb) * scale
                s = jnp.where(valid, s, -jnp.inf)
                sk = sink_ref[h:h + 1, 0:1]
                m = jnp.maximum(jnp.max(s, axis=-1, keepdims=True), sk)
                e = jnp.exp(s - m)
                den = jnp.sum(e, axis=-1, keepdims=True) + jnp.exp(sk - m)
                pn = (e / den).astype(BF16)
                o = o + _dot(pn, vm_ref[2 * kv + a, pl.ds(start, span), :])
            o_ref[pl.ds(q0, BLOCK), cols] = o
        return carry

    lax.fori_loop(0, seq // BLOCK, block, 0)


def _attention(q, k, v, q_gain, k_gain, sink, bsz, seq):
    inv = ROPE_THETA ** (-jnp.arange(0, HEAD_DIM, 2, dtype=F32) / HEAD_DIM)
    ang = jnp.arange(seq, dtype=F32)[:, None] * inv[None, :]
    cos = jnp.tile(jnp.cos(ang), (1, LANES // (HEAD_DIM // 2)))
    sn = jnp.sin(ang)
    sin = jnp.tile(jnp.concatenate([-sn, sn], axis=1), (1, LANES // HEAD_DIM))
    qg = jnp.tile(q_gain, LANES // HEAD_DIM)[None, :]
    kg = jnp.tile(k_gain, LANES // HEAD_DIM)[None, :]
    sinkb = jnp.broadcast_to(sink[:, None], (N_Q_HEADS, LANES))
    aw = N_Q_HEADS * HEAD_DIM
    kw = N_KV_HEADS * HEAD_DIM
    const = lambda b: (0, 0)
    return pl.pallas_call(
        functools.partial(_attn_kernel, seq=seq),
        grid=(bsz,),
        in_specs=[pl.BlockSpec((seq, aw), lambda b: (b, 0)),
                  pl.BlockSpec((seq, kw), lambda b: (b, 0)),
                  pl.BlockSpec((seq, kw), lambda b: (b, 0)),
                  pl.BlockSpec((seq, LANES), const),
                  pl.BlockSpec((seq, LANES), const),
                  pl.BlockSpec((1, LANES), const),
                  pl.BlockSpec((1, LANES), const),
                  pl.BlockSpec((N_Q_HEADS, LANES), const)],
        out_specs=pl.BlockSpec((seq, aw), lambda b: (b, 0)),
        out_shape=jax.ShapeDtypeStruct((bsz * seq, aw), F32),
        scratch_shapes=[pltpu.VMEM((seq, aw), BF16),
                        pltpu.VMEM((4, seq, LANES), BF16),
                        pltpu.VMEM((4, seq, LANES), BF16)],
        compiler_params=_cparams(("arbitrary",)),
        name="window_attention",
    )(q, k, v, cos, sin, qg, kg, sinkb)


def _merge_kernel(x_ref, mod_ref, ya_ref, yb_ref, ga_ref, gb_ref, wpa_ref, wpb_ref, wo_ref, o_ref):
    pa = _dot(ya_ref[...].astype(BF16), wpa_ref[...])
    pb = _dot(yb_ref[...].astype(BF16), wpb_ref[...])
    merged = _sigmoid(ga_ref[...]) * pa + _sigmoid(gb_ref[...]) * pb
    out = _dot(merged.astype(BF16), wo_ref[...])
    o_ref[...] = x_ref[...] + mod_ref[0][2:3] * out


def _merge(x2, mod3, ya, yb, ga, gb, w_pa_bf, w_pb_bf, w_out_bf, seq, tm=512):
    t, d = x2.shape
    per = seq // tm
    row = lambda w: pl.BlockSpec((tm, w), lambda i: (i, 0))
    full = lambda a: pl.BlockSpec(a.shape, lambda i: (0, 0))
    return pl.pallas_call(
        _merge_kernel,
        grid=(t // tm,),
        in_specs=[row(d), pl.BlockSpec((1, 6, d), lambda i: (i // per, 0, 0)),
                  row(ya.shape[1]), row(yb.shape[1]), row(d), row(d),
                  full(w_pa_bf), full(w_pb_bf), full(w_out_bf)],
        out_specs=row(d),
        out_shape=jax.ShapeDtypeStruct((t, d), F32),
        compiler_params=_cparams(("arbitrary",)),
        name="merge_outproj",
    )(x2, mod3, ya, yb, ga, gb, w_pa_bf, w_pb_bf, w_out_bf)


def _topk_rows(s, k, val_ref, idx_ref):
    n = s.shape[0]
    row = lax.broadcasted_iota(jnp.int32, s.shape, 0)
    for i in range(k):
        m = jnp.max(s, axis=0, keepdims=True)
        first = jnp.min(jnp.where(s == m, row, n), axis=0, keepdims=True)
        val_ref[i:i + 1, :] = m
        idx_ref[i:i + 1, :] = first
        s = jnp.where(row == first, -jnp.inf, s)


def _peer_score_kernel(x_ref, mod_ref, g_ref, wq_ref, k1_ref, k2_ref, h_ref, idx_ref, gate_ref,
                       v1_ref, i1_ref, v2_ref, i2_ref, cv_ref, ce_ref, it_ref, gt_ref, *, tm):
    m = mod_ref[0]
    h = _norm_mod(x_ref[...], g_ref[...], m[3:4], m[4:5])
    h_ref[...] = h
    q = _dot(h.astype(BF16), wq_ref[...]).astype(BF16)
    half = PEER_QDIM // 2
    neg = jnp.full((1, tm), -jnp.inf, F32)
    for hd in range(PEER_HEADS):
        q1 = q[:, hd * PEER_QDIM: hd * PEER_QDIM + half]
        q2 = q[:, hd * PEER_QDIM + half: (hd + 1) * PEER_QDIM]
        _topk_rows(_dot_nt(k1_ref[...], q1), PEER_TOPK, v1_ref, i1_ref)
        _topk_rows(_dot_nt(k2_ref[...], q2), PEER_TOPK, v2_ref, i2_ref)
        for r, (i, j) in enumerate(_STAIR):
            cv_ref[r:r + 1, :] = v1_ref[i:i + 1, :] + v2_ref[j:j + 1, :]
            ce_ref[r:r + 1, :] = i1_ref[i:i + 1, :] * N_KEYS + i2_ref[j:j + 1, :]
        for r in range(len(_STAIR), _STAIR_ROWS):
            cv_ref[r:r + 1, :] = neg
            ce_ref[r:r + 1, :] = jnp.zeros((1, tm), jnp.int32)
        cand = cv_ref[...]
        ce = ce_ref[...]
        row = lax.broadcasted_iota(jnp.int32, cand.shape, 0)
        vals = []
        for kk in range(PEER_TOPK):
            mx = jnp.max(cand, axis=0, keepdims=True)
            first = jnp.min(jnp.where(cand == mx, row, _STAIR_ROWS), axis=0, keepdims=True)
            sel = row == first
            it_ref[hd * PEER_TOPK + kk: hd * PEER_TOPK + kk + 1, :] = jnp.sum(jnp.where(sel, ce, 0), axis=0, keepdims=True)
            vals.append(mx)
            cand = jnp.where(sel, -jnp.inf, cand)
        ex = [jnp.exp(v - vals[0]) for v in vals]
        tot = ex[0]
        for e in ex[1:]:
            tot = tot + e
        for kk in range(PEER_TOPK):
            gt_ref[hd * PEER_TOPK + kk: hd * PEER_TOPK + kk + 1, :] = ex[kk] / tot
    idx_ref[...] = it_ref[...].T
    gate_ref[...] = gt_ref[...].T


def _peer_score(x2, mod3, gain, wq_bf, k1_bf, k2_bf, seq, tm=256):
    t, d = x2.shape
    per = seq // tm
    slots = PEER_HEADS * PEER_TOPK
    full = lambda a: pl.BlockSpec(a.shape, lambda i: (0, 0))
    return pl.pallas_call(
        functools.partial(_peer_score_kernel, tm=tm),
        grid=(t // tm,),
        in_specs=[pl.BlockSpec((tm, d), lambda i: (i, 0)),
                  pl.BlockSpec((1, 6, d), lambda i: (i // per, 0, 0)),
                  pl.BlockSpec((1, d), lambda i: (0, 0)),
                  full(wq_bf), full(k1_bf), full(k2_bf)],
        out_specs=[pl.BlockSpec((tm, d), lambda i: (i, 0)),
                   pl.BlockSpec((tm, slots), lambda i: (i, 0)),
                   pl.BlockSpec((tm, slots), lambda i: (i, 0))],
        out_shape=[jax.ShapeDtypeStruct((t, d), F32),
                   jax.ShapeDtypeStruct((t, slots), jnp.int32),
                   jax.ShapeDtypeStruct((t, slots), F32)],
        scratch_shapes=[pltpu.VMEM((PEER_TOPK, tm), F32), pltpu.VMEM((PEER_TOPK, tm), jnp.int32),
                        pltpu.VMEM((PEER_TOPK, tm), F32), pltpu.VMEM((PEER_TOPK, tm), jnp.int32),
                        pltpu.VMEM((_STAIR_ROWS, tm), F32), pltpu.VMEM((_STAIR_ROWS, tm), jnp.int32),
                        pltpu.VMEM((slots, tm), jnp.int32), pltpu.VMEM((slots, tm), F32)],
        compiler_params=_cparams(("arbitrary",)),
        name="peer_score_topk",
    )(x2, mod3, gain.reshape(1, d), wq_bf, k1_bf, k2_bf)


def _peer_gather_kernel(idx_ref, idxn_ref, tab_ref, h_ref, gate_ref, x_ref, mod_ref, o_ref,
                        buf_ref, sem_ref, *, tc, d, nsteps):
    slots = PEER_HEADS * PEER_TOPK
    step = pl.program_id(0)

    def issue_token(chunk, t):
        ids_ref, c = (idx_ref, chunk) if chunk < GATHER_RING else (idxn_ref, chunk - GATHER_RING)
        for j in range(slots):
            pltpu.make_async_copy(tab_ref.at[ids_ref[c * tc + t, j]],
                                  buf_ref.at[c, t, pl.ds(j, 1), :],
                                  sem_ref.at[c]).start(priority=j % 2)

    def wait(slot):
        pltpu.make_async_copy(buf_ref.at[slot], buf_ref.at[slot], sem_ref.at[slot]).wait()

    eye = (lax.broadcasted_iota(jnp.int32, (slots, slots), 0) ==
           lax.broadcasted_iota(jnp.int32, (slots, slots), 1))
    gt2 = mod_ref[0][5:6]

    def mix_token(slot, t):
        r = slot * tc + t
        word = buf_ref[slot, t]
        u = lax.bitcast_convert_type(word << 16, F32)
        prod = u * h_ref[r:r + 1, :]
        part = prod[:, 0:LANES]
        for c in range(1, d // LANES):
            part = part + prod[:, c * LANES:(c + 1) * LANES]
        a = jnp.sum(part, axis=1, keepdims=True)
        grow = jnp.broadcast_to(gate_ref[r:r + 1, :], (slots, slots))
        gcol = jnp.sum(jnp.where(eye, grow, 0.0), axis=1, keepdims=True)
        w = gcol * (0.5 * a * (1.0 + lax.erf(a * (2.0 ** -0.5))))
        v = lax.bitcast_convert_type(buf_ref[slot, t] & jnp.uint32(0xFFFF0000), F32)
        y = jnp.sum(v * w, axis=0, keepdims=True)
        o_ref[r:r + 1, :] = x_ref[r:r + 1, :] + gt2 * y

    @pl.when(step == 0)
    def _():
        for c in range(GATHER_AHEAD):
            for t in range(tc):
                issue_token(c, t)

    for p in range(GATHER_RING):
        wait(p)
        for t in range(tc):
            issue_token(p + GATHER_AHEAD, t)
            mix_token(p, t)

    @pl.when(step == nsteps - 1)
    def _():
        for c in range(GATHER_AHEAD):
            wait(c)


def _peer_gather(x2, mod3, h2, idx, gate, table, seq, tc=8):
    t, d = x2.shape
    slots = idx.shape[1]
    tb = GATHER_RING * tc
    nsteps = t // tb
    per = seq // tb
    row = lambda w: pl.BlockSpec((tb, w), lambda i: (i, 0))
    return pl.pallas_call(
        functools.partial(_peer_gather_kernel, tc=tc, d=d, nsteps=nsteps),
        grid=(nsteps,),
        in_specs=[pl.BlockSpec((tb, slots), lambda i: (i, 0), memory_space=pltpu.SMEM),
                  pl.BlockSpec((tb, slots), lambda i: (jnp.minimum(i + 1, nsteps - 1), 0), memory_space=pltpu.SMEM),
                  pl.BlockSpec(memory_space=pl.ANY),
                  row(d), row(slots), row(d),
                  pl.BlockSpec((1, 6, d), lambda i: (i // per, 0, 0))],
        out_specs=row(d),
        out_shape=jax.ShapeDtypeStruct((t, d), F32),
        scratch_shapes=[pltpu.VMEM((GATHER_RING, tc, slots, d), jnp.uint32),
                        pltpu.SemaphoreType.DMA((GATHER_RING,))],
        compiler_params=_cparams(("arbitrary",)),
        name="peer_gather_mix",
    )(idx, idx, table.reshape(table.shape[0], 1, d), h2, gate, x2, mod3)


def _pack_expert_tables(u_tab, v_tab):
    ub = lax.bitcast_convert_type(u_tab.astype(BF16), jnp.uint16).astype(jnp.uint32)
    vb = lax.bitcast_convert_type(v_tab.astype(BF16), jnp.uint16).astype(jnp.uint32)
    return (vb << 16) | ub


def kernel(x_prompt, x_sample, c_prompt, c_sample, w_mod, b_mod, g_norm1, g_norm2, w_in, conv_w, conv_b, f_w1, f_b1, f_freq, f_w2, f_b2, f_w3, f_bias, q_gain, k_gain, sink, w_pa, w_pb, w_out, peer_wq, peer_k1, peer_k2, peer_u, peer_v):
    depth = w_mod.shape[0]
    bp, seq, d = x_prompt.shape
    bs = x_sample.shape[0]
    assert x_sample.shape[1] == seq
    bsz = bp + bs
    x = jnp.concatenate([x_prompt, x_sample], axis=0).reshape(bsz * seq, d)
    c = jnp.concatenate([c_prompt, c_sample], axis=0)

    hyw = w_pa.shape[1]
    aw = w_pb.shape[1]
    kw = N_KV_HEADS * HEAD_DIM
    widths = (HY_ORDER + 1) * hyw, aw, kw, kw, d, d
    cblk = hyw // 256

    mod = _modulation(c, w_mod.astype(BF16), b_mod)
    fc, fs = _dft_tables(seq)

    for l in range(depth):
        mod3 = mod[l].reshape(bsz, 6, d)
        hy, q, k, v, ga, gb = _inproj(x, mod3, g_norm1[l], w_in[l].astype(BF16), seq, widths)
        hcat = _filters_time(seq, f_w1[l], f_b1[l], f_freq[l], f_w2[l], f_b2[l], f_w3[l])
        spec_a, spec_b, nyq = _filter_spectra(fc, fs, hcat)
        cw, cb = conv_w[l], conv_b[l][None, :]
        zz = _long_conv(fc, fs, hy, 0, hy, cblk, spec_a, spec_b, nyq, 0, f_bias[l][0:1], cw, cb,
                        0, cblk, bsz, seq, conv_u=True)
        ya = _long_conv(fc, fs, zz, 0, hy, 2 * cblk, spec_a, spec_b, nyq, cblk, f_bias[l][1:2], cw, cb,
                        0, 2 * cblk, bsz, seq, conv_u=False)
        yb = _attention(q, k, v, q_gain[l], k_gain[l], sink[l], bsz, seq)
        x = _merge(x, mod3, ya, yb, ga, gb, w_pa[l].astype(BF16), w_pb[l].astype(BF16), w_out[l].astype(BF16), seq)
        h2, idx, gate = _peer_score(x, mod3, g_norm2[l], peer_wq[l].astype(BF16),
                                    peer_k1[l].astype(BF16), peer_k2[l].astype(BF16), seq)
        table = _pack_expert_tables(peer_u[l], peer_v[l])
        x = _peer_gather(x, mod3, h2, idx, gate, table, seq)

    x = x.reshape(bsz, seq, d)
    return x[:bp], x[bp:]
```

```python
import functools
import math

import jax
import jax.numpy as jnp
import numpy as np
from jax import lax
from jax.experimental import pallas as pl
from jax.experimental.pallas import tpu as pltpu
from jax.experimental.pallas import tpu_sc as plsc

F32 = jnp.float32
BF16 = jnp.bfloat16

EPS = 1e-6
HEAD_DIM = 64
N_Q_HEADS = 8
N_KV_HEADS = 2
WINDOW = 128
BLOCK = 128
ROPE_THETA = 10000.0
HY_ORDER = 2
N_DIR = 2
FILT_BANDS = 16
DECAY_TARGET = 1e-2
FAST_DECAY_PCT = 0.3
SLOW_DECAY_PCT = 1.5
PEER_HEADS = 8
N_KEYS = 128
PEER_TOPK = 16
PEER_QDIM = 256
LANES = 128
SUBLANES = 8
VMEM_LIMIT = 56 * 1024 * 1024
GATHER_RING = 4
GATHER_AHEAD = 2

_STAIR = [(i, j) for i in range(PEER_TOPK) for j in range(PEER_TOPK) if (i + 1) * (j + 1) <= PEER_TOPK]
_STAIR_ROWS = -(-len(_STAIR) // SUBLANES) * SUBLANES


def _cparams(sem, vmem=VMEM_LIMIT):
    return pltpu.CompilerParams(dimension_semantics=sem, vmem_limit_bytes=vmem)


def _dot(a, b):
    return jnp.dot(a, b, preferred_element_type=F32)


def _dot_nt(a, b):
    return lax.dot_general(a, b, (((1,), (1,)), ((), ())), preferred_element_type=F32)


def _dot_hi(a, b):
    return jnp.dot(a, b, preferred_element_type=F32, precision=lax.Precision.HIGHEST)


def _sigmoid(x):
    return 1.0 / (1.0 + jnp.exp(-x))


def _mod_kernel(c_ref, w_ref, b_ref, o_ref):
    c = c_ref[...]
    s = c * _sigmoid(c)
    o_ref[0] = _dot(s.astype(BF16), w_ref[0]) + b_ref[0]


def _modulation(c, w_mod_bf, b_mod):
    depth, d, n6 = w_mod_bf.shape
    bsz = c.shape[0]
    tn = 1536
    return pl.pallas_call(
        _mod_kernel,
        grid=(depth, n6 // tn),
        in_specs=[pl.BlockSpec((bsz, d), lambda l, j: (0, 0)),
                  pl.BlockSpec((1, d, tn), lambda l, j: (l, 0, j)),
                  pl.BlockSpec((1, 1, tn), lambda l, j: (l, 0, j))],
        out_specs=pl.BlockSpec((1, bsz, tn), lambda l, j: (l, 0, j)),
        out_shape=jax.ShapeDtypeStruct((depth, bsz, n6), F32),
        compiler_params=_cparams(("arbitrary", "arbitrary")),
        name="adaln_mod",
    )(c, w_mod_bf, b_mod.reshape(depth, 1, n6))


def _norm_mod(x, gain, shift, scale):
    y = x * lax.rsqrt(jnp.mean(x * x, axis=-1, keepdims=True) + EPS)
    return (y * gain) * (1.0 + scale) + shift


def _inproj_kernel(x_ref, mod_ref, g_ref, w_ref, hy_ref, q_ref, k_ref, v_ref, ga_ref, gb_ref, *, splits):
    m = mod_ref[0]
    h = _norm_mod(x_ref[...], g_ref[...], m[0:1], m[1:2])
    z = _dot(h.astype(BF16), w_ref[...])
    outs = (hy_ref, q_ref, k_ref, v_ref, ga_ref, gb_ref)
    lo = 0
    for ref, hi in zip(outs, splits):
        ref[...] = z[:, lo:hi]
        lo = hi


def _inproj(x2, mod3, gain, w_in_bf, seq, widths, tm=256):
    t, d = x2.shape
    ncols = w_in_bf.shape[1]
    per = seq // tm
    splits = tuple(int(s) for s in np.cumsum(widths))
    return pl.pallas_call(
        functools.partial(_inproj_kernel, splits=splits),
        grid=(t // tm,),
        in_specs=[pl.BlockSpec((tm, d), lambda i: (i, 0)),
                  pl.BlockSpec((1, 6, d), lambda i: (i // per, 0, 0)),
                  pl.BlockSpec((1, d), lambda i: (0, 0)),
                  pl.BlockSpec((d, ncols), lambda i: (0, 0))],
        out_specs=[pl.BlockSpec((tm, w), lambda i: (i, 0)) for w in widths],
        out_shape=[jax.ShapeDtypeStruct((t, w), F32) for w in widths],
        compiler_params=_cparams(("arbitrary",)),
        name="inproj",
    )(x2, mod3, gain.reshape(1, d), w_in_bf)


def _filter_kernel(z_ref, w1_ref, b1_ref, fr_ref, w2_ref, b2_ref, w3_ref, ad_ref, o_ref):
    z = z_ref[0]
    fr = fr_ref[...]
    a = jnp.sin(fr * (_dot_hi(z, w1_ref[...]) + b1_ref[...]))
    a = jnp.sin(fr * (_dot_hi(a, w2_ref[...]) + b2_ref[...]))
    h = _dot_hi(a, w3_ref[0])
    h = h * jnp.exp(-z[:, 0:1] * ad_ref[...])
    row = lax.broadcasted_iota(jnp.int32, h.shape, 0)
    dead = (pl.program_id(0) == 1) & (pl.program_id(1) == 0) & (row == 0)
    o_ref[0] = jnp.where(dead, 0.0, h)


def _filters_time(seq, f_w1, f_b1, f_freq, f_w2, f_b2, f_w3, tm=512):
    hidden = f_w1.shape[1]
    cw = f_w3.shape[1] // (HY_ORDER * N_DIR)
    t = jnp.linspace(0.0, 1.0, seq, dtype=F32)[:, None]
    w = 2.0 * math.pi * jnp.arange(seq, dtype=F32)[:, None] / seq
    bands = jnp.linspace(1e-4, FILT_BANDS - 1, FILT_BANDS, dtype=F32)[None, :]
    z = jnp.concatenate([t, jnp.cos(bands * w), -jnp.sin(bands * w)], axis=-1)
    emb = z.shape[1]
    z = jnp.pad(z, ((0, 0), (0, LANES - emb)))
    zcat = jnp.stack([z, jnp.concatenate([z[:1], z[:0:-1]], axis=0)], axis=0)
    w1p = jnp.pad(f_w1, ((0, LANES - emb), (0, 0)))
    w3d = f_w3.reshape(hidden, HY_ORDER, N_DIR, cw).transpose(2, 0, 1, 3).reshape(N_DIR, hidden, HY_ORDER * cw)
    max_decay = math.log(DECAY_TARGET) / FAST_DECAY_PCT
    min_decay = math.log(DECAY_TARGET) / SLOW_DECAY_PCT
    ad = jnp.abs(jnp.linspace(min_decay, max_decay, cw, dtype=F32))
    ad = jnp.tile(ad, HY_ORDER)[None, :]
    oc = HY_ORDER * cw
    return pl.pallas_call(
        _filter_kernel,
        grid=(N_DIR, seq // tm),
        in_specs=[pl.BlockSpec((1, tm, LANES), lambda g, r: (g, r, 0)),
                  pl.BlockSpec((LANES, hidden), lambda g, r: (0, 0)),
                  pl.BlockSpec((1, hidden), lambda g, r: (0, 0)),
                  pl.BlockSpec((1, hidden), lambda g, r: (0, 0)),
                  pl.BlockSpec((hidden, hidden), lambda g, r: (0, 0)),
                  pl.BlockSpec((1, hidden), lambda g, r: (0, 0)),
                  pl.BlockSpec((1, hidden, oc), lambda g, r: (g, 0, 0)),
                  pl.BlockSpec((1, oc), lambda g, r: (0, 0))],
        out_specs=pl.BlockSpec((1, tm, oc), lambda g, r: (g, r, 0)),
        out_shape=jax.ShapeDtypeStruct((N_DIR, seq, oc), F32),
        compiler_params=_cparams(("arbitrary", "arbitrary")),
        name="hyena_filter_mlp",
    )(zcat, w1p, f_b1[None, :], f_freq[None, :], f_w2, f_b2[None, :], w3d, ad)


def _dft_tables(seq):
    n2 = 2 * seq
    f = jnp.arange(seq, dtype=jnp.int32)
    ft = (f[:, None] * f[None, :]) % n2
    ang = ft.astype(F32) * (2.0 * math.pi / n2)
    return jnp.cos(ang).astype(BF16), jnp.sin(ang).astype(BF16)


def _spec_kernel(fc_ref, fs_ref, h_ref, a_ref, b_ref, nyq_ref, *, seq, tf):
    hlo = h_ref[0]
    hhi = h_ref[1]
    hlo_b = hlo.astype(BF16)
    hhi_b = hhi.astype(BF16)
    f = pl.program_id(1) * tf + lax.broadcasted_iota(jnp.int32, (tf, 1), 0)
    sgn = jnp.where(f % 2 == 0, 1.0, -1.0)
    fc = fc_ref[...]
    fs = fs_ref[...]
    hr = _dot(fc, hlo_b) + sgn * _dot(fc, hhi_b)
    hs = _dot(fs, hlo_b) + sgn * _dot(fs, hhi_b)
    w = jnp.where(f == 0, 1.0, 2.0) * (1.0 / (2 * seq))
    a_ref[...] = w * hr
    b_ref[...] = -(w * hs)
    t = lax.broadcasted_iota(jnp.int32, (seq, 1), 0)
    alt = jnp.where(t % 2 == 0, 1.0, -1.0)
    nyq = jnp.sum(alt * (hlo + hhi), axis=0, keepdims=True) * (1.0 / (2 * seq))
    nyq_ref[...] = jnp.broadcast_to(nyq, nyq_ref.shape)


def _filter_spectra(fc, fs, hcat, tf=512, tcol=512):
    seq = fc.shape[0]
    oc = hcat.shape[2]
    return pl.pallas_call(
        functools.partial(_spec_kernel, seq=seq, tf=tf),
        grid=(oc // tcol, seq // tf),
        in_specs=[pl.BlockSpec((tf, seq), lambda j, i: (i, 0)),
                  pl.BlockSpec((tf, seq), lambda j, i: (i, 0)),
                  pl.BlockSpec((2, seq, tcol), lambda j, i: (0, 0, j))],
        out_specs=[pl.BlockSpec((tf, tcol), lambda j, i: (i, j)),
                   pl.BlockSpec((tf, tcol), lambda j, i: (i, j)),
                   pl.BlockSpec((SUBLANES, tcol), lambda j, i: (0, j))],
        out_shape=[jax.ShapeDtypeStruct((seq, oc), F32),
                   jax.ShapeDtypeStruct((seq, oc), F32),
                   jax.ShapeDtypeStruct((SUBLANES, oc), F32)],
        compiler_params=_cparams(("arbitrary", "arbitrary")),
        name="hyena_filter_spectra",
    )(fc, fs, hcat)


def _shortconv(x, w_ref, b_ref, seq):
    row = lax.broadcasted_iota(jnp.int32, (seq, 1), 0)
    xm = jnp.where(row == 0, 0.0, pltpu.roll(x, 1, 0))
    xp = jnp.where(row == seq - 1, 0.0, pltpu.roll(x, seq - 1, 0))
    return xm * w_ref[0:1, :] + x * w_ref[1:2, :] + xp * w_ref[2:3, :] + b_ref[...]


def _conv_kernel(fc_ref, fs_ref, u_ref, g_ref, a_ref, b_ref, nyq_ref, bias_ref,
                 cwu_ref, cbu_ref, cwg_ref, cbg_ref, o_ref, acc_ref, *, seq, fb, conv_u):
    u = u_ref[...]
    if conv_u:
        u = _shortconv(u, cwu_ref, cbu_ref, seq)
    gate = _shortconv(g_ref[...], cwg_ref, cbg_ref, seq)
    ub = u.astype(BF16)
    for c in range(seq // fb):
        rows = slice(c * fb, (c + 1) * fb)
        ur = _dot(fc_ref[rows, :], ub)
        us = _dot(fs_ref[rows, :], ub)
        a = a_ref[rows, :]
        b = b_ref[rows, :]
        qr = (ur * a + us * b).astype(BF16)
        qi = (us * a - ur * b).astype(BF16)
        part = _dot(fc_ref[:, rows], qr) + _dot(fs_ref[:, rows], qi)
        if c == 0:
            acc_ref[...] = part
        else:
            acc_ref[...] += part
    t = lax.broadcasted_iota(jnp.int32, (seq, 1), 0)
    alt = jnp.where(t % 2 == 0, 1.0, -1.0)
    unyq = jnp.sum(alt * u, axis=0, keepdims=True)
    y = acc_ref[...] + alt * (unyq * nyq_ref[0:1, :]) + bias_ref[...] * u
    o_ref[...] = gate * y


def _long_conv(fc, fs, u_src, u_blk0, g_src, g_blk0, spec_a, spec_b, nyq, s_blk0, bias, cw, cb,
               cu_blk0, cg_blk0, bsz, seq, conv_u, tc=256, fb=512):
    nct = 512 // tc
    t = bsz * seq
    const = lambda j, b: (0, 0)
    return pl.pallas_call(
        functools.partial(_conv_kernel, seq=seq, fb=fb, conv_u=conv_u),
        grid=(nct, bsz),
        in_specs=[pl.BlockSpec(memory_space=pltpu.VMEM),
                  pl.BlockSpec(memory_space=pltpu.VMEM),
                  pl.BlockSpec((seq, tc), lambda j, b: (b, u_blk0 + j)),
                  pl.BlockSpec((seq, tc), lambda j, b: (b, g_blk0 + j)),
                  pl.BlockSpec((seq, tc), lambda j, b: (0, s_blk0 + j)),
                  pl.BlockSpec((seq, tc), lambda j, b: (0, s_blk0 + j)),
                  pl.BlockSpec((SUBLANES, tc), lambda j, b: (0, s_blk0 + j)),
                  pl.BlockSpec((1, tc), lambda j, b: (0, j)),
                  pl.BlockSpec((3, tc), lambda j, b: (0, cu_blk0 + j)),
                  pl.BlockSpec((1, tc), lambda j, b: (0, cu_blk0 + j)),
                  pl.BlockSpec((3, tc), lambda j, b: (0, cg_blk0 + j)),
                  pl.BlockSpec((1, tc), lambda j, b: (0, cg_blk0 + j))],
        out_specs=pl.BlockSpec((seq, tc), lambda j, b: (b, j)),
        out_shape=jax.ShapeDtypeStruct((t, 512), F32),
        scratch_shapes=[pltpu.VMEM((seq, tc), F32)],
        compiler_params=_cparams(("arbitrary", "arbitrary")),
        name="hyena_long_conv_u" if conv_u else "hyena_long_conv",
    )(fc, fs, u_src, g_src, spec_a, spec_b, nyq, bias, cw, cb, cw, cb)


def _attn_kernel(q_ref, k_ref, v_ref, cos_ref, sin_ref, qg_ref, kg_ref, sink_ref, o_ref,
                 qn_ref, km_ref, vm_ref, *, seq):
    lane = lax.broadcasted_iota(jnp.int32, (1, LANES), 1)
    r = lax.broadcasted_iota(jnp.int32, (LANES, LANES), 0) // HEAD_DIM
    c = lax.broadcasted_iota(jnp.int32, (LANES, LANES), 1) // HEAD_DIM
    bd = jnp.where(r == c, 1.0, 0.0).astype(BF16)
    first_half = (lane % HEAD_DIM) < (HEAD_DIM // 2)
    cos = cos_ref[...]
    sin = sin_ref[...]

    def norm_rope(x, gain):
        sq = x * x
        hi = sq.astype(BF16)
        lo = (sq - hi.astype(F32)).astype(BF16)
        ss = _dot(hi, bd) + _dot(lo, bd)
        y = (x * lax.rsqrt(ss * (1.0 / HEAD_DIM) + EPS)) * gain
        partner = jnp.where(first_half, pltpu.roll(y, LANES - HEAD_DIM // 2, 1), pltpu.roll(y, HEAD_DIM // 2, 1))
        return y * cos + partner * sin

    for p in range(N_Q_HEADS // 2):
        cols = slice(p * LANES, (p + 1) * LANES)
        qn_ref[:, cols] = norm_rope(q_ref[:, cols], qg_ref[...]).astype(BF16)
    kn = norm_rope(k_ref[...], kg_ref[...])
    left = lane < HEAD_DIM
    for src_ref, dst_ref in ((None, km_ref), (v_ref, vm_ref)):
        val = kn if src_ref is None else src_ref[...]
        rolled = pltpu.roll(val, HEAD_DIM, 1)
        dst_ref[0] = jnp.where(left, val, 0.0).astype(BF16)
        dst_ref[1] = jnp.where(left, 0.0, rolled).astype(BF16)
        dst_ref[2] = jnp.where(left, rolled, 0.0).astype(BF16)
        dst_ref[3] = jnp.where(left, 0.0, val).astype(BF16)

    span = 3 * BLOCK
    scale = HEAD_DIM ** -0.5
    ii = lax.broadcasted_iota(jnp.int32, (BLOCK, span), 0)
    jj = lax.broadcasted_iota(jnp.int32, (BLOCK, span), 1)

    def block(n, carry):
        q0 = pl.multiple_of(n * BLOCK, BLOCK)
        start = pl.multiple_of(jnp.clip((n - 1) * BLOCK, 0, seq - span), BLOCK)
        valid = jnp.abs((start - q0) + jj - ii) <= WINDOW
        for p in range(N_Q_HEADS // 2):
            cols = slice(p * LANES, (p + 1) * LANES)
            kv = (2 * p) // (N_Q_HEADS // N_KV_HEADS)
            qp = qn_ref[pl.ds(q0, BLOCK), cols]
            o = jnp.zeros((BLOCK, LANES), F32)
            for a in range(2):
                h = 2 * p + a
                kb = km_ref[2 * kv + a, pl.ds(start, span), :]
                s = _dot_nt(qp, kb) * scale
                s = jnp.where(valid, s, -jnp.inf)
                sk = sink_ref[h:h + 1, 0:1]
                m = jnp.maximum(jnp.max(s, axis=-1, keepdims=True), sk)
                e = jnp.exp(s - m)
                den = jnp.sum(e, axis=-1, keepdims=True) + jnp.exp(sk - m)
                pn = (e / den).astype(BF16)
                o = o + _dot(pn, vm_ref[2 * kv + a, pl.ds(start, span), :])
            o_ref[pl.ds(q0, BLOCK), cols] = o
        return carry

    lax.fori_loop(0, seq // BLOCK, block, 0)


def _attention(q, k, v, q_gain, k_gain, sink, bsz, seq):
    inv = ROPE_THETA ** (-jnp.arange(0, HEAD_DIM, 2, dtype=F32) / HEAD_DIM)
    ang = jnp.arange(seq, dtype=F32)[:, None] * inv[None, :]
    cos = jnp.tile(jnp.cos(ang), (1, LANES // (HEAD_DIM // 2)))
    sn = jnp.sin(ang)
    sin = jnp.tile(jnp.concatenate([-sn, sn], axis=1), (1, LANES // HEAD_DIM))
    qg = jnp.tile(q_gain, LANES // HEAD_DIM)[None, :]
    kg = jnp.tile(k_gain, LANES // HEAD_DIM)[None, :]
    sinkb = jnp.broadcast_to(sink[:, None], (N_Q_HEADS, LANES))
    aw = N_Q_HEADS * HEAD_DIM
    kw = N_KV_HEADS * HEAD_DIM
    const = lambda b: (0, 0)
    return pl.pallas_call(
        functools.partial(_attn_kernel, seq=seq),
        grid=(bsz,),
        in_specs=[pl.BlockSpec((seq, aw), lambda b: (b, 0)),
                  pl.BlockSpec((seq, kw), lambda b: (b, 0)),
                  pl.BlockSpec((seq, kw), lambda b: (b, 0)),
                  pl.BlockSpec((seq, LANES), const),
                  pl.BlockSpec((seq, LANES), const),
                  pl.BlockSpec((1, LANES), const),
                  pl.BlockSpec((1, LANES), const),
                  pl.BlockSpec((N_Q_HEADS, LANES), const)],
        out_specs=pl.BlockSpec((seq, aw), lambda b: (b, 0)),
        out_shape=jax.ShapeDtypeStruct((bsz * seq, aw), F32),
        scratch_shapes=[pltpu.VMEM((seq, aw), BF16),
                        pltpu.VMEM((4, seq, LANES), BF16),
                        pltpu.VMEM((4, seq, LANES), BF16)],
        compiler_params=_cparams(("arbitrary",)),
        name="window_attention",
    )(q, k, v, cos, sin, qg, kg, sinkb)


def _merge_kernel(x_ref, mod_ref, ya_ref, yb_ref, ga_ref, gb_ref, wpa_ref, wpb_ref, wo_ref, o_ref):
    pa = _dot(ya_ref[...].astype(BF16), wpa_ref[...])
    pb = _dot(yb_ref[...].astype(BF16), wpb_ref[...])
    merged = _sigmoid(ga_ref[...]) * pa + _sigmoid(gb_ref[...]) * pb
    out = _dot(merged.astype(BF16), wo_ref[...])
    o_ref[...] = x_ref[...] + mod_ref[0][2:3] * out


def _merge(x2, mod3, ya, yb, ga, gb, w_pa_bf, w_pb_bf, w_out_bf, seq, tm=512):
    t, d = x2.shape
    per = seq // tm
    row = lambda w: pl.BlockSpec((tm, w), lambda i: (i, 0))
    full = lambda a: pl.BlockSpec(a.shape, lambda i: (0, 0))
    return pl.pallas_call(
        _merge_kernel,
        grid=(t // tm,),
        in_specs=[row(d), pl.BlockSpec((1, 6, d), lambda i: (i // per, 0, 0)),
                  row(ya.shape[1]), row(yb.shape[1]), row(d), row(d),
                  full(w_pa_bf), full(w_pb_bf), full(w_out_bf)],
        out_specs=row(d),
        out_shape=jax.ShapeDtypeStruct((t, d), F32),
        compiler_params=_cparams(("arbitrary",)),
        name="merge_outproj",
    )(x2, mod3, ya, yb, ga, gb, w_pa_bf, w_pb_bf, w_out_bf)


def _topk_rows(s, k, val_ref, idx_ref):
    n = s.shape[0]
    row = lax.broadcasted_iota(jnp.int32, s.shape, 0)
    for i in range(k):
        m = jnp.max(s, axis=0, keepdims=True)
        first = jnp.min(jnp.where(s == m, row, n), axis=0, keepdims=True)
        val_ref[i:i + 1, :] = m
        idx_ref[i:i + 1, :] = first
        s = jnp.where(row == first, -jnp.inf, s)


def _peer_score_kernel(x_ref, mod_ref, g_ref, wq_ref, k1_ref, k2_ref, h_ref, idx_ref, gate_ref,
                       v1_ref, i1_ref, v2_ref, i2_ref, cv_ref, ce_ref, it_ref, gt_ref, *, tm):
    m = mod_ref[0]
    h = _norm_mod(x_ref[...], g_ref[...], m[3:4], m[4:5])
    h_ref[...] = h
    q = _dot(h.astype(BF16), wq_ref[...]).astype(BF16)
    half = PEER_QDIM // 2
    neg = jnp.full((1, tm), -jnp.inf, F32)
    for hd in range(PEER_HEADS):
        q1 = q[:, hd * PEER_QDIM: hd * PEER_QDIM + half]
        q2 = q[:, hd * PEER_QDIM + half: (hd + 1) * PEER_QDIM]
        _topk_rows(_dot_nt(k1_ref[...], q1), PEER_TOPK, v1_ref, i1_ref)
        _topk_rows(_dot_nt(k2_ref[...], q2), PEER_TOPK, v2_ref, i2_ref)
        for r, (i, j) in enumerate(_STAIR):
            cv_ref[r:r + 1, :] = v1_ref[i:i + 1, :] + v2_ref[j:j + 1, :]
            ce_ref[r:r + 1, :] = i1_ref[i:i + 1, :] * N_KEYS + i2_ref[j:j + 1, :]
        for r in range(len(_STAIR), _STAIR_ROWS):
            cv_ref[r:r + 1, :] = neg
            ce_ref[r:r + 1, :] = jnp.zeros((1, tm), jnp.int32)
        cand = cv_ref[...]
        ce = ce_ref[...]
        row = lax.broadcasted_iota(jnp.int32, cand.shape, 0)
        vals = []
        for kk in range(PEER_TOPK):
            mx = jnp.max(cand, axis=0, keepdims=True)
            first = jnp.min(jnp.where(cand == mx, row, _STAIR_ROWS), axis=0, keepdims=True)
            sel = row == first
            it_ref[hd * PEER_TOPK + kk: hd * PEER_TOPK + kk + 1, :] = jnp.sum(jnp.where(sel, ce, 0), axis=0, keepdims=True)
            vals.append(mx)
            cand = jnp.where(sel, -jnp.inf, cand)
        ex = [jnp.exp(v - vals[0]) for v in vals]
        tot = ex[0]
        for e in ex[1:]:
            tot = tot + e
        for kk in range(PEER_TOPK):
            gt_ref[hd * PEER_TOPK + kk: hd * PEER_TOPK + kk + 1, :] = ex[kk] / tot
    idx_ref[...] = it_ref[...].T
    gate_ref[...] = gt_ref[...].T


def _peer_score(x2, mod3, gain, wq_bf, k1_bf, k2_bf, seq, tm=256):
    t, d = x2.shape
    per = seq // tm
    slots = PEER_HEADS * PEER_TOPK
    full = lambda a: pl.BlockSpec(a.shape, lambda i: (0, 0))
    return pl.pallas_call(
        functools.partial(_peer_score_kernel, tm=tm),
        grid=(t // tm,),
        in_specs=[pl.BlockSpec((tm, d), lambda i: (i, 0)),
                  pl.BlockSpec((1, 6, d), lambda i: (i // per, 0, 0)),
                  pl.BlockSpec((1, d), lambda i: (0, 0)),
                  full(wq_bf), full(k1_bf), full(k2_bf)],
        out_specs=[pl.BlockSpec((tm, d), lambda i: (i, 0)),
                   pl.BlockSpec((tm, slots), lambda i: (i, 0)),
                   pl.BlockSpec((tm, slots), lambda i: (i, 0))],
        out_shape=[jax.ShapeDtypeStruct((t, d), F32),
                   jax.ShapeDtypeStruct((t, slots), jnp.int32),
                   jax.ShapeDtypeStruct((t, slots), F32)],
        scratch_shapes=[pltpu.VMEM((PEER_TOPK, tm), F32), pltpu.VMEM((PEER_TOPK, tm), jnp.int32),
                        pltpu.VMEM((PEER_TOPK, tm), F32), pltpu.VMEM((PEER_TOPK, tm), jnp.int32),
                        pltpu.VMEM((_STAIR_ROWS, tm), F32), pltpu.VMEM((_STAIR_ROWS, tm), jnp.int32),
                        pltpu.VMEM((slots, tm), jnp.int32), pltpu.VMEM((slots, tm), F32)],
        compiler_params=_cparams(("arbitrary",)),
        name="peer_score_topk",
    )(x2, mod3, gain.reshape(1, d), wq_bf, k1_bf, k2_bf)


def _peer_gather_kernel(idx_ref, idxn_ref, tab_ref, h_ref, gate_ref, x_ref, mod_ref, o_ref,
                        buf_ref, sem_ref, *, tc, d, nsteps):
    slots = PEER_HEADS * PEER_TOPK
    step = pl.program_id(0)

    def issue_token(chunk, t):
        ids_ref, c = (idx_ref, chunk) if chunk < GATHER_RING else (idxn_ref, chunk - GATHER_RING)
        for j in range(slots):
            pltpu.make_async_copy(tab_ref.at[ids_ref[c * tc + t, j]],
                                  buf_ref.at[c, t, pl.ds(j, 1), :],
                                  sem_ref.at[c]).start(priority=j % 2)

    def wait(slot):
        pltpu.make_async_copy(buf_ref.at[slot], buf_ref.at[slot], sem_ref.at[slot]).wait()

    eye = (lax.broadcasted_iota(jnp.int32, (slots, slots), 0) ==
           lax.broadcasted_iota(jnp.int32, (slots, slots), 1))
    gt2 = mod_ref[0][5:6]

    def mix_token(slot, t):
        r = slot * tc + t
        word = buf_ref[slot, t]
        u = lax.bitcast_convert_type(word << 16, F32)
        prod = u * h_ref[r:r + 1, :]
        part = prod[:, 0:LANES]
        for c in range(1, d // LANES):
            part = part + prod[:, c * LANES:(c + 1) * LANES]
        a = jnp.sum(part, axis=1, keepdims=True)
        grow = jnp.broadcast_to(gate_ref[r:r + 1, :], (slots, slots))
        gcol = jnp.sum(jnp.where(eye, grow, 0.0), axis=1, keepdims=True)
        w = gcol * (0.5 * a * (1.0 + lax.erf(a * (2.0 ** -0.5))))
        v = lax.bitcast_convert_type(buf_ref[slot, t] & jnp.uint32(0xFFFF0000), F32)
        y = jnp.sum(v * w, axis=0, keepdims=True)
        o_ref[r:r + 1, :] = x_ref[r:r + 1, :] + gt2 * y

    @pl.when(step == 0)
    def _():
        for c in range(GATHER_AHEAD):
            for t in range(tc):
                issue_token(c, t)

    for p in range(GATHER_RING):
        wait(p)
        for t in range(tc):
            issue_token(p + GATHER_AHEAD, t)
            mix_token(p, t)

    @pl.when(step == nsteps - 1)
    def _():
        for c in range(GATHER_AHEAD):
            wait(c)


def _peer_gather(x2, mod3, h2, idx, gate, table, seq, ntok, tc=8):
    t, d = x2.shape
    slots = idx.shape[1]
    tb = GATHER_RING * tc
    t = ntok
    nsteps = t // tb
    per = seq // tb
    row = lambda w: pl.BlockSpec((tb, w), lambda i: (i, 0))
    return pl.pallas_call(
        functools.partial(_peer_gather_kernel, tc=tc, d=d, nsteps=nsteps),
        grid=(nsteps,),
        in_specs=[pl.BlockSpec((tb, slots), lambda i: (i, 0), memory_space=pltpu.SMEM),
                  pl.BlockSpec((tb, slots), lambda i: (jnp.minimum(i + 1, nsteps - 1), 0), memory_space=pltpu.SMEM),
                  pl.BlockSpec(memory_space=pl.ANY),
                  row(d), row(slots), row(d),
                  pl.BlockSpec((1, 6, d), lambda i: (i // per, 0, 0))],
        out_specs=row(d),
        out_shape=jax.ShapeDtypeStruct((t, d), F32),
        scratch_shapes=[pltpu.VMEM((GATHER_RING, tc, slots, d), jnp.uint32),
                        pltpu.SemaphoreType.DMA((GATHER_RING,))],
        compiler_params=_cparams(("arbitrary",)),
        name="peer_gather_mix",
    )(idx, idx, table.reshape(table.shape[0], 1, d), h2, gate, x2, mod3)


def _pack_expert_tables(u_tab, v_tab):
    ub = lax.bitcast_convert_type(u_tab.astype(BF16), jnp.uint16).astype(jnp.uint32)
    vb = lax.bitcast_convert_type(v_tab.astype(BF16), jnp.uint16).astype(jnp.uint32)
    return (vb << 16) | ub


SC_LANES = 16
SC_CORES = 2
SC_SUBCORES = 16
SC_GROUP = 8
SC_HALF = 64
SC_SEQS = 8


def _pack_pairs(tab):
    e, d = tab.shape
    bits = lax.bitcast_convert_type(tab.astype(BF16), jnp.uint16).astype(jnp.uint32)
    bits = bits.reshape(e, d // (2 * SC_LANES), 2, SC_LANES)
    return ((bits[:, :, 1, :] << 16) | bits[:, :, 0, :]).reshape(e, d // 2)


def _sc_worker_base(n_per):
    return (lax.axis_index("s") * SC_CORES + lax.axis_index("c")) * n_per


def _sc_halves(tab_hbm, idx_v, rows_v, sems, tt, half):
    return pltpu.make_async_copy(tab_hbm.at[idx_v.at[tt, half]], rows_v.at[half], sems.at[half])


def _sc_token_loop(tab_hbm, idx_v, rows_v, sems, compute_half):
    _sc_halves(tab_hbm, idx_v, rows_v, sems, 0, 0).start()

    def tok(tt, carry):
        _sc_halves(tab_hbm, idx_v, rows_v, sems, tt, 1).start()
        _sc_halves(tab_hbm, idx_v, rows_v, sems, tt, 0).wait()
        compute_half(tt, 0)

        @pl.when(tt + 1 < SC_GROUP)
        def _():
            _sc_halves(tab_hbm, idx_v, rows_v, sems, tt + 1, 0).start()

        _sc_halves(tab_hbm, idx_v, rows_v, sems, tt, 1).wait()
        compute_half(tt, 1)
        return carry

    lax.fori_loop(0, SC_GROUP, tok, 0)


def _unpack_pair(word):
    lo = lax.bitcast_convert_type(word << 16, F32)
    hi = lax.bitcast_convert_type(word & jnp.uint32(0xFFFF0000), F32)
    return lo, hi


def _sc_dot_body(tab_hbm, idx_hbm, h_hbm, a_hbm, idx_v, h_v, rows_v, a_v, sems, *, tok0, n_per, d):
    base = _sc_worker_base(n_per)
    lane = lax.iota(jnp.int32, SC_LANES)
    nk = d // (2 * SC_LANES)

    def compute_half(tt, half):
        def rows16(gi, carry):
            avec = jnp.zeros((SC_LANES,), F32)
            for rr in range(0, SC_LANES, 4):
                accs = [jnp.zeros((SC_LANES,), F32) for _ in range(4)]
                for k in range(nk):
                    hlo = h_v[tt, pl.ds(2 * SC_LANES * k, SC_LANES)]
                    hhi = h_v[tt, pl.ds(2 * SC_LANES * k + SC_LANES, SC_LANES)]
                    for q in range(4):
                        lo, hi = _unpack_pair(rows_v[half, gi * SC_LANES + rr + q, pl.ds(SC_LANES * k, SC_LANES)])
                        accs[q] = accs[q] + lo * hlo + hi * hhi
                for q in range(4):
                    avec = jnp.where(lane == rr + q, jnp.sum(accs[q]), avec)
            a_v[tt, pl.ds(half * SC_HALF + gi * SC_LANES, SC_LANES)] = avec
            return carry
        lax.fori_loop(0, SC_HALF // SC_LANES, rows16, 0)

    def group(g, carry):
        loc = base + g * SC_GROUP
        pltpu.sync_copy(idx_hbm.at[pl.ds(tok0 + loc, SC_GROUP)], idx_v)
        pltpu.sync_copy(h_hbm.at[pl.ds(tok0 + loc, SC_GROUP)], h_v)
        _sc_token_loop(tab_hbm, idx_v, rows_v, sems, compute_half)
        pltpu.sync_copy(a_v, a_hbm.at[pl.ds(loc, SC_GROUP)])
        return carry

    lax.fori_loop(0, n_per // SC_GROUP, group, 0)


def _sc_mix_body(tab_hbm, idx_hbm, w_hbm, y_hbm, idx_v, w_v, rows_v, y_v, sems, *, tok0, n_per, d):
    base = _sc_worker_base(n_per)
    nk = d // (2 * SC_LANES)
    nq = 2
    kq = nk // nq

    def compute_half(tt, half):
        ttv = jnp.full((SC_LANES,), tt, jnp.int32)
        for piece in range(nq):
            col0 = piece * kq * 2 * SC_LANES
            if half == 0:
                accs = tuple(jnp.zeros((SC_LANES,), F32) for _ in range(2 * kq))
            else:
                accs = tuple(y_v[tt, pl.ds(col0 + SC_LANES * m, SC_LANES)] for m in range(2 * kq))

            def row(r, accs):
                wsplat = plsc.load_gather(w_v, [ttv, jnp.full((SC_LANES,), half * SC_HALF + r, jnp.int32)])
                out = []
                for k in range(kq):
                    lo, hi = _unpack_pair(rows_v[half, r, pl.ds(SC_LANES * (piece * kq + k), SC_LANES)])
                    out.append(accs[2 * k] + wsplat * lo)
                    out.append(accs[2 * k + 1] + wsplat * hi)
                return tuple(out)

            accs = lax.fori_loop(0, SC_HALF, row, accs)
            for m in range(2 * kq):
                y_v[tt, pl.ds(col0 + SC_LANES * m, SC_LANES)] = accs[m]

    def group(g, carry):
        loc = base + g * SC_GROUP
        pltpu.sync_copy(idx_hbm.at[pl.ds(tok0 + loc, SC_GROUP)], idx_v)
        pltpu.sync_copy(w_hbm.at[pl.ds(loc, SC_GROUP)], w_v)
        _sc_token_loop(tab_hbm, idx_v, rows_v, sems, compute_half)
        pltpu.sync_copy(y_v, y_hbm.at[pl.ds(loc, SC_GROUP)])
        return carry

    lax.fori_loop(0, n_per // SC_GROUP, group, 0)


def _sc_mesh():
    return plsc.VectorSubcoreMesh(core_axis_name="c", subcore_axis_name="s")


def _sc_dot(u_words, idx3, h2, tok0, ntok):
    d = h2.shape[1]
    slots = idx3.shape[1] * idx3.shape[2]
    n_per = ntok // (SC_CORES * SC_SUBCORES)
    return pl.kernel(
        functools.partial(_sc_dot_body, tok0=tok0, n_per=n_per, d=d),
        out_type=jax.ShapeDtypeStruct((ntok, slots), F32),
        mesh=_sc_mesh(),
        scratch_types=[pltpu.VMEM((SC_GROUP, 2, SC_HALF), jnp.int32),
                       pltpu.VMEM((SC_GROUP, d), F32),
                       pltpu.VMEM((2, SC_HALF, d // 2), jnp.uint32),
                       pltpu.VMEM((SC_GROUP, slots), F32),
                       pltpu.SemaphoreType.DMA((2,))],
        compiler_params=pltpu.CompilerParams(needs_layout_passes=False),
        name="peer_sc_dot",
    )(u_words, idx3, h2)


def _sc_mix(v_words, idx3, w, tok0, ntok, d):
    slots = idx3.shape[1] * idx3.shape[2]
    n_per = ntok // (SC_CORES * SC_SUBCORES)
    return pl.kernel(
        functools.partial(_sc_mix_body, tok0=tok0, n_per=n_per, d=d),
        out_type=jax.ShapeDtypeStruct((ntok, d), F32),
        mesh=_sc_mesh(),
        scratch_types=[pltpu.VMEM((SC_GROUP, 2, SC_HALF), jnp.int32),
                       pltpu.VMEM((SC_GROUP, slots), F32),
                       pltpu.VMEM((2, SC_HALF, d // 2), jnp.uint32),
                       pltpu.VMEM((SC_GROUP, d), F32),
                       pltpu.SemaphoreType.DMA((2,))],
        compiler_params=pltpu.CompilerParams(needs_layout_passes=False),
        name="peer_sc_mix",
    )(v_words, idx3, w)


def _gelu_gate_kernel(a_ref, g_ref, o_ref):
    a = a_ref[...]
    o_ref[...] = g_ref[...] * (0.5 * a * (1.0 + lax.erf(a * (2.0 ** -0.5))))


def _gelu_gate(a, gate, tok0, tm=2048):
    n, slots = a.shape
    off = tok0 // tm
    return pl.pallas_call(
        _gelu_gate_kernel,
        grid=(n // tm,),
        in_specs=[pl.BlockSpec((tm, slots), lambda i: (i, 0)),
                  pl.BlockSpec((tm, slots), lambda i: (i + off, 0))],
        out_specs=pl.BlockSpec((tm, slots), lambda i: (i, 0)),
        out_shape=jax.ShapeDtypeStruct((n, slots), F32),
        compiler_params=_cparams(("arbitrary",)),
        name="peer_gelu_gate",
    )(a, gate)


def _residual_kernel(x_ref, mod_ref, y_ref, o_ref):
    o_ref[...] = x_ref[...] + mod_ref[0][5:6] * y_ref[...]


def _residual(x2, mod3, y, tok0, seq, tm=512):
    n, d = y.shape
    off = tok0 // tm
    per = seq // tm
    return pl.pallas_call(
        _residual_kernel,
        grid=(n // tm,),
        in_specs=[pl.BlockSpec((tm, d), lambda i: (i + off, 0)),
                  pl.BlockSpec((1, 6, d), lambda i: ((i + off) // per, 0, 0)),
                  pl.BlockSpec((tm, d), lambda i: (i, 0))],
        out_specs=pl.BlockSpec((tm, d), lambda i: (i, 0)),
        out_shape=jax.ShapeDtypeStruct((n, d), F32),
        compiler_params=_cparams(("arbitrary",)),
        name="peer_residual",
    )(x2, mod3, y)


def kernel(x_prompt, x_sample, c_prompt, c_sample, w_mod, b_mod, g_norm1, g_norm2, w_in, conv_w, conv_b, f_w1, f_b1, f_freq, f_w2, f_b2, f_w3, f_bias, q_gain, k_gain, sink, w_pa, w_pb, w_out, peer_wq, peer_k1, peer_k2, peer_u, peer_v):
    depth = w_mod.shape[0]
    bp, seq, d = x_prompt.shape
    bs = x_sample.shape[0]
    assert x_sample.shape[1] == seq
    bsz = bp + bs
    x = jnp.concatenate([x_prompt, x_sample], axis=0).reshape(bsz * seq, d)
    c = jnp.concatenate([c_prompt, c_sample], axis=0)

    hyw = w_pa.shape[1]
    aw = w_pb.shape[1]
    kw = N_KV_HEADS * HEAD_DIM
    widths = (HY_ORDER + 1) * hyw, aw, kw, kw, d, d
    cblk = hyw // 256

    mod = _modulation(c, w_mod.astype(BF16), b_mod)
    fc, fs = _dft_tables(seq)

    for l in range(depth):
        mod3 = mod[l].reshape(bsz, 6, d)
        hy, q, k, v, ga, gb = _inproj(x, mod3, g_norm1[l], w_in[l].astype(BF16), seq, widths)
        hcat = _filters_time(seq, f_w1[l], f_b1[l], f_freq[l], f_w2[l], f_b2[l], f_w3[l])
        spec_a, spec_b, nyq = _filter_spectra(fc, fs, hcat)
        cw, cb = conv_w[l], conv_b[l][None, :]
        zz = _long_conv(fc, fs, hy, 0, hy, cblk, spec_a, spec_b, nyq, 0, f_bias[l][0:1], cw, cb,
                        0, cblk, bsz, seq, conv_u=True)
        ya = _long_conv(fc, fs, zz, 0, hy, 2 * cblk, spec_a, spec_b, nyq, cblk, f_bias[l][1:2], cw, cb,
                        0, 2 * cblk, bsz, seq, conv_u=False)
        yb = _attention(q, k, v, q_gain[l], k_gain[l], sink[l], bsz, seq)
        x = _merge(x, mod3, ya, yb, ga, gb, w_pa[l].astype(BF16), w_pb[l].astype(BF16), w_out[l].astype(BF16), seq)
        h2, idx, gate = _peer_score(x, mod3, g_norm2[l], peer_wq[l].astype(BF16),
                                    peer_k1[l].astype(BF16), peer_k2[l].astype(BF16), seq)
        table = _pack_expert_tables(peer_u[l], peer_v[l])
        t_tc = (bsz - SC_SEQS) * seq
        t_sc = SC_SEQS * seq
        idx3 = idx.reshape(bsz * seq, 2, SC_HALF)
        a_sc = _sc_dot(_pack_pairs(peer_u[l]), idx3, h2, t_tc, t_sc)
        x_tc = _peer_gather(x, mod3, h2, idx, gate, table, seq, t_tc)
        w_sc = _gelu_gate(a_sc, gate, t_tc)
        y_sc = _sc_mix(_pack_pairs(peer_v[l]), idx3, w_sc, t_tc, t_sc, d)
        x = jnp.concatenate([x_tc, _residual(x, mod3, y_sc, t_tc, seq)], axis=0)

    x = x.reshape(bsz, seq, d)
    return x[:bp], x[bp:]
```

```python
import functools
import math

import jax
import jax.numpy as jnp
import numpy as np
from jax import lax
from jax.experimental import pallas as pl
from jax.experimental.pallas import tpu as pltpu
from jax.experimental.pallas import tpu_sc as plsc

F32 = jnp.float32
BF16 = jnp.bfloat16

EPS = 1e-6
HEAD_DIM = 64
N_Q_HEADS = 8
N_KV_HEADS = 2
WINDOW = 128
BLOCK = 128
ROPE_THETA = 10000.0
HY_ORDER = 2
N_DIR = 2
FILT_BANDS = 16
DECAY_TARGET = 1e-2
FAST_DECAY_PCT = 0.3
SLOW_DECAY_PCT = 1.5
PEER_HEADS = 8
N_KEYS = 128
PEER_TOPK = 16
PEER_QDIM = 256
LANES = 128
SUBLANES = 8
VMEM_LIMIT = 56 * 1024 * 1024
GATHER_RING = 4
GATHER_AHEAD = 2

_STAIR = [(i, j) for i in range(PEER_TOPK) for j in range(PEER_TOPK) if (i + 1) * (j + 1) <= PEER_TOPK]
_STAIR_ROWS = -(-len(_STAIR) // SUBLANES) * SUBLANES


def _cparams(sem, vmem=VMEM_LIMIT):
    return pltpu.CompilerParams(dimension_semantics=sem, vmem_limit_bytes=vmem)


def _dot(a, b):
    return jnp.dot(a, b, preferred_element_type=F32)


def _dot_nt(a, b):
    return lax.dot_general(a, b, (((1,), (1,)), ((), ())), preferred_element_type=F32)


def _dot_hi(a, b):
    return jnp.dot(a, b, preferred_element_type=F32, precision=lax.Precision.HIGHEST)


def _sigmoid(x):
    return 1.0 / (1.0 + jnp.exp(-x))


def _mod_kernel(c_ref, w_ref, b_ref, o_ref):
    c = c_ref[...]
    s = c * _sigmoid(c)
    o_ref[0] = _dot(s.astype(BF16), w_ref[0]) + b_ref[0]


def _modulation(c, w_mod_bf, b_mod):
    depth, d, n6 = w_mod_bf.shape
    bsz = c.shape[0]
    tn = 1536
    return pl.pallas_call(
        _mod_kernel,
        grid=(depth, n6 // tn),
        in_specs=[pl.BlockSpec((bsz, d), lambda l, j: (0, 0)),
                  pl.BlockSpec((1, d, tn), lambda l, j: (l, 0, j)),
                  pl.BlockSpec((1, 1, tn), lambda l, j: (l, 0, j))],
        out_specs=pl.BlockSpec((1, bsz, tn), lambda l, j: (l, 0, j)),
        out_shape=jax.ShapeDtypeStruct((depth, bsz, n6), F32),
        compiler_params=_cparams(("arbitrary", "arbitrary")),
        name="adaln_mod",
    )(c, w_mod_bf, b_mod.reshape(depth, 1, n6))


def _norm_mod(x, gain, shift, scale):
    y = x * lax.rsqrt(jnp.mean(x * x, axis=-1, keepdims=True) + EPS)
    return (y * gain) * (1.0 + scale) + shift


def _inproj_kernel(x_ref, mod_ref, g_ref, w_ref, hy_ref, q_ref, k_ref, v_ref, ga_ref, gb_ref, *, splits):
    m = mod_ref[0]
    h = _norm_mod(x_ref[...], g_ref[...], m[0:1], m[1:2])
    z = _dot(h.astype(BF16), w_ref[...])
    outs = (hy_ref, q_ref, k_ref, v_ref, ga_ref, gb_ref)
    lo = 0
    for ref, hi in zip(outs, splits):
        ref[...] = z[:, lo:hi]
        lo = hi


def _inproj(x2, mod3, gain, w_in_bf, seq, widths, tm=256):
    t, d = x2.shape
    ncols = w_in_bf.shape[1]
    per = seq // tm
    splits = tuple(int(s) for s in np.cumsum(widths))
    return pl.pallas_call(
        functools.partial(_inproj_kernel, splits=splits),
        grid=(t // tm,),
        in_specs=[pl.BlockSpec((tm, d), lambda i: (i, 0)),
                  pl.BlockSpec((1, 6, d), lambda i: (i // per, 0, 0)),
                  pl.BlockSpec((1, d), lambda i: (0, 0)),
                  pl.BlockSpec((d, ncols), lambda i: (0, 0))],
        out_specs=[pl.BlockSpec((tm, w), lambda i: (i, 0)) for w in widths],
        out_shape=[jax.ShapeDtypeStruct((t, w), F32) for w in widths],
        compiler_params=_cparams(("arbitrary",)),
        name="inproj",
    )(x2, mod3, gain.reshape(1, d), w_in_bf)


def _filter_kernel(z_ref, w1_ref, b1_ref, fr_ref, w2_ref, b2_ref, w3_ref, ad_ref, o_ref):
    z = z_ref[0]
    fr = fr_ref[...]
    a = jnp.sin(fr * (_dot_hi(z, w1_ref[...]) + b1_ref[...]))
    a = jnp.sin(fr * (_dot_hi(a, w2_ref[...]) + b2_ref[...]))
    h = _dot_hi(a, w3_ref[0])
    h = h * jnp.exp(-z[:, 0:1] * ad_ref[...])
    row = lax.broadcasted_iota(jnp.int32, h.shape, 0)
    dead = (pl.program_id(0) == 1) & (pl.program_id(1) == 0) & (row == 0)
    o_ref[0] = jnp.where(dead, 0.0, h)


def _filters_time(seq, f_w1, f_b1, f_freq, f_w2, f_b2, f_w3, tm=512):
    hidden = f_w1.shape[1]
    cw = f_w3.shape[1] // (HY_ORDER * N_DIR)
    t = jnp.linspace(0.0, 1.0, seq, dtype=F32)[:, None]
    w = 2.0 * math.pi * jnp.arange(seq, dtype=F32)[:, None] / seq
    bands = jnp.linspace(1e-4, FILT_BANDS - 1, FILT_BANDS, dtype=F32)[None, :]
    z = jnp.concatenate([t, jnp.cos(bands * w), -jnp.sin(bands * w)], axis=-1)
    emb = z.shape[1]
    z = jnp.pad(z, ((0, 0), (0, LANES - emb)))
    zcat = jnp.stack([z, jnp.concatenate([z[:1], z[:0:-1]], axis=0)], axis=0)
    w1p = jnp.pad(f_w1, ((0, LANES - emb), (0, 0)))
    w3d = f_w3.reshape(hidden, HY_ORDER, N_DIR, cw).transpose(2, 0, 1, 3).reshape(N_DIR, hidden, HY_ORDER * cw)
    max_decay = math.log(DECAY_TARGET) / FAST_DECAY_PCT
    min_decay = math.log(DECAY_TARGET) / SLOW_DECAY_PCT
    ad = jnp.abs(jnp.linspace(min_decay, max_decay, cw, dtype=F32))
    ad = jnp.tile(ad, HY_ORDER)[None, :]
    oc = HY_ORDER * cw
    return pl.pallas_call(
        _filter_kernel,
        grid=(N_DIR, seq // tm),
        in_specs=[pl.BlockSpec((1, tm, LANES), lambda g, r: (g, r, 0)),
                  pl.BlockSpec((LANES, hidden), lambda g, r: (0, 0)),
                  pl.BlockSpec((1, hidden), lambda g, r: (0, 0)),
                  pl.BlockSpec((1, hidden), lambda g, r: (0, 0)),
                  pl.BlockSpec((hidden, hidden), lambda g, r: (0, 0)),
                  pl.BlockSpec((1, hidden), lambda g, r: (0, 0)),
                  pl.BlockSpec((1, hidden, oc), lambda g, r: (g, 0, 0)),
                  pl.BlockSpec((1, oc), lambda g, r: (0, 0))],
        out_specs=pl.BlockSpec((1, tm, oc), lambda g, r: (g, r, 0)),
        out_shape=jax.ShapeDtypeStruct((N_DIR, seq, oc), F32),
        compiler_params=_cparams(("arbitrary", "arbitrary")),
        name="hyena_filter_mlp",
    )(zcat, w1p, f_b1[None, :], f_freq[None, :], f_w2, f_b2[None, :], w3d, ad)


def _dft_tables(seq):
    n2 = 2 * seq
    f = jnp.arange(seq, dtype=jnp.int32)
    ft = (f[:, None] * f[None, :]) % n2
    ang = ft.astype(F32) * (2.0 * math.pi / n2)
    return jnp.cos(ang).astype(BF16), jnp.sin(ang).astype(BF16)


def _spec_kernel(fc_ref, fs_ref, h_ref, a_ref, b_ref, nyq_ref, *, seq, tf):
    hlo = h_ref[0]
    hhi = h_ref[1]
    hlo_b = hlo.astype(BF16)
    hhi_b = hhi.astype(BF16)
    f = pl.program_id(1) * tf + lax.broadcasted_iota(jnp.int32, (tf, 1), 0)
    sgn = jnp.where(f % 2 == 0, 1.0, -1.0)
    fc = fc_ref[...]
    fs = fs_ref[...]
    hr = _dot(fc, hlo_b) + sgn * _dot(fc, hhi_b)
    hs = _dot(fs, hlo_b) + sgn * _dot(fs, hhi_b)
    w = jnp.where(f == 0, 1.0, 2.0) * (1.0 / (2 * seq))
    a_ref[...] = w * hr
    b_ref[...] = -(w * hs)
    t = lax.broadcasted_iota(jnp.int32, (seq, 1), 0)
    alt = jnp.where(t % 2 == 0, 1.0, -1.0)
    nyq = jnp.sum(alt * (hlo + hhi), axis=0, keepdims=True) * (1.0 / (2 * seq))
    nyq_ref[...] = jnp.broadcast_to(nyq, nyq_ref.shape)


def _filter_spectra(fc, fs, hcat, tf=512, tcol=512):
    seq = fc.shape[0]
    oc = hcat.shape[2]
    return pl.pallas_call(
        functools.partial(_spec_kernel, seq=seq, tf=tf),
        grid=(oc // tcol, seq // tf),
        in_specs=[pl.BlockSpec((tf, seq), lambda j, i: (i, 0)),
                  pl.BlockSpec((tf, seq), lambda j, i: (i, 0)),
                  pl.BlockSpec((2, seq, tcol), lambda j, i: (0, 0, j))],
        out_specs=[pl.BlockSpec((tf, tcol), lambda j, i: (i, j)),
                   pl.BlockSpec((tf, tcol), lambda j, i: (i, j)),
                   pl.BlockSpec((SUBLANES, tcol), lambda j, i: (0, j))],
        out_shape=[jax.ShapeDtypeStruct((seq, oc), F32),
                   jax.ShapeDtypeStruct((seq, oc), F32),
                   jax.ShapeDtypeStruct((SUBLANES, oc), F32)],
        compiler_params=_cparams(("arbitrary", "arbitrary")),
        name="hyena_filter_spectra",
    )(fc, fs, hcat)


def _shortconv(x, w_ref, b_ref, seq):
    row = lax.broadcasted_iota(jnp.int32, (seq, 1), 0)
    xm = jnp.where(row == 0, 0.0, pltpu.roll(x, 1, 0))
    xp = jnp.where(row == seq - 1, 0.0, pltpu.roll(x, seq - 1, 0))
    return xm * w_ref[0:1, :] + x * w_ref[1:2, :] + xp * w_ref[2:3, :] + b_ref[...]


def _conv_kernel(fc_ref, fs_ref, u_ref, g_ref, a_ref, b_ref, nyq_ref, bias_ref,
                 cwu_ref, cbu_ref, cwg_ref, cbg_ref, o_ref, acc_ref, *, seq, fb, conv_u):
    u = u_ref[...]
    if conv_u:
        u = _shortconv(u, cwu_ref, cbu_ref, seq)
    gate = _shortconv(g_ref[...], cwg_ref, cbg_ref, seq)
    ub = u.astype(BF16)
    for c in range(seq // fb):
        rows = slice(c * fb, (c + 1) * fb)
        ur = _dot(fc_ref[rows, :], ub)
        us = _dot(fs_ref[rows, :], ub)
        a = a_ref[rows, :]
        b = b_ref[rows, :]
        qr = (ur * a + us * b).astype(BF16)
        qi = (us * a - ur * b).astype(BF16)
        part = _dot(fc_ref[:, rows], qr) + _dot(fs_ref[:, rows], qi)
        if c == 0:
            acc_ref[...] = part
        else:
            acc_ref[...] += part
    t = lax.broadcasted_iota(jnp.int32, (seq, 1), 0)
    alt = jnp.where(t % 2 == 0, 1.0, -1.0)
    unyq = jnp.sum(alt * u, axis=0, keepdims=True)
    y = acc_ref[...] + alt * (unyq * nyq_ref[0:1, :]) + bias_ref[...] * u
    o_ref[...] = gate * y


def _long_conv(fc, fs, u_src, u_blk0, g_src, g_blk0, spec_a, spec_b, nyq, s_blk0, bias, cw, cb,
               cu_blk0, cg_blk0, bsz, seq, conv_u, tc=256, fb=512):
    nct = 512 // tc
    t = bsz * seq
    const = lambda j, b: (0, 0)
    return pl.pallas_call(
        functools.partial(_conv_kernel, seq=seq, fb=fb, conv_u=conv_u),
        grid=(nct, bsz),
        in_specs=[pl.BlockSpec(memory_space=pltpu.VMEM),
                  pl.BlockSpec(memory_space=pltpu.VMEM),
                  pl.BlockSpec((seq, tc), lambda j, b: (b, u_blk0 + j)),
                  pl.BlockSpec((seq, tc), lambda j, b: (b, g_blk0 + j)),
                  pl.BlockSpec((seq, tc), lambda j, b: (0, s_blk0 + j)),
                  pl.BlockSpec((seq, tc), lambda j, b: (0, s_blk0 + j)),
                  pl.BlockSpec((SUBLANES, tc), lambda j, b: (0, s_blk0 + j)),
                  pl.BlockSpec((1, tc), lambda j, b: (0, j)),
                  pl.BlockSpec((3, tc), lambda j, b: (0, cu_blk0 + j)),
                  pl.BlockSpec((1, tc), lambda j, b: (0, cu_blk0 + j)),
                  pl.BlockSpec((3, tc), lambda j, b: (0, cg_blk0 + j)),
                  pl.BlockSpec((1, tc), lambda j, b: (0, cg_blk0 + j))],
        out_specs=pl.BlockSpec((seq, tc), lambda j, b: (b, j)),
        out_shape=jax.ShapeDtypeStruct((t, 512), F32),
        scratch_shapes=[pltpu.VMEM((seq, tc), F32)],
        compiler_params=_cparams(("arbitrary", "arbitrary")),
        name="hyena_long_conv_u" if conv_u else "hyena_long_conv",
    )(fc, fs, u_src, g_src, spec_a, spec_b, nyq, bias, cw, cb, cw, cb)


def _attn_kernel(q_ref, k_ref, v_ref, cos_ref, sin_ref, qg_ref, kg_ref, sink_ref, o_ref,
                 qn_ref, km_ref, vm_ref, *, seq):
    lane = lax.broadcasted_iota(jnp.int32, (1, LANES), 1)
    r = lax.broadcasted_iota(jnp.int32, (LANES, LANES), 0) // HEAD_DIM
    c = lax.broadcasted_iota(jnp.int32, (LANES, LANES), 1) // HEAD_DIM
    bd = jnp.where(r == c, 1.0, 0.0).astype(BF16)
    first_half = (lane % HEAD_DIM) < (HEAD_DIM // 2)
    cos = cos_ref[...]
    sin = sin_ref[...]

    def norm_rope(x, gain):
        sq = x * x
        hi = sq.astype(BF16)
        lo = (sq - hi.astype(F32)).astype(BF16)
        ss = _dot(hi, bd) + _dot(lo, bd)
        y = (x * lax.rsqrt(ss * (1.0 / HEAD_DIM) + EPS)) * gain
        partner = jnp.where(first_half, pltpu.roll(y, LANES - HEAD_DIM // 2, 1), pltpu.roll(y, HEAD_DIM // 2, 1))
        return y * cos + partner * sin

    for p in range(N_Q_HEADS // 2):
        cols = slice(p * LANES, (p + 1) * LANES)
        qn_ref[:, cols] = norm_rope(q_ref[:, cols], qg_ref[...]).astype(BF16)
    kn = norm_rope(k_ref[...], kg_ref[...])
    left = lane < HEAD_DIM
    for src_ref, dst_ref in ((None, km_ref), (v_ref, vm_ref)):
        val = kn if src_ref is None else src_ref[...]
        rolled = pltpu.roll(val, HEAD_DIM, 1)
        dst_ref[0] = jnp.where(left, val, 0.0).astype(BF16)
        dst_ref[1] = jnp.where(left, 0.0, rolled).astype(BF16)
        dst_ref[2] = jnp.where(left, rolled, 0.0).astype(BF16)
        dst_ref[3] = jnp.where(left, 0.0, val).astype(BF16)

    span = 3 * BLOCK
    scale = HEAD_DIM ** -0.5
    ii = lax.broadcasted_iota(jnp.int32, (BLOCK, span), 0)
    jj = lax.broadcasted_iota(jnp.int32, (BLOCK, span), 1)

    def block(n, carry):
        q0 = pl.multiple_of(n * BLOCK, BLOCK)
        start = pl.multiple_of(jnp.clip((n - 1) * BLOCK, 0, seq - span), BLOCK)
        valid = jnp.abs((start - q0) + jj - ii) <= WINDOW
        for p in range(N_Q_HEADS // 2):
            cols = slice(p * LANES, (p + 1) * LANES)
            kv = (2 * p) // (N_Q_HEADS // N_KV_HEADS)
            qp = qn_ref[pl.ds(q0, BLOCK), cols]
            o = jnp.zeros((BLOCK, LANES), F32)
            for a in range(2):
                h = 2 * p + a
                kb = km_ref[2 * kv + a, pl.ds(start, span), :]
                s = _dot_nt(qp, kb) * scale
                s = jnp.where(valid, s, -jnp.inf)
                sk = sink_ref[h:h + 1, 0:1]
                m = jnp.maximum(jnp.max(s, axis=-1, keepdims=True), sk)
                e = jnp.exp(s - m)
                den = jnp.sum(e, axis=-1, keepdims=True) + jnp.exp(sk - m)
                pn = (e / den).astype(BF16)
                o = o + _dot(pn, vm_ref[2 * kv + a, pl.ds(start, span), :])
            o_ref[pl.ds(q0, BLOCK), cols] = o
        return carry

    lax.fori_loop(0, seq // BLOCK, block, 0)


def _attention(q, k, v, q_gain, k_gain, sink, bsz, seq):
    inv = ROPE_THETA ** (-jnp.arange(0, HEAD_DIM, 2, dtype=F32) / HEAD_DIM)
    ang = jnp.arange(seq, dtype=F32)[:, None] * inv[None, :]
    cos = jnp.tile(jnp.cos(ang), (1, LANES // (HEAD_DIM // 2)))
    sn = jnp.sin(ang)
    sin = jnp.tile(jnp.concatenate([-sn, sn], axis=1), (1, LANES // HEAD_DIM))
    qg = jnp.tile(q_gain, LANES // HEAD_DIM)[None, :]
    kg = jnp.tile(k_gain, LANES // HEAD_DIM)[None, :]
    sinkb = jnp.broadcast_to(sink[:, None], (N_Q_HEADS, LANES))
    aw = N_Q_HEADS * HEAD_DIM
    kw = N_KV_HEADS * HEAD_DIM
    const = lambda b: (0, 0)
    return pl.pallas_call(
        functools.partial(_attn_kernel, seq=seq),
        grid=(bsz,),
        in_specs=[pl.BlockSpec((seq, aw), lambda b: (b, 0)),
                  pl.BlockSpec((seq, kw), lambda b: (b, 0)),
                  pl.BlockSpec((seq, kw), lambda b: (b, 0)),
                  pl.BlockSpec((seq, LANES), const),
                  pl.BlockSpec((seq, LANES), const),
                  pl.BlockSpec((1, LANES), const),
                  pl.BlockSpec((1, LANES), const),
                  pl.BlockSpec((N_Q_HEADS, LANES), const)],
        out_specs=pl.BlockSpec((seq, aw), lambda b: (b, 0)),
        out_shape=jax.ShapeDtypeStruct((bsz * seq, aw), F32),
        scratch_shapes=[pltpu.VMEM((seq, aw), BF16),
                        pltpu.VMEM((4, seq, LANES), BF16),
                        pltpu.VMEM((4, seq, LANES), BF16)],
        compiler_params=_cparams(("arbitrary",)),
        name="window_attention",
    )(q, k, v, cos, sin, qg, kg, sinkb)


def _merge_kernel(x_ref, mod_ref, ya_ref, yb_ref, ga_ref, gb_ref, wpa_ref, wpb_ref, wo_ref, o_ref):
    pa = _dot(ya_ref[...].astype(BF16), wpa_ref[...])
    pb = _dot(yb_ref[...].astype(BF16), wpb_ref[...])
    merged = _sigmoid(ga_ref[...]) * pa + _sigmoid(gb_ref[...]) * pb
    out = _dot(merged.astype(BF16), wo_ref[...])
    o_ref[...] = x_ref[...] + mod_ref[0][2:3] * out


def _merge(x2, mod3, ya, yb, ga, gb, w_pa_bf, w_pb_bf, w_out_bf, seq, tm=512):
    t, d = x2.shape
    per = seq // tm
    row = lambda w: pl.BlockSpec((tm, w), lambda i: (i, 0))
    full = lambda a: pl.BlockSpec(a.shape, lambda i: (0, 0))
    return pl.pallas_call(
        _merge_kernel,
        grid=(t // tm,),
        in_specs=[row(d), pl.BlockSpec((1, 6, d), lambda i: (i // per, 0, 0)),
                  row(ya.shape[1]), row(yb.shape[1]), row(d), row(d),
                  full(w_pa_bf), full(w_pb_bf), full(w_out_bf)],
        out_specs=row(d),
        out_shape=jax.ShapeDtypeStruct((t, d), F32),
        compiler_params=_cparams(("arbitrary",)),
        name="merge_outproj",
    )(x2, mod3, ya, yb, ga, gb, w_pa_bf, w_pb_bf, w_out_bf)


def _topk_rows(s, k, val_ref, idx_ref):
    n = s.shape[0]
    row = lax.broadcasted_iota(jnp.int32, s.shape, 0)
    for i in range(k):
        m = jnp.max(s, axis=0, keepdims=True)
        first = jnp.min(jnp.where(s == m, row, n), axis=0, keepdims=True)
        val_ref[i:i + 1, :] = m
        idx_ref[i:i + 1, :] = first
        s = jnp.where(row == first, -jnp.inf, s)


def _peer_score_kernel(x_ref, mod_ref, g_ref, wq_ref, k1_ref, k2_ref, h_ref, idx_ref, gate_ref,
                       v1_ref, i1_ref, v2_ref, i2_ref, cv_ref, ce_ref, it_ref, gt_ref, *, tm):
    m = mod_ref[0]
    h = _norm_mod(x_ref[...], g_ref[...], m[3:4], m[4:5])
    h_ref[...] = h
    q = _dot(h.astype(BF16), wq_ref[...]).astype(BF16)
    half = PEER_QDIM // 2
    neg = jnp.full((1, tm), -jnp.inf, F32)
    for hd in range(PEER_HEADS):
        q1 = q[:, hd * PEER_QDIM: hd * PEER_QDIM + half]
        q2 = q[:, hd * PEER_QDIM + half: (hd + 1) * PEER_QDIM]
        _topk_rows(_dot_nt(k1_ref[...], q1), PEER_TOPK, v1_ref, i1_ref)
        _topk_rows(_dot_nt(k2_ref[...], q2), PEER_TOPK, v2_ref, i2_ref)
        for r, (i, j) in enumerate(_STAIR):
            cv_ref[r:r + 1, :] = v1_ref[i:i + 1, :] + v2_ref[j:j + 1, :]
            ce_ref[r:r + 1, :] = i1_ref[i:i + 1, :] * N_KEYS + i2_ref[j:j + 1, :]
        for r in range(len(_STAIR), _STAIR_ROWS):
            cv_ref[r:r + 1, :] = neg
            ce_ref[r:r + 1, :] = jnp.zeros((1, tm), jnp.int32)
        cand = cv_ref[...]
        ce = ce_ref[...]
        row = lax.broadcasted_iota(jnp.int32, cand.shape, 0)
        vals = []
        for kk in range(PEER_TOPK):
            mx = jnp.max(cand, axis=0, keepdims=True)
            first = jnp.min(jnp.where(cand == mx, row, _STAIR_ROWS), axis=0, keepdims=True)
            sel = row == first
            it_ref[hd * PEER_TOPK + kk: hd * PEER_TOPK + kk + 1, :] = jnp.sum(jnp.where(sel, ce, 0), axis=0, keepdims=True)
            vals.append(mx)
            cand = jnp.where(sel, -jnp.inf, cand)
        ex = [jnp.exp(v - vals[0]) for v in vals]
        tot = ex[0]
        for e in ex[1:]:
            tot = tot + e
        for kk in range(PEER_TOPK):
            gt_ref[hd * PEER_TOPK + kk: hd * PEER_TOPK + kk + 1, :] = ex[kk] / tot
    idx_ref[...] = it_ref[...].T
    gate_ref[...] = gt_ref[...].T


def _peer_score(x2, mod3, gain, wq_bf, k1_bf, k2_bf, seq, tm=256):
    t, d = x2.shape
    per = seq // tm
    slots = PEER_HEADS * PEER_TOPK
    full = lambda a: pl.BlockSpec(a.shape, lambda i: (0, 0))
    return pl.pallas_call(
        functools.partial(_peer_score_kernel, tm=tm),
        grid=(t // tm,),
        in_specs=[pl.BlockSpec((tm, d), lambda i: (i, 0)),
                  pl.BlockSpec((1, 6, d), lambda i: (i // per, 0, 0)),
                  pl.BlockSpec((1, d), lambda i: (0, 0)),
                  full(wq_bf), full(k1_bf), full(k2_bf)],
        out_specs=[pl.BlockSpec((tm, d), lambda i: (i, 0)),
                   pl.BlockSpec((tm, slots), lambda i: (i, 0)),
                   pl.BlockSpec((tm, slots), lambda i: (i, 0))],
        out_shape=[jax.ShapeDtypeStruct((t, d), F32),
                   jax.ShapeDtypeStruct((t, slots), jnp.int32),
                   jax.ShapeDtypeStruct((t, slots), F32)],
        scratch_shapes=[pltpu.VMEM((PEER_TOPK, tm), F32), pltpu.VMEM((PEER_TOPK, tm), jnp.int32),
                        pltpu.VMEM((PEER_TOPK, tm), F32), pltpu.VMEM((PEER_TOPK, tm), jnp.int32),
                        pltpu.VMEM((_STAIR_ROWS, tm), F32), pltpu.VMEM((_STAIR_ROWS, tm), jnp.int32),
                        pltpu.VMEM((slots, tm), jnp.int32), pltpu.VMEM((slots, tm), F32)],
        compiler_params=_cparams(("arbitrary",)),
        name="peer_score_topk",
    )(x2, mod3, gain.reshape(1, d), wq_bf, k1_bf, k2_bf)


def _peer_gather_kernel(idx_ref, idxn_ref, tab_ref, h_ref, gate_ref, x_ref, mod_ref, after_ref, o_ref,
                        buf_ref, sem_ref, *, tc, d, nsteps):
    slots = PEER_HEADS * PEER_TOPK
    step = pl.program_id(0)

    def issue_token(chunk, t):
        ids_ref, c = (idx_ref, chunk) if chunk < GATHER_RING else (idxn_ref, chunk - GATHER_RING)
        for j in range(slots):
            pltpu.make_async_copy(tab_ref.at[ids_ref[c * tc + t, j]],
                                  buf_ref.at[c, t, pl.ds(j, 1), :],
                                  sem_ref.at[c]).start(priority=j % 2)

    def wait(slot):
        pltpu.make_async_copy(buf_ref.at[slot], buf_ref.at[slot], sem_ref.at[slot]).wait()

    eye = (lax.broadcasted_iota(jnp.int32, (slots, slots), 0) ==
           lax.broadcasted_iota(jnp.int32, (slots, slots), 1))
    gt2 = mod_ref[0][5:6]

    def mix_token(slot, t):
        r = slot * tc + t
        word = buf_ref[slot, t]
        u = lax.bitcast_convert_type(word << 16, F32)
        prod = u * h_ref[r:r + 1, :]
        part = prod[:, 0:LANES]
        for c in range(1, d // LANES):
            part = part + prod[:, c * LANES:(c + 1) * LANES]
        a = jnp.sum(part, axis=1, keepdims=True)
        grow = jnp.broadcast_to(gate_ref[r:r + 1, :], (slots, slots))
        gcol = jnp.sum(jnp.where(eye, grow, 0.0), axis=1, keepdims=True)
        w = gcol * (0.5 * a * (1.0 + lax.erf(a * (2.0 ** -0.5))))
        v = lax.bitcast_convert_type(buf_ref[slot, t] & jnp.uint32(0xFFFF0000), F32)
        y = jnp.sum(v * w, axis=0, keepdims=True)
        o_ref[r:r + 1, :] = x_ref[r:r + 1, :] + gt2 * y

    @pl.when(step == 0)
    def _():
        for c in range(GATHER_AHEAD):
            for t in range(tc):
                issue_token(c, t)

    for p in range(GATHER_RING):
        wait(p)
        for t in range(tc):
            issue_token(p + GATHER_AHEAD, t)
            mix_token(p, t)

    @pl.when(step == nsteps - 1)
    def _():
        for c in range(GATHER_AHEAD):
            wait(c)


def _peer_gather(x2, mod3, h2, idx, gate, table, seq, tok0, ntok, after, tc=8):
    d = x2.shape[1]
    slots = idx.shape[1]
    tb = GATHER_RING * tc
    nsteps = ntok // tb
    off = tok0 // tb
    per = seq // tb
    row = lambda w: pl.BlockSpec((tb, w), lambda i: (i + off, 0))
    return pl.pallas_call(
        functools.partial(_peer_gather_kernel, tc=tc, d=d, nsteps=nsteps),
        grid=(nsteps,),
        in_specs=[pl.BlockSpec((tb, slots), lambda i: (i + off, 0), memory_space=pltpu.SMEM),
                  pl.BlockSpec((tb, slots), lambda i: (jnp.minimum(i + 1, nsteps - 1) + off, 0),
                               memory_space=pltpu.SMEM),
                  pl.BlockSpec(memory_space=pl.ANY),
                  row(d), row(slots), row(d),
                  pl.BlockSpec((1, 6, d), lambda i: ((i + off) // per, 0, 0)),
                  pl.BlockSpec(memory_space=pl.ANY)],
        out_specs=pl.BlockSpec((tb, d), lambda i: (i, 0)),
        out_shape=jax.ShapeDtypeStruct((ntok, d), F32),
        scratch_shapes=[pltpu.VMEM((GATHER_RING, tc, slots, d), jnp.uint32),
                        pltpu.SemaphoreType.DMA((GATHER_RING,))],
        compiler_params=_cparams(("arbitrary",)),
        name="peer_gather_mix",
    )(idx, idx, table.reshape(table.shape[0], 1, d), h2, gate, x2, mod3, after)


def _pack_expert_tables(u_tab, v_tab):
    ub = lax.bitcast_convert_type(u_tab.astype(BF16), jnp.uint16).astype(jnp.uint32)
    vb = lax.bitcast_convert_type(v_tab.astype(BF16), jnp.uint16).astype(jnp.uint32)
    return (vb << 16) | ub


SC_LANES = 16
SC_CORES = 2
SC_SUBCORES = 16
SC_GROUP = 8
SC_HALF = 64
SC_SEQS = 10


def _pack_pairs(tab):
    e, d = tab.shape
    bits = lax.bitcast_convert_type(tab.astype(BF16), jnp.uint16).astype(jnp.uint32)
    bits = bits.reshape(e, d // (2 * SC_LANES), 2, SC_LANES)
    return ((bits[:, :, 1, :] << 16) | bits[:, :, 0, :]).reshape(e, d // 2)


def _sc_worker_base(n_per):
    return (lax.axis_index("s") * SC_CORES + lax.axis_index("c")) * n_per


def _sc_halves(tab_hbm, idx_v, rows_v, sems, tt, half):
    return pltpu.make_async_copy(tab_hbm.at[idx_v.at[tt, half]], rows_v.at[half], sems.at[half])


def _sc_token_loop(tab_hbm, idx_v, rows_v, sems, compute_half):
    _sc_halves(tab_hbm, idx_v, rows_v, sems, 0, 0).start()

    def tok(tt, carry):
        _sc_halves(tab_hbm, idx_v, rows_v, sems, tt, 1).start()
        _sc_halves(tab_hbm, idx_v, rows_v, sems, tt, 0).wait()
        compute_half(tt, 0)

        @pl.when(tt + 1 < SC_GROUP)
        def _():
            _sc_halves(tab_hbm, idx_v, rows_v, sems, tt + 1, 0).start()

        _sc_halves(tab_hbm, idx_v, rows_v, sems, tt, 1).wait()
        compute_half(tt, 1)
        return carry

    lax.fori_loop(0, SC_GROUP, tok, 0)


def _unpack_pair(word):
    lo = lax.bitcast_convert_type(word << 16, F32)
    hi = lax.bitcast_convert_type(word & jnp.uint32(0xFFFF0000), F32)
    return lo, hi


def _sc_dot_body(tab_hbm, idx_hbm, h_hbm, a_hbm, idx_v, h_v, rows_v, a_v, sems, *, tok0, n_per, d):
    base = _sc_worker_base(n_per)
    lane = lax.iota(jnp.int32, SC_LANES)
    nk = d // (2 * SC_LANES)

    def compute_half(tt, half):
        def rows16(gi, carry):
            avec = jnp.zeros((SC_LANES,), F32)
            for rr in range(0, SC_LANES, 4):
                accs = [jnp.zeros((SC_LANES,), F32) for _ in range(4)]
                for k in range(nk):
                    hlo = h_v[tt, pl.ds(2 * SC_LANES * k, SC_LANES)]
                    hhi = h_v[tt, pl.ds(2 * SC_LANES * k + SC_LANES, SC_LANES)]
                    for q in range(4):
                        lo, hi = _unpack_pair(rows_v[half, gi * SC_LANES + rr + q, pl.ds(SC_LANES * k, SC_LANES)])
                        accs[q] = accs[q] + lo * hlo + hi * hhi
                for q in range(4):
                    avec = jnp.where(lane == rr + q, jnp.sum(accs[q]), avec)
            a_v[tt, pl.ds(half * SC_HALF + gi * SC_LANES, SC_LANES)] = avec
            return carry
        lax.fori_loop(0, SC_HALF // SC_LANES, rows16, 0)

    def group(g, carry):
        loc = base + g * SC_GROUP
        pltpu.sync_copy(idx_hbm.at[pl.ds(tok0 + loc, SC_GROUP)], idx_v)
        pltpu.sync_copy(h_hbm.at[pl.ds(tok0 + loc, SC_GROUP)], h_v)
        _sc_token_loop(tab_hbm, idx_v, rows_v, sems, compute_half)
        pltpu.sync_copy(a_v, a_hbm.at[pl.ds(loc, SC_GROUP)])
        return carry

    lax.fori_loop(0, n_per // SC_GROUP, group, 0)


def _sc_mix_body(tab_hbm, idx_hbm, w_hbm, y_hbm, idx_v, w_v, rows_v, y_v, sems, *, tok0, n_per, d):
    base = _sc_worker_base(n_per)
    nk = d // (2 * SC_LANES)
    nq = 2
    kq = nk // nq

    def compute_half(tt, half):
        ttv = jnp.full((SC_LANES,), tt, jnp.int32)
        for piece in range(nq):
            col0 = piece * kq * 2 * SC_LANES
            if half == 0:
                accs = tuple(jnp.zeros((SC_LANES,), F32) for _ in range(2 * kq))
            else:
                accs = tuple(y_v[tt, pl.ds(col0 + SC_LANES * m, SC_LANES)] for m in range(2 * kq))

            def row(r, accs):
                wsplat = plsc.load_gather(w_v, [ttv, jnp.full((SC_LANES,), half * SC_HALF + r, jnp.int32)])
                out = []
                for k in range(kq):
                    lo, hi = _unpack_pair(rows_v[half, r, pl.ds(SC_LANES * (piece * kq + k), SC_LANES)])
                    out.append(accs[2 * k] + wsplat * lo)
                    out.append(accs[2 * k + 1] + wsplat * hi)
                return tuple(out)

            accs = lax.fori_loop(0, SC_HALF, row, accs)
            for m in range(2 * kq):
                y_v[tt, pl.ds(col0 + SC_LANES * m, SC_LANES)] = accs[m]

    def group(g, carry):
        loc = base + g * SC_GROUP
        pltpu.sync_copy(idx_hbm.at[pl.ds(tok0 + loc, SC_GROUP)], idx_v)
        pltpu.sync_copy(w_hbm.at[pl.ds(loc, SC_GROUP)], w_v)
        _sc_token_loop(tab_hbm, idx_v, rows_v, sems, compute_half)
        pltpu.sync_copy(y_v, y_hbm.at[pl.ds(loc, SC_GROUP)])
        return carry

    lax.fori_loop(0, n_per // SC_GROUP, group, 0)


def _sc_mesh():
    return plsc.VectorSubcoreMesh(core_axis_name="c", subcore_axis_name="s")


def _sc_dot(u_words, idx3, h2, tok0, ntok):
    d = h2.shape[1]
    slots = idx3.shape[1] * idx3.shape[2]
    n_per = ntok // (SC_CORES * SC_SUBCORES)
    return pl.kernel(
        functools.partial(_sc_dot_body, tok0=tok0, n_per=n_per, d=d),
        out_type=jax.ShapeDtypeStruct((ntok, slots), F32),
        mesh=_sc_mesh(),
        scratch_types=[pltpu.VMEM((SC_GROUP, 2, SC_HALF), jnp.int32),
                       pltpu.VMEM((SC_GROUP, d), F32),
                       pltpu.VMEM((2, SC_HALF, d // 2), jnp.uint32),
                       pltpu.VMEM((SC_GROUP, slots), F32),
                       pltpu.SemaphoreType.DMA((2,))],
        compiler_params=pltpu.CompilerParams(needs_layout_passes=False),
        name="peer_sc_dot",
    )(u_words, idx3, h2)


def _sc_mix(v_words, idx3, w, tok0, ntok, d):
    slots = idx3.shape[1] * idx3.shape[2]
    n_per = ntok // (SC_CORES * SC_SUBCORES)
    return pl.kernel(
        functools.partial(_sc_mix_body, tok0=tok0, n_per=n_per, d=d),
        out_type=jax.ShapeDtypeStruct((ntok, d), F32),
        mesh=_sc_mesh(),
        scratch_types=[pltpu.VMEM((SC_GROUP, 2, SC_HALF), jnp.int32),
                       pltpu.VMEM((SC_GROUP, slots), F32),
                       pltpu.VMEM((2, SC_HALF, d // 2), jnp.uint32),
                       pltpu.VMEM((SC_GROUP, d), F32),
                       pltpu.SemaphoreType.DMA((2,))],
        compiler_params=pltpu.CompilerParams(needs_layout_passes=False),
        name="peer_sc_mix",
    )(v_words, idx3, w)


def _gelu_gate_kernel(a_ref, g_ref, o_ref):
    a = a_ref[...]
    o_ref[...] = g_ref[...] * (0.5 * a * (1.0 + lax.erf(a * (2.0 ** -0.5))))


def _gelu_gate(a, gate, tok0, tm=2048):
    n, slots = a.shape
    off = tok0 // tm
    return pl.pallas_call(
        _gelu_gate_kernel,
        grid=(n // tm,),
        in_specs=[pl.BlockSpec((tm, slots), lambda i: (i, 0)),
                  pl.BlockSpec((tm, slots), lambda i: (i + off, 0))],
        out_specs=pl.BlockSpec((tm, slots), lambda i: (i, 0)),
        out_shape=jax.ShapeDtypeStruct((n, slots), F32),
        compiler_params=_cparams(("arbitrary",)),
        name="peer_gelu_gate",
    )(a, gate)


def _residual_kernel(x_ref, mod_ref, y_ref, o_ref):
    o_ref[...] = x_ref[...] + mod_ref[0][5:6] * y_ref[...]


def _residual(x2, mod3, y, tok0, seq, tm=512):
    n, d = y.shape
    off = tok0 // tm
    per = seq // tm
    return pl.pallas_call(
        _residual_kernel,
        grid=(n // tm,),
        in_specs=[pl.BlockSpec((tm, d), lambda i: (i + off, 0)),
                  pl.BlockSpec((1, 6, d), lambda i: ((i + off) // per, 0, 0)),
                  pl.BlockSpec((tm, d), lambda i: (i, 0))],
        out_specs=pl.BlockSpec((tm, d), lambda i: (i, 0)),
        out_shape=jax.ShapeDtypeStruct((n, d), F32),
        compiler_params=_cparams(("arbitrary",)),
        name="peer_residual",
    )(x2, mod3, y)


def kernel(x_prompt, x_sample, c_prompt, c_sample, w_mod, b_mod, g_norm1, g_norm2, w_in, conv_w, conv_b, f_w1, f_b1, f_freq, f_w2, f_b2, f_w3, f_bias, q_gain, k_gain, sink, w_pa, w_pb, w_out, peer_wq, peer_k1, peer_k2, peer_u, peer_v):
    depth = w_mod.shape[0]
    bp, seq, d = x_prompt.shape
    bs = x_sample.shape[0]
    assert x_sample.shape[1] == seq
    bsz = bp + bs
    x = jnp.concatenate([x_prompt, x_sample], axis=0).reshape(bsz * seq, d)
    c = jnp.concatenate([c_prompt, c_sample], axis=0)

    hyw = w_pa.shape[1]
    aw = w_pb.shape[1]
    kw = N_KV_HEADS * HEAD_DIM
    widths = (HY_ORDER + 1) * hyw, aw, kw, kw, d, d
    cblk = hyw // 256

    mod = _modulation(c, w_mod.astype(BF16), b_mod)
    fc, fs = _dft_tables(seq)

    for l in range(depth):
        mod3 = mod[l].reshape(bsz, 6, d)
        hy, q, k, v, ga, gb = _inproj(x, mod3, g_norm1[l], w_in[l].astype(BF16), seq, widths)
        hcat = _filters_time(seq, f_w1[l], f_b1[l], f_freq[l], f_w2[l], f_b2[l], f_w3[l])
        spec_a, spec_b, nyq = _filter_spectra(fc, fs, hcat)
        cw, cb = conv_w[l], conv_b[l][None, :]
        zz = _long_conv(fc, fs, hy, 0, hy, cblk, spec_a, spec_b, nyq, 0, f_bias[l][0:1], cw, cb,
                        0, cblk, bsz, seq, conv_u=True)
        ya = _long_conv(fc, fs, zz, 0, hy, 2 * cblk, spec_a, spec_b, nyq, cblk, f_bias[l][1:2], cw, cb,
                        0, 2 * cblk, bsz, seq, conv_u=False)
        yb = _attention(q, k, v, q_gain[l], k_gain[l], sink[l], bsz, seq)
        x = _merge(x, mod3, ya, yb, ga, gb, w_pa[l].astype(BF16), w_pb[l].astype(BF16), w_out[l].astype(BF16), seq)
        h2, idx, gate = _peer_score(x, mod3, g_norm2[l], peer_wq[l].astype(BF16),
                                    peer_k1[l].astype(BF16), peer_k2[l].astype(BF16), seq)
        table = _pack_expert_tables(peer_u[l], peer_v[l])
        t_tc = (bsz - SC_SEQS) * seq
        t_sc = SC_SEQS * seq
        t_g1 = ((bsz - SC_SEQS + 1) // 2) * seq
        idx3 = idx.reshape(bsz * seq, 2, SC_HALF)
        a_sc = _sc_dot(_pack_pairs(peer_u[l]), idx3, h2, t_tc, t_sc)
        x_g1 = _peer_gather(x, mod3, h2, idx, gate, table, seq, 0, t_g1, gate)
        w_sc = _gelu_gate(a_sc, gate, t_tc)
        y_sc = _sc_mix(_pack_pairs(peer_v[l]), idx3, w_sc, t_tc, t_sc, d)
        x_g2 = _peer_gather(x, mod3, h2, idx, gate, table, seq, t_g1, t_tc - t_g1, w_sc)
        x = jnp.concatenate([x_g1, x_g2, _residual(x, mod3, y_sc, t_tc, seq)], axis=0)

    x = x.reshape(bsz, seq, d)
    return x[:bp], x[bp:]
```

```python
import functools
import math

import jax
import jax.numpy as jnp
import numpy as np
from jax import lax
from jax.experimental import pallas as pl
from jax.experimental.pallas import tpu as pltpu
from jax.experimental.pallas import tpu_sc as plsc

F32 = jnp.float32
BF16 = jnp.bfloat16

EPS = 1e-6
HEAD_DIM = 64
N_Q_HEADS = 8
N_KV_HEADS = 2
WINDOW = 128
BLOCK = 128
ROPE_THETA = 10000.0
HY_ORDER = 2
N_DIR = 2
FILT_BANDS = 16
DECAY_TARGET = 1e-2
FAST_DECAY_PCT = 0.3
SLOW_DECAY_PCT = 1.5
PEER_HEADS = 8
N_KEYS = 128
PEER_TOPK = 16
PEER_QDIM = 256
LANES = 128
SUBLANES = 8
VMEM_LIMIT = 56 * 1024 * 1024
GATHER_RING = 4
GATHER_AHEAD = 2

_STAIR = [(i, j) for i in range(PEER_TOPK) for j in range(PEER_TOPK) if (i + 1) * (j + 1) <= PEER_TOPK]
_STAIR_ROWS = -(-len(_STAIR) // SUBLANES) * SUBLANES


def _cparams(sem, vmem=VMEM_LIMIT):
    return pltpu.CompilerParams(dimension_semantics=sem, vmem_limit_bytes=vmem)


def _dot(a, b):
    return jnp.dot(a, b, preferred_element_type=F32)


def _dot_nt(a, b):
    return lax.dot_general(a, b, (((1,), (1,)), ((), ())), preferred_element_type=F32)


def _dot_hi(a, b):
    return jnp.dot(a, b, preferred_element_type=F32, precision=lax.Precision.HIGHEST)


def _sigmoid(x):
    return 1.0 / (1.0 + jnp.exp(-x))


def _mod_kernel(c_ref, w_ref, b_ref, o_ref):
    c = c_ref[...]
    s = c * _sigmoid(c)
    o_ref[0] = _dot(s.astype(BF16), w_ref[0]) + b_ref[0]


def _modulation(c, w_mod_bf, b_mod):
    depth, d, n6 = w_mod_bf.shape
    bsz = c.shape[0]
    tn = 1536
    return pl.pallas_call(
        _mod_kernel,
        grid=(depth, n6 // tn),
        in_specs=[pl.BlockSpec((bsz, d), lambda l, j: (0, 0)),
                  pl.BlockSpec((1, d, tn), lambda l, j: (l, 0, j)),
                  pl.BlockSpec((1, 1, tn), lambda l, j: (l, 0, j))],
        out_specs=pl.BlockSpec((1, bsz, tn), lambda l, j: (l, 0, j)),
        out_shape=jax.ShapeDtypeStruct((depth, bsz, n6), F32),
        compiler_params=_cparams(("arbitrary", "arbitrary")),
        name="adaln_mod",
    )(c, w_mod_bf, b_mod.reshape(depth, 1, n6))


def _norm_mod(x, gain, shift, scale):
    y = x * lax.rsqrt(jnp.mean(x * x, axis=-1, keepdims=True) + EPS)
    return (y * gain) * (1.0 + scale) + shift


def _inproj_kernel(x_ref, mod_ref, g_ref, w_ref, hy_ref, q_ref, k_ref, v_ref, ga_ref, gb_ref, *, splits):
    m = mod_ref[0]
    h = _norm_mod(x_ref[...], g_ref[...], m[0:1], m[1:2])
    z = _dot(h.astype(BF16), w_ref[...])
    outs = (hy_ref, q_ref, k_ref, v_ref, ga_ref, gb_ref)
    lo = 0
    for ref, hi in zip(outs, splits):
        ref[...] = z[:, lo:hi]
        lo = hi


def _inproj(x2, mod3, gain, w_in_bf, seq, widths, tm=256):
    t, d = x2.shape
    ncols = w_in_bf.shape[1]
    per = seq // tm
    splits = tuple(int(s) for s in np.cumsum(widths))
    return pl.pallas_call(
        functools.partial(_inproj_kernel, splits=splits),
        grid=(t // tm,),
        in_specs=[pl.BlockSpec((tm, d), lambda i: (i, 0)),
                  pl.BlockSpec((1, 6, d), lambda i: (i // per, 0, 0)),
                  pl.BlockSpec((1, d), lambda i: (0, 0)),
                  pl.BlockSpec((d, ncols), lambda i: (0, 0))],
        out_specs=[pl.BlockSpec((tm, w), lambda i: (i, 0)) for w in widths],
        out_shape=[jax.ShapeDtypeStruct((t, w), F32) for w in widths],
        compiler_params=_cparams(("arbitrary",)),
        name="inproj",
    )(x2, mod3, gain.reshape(1, d), w_in_bf)


def _filter_kernel(z_ref, w1_ref, b1_ref, fr_ref, w2_ref, b2_ref, w3_ref, ad_ref, o_ref):
    z = z_ref[0]
    fr = fr_ref[...]
    a = jnp.sin(fr * (_dot_hi(z, w1_ref[...]) + b1_ref[...]))
    a = jnp.sin(fr * (_dot_hi(a, w2_ref[...]) + b2_ref[...]))
    h = _dot_hi(a, w3_ref[0])
    h = h * jnp.exp(-z[:, 0:1] * ad_ref[...])
    row = lax.broadcasted_iota(jnp.int32, h.shape, 0)
    dead = (pl.program_id(0) == 1) & (pl.program_id(1) == 0) & (row == 0)
    o_ref[0] = jnp.where(dead, 0.0, h)


def _filters_time(seq, f_w1, f_b1, f_freq, f_w2, f_b2, f_w3, tm=512):
    hidden = f_w1.shape[1]
    cw = f_w3.shape[1] // (HY_ORDER * N_DIR)
    t = jnp.linspace(0.0, 1.0, seq, dtype=F32)[:, None]
    w = 2.0 * math.pi * jnp.arange(seq, dtype=F32)[:, None] / seq
    bands = jnp.linspace(1e-4, FILT_BANDS - 1, FILT_BANDS, dtype=F32)[None, :]
    z = jnp.concatenate([t, jnp.cos(bands * w), -jnp.sin(bands * w)], axis=-1)
    emb = z.shape[1]
    z = jnp.pad(z, ((0, 0), (0, LANES - emb)))
    zcat = jnp.stack([z, jnp.concatenate([z[:1], z[:0:-1]], axis=0)], axis=0)
    w1p = jnp.pad(f_w1, ((0, LANES - emb), (0, 0)))
    w3d = f_w3.reshape(hidden, HY_ORDER, N_DIR, cw).transpose(2, 0, 1, 3).reshape(N_DIR, hidden, HY_ORDER * cw)
    max_decay = math.log(DECAY_TARGET) / FAST_DECAY_PCT
    min_decay = math.log(DECAY_TARGET) / SLOW_DECAY_PCT
    ad = jnp.abs(jnp.linspace(min_decay, max_decay, cw, dtype=F32))
    ad = jnp.tile(ad, HY_ORDER)[None, :]
    oc = HY_ORDER * cw
    return pl.pallas_call(
        _filter_kernel,
        grid=(N_DIR, seq // tm),
        in_specs=[pl.BlockSpec((1, tm, LANES), lambda g, r: (g, r, 0)),
                  pl.BlockSpec((LANES, hidden), lambda g, r: (0, 0)),
                  pl.BlockSpec((1, hidden), lambda g, r: (0, 0)),
                  pl.BlockSpec((1, hidden), lambda g, r: (0, 0)),
                  pl.BlockSpec((hidden, hidden), lambda g, r: (0, 0)),
                  pl.BlockSpec((1, hidden), lambda g, r: (0, 0)),
                  pl.BlockSpec((1, hidden, oc), lambda g, r: (g, 0, 0)),
                  pl.BlockSpec((1, oc), lambda g, r: (0, 0))],
        out_specs=pl.BlockSpec((1, tm, oc), lambda g, r: (g, r, 0)),
        out_shape=jax.ShapeDtypeStruct((N_DIR, seq, oc), F32),
        compiler_params=_cparams(("arbitrary", "arbitrary")),
        name="hyena_filter_mlp",
    )(zcat, w1p, f_b1[None, :], f_freq[None, :], f_w2, f_b2[None, :], w3d, ad)


def _dft_tables(seq):
    n2 = 2 * seq
    f = jnp.arange(seq, dtype=jnp.int32)
    ft = (f[:, None] * f[None, :]) % n2
    ang = ft.astype(F32) * (2.0 * math.pi / n2)
    return jnp.cos(ang).astype(BF16), jnp.sin(ang).astype(BF16)


def _spec_kernel(fc_ref, fs_ref, h_ref, a_ref, b_ref, nyq_ref, *, seq, tf):
    hlo = h_ref[0]
    hhi = h_ref[1]
    hlo_b = hlo.astype(BF16)
    hhi_b = hhi.astype(BF16)
    f = pl.program_id(1) * tf + lax.broadcasted_iota(jnp.int32, (tf, 1), 0)
    sgn = jnp.where(f % 2 == 0, 1.0, -1.0)
    fc = fc_ref[...]
    fs = fs_ref[...]
    hr = _dot(fc, hlo_b) + sgn * _dot(fc, hhi_b)
    hs = _dot(fs, hlo_b) + sgn * _dot(fs, hhi_b)
    w = jnp.where(f == 0, 1.0, 2.0) * (1.0 / (2 * seq))
    a_ref[...] = w * hr
    b_ref[...] = -(w * hs)
    t = lax.broadcasted_iota(jnp.int32, (seq, 1), 0)
    alt = jnp.where(t % 2 == 0, 1.0, -1.0)
    nyq = jnp.sum(alt * (hlo + hhi), axis=0, keepdims=True) * (1.0 / (2 * seq))
    nyq_ref[...] = jnp.broadcast_to(nyq, nyq_ref.shape)


def _filter_spectra(fc, fs, hcat, tf=512, tcol=512):
    seq = fc.shape[0]
    oc = hcat.shape[2]
    return pl.pallas_call(
        functools.partial(_spec_kernel, seq=seq, tf=tf),
        grid=(oc // tcol, seq // tf),
        in_specs=[pl.BlockSpec((tf, seq), lambda j, i: (i, 0)),
                  pl.BlockSpec((tf, seq), lambda j, i: (i, 0)),
                  pl.BlockSpec((2, seq, tcol), lambda j, i: (0, 0, j))],
        out_specs=[pl.BlockSpec((tf, tcol), lambda j, i: (i, j)),
                   pl.BlockSpec((tf, tcol), lambda j, i: (i, j)),
                   pl.BlockSpec((SUBLANES, tcol), lambda j, i: (0, j))],
        out_shape=[jax.ShapeDtypeStruct((seq, oc), F32),
                   jax.ShapeDtypeStruct((seq, oc), F32),
                   jax.ShapeDtypeStruct((SUBLANES, oc), F32)],
        compiler_params=_cparams(("arbitrary", "arbitrary")),
        name="hyena_filter_spectra",
    )(fc, fs, hcat)


def _shortconv(x, w_ref, b_ref, seq):
    row = lax.broadcasted_iota(jnp.int32, (seq, 1), 0)
    xm = jnp.where(row == 0, 0.0, pltpu.roll(x, 1, 0))
    xp = jnp.where(row == seq - 1, 0.0, pltpu.roll(x, seq - 1, 0))
    return xm * w_ref[0:1, :] + x * w_ref[1:2, :] + xp * w_ref[2:3, :] + b_ref[...]


def _conv_kernel(fc_ref, fs_ref, u_ref, g_ref, a_ref, b_ref, nyq_ref, bias_ref,
                 cwu_ref, cbu_ref, cwg_ref, cbg_ref, o_ref, acc_ref, *, seq, fb, conv_u):
    u = u_ref[...]
    if conv_u:
        u = _shortconv(u, cwu_ref, cbu_ref, seq)
    gate = _shortconv(g_ref[...], cwg_ref, cbg_ref, seq)
    ub = u.astype(BF16)
    for c in range(seq // fb):
        rows = slice(c * fb, (c + 1) * fb)
        ur = _dot(fc_ref[rows, :], ub)
        us = _dot(fs_ref[rows, :], ub)
        a = a_ref[rows, :]
        b = b_ref[rows, :]
        qr = (ur * a + us * b).astype(BF16)
        qi = (us * a - ur * b).astype(BF16)
        part = _dot(fc_ref[:, rows], qr) + _dot(fs_ref[:, rows], qi)
        if c == 0:
            acc_ref[...] = part
        else:
            acc_ref[...] += part
    t = lax.broadcasted_iota(jnp.int32, (seq, 1), 0)
    alt = jnp.where(t % 2 == 0, 1.0, -1.0)
    unyq = jnp.sum(alt * u, axis=0, keepdims=True)
    y = acc_ref[...] + alt * (unyq * nyq_ref[0:1, :]) + bias_ref[...] * u
    o_ref[...] = gate * y


def _long_conv(fc, fs, u_src, u_blk0, g_src, g_blk0, spec_a, spec_b, nyq, s_blk0, bias, cw, cb,
               cu_blk0, cg_blk0, bsz, seq, conv_u, tc=256, fb=512):
    nct = 512 // tc
    t = bsz * seq
    const = lambda j, b: (0, 0)
    return pl.pallas_call(
        functools.partial(_conv_kernel, seq=seq, fb=fb, conv_u=conv_u),
        grid=(nct, bsz),
        in_specs=[pl.BlockSpec(memory_space=pltpu.VMEM),
                  pl.BlockSpec(memory_space=pltpu.VMEM),
                  pl.BlockSpec((seq, tc), lambda j, b: (b, u_blk0 + j)),
                  pl.BlockSpec((seq, tc), lambda j, b: (b, g_blk0 + j)),
                  pl.BlockSpec((seq, tc), lambda j, b: (0, s_blk0 + j)),
                  pl.BlockSpec((seq, tc), lambda j, b: (0, s_blk0 + j)),
                  pl.BlockSpec((SUBLANES, tc), lambda j, b: (0, s_blk0 + j)),
                  pl.BlockSpec((1, tc), lambda j, b: (0, j)),
                  pl.BlockSpec((3, tc), lambda j, b: (0, cu_blk0 + j)),
                  pl.BlockSpec((1, tc), lambda j, b: (0, cu_blk0 + j)),
                  pl.BlockSpec((3, tc), lambda j, b: (0, cg_blk0 + j)),
                  pl.BlockSpec((1, tc), lambda j, b: (0, cg_blk0 + j))],
        out_specs=pl.BlockSpec((seq, tc), lambda j, b: (b, j)),
        out_shape=jax.ShapeDtypeStruct((t, 512), F32),
        scratch_shapes=[pltpu.VMEM((seq, tc), F32)],
        compiler_params=_cparams(("arbitrary", "arbitrary")),
        name="hyena_long_conv_u" if conv_u else "hyena_long_conv",
    )(fc, fs, u_src, g_src, spec_a, spec_b, nyq, bias, cw, cb, cw, cb)


def _attn_kernel(q_ref, k_ref, v_ref, cos_ref, sin_ref, qg_ref, kg_ref, sink_ref, o_ref,
                 qn_ref, km_ref, vm_ref, *, seq):
    lane = lax.broadcasted_iota(jnp.int32, (1, LANES), 1)
    r = lax.broadcasted_iota(jnp.int32, (LANES, LANES), 0) // HEAD_DIM
    c = lax.broadcasted_iota(jnp.int32, (LANES, LANES), 1) // HEAD_DIM
    bd = jnp.where(r == c, 1.0, 0.0).astype(BF16)
    first_half = (lane % HEAD_DIM) < (HEAD_DIM // 2)
    cos = cos_ref[...]
    sin = sin_ref[...]

    def norm_rope(x, gain):
        sq = x * x
        hi = sq.astype(BF16)
        lo = (sq - hi.astype(F32)).astype(BF16)
        ss = _dot(hi, bd) + _dot(lo, bd)
        y = (x * lax.rsqrt(ss * (1.0 / HEAD_DIM) + EPS)) * gain
        partner = jnp.where(first_half, pltpu.roll(y, LANES - HEAD_DIM // 2, 1), pltpu.roll(y, HEAD_DIM // 2, 1))
        return y * cos + partner * sin

    for p in range(N_Q_HEADS // 2):
        cols = slice(p * LANES, (p + 1) * LANES)
        qn_ref[:, cols] = norm_rope(q_ref[:, cols], qg_ref[...]).astype(BF16)
    kn = norm_rope(k_ref[...], kg_ref[...])
    left = lane < HEAD_DIM
    for src_ref, dst_ref in ((None, km_ref), (v_ref, vm_ref)):
        val = kn if src_ref is None else src_ref[...]
        rolled = pltpu.roll(val, HEAD_DIM, 1)
        dst_ref[0] = jnp.where(left, val, 0.0).astype(BF16)
        dst_ref[1] = jnp.where(left, 0.0, rolled).astype(BF16)
        dst_ref[2] = jnp.where(left, rolled, 0.0).astype(BF16)
        dst_ref[3] = jnp.where(left, 0.0, val).astype(BF16)

    span = 3 * BLOCK
    scale = HEAD_DIM ** -0.5
    ii = lax.broadcasted_iota(jnp.int32, (BLOCK, span), 0)
    jj = lax.broadcasted_iota(jnp.int32, (BLOCK, span), 1)

    def block(n, carry):
        q0 = pl.multiple_of(n * BLOCK, BLOCK)
        start = pl.multiple_of(jnp.clip((n - 1) * BLOCK, 0, seq - span), BLOCK)
        valid = jnp.abs((start - q0) + jj - ii) <= WINDOW
        for p in range(N_Q_HEADS // 2):
            cols = slice(p * LANES, (p + 1) * LANES)
            kv = (2 * p) // (N_Q_HEADS // N_KV_HEADS)
            qp = qn_ref[pl.ds(q0, BLOCK), cols]
            o = jnp.zeros((BLOCK, LANES), F32)
            for a in range(2):
                h = 2 * p + a
                kb = km_ref[2 * kv + a, pl.ds(start, span), :]
                s = _dot_nt(qp, kb) * scale
                s = jnp.where(valid, s, -jnp.inf)
                sk = sink_ref[h:h + 1, 0:1]
                m = jnp.maximum(jnp.max(s, axis=-1, keepdims=True), sk)
                e = jnp.exp(s - m)
                den = jnp.sum(e, axis=-1, keepdims=True) + jnp.exp(sk - m)
                pn = (e / den).astype(BF16)
                o = o + _dot(pn, vm_ref[2 * kv + a, pl.ds(start, span), :])
            o_ref[pl.ds(q0, BLOCK), cols] = o
        return carry

    lax.fori_loop(0, seq // BLOCK, block, 0)


def _attention(q, k, v, q_gain, k_gain, sink, bsz, seq):
    inv = ROPE_THETA ** (-jnp.arange(0, HEAD_DIM, 2, dtype=F32) / HEAD_DIM)
    ang = jnp.arange(seq, dtype=F32)[:, None] * inv[None, :]
    cos = jnp.tile(jnp.cos(ang), (1, LANES // (HEAD_DIM // 2)))
    sn = jnp.sin(ang)
    sin = jnp.tile(jnp.concatenate([-sn, sn], axis=1), (1, LANES // HEAD_DIM))
    qg = jnp.tile(q_gain, LANES // HEAD_DIM)[None, :]
    kg = jnp.tile(k_gain, LANES // HEAD_DIM)[None, :]
    sinkb = jnp.broadcast_to(sink[:, None], (N_Q_HEADS, LANES))
    aw = N_Q_HEADS * HEAD_DIM
    kw = N_KV_HEADS * HEAD_DIM
    const = lambda b: (0, 0)
    return pl.pallas_call(
        functools.partial(_attn_kernel, seq=seq),
        grid=(bsz,),
        in_specs=[pl.BlockSpec((seq, aw), lambda b: (b, 0)),
                  pl.BlockSpec((seq, kw), lambda b: (b, 0)),
                  pl.BlockSpec((seq, kw), lambda b: (b, 0)),
                  pl.BlockSpec((seq, LANES), const),
                  pl.BlockSpec((seq, LANES), const),
                  pl.BlockSpec((1, LANES), const),
                  pl.BlockSpec((1, LANES), const),
                  pl.BlockSpec((N_Q_HEADS, LANES), const)],
        out_specs=pl.BlockSpec((seq, aw), lambda b: (b, 0)),
        out_shape=jax.ShapeDtypeStruct((bsz * seq, aw), F32),
        scratch_shapes=[pltpu.VMEM((seq, aw), BF16),
                        pltpu.VMEM((4, seq, LANES), BF16),
                        pltpu.VMEM((4, seq, LANES), BF16)],
        compiler_params=_cparams(("arbitrary",)),
        name="window_attention",
    )(q, k, v, cos, sin, qg, kg, sinkb)


def _merge_kernel(x_ref, mod_ref, ya_ref, yb_ref, ga_ref, gb_ref, wpa_ref, wpb_ref, wo_ref, o_ref):
    pa = _dot(ya_ref[...].astype(BF16), wpa_ref[...])
    pb = _dot(yb_ref[...].astype(BF16), wpb_ref[...])
    merged = _sigmoid(ga_ref[...]) * pa + _sigmoid(gb_ref[...]) * pb
    out = _dot(merged.astype(BF16), wo_ref[...])
    o_ref[...] = x_ref[...] + mod_ref[0][2:3] * out


def _merge(x2, mod3, ya, yb, ga, gb, w_pa_bf, w_pb_bf, w_out_bf, seq, tm=512):
    t, d = x2.shape
    per = seq // tm
    row = lambda w: pl.BlockSpec((tm, w), lambda i: (i, 0))
    full = lambda a: pl.BlockSpec(a.shape, lambda i: (0, 0))
    return pl.pallas_call(
        _merge_kernel,
        grid=(t // tm,),
        in_specs=[row(d), pl.BlockSpec((1, 6, d), lambda i: (i // per, 0, 0)),
                  row(ya.shape[1]), row(yb.shape[1]), row(d), row(d),
                  full(w_pa_bf), full(w_pb_bf), full(w_out_bf)],
        out_specs=row(d),
        out_shape=jax.ShapeDtypeStruct((t, d), F32),
        compiler_params=_cparams(("arbitrary",)),
        name="merge_outproj",
    )(x2, mod3, ya, yb, ga, gb, w_pa_bf, w_pb_bf, w_out_bf)


def _topk_rows(s, k, val_ref, idx_ref):
    n = s.shape[0]
    row = lax.broadcasted_iota(jnp.int32, s.shape, 0)
    for i in range(k):
        m = jnp.max(s, axis=0, keepdims=True)
        first = jnp.min(jnp.where(s == m, row, n), axis=0, keepdims=True)
        val_ref[i:i + 1, :] = m
        idx_ref[i:i + 1, :] = first
        s = jnp.where(row == first, -jnp.inf, s)


def _peer_score_kernel(x_ref, mod_ref, g_ref, wq_ref, k1_ref, k2_ref, h_ref, idx_ref, gate_ref,
                       v1_ref, i1_ref, v2_ref, i2_ref, cv_ref, ce_ref, it_ref, gt_ref, *, tm):
    m = mod_ref[0]
    h = _norm_mod(x_ref[...], g_ref[...], m[3:4], m[4:5])
    h_ref[...] = h
    q = _dot(h.astype(BF16), wq_ref[...]).astype(BF16)
    half = PEER_QDIM // 2
    neg = jnp.full((1, tm), -jnp.inf, F32)
    for hd in range(PEER_HEADS):
        q1 = q[:, hd * PEER_QDIM: hd * PEER_QDIM + half]
        q2 = q[:, hd * PEER_QDIM + half: (hd + 1) * PEER_QDIM]
        _topk_rows(_dot_nt(k1_ref[...], q1), PEER_TOPK, v1_ref, i1_ref)
        _topk_rows(_dot_nt(k2_ref[...], q2), PEER_TOPK, v2_ref, i2_ref)
        for r, (i, j) in enumerate(_STAIR):
            cv_ref[r:r + 1, :] = v1_ref[i:i + 1, :] + v2_ref[j:j + 1, :]
            ce_ref[r:r + 1, :] = i1_ref[i:i + 1, :] * N_KEYS + i2_ref[j:j + 1, :]
        for r in range(len(_STAIR), _STAIR_ROWS):
            cv_ref[r:r + 1, :] = neg
            ce_ref[r:r + 1, :] = jnp.zeros((1, tm), jnp.int32)
        cand = cv_ref[...]
        ce = ce_ref[...]
        row = lax.broadcasted_iota(jnp.int32, cand.shape, 0)
        vals = []
        for kk in range(PEER_TOPK):
            mx = jnp.max(cand, axis=0, keepdims=True)
            first = jnp.min(jnp.where(cand == mx, row, _STAIR_ROWS), axis=0, keepdims=True)
            sel = row == first
            it_ref[hd * PEER_TOPK + kk: hd * PEER_TOPK + kk + 1, :] = jnp.sum(jnp.where(sel, ce, 0), axis=0, keepdims=True)
            vals.append(mx)
            cand = jnp.where(sel, -jnp.inf, cand)
        ex = [jnp.exp(v - vals[0]) for v in vals]
        tot = ex[0]
        for e in ex[1:]:
            tot = tot + e
        for kk in range(PEER_TOPK):
            gt_ref[hd * PEER_TOPK + kk: hd * PEER_TOPK + kk + 1, :] = ex[kk] / tot
    idx_ref[...] = it_ref[...].T
    gate_ref[...] = gt_ref[...].T


def _peer_score(x2, mod3, gain, wq_bf, k1_bf, k2_bf, seq, tm=256):
    t, d = x2.shape
    per = seq // tm
    slots = PEER_HEADS * PEER_TOPK
    full = lambda a: pl.BlockSpec(a.shape, lambda i: (0, 0))
    return pl.pallas_call(
        functools.partial(_peer_score_kernel, tm=tm),
        grid=(t // tm,),
        in_specs=[pl.BlockSpec((tm, d), lambda i: (i, 0)),
                  pl.BlockSpec((1, 6, d), lambda i: (i // per, 0, 0)),
                  pl.BlockSpec((1, d), lambda i: (0, 0)),
                  full(wq_bf), full(k1_bf), full(k2_bf)],
        out_specs=[pl.BlockSpec((tm, d), lambda i: (i, 0)),
                   pl.BlockSpec((tm, slots), lambda i: (i, 0)),
                   pl.BlockSpec((tm, slots), lambda i: (i, 0))],
        out_shape=[jax.ShapeDtypeStruct((t, d), F32),
                   jax.ShapeDtypeStruct((t, slots), jnp.int32),
                   jax.ShapeDtypeStruct((t, slots), F32)],
        scratch_shapes=[pltpu.VMEM((PEER_TOPK, tm), F32), pltpu.VMEM((PEER_TOPK, tm), jnp.int32),
                        pltpu.VMEM((PEER_TOPK, tm), F32), pltpu.VMEM((PEER_TOPK, tm), jnp.int32),
                        pltpu.VMEM((_STAIR_ROWS, tm), F32), pltpu.VMEM((_STAIR_ROWS, tm), jnp.int32),
                        pltpu.VMEM((slots, tm), jnp.int32), pltpu.VMEM((slots, tm), F32)],
        compiler_params=_cparams(("arbitrary",)),
        name="peer_score_topk",
    )(x2, mod3, gain.reshape(1, d), wq_bf, k1_bf, k2_bf)


def _peer_gather_kernel(idx_ref, idxn_ref, tab_ref, h_ref, gate_ref, x_ref, mod_ref, after_ref, o_ref,
                        buf_ref, sem_ref, *, tc, d, nsteps):
    slots = PEER_HEADS * PEER_TOPK
    step = pl.program_id(0)

    def issue_token(chunk, t):
        ids_ref, c = (idx_ref, chunk) if chunk < GATHER_RING else (idxn_ref, chunk - GATHER_RING)
        for j in range(slots):
            pltpu.make_async_copy(tab_ref.at[ids_ref[c * tc + t, j]],
                                  buf_ref.at[c, t, pl.ds(j, 1), :],
                                  sem_ref.at[c]).start(priority=j % 2)

    def wait(slot):
        pltpu.make_async_copy(buf_ref.at[slot], buf_ref.at[slot], sem_ref.at[slot]).wait()

    eye = (lax.broadcasted_iota(jnp.int32, (slots, slots), 0) ==
           lax.broadcasted_iota(jnp.int32, (slots, slots), 1))
    gt2 = mod_ref[0][5:6]

    def mix_token(slot, t):
        r = slot * tc + t
        word = buf_ref[slot, t]
        u = lax.bitcast_convert_type(word << 16, F32)
        prod = u * h_ref[r:r + 1, :]
        part = prod[:, 0:LANES]
        for c in range(1, d // LANES):
            part = part + prod[:, c * LANES:(c + 1) * LANES]
        a = jnp.sum(part, axis=1, keepdims=True)
        grow = jnp.broadcast_to(gate_ref[r:r + 1, :], (slots, slots))
        gcol = jnp.sum(jnp.where(eye, grow, 0.0), axis=1, keepdims=True)
        w = gcol * (0.5 * a * (1.0 + lax.erf(a * (2.0 ** -0.5))))
        v = lax.bitcast_convert_type(buf_ref[slot, t] & jnp.uint32(0xFFFF0000), F32)
        y = jnp.sum(v * w, axis=0, keepdims=True)
        o_ref[r:r + 1, :] = x_ref[r:r + 1, :] + gt2 * y

    @pl.when(step == 0)
    def _():
        for c in range(GATHER_AHEAD):
            for t in range(tc):
                issue_token(c, t)

    for p in range(GATHER_RING):
        wait(p)
        for t in range(tc):
            issue_token(p + GATHER_AHEAD, t)
            mix_token(p, t)

    @pl.when(step == nsteps - 1)
    def _():
        for c in range(GATHER_AHEAD):
            wait(c)


def _peer_gather(x2, mod3, h2, idx, gate, table, seq, tok0, ntok, after, tc=8):
    d = x2.shape[1]
    slots = idx.shape[1]
    tb = GATHER_RING * tc
    nsteps = ntok // tb
    off = tok0 // tb
    per = seq // tb
    row = lambda w: pl.BlockSpec((tb, w), lambda i: (i + off, 0))
    return pl.pallas_call(
        functools.partial(_peer_gather_kernel, tc=tc, d=d, nsteps=nsteps),
        grid=(nsteps,),
        in_specs=[pl.BlockSpec((tb, slots), lambda i: (i + off, 0), memory_space=pltpu.SMEM),
                  pl.BlockSpec((tb, slots), lambda i: (jnp.minimum(i + 1, nsteps - 1) + off, 0),
                               memory_space=pltpu.SMEM),
                  pl.BlockSpec(memory_space=pl.ANY),
                  row(d), row(slots), row(d),
                  pl.BlockSpec((1, 6, d), lambda i: ((i + off) // per, 0, 0)),
                  pl.BlockSpec(memory_space=pl.ANY)],
        out_specs=pl.BlockSpec((tb, d), lambda i: (i, 0)),
        out_shape=jax.ShapeDtypeStruct((ntok, d), F32),
        scratch_shapes=[pltpu.VMEM((GATHER_RING, tc, slots, d), jnp.uint32),
                        pltpu.SemaphoreType.DMA((GATHER_RING,))],
        compiler_params=_cparams(("arbitrary",)),
        name="peer_gather_mix",
    )(idx, idx, table.reshape(table.shape[0], 1, d), h2, gate, x2, mod3, after)


def _pack_expert_tables(u_tab, v_tab):
    ub = lax.bitcast_convert_type(u_tab.astype(BF16), jnp.uint16).astype(jnp.uint32)
    vb = lax.bitcast_convert_type(v_tab.astype(BF16), jnp.uint16).astype(jnp.uint32)
    return (vb << 16) | ub


SC_LANES = 16
SC_CORES = 2
SC_SUBCORES = 16
SC_GROUP = 8
SC_HALF = 64
SC_SEQS = 10


def _pack_pairs(tab):
    e, d = tab.shape
    bits = lax.bitcast_convert_type(tab.astype(BF16), jnp.uint16).astype(jnp.uint32)
    bits = bits.reshape(e, d // (2 * SC_LANES), 2, SC_LANES)
    return ((bits[:, :, 1, :] << 16) | bits[:, :, 0, :]).reshape(e, d // 2)


def _sc_worker_base(n_per):
    return (lax.axis_index("s") * SC_CORES + lax.axis_index("c")) * n_per


def _sc_halves(tab_hbm, idx_v, rows_v, sems, tt, half):
    return pltpu.make_async_copy(tab_hbm.at[idx_v.at[tt, half]], rows_v.at[half], sems.at[half])


def _sc_token_loop(tab_hbm, idx_v, rows_v, sems, compute_half):
    _sc_halves(tab_hbm, idx_v, rows_v, sems, 0, 0).start()

    def tok(tt, carry):
        _sc_halves(tab_hbm, idx_v, rows_v, sems, tt, 1).start()
        _sc_halves(tab_hbm, idx_v, rows_v, sems, tt, 0).wait()
        compute_half(tt, 0)

        @pl.when(tt + 1 < SC_GROUP)
        def _():
            _sc_halves(tab_hbm, idx_v, rows_v, sems, tt + 1, 0).start()

        _sc_halves(tab_hbm, idx_v, rows_v, sems, tt, 1).wait()
        compute_half(tt, 1)
        return carry

    lax.fori_loop(0, SC_GROUP, tok, 0)


def _unpack_pair(word):
    lo = lax.bitcast_convert_type(word << 16, F32)
    hi = lax.bitcast_convert_type(word & jnp.uint32(0xFFFF0000), F32)
    return lo, hi


def _sc_dot_body(tab_hbm, idx_hbm, h_hbm, a_hbm, idx_v, h_v, rows_v, a_v, sems, *, tok0, n_per, d):
    base = _sc_worker_base(n_per)
    lane = lax.iota(jnp.int32, SC_LANES)
    nk = d // (2 * SC_LANES)

    def compute_half(tt, half):
        def rows16(gi, carry):
            avec = jnp.zeros((SC_LANES,), F32)
            for rr in range(0, SC_LANES, 4):
                accs = [jnp.zeros((SC_LANES,), F32) for _ in range(4)]
                for k in range(nk):
                    hlo = h_v[tt, pl.ds(2 * SC_LANES * k, SC_LANES)]
                    hhi = h_v[tt, pl.ds(2 * SC_LANES * k + SC_LANES, SC_LANES)]
                    for q in range(4):
                        lo, hi = _unpack_pair(rows_v[half, gi * SC_LANES + rr + q, pl.ds(SC_LANES * k, SC_LANES)])
                        accs[q] = accs[q] + lo * hlo + hi * hhi
                for q in range(4):
                    avec = jnp.where(lane == rr + q, jnp.sum(accs[q]), avec)
            a_v[tt, pl.ds(half * SC_HALF + gi * SC_LANES, SC_LANES)] = avec
            return carry
        lax.fori_loop(0, SC_HALF // SC_LANES, rows16, 0)

    def group(g, carry):
        loc = base + g * SC_GROUP
        pltpu.sync_copy(idx_hbm.at[pl.ds(tok0 + loc, SC_GROUP)], idx_v)
        pltpu.sync_copy(h_hbm.at[pl.ds(tok0 + loc, SC_GROUP)], h_v)
        _sc_token_loop(tab_hbm, idx_v, rows_v, sems, compute_half)
        pltpu.sync_copy(a_v, a_hbm.at[pl.ds(loc, SC_GROUP)])
        return carry

    lax.fori_loop(0, n_per // SC_GROUP, group, 0)


def _sc_mix_body(tab_hbm, idx_hbm, w_hbm, y_hbm, idx_v, w_v, rows_v, y_v, sems, *, tok0, n_per, d):
    base = _sc_worker_base(n_per)
    nk = d // (2 * SC_LANES)
    nq = 2
    kq = nk // nq

    def compute_half(tt, half):
        ttv = jnp.full((SC_LANES,), tt, jnp.int32)
        for piece in range(nq):
            col0 = piece * kq * 2 * SC_LANES
            if half == 0:
                accs = tuple(jnp.zeros((SC_LANES,), F32) for _ in range(2 * kq))
            else:
                accs = tuple(y_v[tt, pl.ds(col0 + SC_LANES * m, SC_LANES)] for m in range(2 * kq))

            def row(r, accs):
                wsplat = plsc.load_gather(w_v, [ttv, jnp.full((SC_LANES,), half * SC_HALF + r, jnp.int32)])
                out = []
                for k in range(kq):
                    lo, hi = _unpack_pair(rows_v[half, r, pl.ds(SC_LANES * (piece * kq + k), SC_LANES)])
                    out.append(accs[2 * k] + wsplat * lo)
                    out.append(accs[2 * k + 1] + wsplat * hi)
                return tuple(out)

            accs = lax.fori_loop(0, SC_HALF, row, accs)
            for m in range(2 * kq):
                y_v[tt, pl.ds(col0 + SC_LANES * m, SC_LANES)] = accs[m]

    def group(g, carry):
        loc = base + g * SC_GROUP
        pltpu.sync_copy(idx_hbm.at[pl.ds(tok0 + loc, SC_GROUP)], idx_v)
        pltpu.sync_copy(w_hbm.at[pl.ds(loc, SC_GROUP)], w_v)
        _sc_token_loop(tab_hbm, idx_v, rows_v, sems, compute_half)
        pltpu.sync_copy(y_v, y_hbm.at[pl.ds(loc, SC_GROUP)])
        return carry

    lax.fori_loop(0, n_per // SC_GROUP, group, 0)


def _sc_mesh():
    return plsc.VectorSubcoreMesh(core_axis_name="c", subcore_axis_name="s")


def _sc_dot(u_words, idx3, h2, tok0, ntok):
    d = h2.shape[1]
    slots = idx3.shape[1] * idx3.shape[2]
    n_per = ntok // (SC_CORES * SC_SUBCORES)
    return pl.kernel(
        functools.partial(_sc_dot_body, tok0=tok0, n_per=n_per, d=d),
        out_type=jax.ShapeDtypeStruct((ntok, slots), F32),
        mesh=_sc_mesh(),
        scratch_types=[pltpu.VMEM((SC_GROUP, 2, SC_HALF), jnp.int32),
                       pltpu.VMEM((SC_GROUP, d), F32),
                       pltpu.VMEM((2, SC_HALF, d // 2), jnp.uint32),
                       pltpu.VMEM((SC_GROUP, slots), F32),
                       pltpu.SemaphoreType.DMA((2,))],
        compiler_params=pltpu.CompilerParams(needs_layout_passes=False),
        name="peer_sc_dot",
    )(u_words, idx3, h2)


def _sc_mix(v_words, idx3, w, tok0, ntok, d):
    slots = idx3.shape[1] * idx3.shape[2]
    n_per = ntok // (SC_CORES * SC_SUBCORES)
    return pl.kernel(
        functools.partial(_sc_mix_body, tok0=tok0, n_per=n_per, d=d),
        out_type=jax.ShapeDtypeStruct((ntok, d), F32),
        mesh=_sc_mesh(),
        scratch_types=[pltpu.VMEM((SC_GROUP, 2, SC_HALF), jnp.int32),
                       pltpu.VMEM((SC_GROUP, slots), F32),
                       pltpu.VMEM((2, SC_HALF, d // 2), jnp.uint32),
                       pltpu.VMEM((SC_GROUP, d), F32),
                       pltpu.SemaphoreType.DMA((2,))],
        compiler_params=pltpu.CompilerParams(needs_layout_passes=False),
        name="peer_sc_mix",
    )(v_words, idx3, w)


def _gelu_gate_kernel(a_ref, g_ref, after_ref, o_ref):
    a = a_ref[...]
    o_ref[...] = g_ref[...] * (0.5 * a * (1.0 + lax.erf(a * (2.0 ** -0.5))))


def _gelu_gate(a, gate, tok0, after, tm=2048):
    n, slots = a.shape
    off = tok0 // tm
    return pl.pallas_call(
        _gelu_gate_kernel,
        grid=(n // tm,),
        in_specs=[pl.BlockSpec((tm, slots), lambda i: (i, 0)),
                  pl.BlockSpec((tm, slots), lambda i: (i + off, 0)),
                  pl.BlockSpec(memory_space=pl.ANY)],
        out_specs=pl.BlockSpec((tm, slots), lambda i: (i, 0)),
        out_shape=jax.ShapeDtypeStruct((n, slots), F32),
        compiler_params=_cparams(("arbitrary",)),
        name="peer_gelu_gate",
    )(a, gate, after)


def _residual_kernel(x_ref, mod_ref, y_ref, o_ref):
    o_ref[...] = x_ref[...] + mod_ref[0][5:6] * y_ref[...]


def _residual(x2, mod3, y, tok0, seq, tm=512):
    n, d = y.shape
    off = tok0 // tm
    per = seq // tm
    return pl.pallas_call(
        _residual_kernel,
        grid=(n // tm,),
        in_specs=[pl.BlockSpec((tm, d), lambda i: (i + off, 0)),
                  pl.BlockSpec((1, 6, d), lambda i: ((i + off) // per, 0, 0)),
                  pl.BlockSpec((tm, d), lambda i: (i, 0))],
        out_specs=pl.BlockSpec((tm, d), lambda i: (i, 0)),
        out_shape=jax.ShapeDtypeStruct((n, d), F32),
        compiler_params=_cparams(("arbitrary",)),
        name="peer_residual",
    )(x2, mod3, y)


def kernel(x_prompt, x_sample, c_prompt, c_sample, w_mod, b_mod, g_norm1, g_norm2, w_in, conv_w, conv_b, f_w1, f_b1, f_freq, f_w2, f_b2, f_w3, f_bias, q_gain, k_gain, sink, w_pa, w_pb, w_out, peer_wq, peer_k1, peer_k2, peer_u, peer_v):
    depth = w_mod.shape[0]
    bp, seq, d = x_prompt.shape
    bs = x_sample.shape[0]
    assert x_sample.shape[1] == seq
    bsz = bp + bs
    x = jnp.concatenate([x_prompt, x_sample], axis=0).reshape(bsz * seq, d)
    c = jnp.concatenate([c_prompt, c_sample], axis=0)

    hyw = w_pa.shape[1]
    aw = w_pb.shape[1]
    kw = N_KV_HEADS * HEAD_DIM
    widths = (HY_ORDER + 1) * hyw, aw, kw, kw, d, d
    cblk = hyw // 256

    mod = _modulation(c, w_mod.astype(BF16), b_mod)
    fc, fs = _dft_tables(seq)

    for l in range(depth):
        mod3 = mod[l].reshape(bsz, 6, d)
        hy, q, k, v, ga, gb = _inproj(x, mod3, g_norm1[l], w_in[l].astype(BF16), seq, widths)
        hcat = _filters_time(seq, f_w1[l], f_b1[l], f_freq[l], f_w2[l], f_b2[l], f_w3[l])
        spec_a, spec_b, nyq = _filter_spectra(fc, fs, hcat)
        cw, cb = conv_w[l], conv_b[l][None, :]
        zz = _long_conv(fc, fs, hy, 0, hy, cblk, spec_a, spec_b, nyq, 0, f_bias[l][0:1], cw, cb,
                        0, cblk, bsz, seq, conv_u=True)
        ya = _long_conv(fc, fs, zz, 0, hy, 2 * cblk, spec_a, spec_b, nyq, cblk, f_bias[l][1:2], cw, cb,
                        0, 2 * cblk, bsz, seq, conv_u=False)
        yb = _attention(q, k, v, q_gain[l], k_gain[l], sink[l], bsz, seq)
        x = _merge(x, mod3, ya, yb, ga, gb, w_pa[l].astype(BF16), w_pb[l].astype(BF16), w_out[l].astype(BF16), seq)
        h2, idx, gate = _peer_score(x, mod3, g_norm2[l], peer_wq[l].astype(BF16),
                                    peer_k1[l].astype(BF16), peer_k2[l].astype(BF16), seq)
        table = _pack_expert_tables(peer_u[l], peer_v[l])
        t_tc = (bsz - SC_SEQS) * seq
        t_sc = SC_SEQS * seq
        t_g1 = ((bsz - SC_SEQS + 1) // 2) * seq
        idx3 = idx.reshape(bsz * seq, 2, SC_HALF)
        a_sc = _sc_dot(_pack_pairs(peer_u[l]), idx3, h2, t_tc, t_sc)
        x_g1 = _peer_gather(x, mod3, h2, idx, gate, table, seq, 0, t_g1, gate)
        w_sc = _gelu_gate(a_sc, gate, t_tc, x_g1)
        y_sc = _sc_mix(_pack_pairs(peer_v[l]), idx3, w_sc, t_tc, t_sc, d)
        x_g2 = _peer_gather(x, mod3, h2, idx, gate, table, seq, t_g1, t_tc - t_g1, w_sc)
        x = jnp.concatenate([x_g1, x_g2, _residual(x, mod3, y_sc, t_tc, seq)], axis=0)

    x = x.reshape(bsz, seq, d)
    return x[:bp], x[bp:]
```

```python
import functools
import math

import jax
import jax.numpy as jnp
import numpy as np
from jax import lax
from jax.experimental import pallas as pl
from jax.experimental.pallas import tpu as pltpu
from jax.experimental.pallas import tpu_sc as plsc

F32 = jnp.float32
BF16 = jnp.bfloat16

EPS = 1e-6
HEAD_DIM = 64
N_Q_HEADS = 8
N_KV_HEADS = 2
WINDOW = 128
BLOCK = 128
ROPE_THETA = 10000.0
HY_ORDER = 2
N_DIR = 2
FILT_BANDS = 16
DECAY_TARGET = 1e-2
FAST_DECAY_PCT = 0.3
SLOW_DECAY_PCT = 1.5
PEER_HEADS = 8
N_KEYS = 128
PEER_TOPK = 16
PEER_QDIM = 256
LANES = 128
SUBLANES = 8
VMEM_LIMIT = 56 * 1024 * 1024
GATHER_RING = 4
GATHER_AHEAD = 2

_STAIR = [(i, j) for i in range(PEER_TOPK) for j in range(PEER_TOPK) if (i + 1) * (j + 1) <= PEER_TOPK]
_STAIR_ROWS = -(-len(_STAIR) // SUBLANES) * SUBLANES


def _cparams(sem, vmem=VMEM_LIMIT):
    return pltpu.CompilerParams(dimension_semantics=sem, vmem_limit_bytes=vmem)


def _dot(a, b):
    return jnp.dot(a, b, preferred_element_type=F32)


def _dot_nt(a, b):
    return lax.dot_general(a, b, (((1,), (1,)), ((), ())), preferred_element_type=F32)


def _dot_hi(a, b):
    return jnp.dot(a, b, preferred_element_type=F32, precision=lax.Precision.HIGHEST)


def _sigmoid(x):
    return 1.0 / (1.0 + jnp.exp(-x))


def _mod_kernel(c_ref, w_ref, b_ref, o_ref):
    c = c_ref[...]
    s = c * _sigmoid(c)
    o_ref[0] = _dot(s.astype(BF16), w_ref[0]) + b_ref[0]


def _modulation(c, w_mod_bf, b_mod):
    depth, d, n6 = w_mod_bf.shape
    bsz = c.shape[0]
    tn = 1536
    return pl.pallas_call(
        _mod_kernel,
        grid=(depth, n6 // tn),
        in_specs=[pl.BlockSpec((bsz, d), lambda l, j: (0, 0)),
                  pl.BlockSpec((1, d, tn), lambda l, j: (l, 0, j)),
                  pl.BlockSpec((1, 1, tn), lambda l, j: (l, 0, j))],
        out_specs=pl.BlockSpec((1, bsz, tn), lambda l, j: (l, 0, j)),
        out_shape=jax.ShapeDtypeStruct((depth, bsz, n6), F32),
        compiler_params=_cparams(("arbitrary", "arbitrary")),
        name="adaln_mod",
    )(c, w_mod_bf, b_mod.reshape(depth, 1, n6))


def _norm_mod(x, gain, shift, scale):
    y = x * lax.rsqrt(jnp.mean(x * x, axis=-1, keepdims=True) + EPS)
    return (y * gain) * (1.0 + scale) + shift


def _inproj_kernel(x_ref, mod_ref, g_ref, w_ref, hy_ref, q_ref, k_ref, v_ref, ga_ref, gb_ref, *, splits):
    m = mod_ref[0]
    h = _norm_mod(x_ref[...], g_ref[...], m[0:1], m[1:2])
    z = _dot(h.astype(BF16), w_ref[...])
    outs = (hy_ref, q_ref, k_ref, v_ref, ga_ref, gb_ref)
    lo = 0
    for ref, hi in zip(outs, splits):
        ref[...] = z[:, lo:hi]
        lo = hi


def _inproj(x2, mod3, gain, w_in_bf, seq, widths, tm=256):
    t, d = x2.shape
    ncols = w_in_bf.shape[1]
    per = seq // tm
    splits = tuple(int(s) for s in np.cumsum(widths))
    return pl.pallas_call(
        functools.partial(_inproj_kernel, splits=splits),
        grid=(t // tm,),
        in_specs=[pl.BlockSpec((tm, d), lambda i: (i, 0)),
                  pl.BlockSpec((1, 6, d), lambda i: (i // per, 0, 0)),
                  pl.BlockSpec((1, d), lambda i: (0, 0)),
                  pl.BlockSpec((d, ncols), lambda i: (0, 0))],
        out_specs=[pl.BlockSpec((tm, w), lambda i: (i, 0)) for w in widths],
        out_shape=[jax.ShapeDtypeStruct((t, w), F32) for w in widths],
        compiler_params=_cparams(("arbitrary",)),
        name="inproj",
    )(x2, mod3, gain.reshape(1, d), w_in_bf)


def _filter_kernel(z_ref, w1_ref, b1_ref, fr_ref, w2_ref, b2_ref, w3_ref, ad_ref, o_ref):
    z = z_ref[0]
    fr = fr_ref[...]
    a = jnp.sin(fr * (_dot_hi(z, w1_ref[...]) + b1_ref[...]))
    a = jnp.sin(fr * (_dot_hi(a, w2_ref[...]) + b2_ref[...]))
    h = _dot_hi(a, w3_ref[0])
    h = h * jnp.exp(-z[:, 0:1] * ad_ref[...])
    row = lax.broadcasted_iota(jnp.int32, h.shape, 0)
    dead = (pl.program_id(0) == 1) & (pl.program_id(1) == 0) & (row == 0)
    o_ref[0] = jnp.where(dead, 0.0, h)


def _filters_time(seq, f_w1, f_b1, f_freq, f_w2, f_b2, f_w3, tm=512):
    hidden = f_w1.shape[1]
    cw = f_w3.shape[1] // (HY_ORDER * N_DIR)
    t = jnp.linspace(0.0, 1.0, seq, dtype=F32)[:, None]
    w = 2.0 * math.pi * jnp.arange(seq, dtype=F32)[:, None] / seq
    bands = jnp.linspace(1e-4, FILT_BANDS - 1, FILT_BANDS, dtype=F32)[None, :]
    z = jnp.concatenate([t, jnp.cos(bands * w), -jnp.sin(bands * w)], axis=-1)
    emb = z.shape[1]
    z = jnp.pad(z, ((0, 0), (0, LANES - emb)))
    zcat = jnp.stack([z, jnp.concatenate([z[:1], z[:0:-1]], axis=0)], axis=0)
    w1p = jnp.pad(f_w1, ((0, LANES - emb), (0, 0)))
    w3d = f_w3.reshape(hidden, HY_ORDER, N_DIR, cw).transpose(2, 0, 1, 3).reshape(N_DIR, hidden, HY_ORDER * cw)
    max_decay = math.log(DECAY_TARGET) / FAST_DECAY_PCT
    min_decay = math.log(DECAY_TARGET) / SLOW_DECAY_PCT
    ad = jnp.abs(jnp.linspace(min_decay, max_decay, cw, dtype=F32))
    ad = jnp.tile(ad, HY_ORDER)[None, :]
    oc = HY_ORDER * cw
    return pl.pallas_call(
        _filter_kernel,
        grid=(N_DIR, seq // tm),
        in_specs=[pl.BlockSpec((1, tm, LANES), lambda g, r: (g, r, 0)),
                  pl.BlockSpec((LANES, hidden), lambda g, r: (0, 0)),
                  pl.BlockSpec((1, hidden), lambda g, r: (0, 0)),
                  pl.BlockSpec((1, hidden), lambda g, r: (0, 0)),
                  pl.BlockSpec((hidden, hidden), lambda g, r: (0, 0)),
                  pl.BlockSpec((1, hidden), lambda g, r: (0, 0)),
                  pl.BlockSpec((1, hidden, oc), lambda g, r: (g, 0, 0)),
                  pl.BlockSpec((1, oc), lambda g, r: (0, 0))],
        out_specs=pl.BlockSpec((1, tm, oc), lambda g, r: (g, r, 0)),
        out_shape=jax.ShapeDtypeStruct((N_DIR, seq, oc), F32),
        compiler_params=_cparams(("arbitrary", "arbitrary")),
        name="hyena_filter_mlp",
    )(zcat, w1p, f_b1[None, :], f_freq[None, :], f_w2, f_b2[None, :], w3d, ad)


def _dft_tables(seq):
    n2 = 2 * seq
    f = jnp.arange(seq, dtype=jnp.int32)
    ft = (f[:, None] * f[None, :]) % n2
    ang = ft.astype(F32) * (2.0 * math.pi / n2)
    return jnp.cos(ang).astype(BF16), jnp.sin(ang).astype(BF16)


def _spec_kernel(fc_ref, fs_ref, h_ref, a_ref, b_ref, nyq_ref, *, seq, tf):
    hlo = h_ref[0]
    hhi = h_ref[1]
    hlo_b = hlo.astype(BF16)
    hhi_b = hhi.astype(BF16)
    f = pl.program_id(1) * tf + lax.broadcasted_iota(jnp.int32, (tf, 1), 0)
    sgn = jnp.where(f % 2 == 0, 1.0, -1.0)
    fc = fc_ref[...]
    fs = fs_ref[...]
    hr = _dot(fc, hlo_b) + sgn * _dot(fc, hhi_b)
    hs = _dot(fs, hlo_b) + sgn * _dot(fs, hhi_b)
    w = jnp.where(f == 0, 1.0, 2.0) * (1.0 / (2 * seq))
    a_ref[...] = w * hr
    b_ref[...] = -(w * hs)
    t = lax.broadcasted_iota(jnp.int32, (seq, 1), 0)
    alt = jnp.where(t % 2 == 0, 1.0, -1.0)
    nyq = jnp.sum(alt * (hlo + hhi), axis=0, keepdims=True) * (1.0 / (2 * seq))
    nyq_ref[...] = jnp.broadcast_to(nyq, nyq_ref.shape)


def _filter_spectra(fc, fs, hcat, tf=512, tcol=512):
    seq = fc.shape[0]
    oc = hcat.shape[2]
    return pl.pallas_call(
        functools.partial(_spec_kernel, seq=seq, tf=tf),
        grid=(oc // tcol, seq // tf),
        in_specs=[pl.BlockSpec((tf, seq), lambda j, i: (i, 0)),
                  pl.BlockSpec((tf, seq), lambda j, i: (i, 0)),
                  pl.BlockSpec((2, seq, tcol), lambda j, i: (0, 0, j))],
        out_specs=[pl.BlockSpec((tf, tcol), lambda j, i: (i, j)),
                   pl.BlockSpec((tf, tcol), lambda j, i: (i, j)),
                   pl.BlockSpec((SUBLANES, tcol), lambda j, i: (0, j))],
        out_shape=[jax.ShapeDtypeStruct((seq, oc), F32),
                   jax.ShapeDtypeStruct((seq, oc), F32),
                   jax.ShapeDtypeStruct((SUBLANES, oc), F32)],
        compiler_params=_cparams(("arbitrary", "arbitrary")),
        name="hyena_filter_spectra",
    )(fc, fs, hcat)


def _shortconv(x, w_ref, b_ref, seq):
    row = lax.broadcasted_iota(jnp.int32, (seq, 1), 0)
    xm = jnp.where(row == 0, 0.0, pltpu.roll(x, 1, 0))
    xp = jnp.where(row == seq - 1, 0.0, pltpu.roll(x, seq - 1, 0))
    return xm * w_ref[0:1, :] + x * w_ref[1:2, :] + xp * w_ref[2:3, :] + b_ref[...]


def _conv_kernel(fc_ref, fs_ref, u_ref, g_ref, a_ref, b_ref, nyq_ref, bias_ref,
                 cwu_ref, cbu_ref, cwg_ref, cbg_ref, o_ref, acc_ref, *, seq, fb, conv_u):
    u = u_ref[...]
    if conv_u:
        u = _shortconv(u, cwu_ref, cbu_ref, seq)
    gate = _shortconv(g_ref[...], cwg_ref, cbg_ref, seq)
    ub = u.astype(BF16)
    for c in range(seq // fb):
        rows = slice(c * fb, (c + 1) * fb)
        ur = _dot(fc_ref[rows, :], ub)
        us = _dot(fs_ref[rows, :], ub)
        a = a_ref[rows, :]
        b = b_ref[rows, :]
        qr = (ur * a + us * b).astype(BF16)
        qi = (us * a - ur * b).astype(BF16)
        part = _dot(fc_ref[:, rows], qr) + _dot(fs_ref[:, rows], qi)
        if c == 0:
            acc_ref[...] = part
        else:
            acc_ref[...] += part
    t = lax.broadcasted_iota(jnp.int32, (seq, 1), 0)
    alt = jnp.where(t % 2 == 0, 1.0, -1.0)
    unyq = jnp.sum(alt * u, axis=0, keepdims=True)
    y = acc_ref[...] + alt * (unyq * nyq_ref[0:1, :]) + bias_ref[...] * u
    o_ref[...] = gate * y


def _long_conv(fc, fs, u_src, u_blk0, g_src, g_blk0, spec_a, spec_b, nyq, s_blk0, bias, cw, cb,
               cu_blk0, cg_blk0, bsz, seq, conv_u, tc=256, fb=512):
    nct = 512 // tc
    t = bsz * seq
    const = lambda j, b: (0, 0)
    return pl.pallas_call(
        functools.partial(_conv_kernel, seq=seq, fb=fb, conv_u=conv_u),
        grid=(nct, bsz),
        in_specs=[pl.BlockSpec(memory_space=pltpu.VMEM),
                  pl.BlockSpec(memory_space=pltpu.VMEM),
                  pl.BlockSpec((seq, tc), lambda j, b: (b, u_blk0 + j)),
                  pl.BlockSpec((seq, tc), lambda j, b: (b, g_blk0 + j)),
                  pl.BlockSpec((seq, tc), lambda j, b: (0, s_blk0 + j)),
                  pl.BlockSpec((seq, tc), lambda j, b: (0, s_blk0 + j)),
                  pl.BlockSpec((SUBLANES, tc), lambda j, b: (0, s_blk0 + j)),
                  pl.BlockSpec((1, tc), lambda j, b: (0, j)),
                  pl.BlockSpec((3, tc), lambda j, b: (0, cu_blk0 + j)),
                  pl.BlockSpec((1, tc), lambda j, b: (0, cu_blk0 + j)),
                  pl.BlockSpec((3, tc), lambda j, b: (0, cg_blk0 + j)),
                  pl.BlockSpec((1, tc), lambda j, b: (0, cg_blk0 + j))],
        out_specs=pl.BlockSpec((seq, tc), lambda j, b: (b, j)),
        out_shape=jax.ShapeDtypeStruct((t, 512), F32),
        scratch_shapes=[pltpu.VMEM((seq, tc), F32)],
        compiler_params=_cparams(("arbitrary", "arbitrary")),
        name="hyena_long_conv_u" if conv_u else "hyena_long_conv",
    )(fc, fs, u_src, g_src, spec_a, spec_b, nyq, bias, cw, cb, cw, cb)


def _attn_kernel(q_ref, k_ref, v_ref, cos_ref, sin_ref, qg_ref, kg_ref, sink_ref, o_ref,
                 qn_ref, km_ref, vm_ref, *, seq):
    lane = lax.broadcasted_iota(jnp.int32, (1, LANES), 1)
    r = lax.broadcasted_iota(jnp.int32, (LANES, LANES), 0) // HEAD_DIM
    c = lax.broadcasted_iota(jnp.int32, (LANES, LANES), 1) // HEAD_DIM
    bd = jnp.where(r == c, 1.0, 0.0).astype(BF16)
    first_half = (lane % HEAD_DIM) < (HEAD_DIM // 2)
    cos = cos_ref[...]
    sin = sin_ref[...]

    def norm_rope(x, gain):
        sq = x * x
        hi = sq.astype(BF16)
        lo = (sq - hi.astype(F32)).astype(BF16)
        ss = _dot(hi, bd) + _dot(lo, bd)
        y = (x * lax.rsqrt(ss * (1.0 / HEAD_DIM) + EPS)) * gain
        partner = jnp.where(first_half, pltpu.roll(y, LANES - HEAD_DIM // 2, 1), pltpu.roll(y, HEAD_DIM // 2, 1))
        return y * cos + partner * sin

    for p in range(N_Q_HEADS // 2):
        cols = slice(p * LANES, (p + 1) * LANES)
        qn_ref[:, cols] = norm_rope(q_ref[:, cols], qg_ref[...]).astype(BF16)
    kn = norm_rope(k_ref[...], kg_ref[...])
    left = lane < HEAD_DIM
    for src_ref, dst_ref in ((None, km_ref), (v_ref, vm_ref)):
        val = kn if src_ref is None else src_ref[...]
        rolled = pltpu.roll(val, HEAD_DIM, 1)
        dst_ref[0] = jnp.where(left, val, 0.0).astype(BF16)
        dst_ref[1] = jnp.where(left, 0.0, rolled).astype(BF16)
        dst_ref[2] = jnp.where(left, rolled, 0.0).astype(BF16)
        dst_ref[3] = jnp.where(left, 0.0, val).astype(BF16)

    span = 3 * BLOCK
    scale = HEAD_DIM ** -0.5
    ii = lax.broadcasted_iota(jnp.int32, (BLOCK, span), 0)
    jj = lax.broadcasted_iota(jnp.int32, (BLOCK, span), 1)

    def block(n, carry):
        q0 = pl.multiple_of(n * BLOCK, BLOCK)
        start = pl.multiple_of(jnp.clip((n - 1) * BLOCK, 0, seq - span), BLOCK)
        valid = jnp.abs((start - q0) + jj - ii) <= WINDOW
        for p in range(N_Q_HEADS // 2):
            cols = slice(p * LANES, (p + 1) * LANES)
            kv = (2 * p) // (N_Q_HEADS // N_KV_HEADS)
            qp = qn_ref[pl.ds(q0, BLOCK), cols]
            o = jnp.zeros((BLOCK, LANES), F32)
            for a in range(2):
                h = 2 * p + a
                kb = km_ref[2 * kv + a, pl.ds(start, span), :]
                s = _dot_nt(qp, kb) * scale
                s = jnp.where(valid, s, -jnp.inf)
                sk = sink_ref[h:h + 1, 0:1]
                m = jnp.maximum(jnp.max(s, axis=-1, keepdims=True), sk)
                e = jnp.exp(s - m)
                den = jnp.sum(e, axis=-1, keepdims=True) + jnp.exp(sk - m)
                pn = (e / den).astype(BF16)
                o = o + _dot(pn, vm_ref[2 * kv + a, pl.ds(start, span), :])
            o_ref[pl.ds(q0, BLOCK), cols] = o
        return carry

    lax.fori_loop(0, seq // BLOCK, block, 0)


def _attention(q, k, v, q_gain, k_gain, sink, bsz, seq):
    inv = ROPE_THETA ** (-jnp.arange(0, HEAD_DIM, 2, dtype=F32) / HEAD_DIM)
    ang = jnp.arange(seq, dtype=F32)[:, None] * inv[None, :]
    cos = jnp.tile(jnp.cos(ang), (1, LANES // (HEAD_DIM // 2)))
    sn = jnp.sin(ang)
    sin = jnp.tile(jnp.concatenate([-sn, sn], axis=1), (1, LANES // HEAD_DIM))
    qg = jnp.tile(q_gain, LANES // HEAD_DIM)[None, :]
    kg = jnp.tile(k_gain, LANES // HEAD_DIM)[None, :]
    sinkb = jnp.broadcast_to(sink[:, None], (N_Q_HEADS, LANES))
    aw = N_Q_HEADS * HEAD_DIM
    kw = N_KV_HEADS * HEAD_DIM
    const = lambda b: (0, 0)
    return pl.pallas_call(
        functools.partial(_attn_kernel, seq=seq),
        grid=(bsz,),
        in_specs=[pl.BlockSpec((seq, aw), lambda b: (b, 0)),
                  pl.BlockSpec((seq, kw), lambda b: (b, 0)),
                  pl.BlockSpec((seq, kw), lambda b: (b, 0)),
                  pl.BlockSpec((seq, LANES), const),
                  pl.BlockSpec((seq, LANES), const),
                  pl.BlockSpec((1, LANES), const),
                  pl.BlockSpec((1, LANES), const),
                  pl.BlockSpec((N_Q_HEADS, LANES), const)],
        out_specs=pl.BlockSpec((seq, aw), lambda b: (b, 0)),
        out_shape=jax.ShapeDtypeStruct((bsz * seq, aw), F32),
        scratch_shapes=[pltpu.VMEM((seq, aw), BF16),
                        pltpu.VMEM((4, seq, LANES), BF16),
                        pltpu.VMEM((4, seq, LANES), BF16)],
        compiler_params=_cparams(("arbitrary",)),
        name="window_attention",
    )(q, k, v, cos, sin, qg, kg, sinkb)


def _merge_kernel(x_ref, mod_ref, ya_ref, yb_ref, ga_ref, gb_ref, wpa_ref, wpb_ref, wo_ref, o_ref):
    pa = _dot(ya_ref[...].astype(BF16), wpa_ref[...])
    pb = _dot(yb_ref[...].astype(BF16), wpb_ref[...])
    merged = _sigmoid(ga_ref[...]) * pa + _sigmoid(gb_ref[...]) * pb
    out = _dot(merged.astype(BF16), wo_ref[...])
    o_ref[...] = x_ref[...] + mod_ref[0][2:3] * out


def _merge(x2, mod3, ya, yb, ga, gb, w_pa_bf, w_pb_bf, w_out_bf, seq, tm=512):
    t, d = x2.shape
    per = seq // tm
    row = lambda w: pl.BlockSpec((tm, w), lambda i: (i, 0))
    full = lambda a: pl.BlockSpec(a.shape, lambda i: (0, 0))
    return pl.pallas_call(
        _merge_kernel,
        grid=(t // tm,),
        in_specs=[row(d), pl.BlockSpec((1, 6, d), lambda i: (i // per, 0, 0)),
                  row(ya.shape[1]), row(yb.shape[1]), row(d), row(d),
                  full(w_pa_bf), full(w_pb_bf), full(w_out_bf)],
        out_specs=row(d),
        out_shape=jax.ShapeDtypeStruct((t, d), F32),
        compiler_params=_cparams(("arbitrary",)),
        name="merge_outproj",
    )(x2, mod3, ya, yb, ga, gb, w_pa_bf, w_pb_bf, w_out_bf)


def _topk_rows(s, k, val_ref, idx_ref):
    n = s.shape[0]
    row = lax.broadcasted_iota(jnp.int32, s.shape, 0)
    for i in range(k):
        m = jnp.max(s, axis=0, keepdims=True)
        first = jnp.min(jnp.where(s == m, row, n), axis=0, keepdims=True)
        val_ref[i:i + 1, :] = m
        idx_ref[i:i + 1, :] = first
        s = jnp.where(row == first, -jnp.inf, s)


def _peer_score_kernel(x_ref, mod_ref, g_ref, wq_ref, k1_ref, k2_ref, h_ref, idx_ref, gate_ref,
                       v1_ref, i1_ref, v2_ref, i2_ref, cv_ref, ce_ref, it_ref, gt_ref, *, tm):
    m = mod_ref[0]
    h = _norm_mod(x_ref[...], g_ref[...], m[3:4], m[4:5])
    h_ref[...] = h
    q = _dot(h.astype(BF16), wq_ref[...]).astype(BF16)
    half = PEER_QDIM // 2
    neg = jnp.full((1, tm), -jnp.inf, F32)
    for hd in range(PEER_HEADS):
        q1 = q[:, hd * PEER_QDIM: hd * PEER_QDIM + half]
        q2 = q[:, hd * PEER_QDIM + half: (hd + 1) * PEER_QDIM]
        _topk_rows(_dot_nt(k1_ref[...], q1), PEER_TOPK, v1_ref, i1_ref)
        _topk_rows(_dot_nt(k2_ref[...], q2), PEER_TOPK, v2_ref, i2_ref)
        for r, (i, j) in enumerate(_STAIR):
            cv_ref[r:r + 1, :] = v1_ref[i:i + 1, :] + v2_ref[j:j + 1, :]
            ce_ref[r:r + 1, :] = i1_ref[i:i + 1, :] * N_KEYS + i2_ref[j:j + 1, :]
        for r in range(len(_STAIR), _STAIR_ROWS):
            cv_ref[r:r + 1, :] = neg
            ce_ref[r:r + 1, :] = jnp.zeros((1, tm), jnp.int32)
        cand = cv_ref[...]
        ce = ce_ref[...]
        row = lax.broadcasted_iota(jnp.int32, cand.shape, 0)
        vals = []
        for kk in range(PEER_TOPK):
            mx = jnp.max(cand, axis=0, keepdims=True)
            first = jnp.min(jnp.where(cand == mx, row, _STAIR_ROWS), axis=0, keepdims=True)
            sel = row == first
            it_ref[hd * PEER_TOPK + kk: hd * PEER_TOPK + kk + 1, :] = jnp.sum(jnp.where(sel, ce, 0), axis=0, keepdims=True)
            vals.append(mx)
            cand = jnp.where(sel, -jnp.inf, cand)
        ex = [jnp.exp(v - vals[0]) for v in vals]
        tot = ex[0]
        for e in ex[1:]:
            tot = tot + e
        for kk in range(PEER_TOPK):
            gt_ref[hd * PEER_TOPK + kk: hd * PEER_TOPK + kk + 1, :] = ex[kk] / tot
    idx_ref[...] = it_ref[...].T
    gate_ref[...] = gt_ref[...].T


def _peer_score(x2, mod3, gain, wq_bf, k1_bf, k2_bf, seq, tm=256):
    t, d = x2.shape
    per = seq // tm
    slots = PEER_HEADS * PEER_TOPK
    full = lambda a: pl.BlockSpec(a.shape, lambda i: (0, 0))
    return pl.pallas_call(
        functools.partial(_peer_score_kernel, tm=tm),
        grid=(t // tm,),
        in_specs=[pl.BlockSpec((tm, d), lambda i: (i, 0)),
                  pl.BlockSpec((1, 6, d), lambda i: (i // per, 0, 0)),
                  pl.BlockSpec((1, d), lambda i: (0, 0)),
                  full(wq_bf), full(k1_bf), full(k2_bf)],
        out_specs=[pl.BlockSpec((tm, d), lambda i: (i, 0)),
                   pl.BlockSpec((tm, slots), lambda i: (i, 0)),
                   pl.BlockSpec((tm, slots), lambda i: (i, 0))],
        out_shape=[jax.ShapeDtypeStruct((t, d), F32),
                   jax.ShapeDtypeStruct((t, slots), jnp.int32),
                   jax.ShapeDtypeStruct((t, slots), F32)],
        scratch_shapes=[pltpu.VMEM((PEER_TOPK, tm), F32), pltpu.VMEM((PEER_TOPK, tm), jnp.int32),
                        pltpu.VMEM((PEER_TOPK, tm), F32), pltpu.VMEM((PEER_TOPK, tm), jnp.int32),
                        pltpu.VMEM((_STAIR_ROWS, tm), F32), pltpu.VMEM((_STAIR_ROWS, tm), jnp.int32),
                        pltpu.VMEM((slots, tm), jnp.int32), pltpu.VMEM((slots, tm), F32)],
        compiler_params=_cparams(("arbitrary",)),
        name="peer_score_topk",
    )(x2, mod3, gain.reshape(1, d), wq_bf, k1_bf, k2_bf)


def _peer_gather_kernel(idx_ref, idxn_ref, tab_ref, h_ref, gate_ref, x_ref, mod_ref, after_ref, o_ref,
                        buf_ref, sem_ref, *, tc, d, nsteps):
    slots = PEER_HEADS * PEER_TOPK
    step = pl.program_id(0)

    def issue_token(chunk, t):
        ids_ref, c = (idx_ref, chunk) if chunk < GATHER_RING else (idxn_ref, chunk - GATHER_RING)
        for j in range(slots):
            pltpu.make_async_copy(tab_ref.at[ids_ref[c * tc + t, j]],
                                  buf_ref.at[c, t, pl.ds(j, 1), :],
                                  sem_ref.at[c]).start(priority=j % 2)

    def wait(slot):
        pltpu.make_async_copy(buf_ref.at[slot], buf_ref.at[slot], sem_ref.at[slot]).wait()

    eye = (lax.broadcasted_iota(jnp.int32, (slots, slots), 0) ==
           lax.broadcasted_iota(jnp.int32, (slots, slots), 1))
    gt2 = mod_ref[0][5:6]

    def mix_token(slot, t):
        r = slot * tc + t
        word = buf_ref[slot, t]
        u = lax.bitcast_convert_type(word << 16, F32)
        prod = u * h_ref[r:r + 1, :]
        part = prod[:, 0:LANES]
        for c in range(1, d // LANES):
            part = part + prod[:, c * LANES:(c + 1) * LANES]
        a = jnp.sum(part, axis=1, keepdims=True)
        grow = jnp.broadcast_to(gate_ref[r:r + 1, :], (slots, slots))
        gcol = jnp.sum(jnp.where(eye, grow, 0.0), axis=1, keepdims=True)
        w = gcol * (0.5 * a * (1.0 + lax.erf(a * (2.0 ** -0.5))))
        v = lax.bitcast_convert_type(buf_ref[slot, t] & jnp.uint32(0xFFFF0000), F32)
        y = jnp.sum(v * w, axis=0, keepdims=True)
        o_ref[r:r + 1, :] = x_ref[r:r + 1, :] + gt2 * y

    @pl.when(step == 0)
    def _():
        for c in range(GATHER_AHEAD):
            for t in range(tc):
                issue_token(c, t)

    for p in range(GATHER_RING):
        wait(p)
        for t in range(tc):
            issue_token(p + GATHER_AHEAD, t)
            mix_token(p, t)

    @pl.when(step == nsteps - 1)
    def _():
        for c in range(GATHER_AHEAD):
            wait(c)


def _peer_gather(x2, mod3, h2, idx, gate, table, seq, tok0, ntok, after, tc=8):
    d = x2.shape[1]
    slots = idx.shape[1]
    tb = GATHER_RING * tc
    nsteps = ntok // tb
    off = tok0 // tb
    per = seq // tb
    row = lambda w: pl.BlockSpec((tb, w), lambda i: (i + off, 0))
    return pl.pallas_call(
        functools.partial(_peer_gather_kernel, tc=tc, d=d, nsteps=nsteps),
        grid=(nsteps,),
        in_specs=[pl.BlockSpec((tb, slots), lambda i: (i + off, 0), memory_space=pltpu.SMEM),
                  pl.BlockSpec((tb, slots), lambda i: (jnp.minimum(i + 1, nsteps - 1) + off, 0),
                               memory_space=pltpu.SMEM),
                  pl.BlockSpec(memory_space=pl.ANY),
                  row(d), row(slots), row(d),
                  pl.BlockSpec((1, 6, d), lambda i: ((i + off) // per, 0, 0)),
                  pl.BlockSpec(memory_space=pl.ANY)],
        out_specs=pl.BlockSpec((tb, d), lambda i: (i, 0)),
        out_shape=jax.ShapeDtypeStruct((ntok, d), F32),
        scratch_shapes=[pltpu.VMEM((GATHER_RING, tc, slots, d), jnp.uint32),
                        pltpu.SemaphoreType.DMA((GATHER_RING,))],
        compiler_params=_cparams(("arbitrary",)),
        name="peer_gather_mix",
    )(idx, idx, table.reshape(table.shape[0], 1, d), h2, gate, x2, mod3, after)


def _pack_expert_tables(u_tab, v_tab):
    ub = lax.bitcast_convert_type(u_tab.astype(BF16), jnp.uint16).astype(jnp.uint32)
    vb = lax.bitcast_convert_type(v_tab.astype(BF16), jnp.uint16).astype(jnp.uint32)
    return (vb << 16) | ub


SC_LANES = 16
SC_CORES = 2
SC_SUBCORES = 16
SC_GROUP = 8
SC_HALF = 64
SC_SEQS = 12


def _pack_pairs(tab):
    e, d = tab.shape
    bits = lax.bitcast_convert_type(tab.astype(BF16), jnp.uint16).astype(jnp.uint32)
    bits = bits.reshape(e, d // (2 * SC_LANES), 2, SC_LANES)
    return ((bits[:, :, 1, :] << 16) | bits[:, :, 0, :]).reshape(e, d // 2)


def _sc_worker_base(n_per):
    return (lax.axis_index("s") * SC_CORES + lax.axis_index("c")) * n_per


def _sc_halves(tab_hbm, idx_v, rows_v, sems, tt, half):
    ids = idx_v.at[tt, pl.ds(half * SC_HALF, SC_HALF)]
    return pltpu.make_async_copy(tab_hbm.at[ids], rows_v.at[half], sems.at[half])


def _sc_token_loop(tab_hbm, idx_v, rows_v, sems, compute_half):
    _sc_halves(tab_hbm, idx_v, rows_v, sems, 0, 0).start()

    def tok(tt, carry):
        _sc_halves(tab_hbm, idx_v, rows_v, sems, tt, 1).start()
        _sc_halves(tab_hbm, idx_v, rows_v, sems, tt, 0).wait()
        compute_half(tt, 0)

        @pl.when(tt + 1 < SC_GROUP)
        def _():
            _sc_halves(tab_hbm, idx_v, rows_v, sems, tt + 1, 0).start()

        _sc_halves(tab_hbm, idx_v, rows_v, sems, tt, 1).wait()
        compute_half(tt, 1)
        return carry

    lax.fori_loop(0, SC_GROUP, tok, 0)


def _unpack_pair(word):
    lo = lax.bitcast_convert_type(word << 16, F32)
    hi = lax.bitcast_convert_type(word & jnp.uint32(0xFFFF0000), F32)
    return lo, hi


def _sc_dot_body(tab_hbm, idx_hbm, h_hbm, a_hbm, idx_v, h_v, rows_v, a_v, tr_v, sems, *, tok0, n_per, d):
    base = _sc_worker_base(n_per)
    lane = lax.iota(jnp.int32, SC_LANES)
    nk = d // (2 * SC_LANES)

    def compute_half(tt, half):
        def rows16(gi, carry):
            accs = [jnp.zeros((SC_LANES,), F32) for _ in range(SC_LANES)]
            for k in range(nk):
                hlo = h_v[tt, pl.ds(2 * SC_LANES * k, SC_LANES)]
                hhi = h_v[tt, pl.ds(2 * SC_LANES * k + SC_LANES, SC_LANES)]
                for q in range(SC_LANES):
                    lo, hi = _unpack_pair(rows_v[half, gi * SC_LANES + q, pl.ds(SC_LANES * k, SC_LANES)])
                    accs[q] = accs[q] + lo * hlo + hi * hhi
            for q in range(SC_LANES):
                tr_v[q, :] = accs[q]
            tot = plsc.load_gather(tr_v, [lane, jnp.zeros((SC_LANES,), jnp.int32)])
            for c in range(1, SC_LANES):
                tot = tot + plsc.load_gather(tr_v, [lane, jnp.full((SC_LANES,), c, jnp.int32)])
            a_v[tt, pl.ds(half * SC_HALF + gi * SC_LANES, SC_LANES)] = tot
            return carry
        lax.fori_loop(0, SC_HALF // SC_LANES, rows16, 0)

    def group(g, carry):
        loc = base + g * SC_GROUP
        pltpu.sync_copy(idx_hbm.at[pl.ds(tok0 + loc, SC_GROUP)], idx_v)
        pltpu.sync_copy(h_hbm.at[pl.ds(tok0 + loc, SC_GROUP)], h_v)
        _sc_token_loop(tab_hbm, idx_v, rows_v, sems, compute_half)
        pltpu.sync_copy(a_v, a_hbm.at[pl.ds(loc, SC_GROUP)])
        return carry

    lax.fori_loop(0, n_per // SC_GROUP, group, 0)


def _sc_mix_body(tab_hbm, idx_hbm, w_hbm, y_hbm, idx_v, w_v, rows_v, y_v, sems, *, tok0, n_per, d):
    base = _sc_worker_base(n_per)
    nk = d // (2 * SC_LANES)
    nq = 2
    kq = nk // nq

    def compute_half(tt, half):
        ttv = jnp.full((SC_LANES,), tt, jnp.int32)
        for piece in range(nq):
            col0 = piece * kq * 2 * SC_LANES
            if half == 0:
                accs = tuple(jnp.zeros((SC_LANES,), F32) for _ in range(2 * kq))
            else:
                accs = tuple(y_v[tt, pl.ds(col0 + SC_LANES * m, SC_LANES)] for m in range(2 * kq))

            def row(r, accs):
                wsplat = plsc.load_gather(w_v, [ttv, jnp.full((SC_LANES,), half * SC_HALF + r, jnp.int32)])
                out = []
                for k in range(kq):
                    lo, hi = _unpack_pair(rows_v[half, r, pl.ds(SC_LANES * (piece * kq + k), SC_LANES)])
                    out.append(accs[2 * k] + wsplat * lo)
                    out.append(accs[2 * k + 1] + wsplat * hi)
                return tuple(out)

            accs = lax.fori_loop(0, SC_HALF, row, accs)
            for m in range(2 * kq):
                y_v[tt, pl.ds(col0 + SC_LANES * m, SC_LANES)] = accs[m]

    def group(g, carry):
        loc = base + g * SC_GROUP
        pltpu.sync_copy(idx_hbm.at[pl.ds(tok0 + loc, SC_GROUP)], idx_v)
        pltpu.sync_copy(w_hbm.at[pl.ds(loc, SC_GROUP)], w_v)
        _sc_token_loop(tab_hbm, idx_v, rows_v, sems, compute_half)
        pltpu.sync_copy(y_v, y_hbm.at[pl.ds(loc, SC_GROUP)])
        return carry

    lax.fori_loop(0, n_per // SC_GROUP, group, 0)


def _sc_mesh():
    return plsc.VectorSubcoreMesh(core_axis_name="c", subcore_axis_name="s")


def _sc_dot(u_words, idx, h2, tok0, ntok):
    d = h2.shape[1]
    slots = idx.shape[1]
    n_per = ntok // (SC_CORES * SC_SUBCORES)
    return pl.kernel(
        functools.partial(_sc_dot_body, tok0=tok0, n_per=n_per, d=d),
        out_type=jax.ShapeDtypeStruct((ntok, slots), F32),
        mesh=_sc_mesh(),
        scratch_types=[pltpu.VMEM((SC_GROUP, slots), jnp.int32),
                       pltpu.VMEM((SC_GROUP, d), F32),
                       pltpu.VMEM((2, SC_HALF, d // 2), jnp.uint32),
                       pltpu.VMEM((SC_GROUP, slots), F32),
                       pltpu.VMEM((SC_LANES, SC_LANES), F32),
                       pltpu.SemaphoreType.DMA((2,))],
        compiler_params=pltpu.CompilerParams(needs_layout_passes=False),
        name="peer_sc_dot",
    )(u_words, idx, h2)


def _sc_mix(v_words, idx, w, tok0, ntok, d):
    slots = idx.shape[1]
    n_per = ntok // (SC_CORES * SC_SUBCORES)
    return pl.kernel(
        functools.partial(_sc_mix_body, tok0=tok0, n_per=n_per, d=d),
        out_type=jax.ShapeDtypeStruct((ntok, d), F32),
        mesh=_sc_mesh(),
        scratch_types=[pltpu.VMEM((SC_GROUP, slots), jnp.int32),
                       pltpu.VMEM((SC_GROUP, slots), F32),
                       pltpu.VMEM((2, SC_HALF, d // 2), jnp.uint32),
                       pltpu.VMEM((SC_GROUP, d), F32),
                       pltpu.SemaphoreType.DMA((2,))],
        compiler_params=pltpu.CompilerParams(needs_layout_passes=False),
        name="peer_sc_mix",
    )(v_words, idx, w)


def _gelu_gate_kernel(a_ref, g_ref, after_ref, o_ref):
    a = a_ref[...]
    o_ref[...] = g_ref[...] * (0.5 * a * (1.0 + lax.erf(a * (2.0 ** -0.5))))


def _gelu_gate(a, gate, tok0, after, tm=2048):
    n, slots = a.shape
    off = tok0 // tm
    return pl.pallas_call(
        _gelu_gate_kernel,
        grid=(n // tm,),
        in_specs=[pl.BlockSpec((tm, slots), lambda i: (i, 0)),
                  pl.BlockSpec((tm, slots), lambda i: (i + off, 0)),
                  pl.BlockSpec(memory_space=pl.ANY)],
        out_specs=pl.BlockSpec((tm, slots), lambda i: (i, 0)),
        out_shape=jax.ShapeDtypeStruct((n, slots), F32),
        compiler_params=_cparams(("arbitrary",)),
        name="peer_gelu_gate",
    )(a, gate, after)


def _residual_kernel(x_ref, mod_ref, y_ref, o_ref):
    o_ref[...] = x_ref[...] + mod_ref[0][5:6] * y_ref[...]


def _residual(x2, mod3, y, tok0, seq, tm=512):
    n, d = y.shape
    off = tok0 // tm
    per = seq // tm
    return pl.pallas_call(
        _residual_kernel,
        grid=(n // tm,),
        in_specs=[pl.BlockSpec((tm, d), lambda i: (i + off, 0)),
                  pl.BlockSpec((1, 6, d), lambda i: ((i + off) // per, 0, 0)),
                  pl.BlockSpec((tm, d), lambda i: (i, 0))],
        out_specs=pl.BlockSpec((tm, d), lambda i: (i, 0)),
        out_shape=jax.ShapeDtypeStruct((n, d), F32),
        compiler_params=_cparams(("arbitrary",)),
        name="peer_residual",
    )(x2, mod3, y)


def kernel(x_prompt, x_sample, c_prompt, c_sample, w_mod, b_mod, g_norm1, g_norm2, w_in, conv_w, conv_b, f_w1, f_b1, f_freq, f_w2, f_b2, f_w3, f_bias, q_gain, k_gain, sink, w_pa, w_pb, w_out, peer_wq, peer_k1, peer_k2, peer_u, peer_v):
    depth = w_mod.shape[0]
    bp, seq, d = x_prompt.shape
    bs = x_sample.shape[0]
    assert x_sample.shape[1] == seq
    bsz = bp + bs
    x = jnp.concatenate([x_prompt, x_sample], axis=0).reshape(bsz * seq, d)
    c = jnp.concatenate([c_prompt, c_sample], axis=0)

    hyw = w_pa.shape[1]
    aw = w_pb.shape[1]
    kw = N_KV_HEADS * HEAD_DIM
    widths = (HY_ORDER + 1) * hyw, aw, kw, kw, d, d
    cblk = hyw // 256

    mod = _modulation(c, w_mod.astype(BF16), b_mod)
    fc, fs = _dft_tables(seq)

    for l in range(depth):
        mod3 = mod[l].reshape(bsz, 6, d)
        hy, q, k, v, ga, gb = _inproj(x, mod3, g_norm1[l], w_in[l].astype(BF16), seq, widths)
        hcat = _filters_time(seq, f_w1[l], f_b1[l], f_freq[l], f_w2[l], f_b2[l], f_w3[l])
        spec_a, spec_b, nyq = _filter_spectra(fc, fs, hcat)
        cw, cb = conv_w[l], conv_b[l][None, :]
        zz = _long_conv(fc, fs, hy, 0, hy, cblk, spec_a, spec_b, nyq, 0, f_bias[l][0:1], cw, cb,
                        0, cblk, bsz, seq, conv_u=True)
        ya = _long_conv(fc, fs, zz, 0, hy, 2 * cblk, spec_a, spec_b, nyq, cblk, f_bias[l][1:2], cw, cb,
                        0, 2 * cblk, bsz, seq, conv_u=False)
        yb = _attention(q, k, v, q_gain[l], k_gain[l], sink[l], bsz, seq)
        x = _merge(x, mod3, ya, yb, ga, gb, w_pa[l].astype(BF16), w_pb[l].astype(BF16), w_out[l].astype(BF16), seq)
        h2, idx, gate = _peer_score(x, mod3, g_norm2[l], peer_wq[l].astype(BF16),
                                    peer_k1[l].astype(BF16), peer_k2[l].astype(BF16), seq)
        table = _pack_expert_tables(peer_u[l], peer_v[l])
        t_tc = (bsz - SC_SEQS) * seq
        t_sc = SC_SEQS * seq
        t_g1 = ((bsz - SC_SEQS + 1) // 2) * seq
        a_sc = _sc_dot(_pack_pairs(peer_u[l]), idx, h2, t_tc, t_sc)
        x_g1 = _peer_gather(x, mod3, h2, idx, gate, table, seq, 0, t_g1, gate)
        w_sc = _gelu_gate(a_sc, gate, t_tc, x_g1)
        y_sc = _sc_mix(_pack_pairs(peer_v[l]), idx, w_sc, t_tc, t_sc, d)
        x_g2 = _peer_gather(x, mod3, h2, idx, gate, table, seq, t_g1, t_tc - t_g1, w_sc)
        x = jnp.concatenate([x_g1, x_g2, _residual(x, mod3, y_sc, t_tc, seq)], axis=0)

    x = x.reshape(bsz, seq, d)
    return x[:bp], x[bp:]
```

```python
import functools
import math

import jax
import jax.numpy as jnp
import numpy as np
from jax import lax
from jax.experimental import pallas as pl
from jax.experimental.pallas import tpu as pltpu
from jax.experimental.pallas import tpu_sc as plsc

F32 = jnp.float32
BF16 = jnp.bfloat16

EPS = 1e-6
HEAD_DIM = 64
N_Q_HEADS = 8
N_KV_HEADS = 2
WINDOW = 128
BLOCK = 128
ROPE_THETA = 10000.0
HY_ORDER = 2
N_DIR = 2
FILT_BANDS = 16
DECAY_TARGET = 1e-2
FAST_DECAY_PCT = 0.3
SLOW_DECAY_PCT = 1.5
PEER_HEADS = 8
N_KEYS = 128
PEER_TOPK = 16
PEER_QDIM = 256
LANES = 128
SUBLANES = 8
VMEM_LIMIT = 56 * 1024 * 1024
GATHER_RING = 4
GATHER_AHEAD = 2

_STAIR = [(i, j) for i in range(PEER_TOPK) for j in range(PEER_TOPK) if (i + 1) * (j + 1) <= PEER_TOPK]
_STAIR_ROWS = -(-len(_STAIR) // SUBLANES) * SUBLANES


def _cparams(sem, vmem=VMEM_LIMIT):
    return pltpu.CompilerParams(dimension_semantics=sem, vmem_limit_bytes=vmem)


def _dot(a, b):
    return jnp.dot(a, b, preferred_element_type=F32)


def _dot_nt(a, b):
    return lax.dot_general(a, b, (((1,), (1,)), ((), ())), preferred_element_type=F32)


def _dot_hi(a, b):
    return jnp.dot(a, b, preferred_element_type=F32, precision=lax.Precision.HIGHEST)


def _sigmoid(x):
    return 1.0 / (1.0 + jnp.exp(-x))


def _mod_kernel(c_ref, w_ref, b_ref, o_ref):
    c = c_ref[...]
    s = c * _sigmoid(c)
    o_ref[0] = _dot(s.astype(BF16), w_ref[0]) + b_ref[0]


def _modulation(c, w_mod_bf, b_mod):
    depth, d, n6 = w_mod_bf.shape
    bsz = c.shape[0]
    tn = 1536
    return pl.pallas_call(
        _mod_kernel,
        grid=(depth, n6 // tn),
        in_specs=[pl.BlockSpec((bsz, d), lambda l, j: (0, 0)),
                  pl.BlockSpec((1, d, tn), lambda l, j: (l, 0, j)),
                  pl.BlockSpec((1, 1, tn), lambda l, j: (l, 0, j))],
        out_specs=pl.BlockSpec((1, bsz, tn), lambda l, j: (l, 0, j)),
        out_shape=jax.ShapeDtypeStruct((depth, bsz, n6), F32),
        compiler_params=_cparams(("arbitrary", "arbitrary")),
        name="adaln_mod",
    )(c, w_mod_bf, b_mod.reshape(depth, 1, n6))


def _norm_mod(x, gain, shift, scale):
    y = x * lax.rsqrt(jnp.mean(x * x, axis=-1, keepdims=True) + EPS)
    return (y * gain) * (1.0 + scale) + shift


def _inproj_kernel(x_ref, mod_ref, g_ref, w_ref, hy_ref, q_ref, k_ref, v_ref, ga_ref, gb_ref, *, splits):
    m = mod_ref[0]
    h = _norm_mod(x_ref[...], g_ref[...], m[0:1], m[1:2])
    z = _dot(h.astype(BF16), w_ref[...])
    outs = (hy_ref, q_ref, k_ref, v_ref, ga_ref, gb_ref)
    lo = 0
    for ref, hi in zip(outs, splits):
        ref[...] = z[:, lo:hi]
        lo = hi


def _inproj(x2, mod3, gain, w_in_bf, seq, widths, tm=256):
    t, d = x2.shape
    ncols = w_in_bf.shape[1]
    per = seq // tm
    splits = tuple(int(s) for s in np.cumsum(widths))
    return pl.pallas_call(
        functools.partial(_inproj_kernel, splits=splits),
        grid=(t // tm,),
        in_specs=[pl.BlockSpec((tm, d), lambda i: (i, 0)),
                  pl.BlockSpec((1, 6, d), lambda i: (i // per, 0, 0)),
                  pl.BlockSpec((1, d), lambda i: (0, 0)),
                  pl.BlockSpec((d, ncols), lambda i: (0, 0))],
        out_specs=[pl.BlockSpec((tm, w), lambda i: (i, 0)) for w in widths],
        out_shape=[jax.ShapeDtypeStruct((t, w), F32) for w in widths],
        compiler_params=_cparams(("arbitrary",)),
        name="inproj",
    )(x2, mod3, gain.reshape(1, d), w_in_bf)


def _filter_kernel(z_ref, w1_ref, b1_ref, fr_ref, w2_ref, b2_ref, w3_ref, ad_ref, o_ref):
    z = z_ref[0]
    fr = fr_ref[...]
    a = jnp.sin(fr * (_dot_hi(z, w1_ref[...]) + b1_ref[...]))
    a = jnp.sin(fr * (_dot_hi(a, w2_ref[...]) + b2_ref[...]))
    h = _dot_hi(a, w3_ref[0])
    h = h * jnp.exp(-z[:, 0:1] * ad_ref[...])
    row = lax.broadcasted_iota(jnp.int32, h.shape, 0)
    dead = (pl.program_id(0) == 1) & (pl.program_id(1) == 0) & (row == 0)
    o_ref[0] = jnp.where(dead, 0.0, h)


def _filters_time(seq, f_w1, f_b1, f_freq, f_w2, f_b2, f_w3, tm=512):
    hidden = f_w1.shape[1]
    cw = f_w3.shape[1] // (HY_ORDER * N_DIR)
    t = jnp.linspace(0.0, 1.0, seq, dtype=F32)[:, None]
    w = 2.0 * math.pi * jnp.arange(seq, dtype=F32)[:, None] / seq
    bands = jnp.linspace(1e-4, FILT_BANDS - 1, FILT_BANDS, dtype=F32)[None, :]
    z = jnp.concatenate([t, jnp.cos(bands * w), -jnp.sin(bands * w)], axis=-1)
    emb = z.shape[1]
    z = jnp.pad(z, ((0, 0), (0, LANES - emb)))
    zcat = jnp.stack([z, jnp.concatenate([z[:1], z[:0:-1]], axis=0)], axis=0)
    w1p = jnp.pad(f_w1, ((0, LANES - emb), (0, 0)))
    w3d = f_w3.reshape(hidden, HY_ORDER, N_DIR, cw).transpose(2, 0, 1, 3).reshape(N_DIR, hidden, HY_ORDER * cw)
    max_decay = math.log(DECAY_TARGET) / FAST_DECAY_PCT
    min_decay = math.log(DECAY_TARGET) / SLOW_DECAY_PCT
    ad = jnp.abs(jnp.linspace(min_decay, max_decay, cw, dtype=F32))
    ad = jnp.tile(ad, HY_ORDER)[None, :]
    oc = HY_ORDER * cw
    return pl.pallas_call(
        _filter_kernel,
        grid=(N_DIR, seq // tm),
        in_specs=[pl.BlockSpec((1, tm, LANES), lambda g, r: (g, r, 0)),
                  pl.BlockSpec((LANES, hidden), lambda g, r: (0, 0)),
                  pl.BlockSpec((1, hidden), lambda g, r: (0, 0)),
                  pl.BlockSpec((1, hidden), lambda g, r: (0, 0)),
                  pl.BlockSpec((hidden, hidden), lambda g, r: (0, 0)),
                  pl.BlockSpec((1, hidden), lambda g, r: (0, 0)),
                  pl.BlockSpec((1, hidden, oc), lambda g, r: (g, 0, 0)),
                  pl.BlockSpec((1, oc), lambda g, r: (0, 0))],
        out_specs=pl.BlockSpec((1, tm, oc), lambda g, r: (g, r, 0)),
        out_shape=jax.ShapeDtypeStruct((N_DIR, seq, oc), F32),
        compiler_params=_cparams(("arbitrary", "arbitrary")),
        name="hyena_filter_mlp",
    )(zcat, w1p, f_b1[None, :], f_freq[None, :], f_w2, f_b2[None, :], w3d, ad)


def _dft_tables(seq):
    n2 = 2 * seq
    f = jnp.arange(seq, dtype=jnp.int32)
    ft = (f[:, None] * f[None, :]) % n2
    ang = ft.astype(F32) * (2.0 * math.pi / n2)
    return jnp.cos(ang).astype(BF16), jnp.sin(ang).astype(BF16)


def _spec_kernel(fc_ref, fs_ref, h_ref, a_ref, b_ref, nyq_ref, *, seq, tf):
    hlo = h_ref[0]
    hhi = h_ref[1]
    hlo_b = hlo.astype(BF16)
    hhi_b = hhi.astype(BF16)
    f = pl.program_id(1) * tf + lax.broadcasted_iota(jnp.int32, (tf, 1), 0)
    sgn = jnp.where(f % 2 == 0, 1.0, -1.0)
    fc = fc_ref[...]
    fs = fs_ref[...]
    hr = _dot(fc, hlo_b) + sgn * _dot(fc, hhi_b)
    hs = _dot(fs, hlo_b) + sgn * _dot(fs, hhi_b)
    w = jnp.where(f == 0, 1.0, 2.0) * (1.0 / (2 * seq))
    a_ref[...] = w * hr
    b_ref[...] = -(w * hs)
    t = lax.broadcasted_iota(jnp.int32, (seq, 1), 0)
    alt = jnp.where(t % 2 == 0, 1.0, -1.0)
    nyq = jnp.sum(alt * (hlo + hhi), axis=0, keepdims=True) * (1.0 / (2 * seq))
    nyq_ref[...] = jnp.broadcast_to(nyq, nyq_ref.shape)


def _filter_spectra(fc, fs, hcat, tf=512, tcol=512):
    seq = fc.shape[0]
    oc = hcat.shape[2]
    return pl.pallas_call(
        functools.partial(_spec_kernel, seq=seq, tf=tf),
        grid=(oc // tcol, seq // tf),
        in_specs=[pl.BlockSpec((tf, seq), lambda j, i: (i, 0)),
                  pl.BlockSpec((tf, seq), lambda j, i: (i, 0)),
                  pl.BlockSpec((2, seq, tcol), lambda j, i: (0, 0, j))],
        out_specs=[pl.BlockSpec((tf, tcol), lambda j, i: (i, j)),
                   pl.BlockSpec((tf, tcol), lambda j, i: (i, j)),
                   pl.BlockSpec((SUBLANES, tcol), lambda j, i: (0, j))],
        out_shape=[jax.ShapeDtypeStruct((seq, oc), F32),
                   jax.ShapeDtypeStruct((seq, oc), F32),
                   jax.ShapeDtypeStruct((SUBLANES, oc), F32)],
        compiler_params=_cparams(("arbitrary", "arbitrary")),
        name="hyena_filter_spectra",
    )(fc, fs, hcat)


def _shortconv(x, w_ref, b_ref, seq):
    row = lax.broadcasted_iota(jnp.int32, (seq, 1), 0)
    xm = jnp.where(row == 0, 0.0, pltpu.roll(x, 1, 0))
    xp = jnp.where(row == seq - 1, 0.0, pltpu.roll(x, seq - 1, 0))
    return xm * w_ref[0:1, :] + x * w_ref[1:2, :] + xp * w_ref[2:3, :] + b_ref[...]


def _conv_kernel(fc_ref, fs_ref, u_ref, g_ref, a_ref, b_ref, nyq_ref, bias_ref,
                 cwu_ref, cbu_ref, cwg_ref, cbg_ref, o_ref, acc_ref, *, seq, fb, conv_u):
    u = u_ref[...]
    if conv_u:
        u = _shortconv(u, cwu_ref, cbu_ref, seq)
    gate = _shortconv(g_ref[...], cwg_ref, cbg_ref, seq)
    ub = u.astype(BF16)
    for c in range(seq // fb):
        rows = slice(c * fb, (c + 1) * fb)
        ur = _dot(fc_ref[rows, :], ub)
        us = _dot(fs_ref[rows, :], ub)
        a = a_ref[rows, :]
        b = b_ref[rows, :]
        qr = (ur * a + us * b).astype(BF16)
        qi = (us * a - ur * b).astype(BF16)
        part = _dot(fc_ref[:, rows], qr) + _dot(fs_ref[:, rows], qi)
        if c == 0:
            acc_ref[...] = part
        else:
            acc_ref[...] += part
    t = lax.broadcasted_iota(jnp.int32, (seq, 1), 0)
    alt = jnp.where(t % 2 == 0, 1.0, -1.0)
    unyq = jnp.sum(alt * u, axis=0, keepdims=True)
    y = acc_ref[...] + alt * (unyq * nyq_ref[0:1, :]) + bias_ref[...] * u
    o_ref[...] = gate * y


def _long_conv(fc, fs, u_src, u_blk0, g_src, g_blk0, spec_a, spec_b, nyq, s_blk0, bias, cw, cb,
               cu_blk0, cg_blk0, bsz, seq, conv_u, tc=256, fb=512):
    nct = 512 // tc
    t = bsz * seq
    const = lambda j, b: (0, 0)
    return pl.pallas_call(
        functools.partial(_conv_kernel, seq=seq, fb=fb, conv_u=conv_u),
        grid=(nct, bsz),
        in_specs=[pl.BlockSpec(memory_space=pltpu.VMEM),
                  pl.BlockSpec(memory_space=pltpu.VMEM),
                  pl.BlockSpec((seq, tc), lambda j, b: (b, u_blk0 + j)),
                  pl.BlockSpec((seq, tc), lambda j, b: (b, g_blk0 + j)),
                  pl.BlockSpec((seq, tc), lambda j, b: (0, s_blk0 + j)),
                  pl.BlockSpec((seq, tc), lambda j, b: (0, s_blk0 + j)),
                  pl.BlockSpec((SUBLANES, tc), lambda j, b: (0, s_blk0 + j)),
                  pl.BlockSpec((1, tc), lambda j, b: (0, j)),
                  pl.BlockSpec((3, tc), lambda j, b: (0, cu_blk0 + j)),
                  pl.BlockSpec((1, tc), lambda j, b: (0, cu_blk0 + j)),
                  pl.BlockSpec((3, tc), lambda j, b: (0, cg_blk0 + j)),
                  pl.BlockSpec((1, tc), lambda j, b: (0, cg_blk0 + j))],
        out_specs=pl.BlockSpec((seq, tc), lambda j, b: (b, j)),
        out_shape=jax.ShapeDtypeStruct((t, 512), F32),
        scratch_shapes=[pltpu.VMEM((seq, tc), F32)],
        compiler_params=_cparams(("arbitrary", "arbitrary")),
        name="hyena_long_conv_u" if conv_u else "hyena_long_conv",
    )(fc, fs, u_src, g_src, spec_a, spec_b, nyq, bias, cw, cb, cw, cb)


def _attn_kernel(q_ref, k_ref, v_ref, cos_ref, sin_ref, qg_ref, kg_ref, sink_ref, o_ref,
                 qn_ref, km_ref, vm_ref, *, seq):
    lane = lax.broadcasted_iota(jnp.int32, (1, LANES), 1)
    r = lax.broadcasted_iota(jnp.int32, (LANES, LANES), 0) // HEAD_DIM
    c = lax.broadcasted_iota(jnp.int32, (LANES, LANES), 1) // HEAD_DIM
    bd = jnp.where(r == c, 1.0, 0.0).astype(BF16)
    first_half = (lane % HEAD_DIM) < (HEAD_DIM // 2)
    cos = cos_ref[...]
    sin = sin_ref[...]

    def norm_rope(x, gain):
        sq = x * x
        hi = sq.astype(BF16)
        lo = (sq - hi.astype(F32)).astype(BF16)
        ss = _dot(hi, bd) + _dot(lo, bd)
        y = (x * lax.rsqrt(ss * (1.0 / HEAD_DIM) + EPS)) * gain
        partner = jnp.where(first_half, pltpu.roll(y, LANES - HEAD_DIM // 2, 1), pltpu.roll(y, HEAD_DIM // 2, 1))
        return y * cos + partner * sin

    for p in range(N_Q_HEADS // 2):
        cols = slice(p * LANES, (p + 1) * LANES)
        qn_ref[:, cols] = norm_rope(q_ref[:, cols], qg_ref[...]).astype(BF16)
    kn = norm_rope(k_ref[...], kg_ref[...])
    left = lane < HEAD_DIM
    for src_ref, dst_ref in ((None, km_ref), (v_ref, vm_ref)):
        val = kn if src_ref is None else src_ref[...]
        rolled = pltpu.roll(val, HEAD_DIM, 1)
        dst_ref[0] = jnp.where(left, val, 0.0).astype(BF16)
        dst_ref[1] = jnp.where(left, 0.0, rolled).astype(BF16)
        dst_ref[2] = jnp.where(left, rolled, 0.0).astype(BF16)
        dst_ref[3] = jnp.where(left, 0.0, val).astype(BF16)

    span = 3 * BLOCK
    scale = HEAD_DIM ** -0.5
    ii = lax.broadcasted_iota(jnp.int32, (BLOCK, span), 0)
    jj = lax.broadcasted_iota(jnp.int32, (BLOCK, span), 1)

    def block(n, carry):
        q0 = pl.multiple_of(n * BLOCK, BLOCK)
        start = pl.multiple_of(jnp.clip((n - 1) * BLOCK, 0, seq - span), BLOCK)
        valid = jnp.abs((start - q0) + jj - ii) <= WINDOW
        for p in range(N_Q_HEADS // 2):
            cols = slice(p * LANES, (p + 1) * LANES)
            kv = (2 * p) // (N_Q_HEADS // N_KV_HEADS)
            qp = qn_ref[pl.ds(q0, BLOCK), cols]
            o = jnp.zeros((BLOCK, LANES), F32)
            for a in range(2):
                h = 2 * p + a
                kb = km_ref[2 * kv + a, pl.ds(start, span), :]
                s = _dot_nt(qp, kb) * scale
                s = jnp.where(valid, s, -jnp.inf)
                sk = sink_ref[h:h + 1, 0:1]
                m = jnp.maximum(jnp.max(s, axis=-1, keepdims=True), sk)
                e = jnp.exp(s - m)
                den = jnp.sum(e, axis=-1, keepdims=True) + jnp.exp(sk - m)
                pn = (e / den).astype(BF16)
                o = o + _dot(pn, vm_ref[2 * kv + a, pl.ds(start, span), :])
            o_ref[pl.ds(q0, BLOCK), cols] = o
        return carry

    lax.fori_loop(0, seq // BLOCK, block, 0)


def _attention(q, k, v, q_gain, k_gain, sink, bsz, seq):
    inv = ROPE_THETA ** (-jnp.arange(0, HEAD_DIM, 2, dtype=F32) / HEAD_DIM)
    ang = jnp.arange(seq, dtype=F32)[:, None] * inv[None, :]
    cos = jnp.tile(jnp.cos(ang), (1, LANES // (HEAD_DIM // 2)))
    sn = jnp.sin(ang)
    sin = jnp.tile(jnp.concatenate([-sn, sn], axis=1), (1, LANES // HEAD_DIM))
    qg = jnp.tile(q_gain, LANES // HEAD_DIM)[None, :]
    kg = jnp.tile(k_gain, LANES // HEAD_DIM)[None, :]
    sinkb = jnp.broadcast_to(sink[:, None], (N_Q_HEADS, LANES))
    aw = N_Q_HEADS * HEAD_DIM
    kw = N_KV_HEADS * HEAD_DIM
    const = lambda b: (0, 0)
    return pl.pallas_call(
        functools.partial(_attn_kernel, seq=seq),
        grid=(bsz,),
        in_specs=[pl.BlockSpec((seq, aw), lambda b: (b, 0)),
                  pl.BlockSpec((seq, kw), lambda b: (b, 0)),
                  pl.BlockSpec((seq, kw), lambda b: (b, 0)),
                  pl.BlockSpec((seq, LANES), const),
                  pl.BlockSpec((seq, LANES), const),
                  pl.BlockSpec((1, LANES), const),
                  pl.BlockSpec((1, LANES), const),
                  pl.BlockSpec((N_Q_HEADS, LANES), const)],
        out_specs=pl.BlockSpec((seq, aw), lambda b: (b, 0)),
        out_shape=jax.ShapeDtypeStruct((bsz * seq, aw), F32),
        scratch_shapes=[pltpu.VMEM((seq, aw), BF16),
                        pltpu.VMEM((4, seq, LANES), BF16),
                        pltpu.VMEM((4, seq, LANES), BF16)],
        compiler_params=_cparams(("arbitrary",)),
        name="window_attention",
    )(q, k, v, cos, sin, qg, kg, sinkb)


def _merge_kernel(x_ref, mod_ref, ya_ref, yb_ref, ga_ref, gb_ref, wpa_ref, wpb_ref, wo_ref, o_ref):
    pa = _dot(ya_ref[...].astype(BF16), wpa_ref[...])
    pb = _dot(yb_ref[...].astype(BF16), wpb_ref[...])
    merged = _sigmoid(ga_ref[...]) * pa + _sigmoid(gb_ref[...]) * pb
    out = _dot(merged.astype(BF16), wo_ref[...])
    o_ref[...] = x_ref[...] + mod_ref[0][2:3] * out


def _merge(x2, mod3, ya, yb, ga, gb, w_pa_bf, w_pb_bf, w_out_bf, seq, tm=512):
    t, d = x2.shape
    per = seq // tm
    row = lambda w: pl.BlockSpec((tm, w), lambda i: (i, 0))
    full = lambda a: pl.BlockSpec(a.shape, lambda i: (0, 0))
    return pl.pallas_call(
        _merge_kernel,
        grid=(t // tm,),
        in_specs=[row(d), pl.BlockSpec((1, 6, d), lambda i: (i // per, 0, 0)),
                  row(ya.shape[1]), row(yb.shape[1]), row(d), row(d),
                  full(w_pa_bf), full(w_pb_bf), full(w_out_bf)],
        out_specs=row(d),
        out_shape=jax.ShapeDtypeStruct((t, d), F32),
        compiler_params=_cparams(("arbitrary",)),
        name="merge_outproj",
    )(x2, mod3, ya, yb, ga, gb, w_pa_bf, w_pb_bf, w_out_bf)


def _oddeven_merge(lo, hi, r):
    step = r * 2
    if step < hi - lo:
        yield from _oddeven_merge(lo, hi, step)
        yield from _oddeven_merge(lo + r, hi, step)
        yield from ((i, i + r) for i in range(lo + r, hi - r, step))
    else:
        yield (lo, lo + r)


def _oddeven_merge_sort(lo, hi):
    if hi - lo >= 1:
        mid = lo + (hi - lo) // 2
        yield from _oddeven_merge_sort(lo, mid)
        yield from _oddeven_merge_sort(mid + 1, hi)
        yield from _oddeven_merge(lo, hi, 1)


def _topk_rows(s, k, val_ref, idx_ref):
    n, tm = s.shape
    ng = n // SUBLANES
    assert ng & (ng - 1) == 0 and k <= ng
    sub = lax.broadcasted_iota(jnp.int32, (SUBLANES, tm), 0)
    vals = [s[g * SUBLANES:(g + 1) * SUBLANES, :] for g in range(ng)]
    rows = [sub + g * SUBLANES for g in range(ng)]
    for i, j in _oddeven_merge_sort(0, ng - 1):
        keep = (vals[i] > vals[j]) | ((vals[i] == vals[j]) & (rows[i] < rows[j]))
        vals[i], vals[j] = jnp.where(keep, vals[i], vals[j]), jnp.where(keep, vals[j], vals[i])
        rows[i], rows[j] = jnp.where(keep, rows[i], rows[j]), jnp.where(keep, rows[j], rows[i])
    for t in range(k):
        m = jnp.max(vals[0], axis=0, keepdims=True)
        first = jnp.min(jnp.where(vals[0] == m, rows[0], n), axis=0, keepdims=True)
        val_ref[t:t + 1, :] = m
        idx_ref[t:t + 1, :] = first
        if t + 1 < k:
            popped = rows[0] == first
            for g in range(k - 1 - t):
                vals[g] = jnp.where(popped, vals[g + 1], vals[g])
                rows[g] = jnp.where(popped, rows[g + 1], rows[g])


def _peer_score_kernel(x_ref, mod_ref, g_ref, wq_ref, k1_ref, k2_ref, h_ref, idx_ref, gate_ref,
                       v1_ref, i1_ref, v2_ref, i2_ref, cv_ref, ce_ref, it_ref, gt_ref, *, tm):
    m = mod_ref[0]
    h = _norm_mod(x_ref[...], g_ref[...], m[3:4], m[4:5])
    h_ref[...] = h
    q = _dot(h.astype(BF16), wq_ref[...]).astype(BF16)
    half = PEER_QDIM // 2
    neg = jnp.full((1, tm), -jnp.inf, F32)
    for hd in range(PEER_HEADS):
        q1 = q[:, hd * PEER_QDIM: hd * PEER_QDIM + half]
        q2 = q[:, hd * PEER_QDIM + half: (hd + 1) * PEER_QDIM]
        _topk_rows(_dot_nt(k1_ref[...], q1), PEER_TOPK, v1_ref, i1_ref)
        _topk_rows(_dot_nt(k2_ref[...], q2), PEER_TOPK, v2_ref, i2_ref)
        for r, (i, j) in enumerate(_STAIR):
            cv_ref[r:r + 1, :] = v1_ref[i:i + 1, :] + v2_ref[j:j + 1, :]
            ce_ref[r:r + 1, :] = i1_ref[i:i + 1, :] * N_KEYS + i2_ref[j:j + 1, :]
        for r in range(len(_STAIR), _STAIR_ROWS):
            cv_ref[r:r + 1, :] = neg
            ce_ref[r:r + 1, :] = jnp.zeros((1, tm), jnp.int32)
        cand = cv_ref[...]
        ce = ce_ref[...]
        row = lax.broadcasted_iota(jnp.int32, cand.shape, 0)
        vals = []
        for kk in range(PEER_TOPK):
            mx = jnp.max(cand, axis=0, keepdims=True)
            first = jnp.min(jnp.where(cand == mx, row, _STAIR_ROWS), axis=0, keepdims=True)
            sel = row == first
            it_ref[hd * PEER_TOPK + kk: hd * PEER_TOPK + kk + 1, :] = jnp.sum(jnp.where(sel, ce, 0), axis=0, keepdims=True)
            vals.append(mx)
            cand = jnp.where(sel, -jnp.inf, cand)
        ex = [jnp.exp(v - vals[0]) for v in vals]
        tot = ex[0]
        for e in ex[1:]:
            tot = tot + e
        for kk in range(PEER_TOPK):
            gt_ref[hd * PEER_TOPK + kk: hd * PEER_TOPK + kk + 1, :] = ex[kk] / tot
    idx_ref[...] = it_ref[...].T
    gate_ref[...] = gt_ref[...].T


def _peer_score(x2, mod3, gain, wq_bf, k1_bf, k2_bf, seq, tm=256):
    t, d = x2.shape
    per = seq // tm
    slots = PEER_HEADS * PEER_TOPK
    full = lambda a: pl.BlockSpec(a.shape, lambda i: (0, 0))
    return pl.pallas_call(
        functools.partial(_peer_score_kernel, tm=tm),
        grid=(t // tm,),
        in_specs=[pl.BlockSpec((tm, d), lambda i: (i, 0)),
                  pl.BlockSpec((1, 6, d), lambda i: (i // per, 0, 0)),
                  pl.BlockSpec((1, d), lambda i: (0, 0)),
                  full(wq_bf), full(k1_bf), full(k2_bf)],
        out_specs=[pl.BlockSpec((tm, d), lambda i: (i, 0)),
                   pl.BlockSpec((tm, slots), lambda i: (i, 0)),
                   pl.BlockSpec((tm, slots), lambda i: (i, 0))],
        out_shape=[jax.ShapeDtypeStruct((t, d), F32),
                   jax.ShapeDtypeStruct((t, slots), jnp.int32),
                   jax.ShapeDtypeStruct((t, slots), F32)],
        scratch_shapes=[pltpu.VMEM((PEER_TOPK, tm), F32), pltpu.VMEM((PEER_TOPK, tm), jnp.int32),
                        pltpu.VMEM((PEER_TOPK, tm), F32), pltpu.VMEM((PEER_TOPK, tm), jnp.int32),
                        pltpu.VMEM((_STAIR_ROWS, tm), F32), pltpu.VMEM((_STAIR_ROWS, tm), jnp.int32),
                        pltpu.VMEM((slots, tm), jnp.int32), pltpu.VMEM((slots, tm), F32)],
        compiler_params=_cparams(("arbitrary",)),
        name="peer_score_topk",
    )(x2, mod3, gain.reshape(1, d), wq_bf, k1_bf, k2_bf)


def _peer_gather_kernel(idx_ref, idxn_ref, tab_ref, h_ref, gate_ref, x_ref, mod_ref, after_ref, o_ref,
                        buf_ref, sem_ref, *, tc, d, nsteps):
    slots = PEER_HEADS * PEER_TOPK
    step = pl.program_id(0)

    def issue_token(chunk, t):
        ids_ref, c = (idx_ref, chunk) if chunk < GATHER_RING else (idxn_ref, chunk - GATHER_RING)
        for j in range(slots):
            pltpu.make_async_copy(tab_ref.at[ids_ref[c * tc + t, j]],
                                  buf_ref.at[c, t, pl.ds(j, 1), :],
                                  sem_ref.at[c]).start(priority=j % 2)

    def wait(slot):
        pltpu.make_async_copy(buf_ref.at[slot], buf_ref.at[slot], sem_ref.at[slot]).wait()

    eye = (lax.broadcasted_iota(jnp.int32, (slots, slots), 0) ==
           lax.broadcasted_iota(jnp.int32, (slots, slots), 1))
    gt2 = mod_ref[0][5:6]

    def mix_token(slot, t):
        r = slot * tc + t
        word = buf_ref[slot, t]
        u = lax.bitcast_convert_type(word << 16, F32)
        prod = u * h_ref[r:r + 1, :]
        part = prod[:, 0:LANES]
        for c in range(1, d // LANES):
            part = part + prod[:, c * LANES:(c + 1) * LANES]
        a = jnp.sum(part, axis=1, keepdims=True)
        grow = jnp.broadcast_to(gate_ref[r:r + 1, :], (slots, slots))
        gcol = jnp.sum(jnp.where(eye, grow, 0.0), axis=1, keepdims=True)
        w = gcol * (0.5 * a * (1.0 + lax.erf(a * (2.0 ** -0.5))))
        v = lax.bitcast_convert_type(buf_ref[slot, t] & jnp.uint32(0xFFFF0000), F32)
        y = jnp.sum(v * w, axis=0, keepdims=True)
        o_ref[r:r + 1, :] = x_ref[r:r + 1, :] + gt2 * y

    @pl.when(step == 0)
    def _():
        for c in range(GATHER_AHEAD):
            for t in range(tc):
                issue_token(c, t)

    for p in range(GATHER_RING):
        wait(p)
        for t in range(tc):
            issue_token(p + GATHER_AHEAD, t)
            mix_token(p, t)

    @pl.when(step == nsteps - 1)
    def _():
        for c in range(GATHER_AHEAD):
            wait(c)


def _peer_gather(x2, mod3, h2, idx, gate, table, seq, tok0, ntok, after, tc=8):
    d = x2.shape[1]
    slots = idx.shape[1]
    tb = GATHER_RING * tc
    nsteps = ntok // tb
    off = tok0 // tb
    per = seq // tb
    row = lambda w: pl.BlockSpec((tb, w), lambda i: (i + off, 0))
    return pl.pallas_call(
        functools.partial(_peer_gather_kernel, tc=tc, d=d, nsteps=nsteps),
        grid=(nsteps,),
        in_specs=[pl.BlockSpec((tb, slots), lambda i: (i + off, 0), memory_space=pltpu.SMEM),
                  pl.BlockSpec((tb, slots), lambda i: (jnp.minimum(i + 1, nsteps - 1) + off, 0),
                               memory_space=pltpu.SMEM),
                  pl.BlockSpec(memory_space=pl.ANY),
                  row(d), row(slots), row(d),
                  pl.BlockSpec((1, 6, d), lambda i: ((i + off) // per, 0, 0)),
                  pl.BlockSpec(memory_space=pl.ANY)],
        out_specs=pl.BlockSpec((tb, d), lambda i: (i, 0)),
        out_shape=jax.ShapeDtypeStruct((ntok, d), F32),
        scratch_shapes=[pltpu.VMEM((GATHER_RING, tc, slots, d), jnp.uint32),
                        pltpu.SemaphoreType.DMA((GATHER_RING,))],
        compiler_params=_cparams(("arbitrary",)),
        name="peer_gather_mix",
    )(idx, idx, table.reshape(table.shape[0], 1, d), h2, gate, x2, mod3, after)


def _pack_expert_tables(u_tab, v_tab):
    ub = lax.bitcast_convert_type(u_tab.astype(BF16), jnp.uint16).astype(jnp.uint32)
    vb = lax.bitcast_convert_type(v_tab.astype(BF16), jnp.uint16).astype(jnp.uint32)
    return (vb << 16) | ub


SC_LANES = 16
SC_CORES = 2
SC_SUBCORES = 16
SC_GROUP = 8
SC_HALF = 64
SC_SEQS = 12


def _pack_pairs(tab):
    e, d = tab.shape
    bits = lax.bitcast_convert_type(tab.astype(BF16), jnp.uint16).astype(jnp.uint32)
    bits = bits.reshape(e, d // (2 * SC_LANES), 2, SC_LANES)
    return ((bits[:, :, 1, :] << 16) | bits[:, :, 0, :]).reshape(e, d // 2)


def _sc_worker_base(n_per):
    return (lax.axis_index("s") * SC_CORES + lax.axis_index("c")) * n_per


def _sc_halves(tab_hbm, idx_v, rows_v, sems, tt, half):
    ids = idx_v.at[tt, pl.ds(half * SC_HALF, SC_HALF)]
    return pltpu.make_async_copy(tab_hbm.at[ids], rows_v.at[half], sems.at[half])


def _sc_token_loop(tab_hbm, idx_v, rows_v, sems, compute_half):
    _sc_halves(tab_hbm, idx_v, rows_v, sems, 0, 0).start()

    def tok(tt, carry):
        _sc_halves(tab_hbm, idx_v, rows_v, sems, tt, 1).start()
        _sc_halves(tab_hbm, idx_v, rows_v, sems, tt, 0).wait()
        compute_half(tt, 0)

        @pl.when(tt + 1 < SC_GROUP)
        def _():
            _sc_halves(tab_hbm, idx_v, rows_v, sems, tt + 1, 0).start()

        _sc_halves(tab_hbm, idx_v, rows_v, sems, tt, 1).wait()
        compute_half(tt, 1)
        return carry

    lax.fori_loop(0, SC_GROUP, tok, 0)


def _unpack_pair(word):
    lo = lax.bitcast_convert_type(word << 16, F32)
    hi = lax.bitcast_convert_type(word & jnp.uint32(0xFFFF0000), F32)
    return lo, hi


def _sc_dot_body(tab_hbm, idx_hbm, h_hbm, a_hbm, idx_v, h_v, rows_v, a_v, tr_v, sems, *, tok0, n_per, d):
    base = _sc_worker_base(n_per)
    lane = lax.iota(jnp.int32, SC_LANES)
    nk = d // (2 * SC_LANES)

    npiece = 2
    kp = nk // npiece

    def compute_half(tt, half):
        for piece in range(npiece):
            hs = [h_v[tt, pl.ds(piece * kp * 2 * SC_LANES + SC_LANES * m, SC_LANES)] for m in range(2 * kp)]

            def rows16(gi, carry):
                for q in range(SC_LANES):
                    parts = [None] * 4
                    for k in range(kp):
                        lo, hi = _unpack_pair(
                            rows_v[half, gi * SC_LANES + q, pl.ds(SC_LANES * (piece * kp + k), SC_LANES)])
                        p = lo * hs[2 * k] + hi * hs[2 * k + 1]
                        parts[k % 4] = p if parts[k % 4] is None else parts[k % 4] + p
                    tr_v[q, :] = (parts[0] + parts[1]) + (parts[2] + parts[3])
                tot = plsc.load_gather(tr_v, [lane, jnp.zeros((SC_LANES,), jnp.int32)])
                for c in range(1, SC_LANES):
                    tot = tot + plsc.load_gather(tr_v, [lane, jnp.full((SC_LANES,), c, jnp.int32)])
                out = pl.ds(half * SC_HALF + gi * SC_LANES, SC_LANES)
                a_v[tt, out] = tot if piece == 0 else a_v[tt, out] + tot
                return carry
            lax.fori_loop(0, SC_HALF // SC_LANES, rows16, 0)

    def group(g, carry):
        loc = base + g * SC_GROUP
        pltpu.sync_copy(idx_hbm.at[pl.ds(tok0 + loc, SC_GROUP)], idx_v)
        pltpu.sync_copy(h_hbm.at[pl.ds(tok0 + loc, SC_GROUP)], h_v)
        _sc_token_loop(tab_hbm, idx_v, rows_v, sems, compute_half)
        pltpu.sync_copy(a_v, a_hbm.at[pl.ds(loc, SC_GROUP)])
        return carry

    lax.fori_loop(0, n_per // SC_GROUP, group, 0)


def _sc_mix_body(tab_hbm, idx_hbm, w_hbm, y_hbm, idx_v, w_v, rows_v, y_v, sems, *, tok0, n_per, d):
    base = _sc_worker_base(n_per)
    nk = d // (2 * SC_LANES)
    nq = 2
    kq = nk // nq

    def compute_half(tt, half):
        ttv = jnp.full((SC_LANES,), tt, jnp.int32)
        for piece in range(nq):
            col0 = piece * kq * 2 * SC_LANES
            if half == 0:
                accs = tuple(jnp.zeros((SC_LANES,), F32) for _ in range(2 * kq))
            else:
                accs = tuple(y_v[tt, pl.ds(col0 + SC_LANES * m, SC_LANES)] for m in range(2 * kq))

            def row(r, accs):
                wsplat = plsc.load_gather(w_v, [ttv, jnp.full((SC_LANES,), half * SC_HALF + r, jnp.int32)])
                out = []
                for k in range(kq):
                    lo, hi = _unpack_pair(rows_v[half, r, pl.ds(SC_LANES * (piece * kq + k), SC_LANES)])
                    out.append(accs[2 * k] + wsplat * lo)
                    out.append(accs[2 * k + 1] + wsplat * hi)
                return tuple(out)

            accs = lax.fori_loop(0, SC_HALF, row, accs)
            for m in range(2 * kq):
                y_v[tt, pl.ds(col0 + SC_LANES * m, SC_LANES)] = accs[m]

    def group(g, carry):
        loc = base + g * SC_GROUP
        pltpu.sync_copy(idx_hbm.at[pl.ds(tok0 + loc, SC_GROUP)], idx_v)
        pltpu.sync_copy(w_hbm.at[pl.ds(loc, SC_GROUP)], w_v)
        _sc_token_loop(tab_hbm, idx_v, rows_v, sems, compute_half)
        pltpu.sync_copy(y_v, y_hbm.at[pl.ds(loc, SC_GROUP)])
        return carry

    lax.fori_loop(0, n_per // SC_GROUP, group, 0)


def _sc_mesh():
    return plsc.VectorSubcoreMesh(core_axis_name="c", subcore_axis_name="s")


def _sc_dot(u_words, idx, h2, tok0, ntok):
    d = h2.shape[1]
    slots = idx.shape[1]
    n_per = ntok // (SC_CORES * SC_SUBCORES)
    return pl.kernel(
        functools.partial(_sc_dot_body, tok0=tok0, n_per=n_per, d=d),
        out_type=jax.ShapeDtypeStruct((ntok, slots), F32),
        mesh=_sc_mesh(),
        scratch_types=[pltpu.VMEM((SC_GROUP, slots), jnp.int32),
                       pltpu.VMEM((SC_GROUP, d), F32),
                       pltpu.VMEM((2, SC_HALF, d // 2), jnp.uint32),
                       pltpu.VMEM((SC_GROUP, slots), F32),
                       pltpu.VMEM((SC_LANES, SC_LANES), F32),
                       pltpu.SemaphoreType.DMA((2,))],
        compiler_params=pltpu.CompilerParams(needs_layout_passes=False),
        name="peer_sc_dot",
    )(u_words, idx, h2)


def _sc_mix(v_words, idx, w, tok0, ntok, d):
    slots = idx.shape[1]
    n_per = ntok // (SC_CORES * SC_SUBCORES)
    return pl.kernel(
        functools.partial(_sc_mix_body, tok0=tok0, n_per=n_per, d=d),
        out_type=jax.ShapeDtypeStruct((ntok, d), F32),
        mesh=_sc_mesh(),
        scratch_types=[pltpu.VMEM((SC_GROUP, slots), jnp.int32),
                       pltpu.VMEM((SC_GROUP, slots), F32),
                       pltpu.VMEM((2, SC_HALF, d // 2), jnp.uint32),
                       pltpu.VMEM((SC_GROUP, d), F32),
                       pltpu.SemaphoreType.DMA((2,))],
        compiler_params=pltpu.CompilerParams(needs_layout_passes=False),
        name="peer_sc_mix",
    )(v_words, idx, w)


def _gelu_gate_kernel(a_ref, g_ref, after_ref, o_ref):
    a = a_ref[...]
    o_ref[...] = g_ref[...] * (0.5 * a * (1.0 + lax.erf(a * (2.0 ** -0.5))))


def _gelu_gate(a, gate, tok0, after, tm=2048):
    n, slots = a.shape
    off = tok0 // tm
    return pl.pallas_call(
        _gelu_gate_kernel,
        grid=(n // tm,),
        in_specs=[pl.BlockSpec((tm, slots), lambda i: (i, 0)),
                  pl.BlockSpec((tm, slots), lambda i: (i + off, 0)),
                  pl.BlockSpec(memory_space=pl.ANY)],
        out_specs=pl.BlockSpec((tm, slots), lambda i: (i, 0)),
        out_shape=jax.ShapeDtypeStruct((n, slots), F32),
        compiler_params=_cparams(("arbitrary",)),
        name="peer_gelu_gate",
    )(a, gate, after)


def _residual_kernel(x_ref, mod_ref, y_ref, o_ref):
    o_ref[...] = x_ref[...] + mod_ref[0][5:6] * y_ref[...]


def _residual(x2, mod3, y, tok0, seq, tm=512):
    n, d = y.shape
    off = tok0 // tm
    per = seq // tm
    return pl.pallas_call(
        _residual_kernel,
        grid=(n // tm,),
        in_specs=[pl.BlockSpec((tm, d), lambda i: (i + off, 0)),
                  pl.BlockSpec((1, 6, d), lambda i: ((i + off) // per, 0, 0)),
                  pl.BlockSpec((tm, d), lambda i: (i, 0))],
        out_specs=pl.BlockSpec((tm, d), lambda i: (i, 0)),
        out_shape=jax.ShapeDtypeStruct((n, d), F32),
        compiler_params=_cparams(("arbitrary",)),
        name="peer_residual",
    )(x2, mod3, y)


def kernel(x_prompt, x_sample, c_prompt, c_sample, w_mod, b_mod, g_norm1, g_norm2, w_in, conv_w, conv_b, f_w1, f_b1, f_freq, f_w2, f_b2, f_w3, f_bias, q_gain, k_gain, sink, w_pa, w_pb, w_out, peer_wq, peer_k1, peer_k2, peer_u, peer_v):
    depth = w_mod.shape[0]
    bp, seq, d = x_prompt.shape
    bs = x_sample.shape[0]
    assert x_sample.shape[1] == seq
    bsz = bp + bs
    x = jnp.concatenate([x_prompt, x_sample], axis=0).reshape(bsz * seq, d)
    c = jnp.concatenate([c_prompt, c_sample], axis=0)

    hyw = w_pa.shape[1]
    aw = w_pb.shape[1]
    kw = N_KV_HEADS * HEAD_DIM
    widths = (HY_ORDER + 1) * hyw, aw, kw, kw, d, d
    cblk = hyw // 256

    mod = _modulation(c, w_mod.astype(BF16), b_mod)
    fc, fs = _dft_tables(seq)

    for l in range(depth):
        mod3 = mod[l].reshape(bsz, 6, d)
        hy, q, k, v, ga, gb = _inproj(x, mod3, g_norm1[l], w_in[l].astype(BF16), seq, widths)
        hcat = _filters_time(seq, f_w1[l], f_b1[l], f_freq[l], f_w2[l], f_b2[l], f_w3[l])
        spec_a, spec_b, nyq = _filter_spectra(fc, fs, hcat)
        cw, cb = conv_w[l], conv_b[l][None, :]
        zz = _long_conv(fc, fs, hy, 0, hy, cblk, spec_a, spec_b, nyq, 0, f_bias[l][0:1], cw, cb,
                        0, cblk, bsz, seq, conv_u=True)
        ya = _long_conv(fc, fs, zz, 0, hy, 2 * cblk, spec_a, spec_b, nyq, cblk, f_bias[l][1:2], cw, cb,
                        0, 2 * cblk, bsz, seq, conv_u=False)
        yb = _attention(q, k, v, q_gain[l], k_gain[l], sink[l], bsz, seq)
        x = _merge(x, mod3, ya, yb, ga, gb, w_pa[l].astype(BF16), w_pb[l].astype(BF16), w_out[l].astype(BF16), seq)
        h2, idx, gate = _peer_score(x, mod3, g_norm2[l], peer_wq[l].astype(BF16),
                                    peer_k1[l].astype(BF16), peer_k2[l].astype(BF16), seq)
        table = _pack_expert_tables(peer_u[l], peer_v[l])
        t_tc = (bsz - SC_SEQS) * seq
        t_sc = SC_SEQS * seq
        t_g1 = ((bsz - SC_SEQS + 1) // 2) * seq
        a_sc = _sc_dot(_pack_pairs(peer_u[l]), idx, h2, t_tc, t_sc)
        x_g1 = _peer_gather(x, mod3, h2, idx, gate, table, seq, 0, t_g1, gate)
        w_sc = _gelu_gate(a_sc, gate, t_tc, x_g1)
        y_sc = _sc_mix(_pack_pairs(peer_v[l]), idx, w_sc, t_tc, t_sc, d)
        x_g2 = _peer_gather(x, mod3, h2, idx, gate, table, seq, t_g1, t_tc - t_g1, w_sc)
        x = jnp.concatenate([x_g1, x_g2, _residual(x, mod3, y_sc, t_tc, seq)], axis=0)

    x = x.reshape(bsz, seq, d)
    return x[:bp], x[bp:]
```

```python
import functools
import math

import jax
import jax.numpy as jnp
import numpy as np
from jax import lax
from jax.experimental import pallas as pl
from jax.experimental.pallas import tpu as pltpu
from jax.experimental.pallas import tpu_sc as plsc

F32 = jnp.float32
BF16 = jnp.bfloat16

EPS = 1e-6
HEAD_DIM = 64
N_Q_HEADS = 8
N_KV_HEADS = 2
WINDOW = 128
BLOCK = 128
ROPE_THETA = 10000.0
HY_ORDER = 2
N_DIR = 2
FILT_BANDS = 16
DECAY_TARGET = 1e-2
FAST_DECAY_PCT = 0.3
SLOW_DECAY_PCT = 1.5
PEER_HEADS = 8
N_KEYS = 128
PEER_TOPK = 16
PEER_QDIM = 256
LANES = 128
SUBLANES = 8
VMEM_LIMIT = 56 * 1024 * 1024
GATHER_RING = 4
GATHER_AHEAD = 2

_STAIR = [(i, j) for i in range(PEER_TOPK) for j in range(PEER_TOPK) if (i + 1) * (j + 1) <= PEER_TOPK]
_STAIR_ROWS = -(-len(_STAIR) // SUBLANES) * SUBLANES


def _cparams(sem, vmem=VMEM_LIMIT):
    return pltpu.CompilerParams(dimension_semantics=sem, vmem_limit_bytes=vmem)


def _dot(a, b):
    return jnp.dot(a, b, preferred_element_type=F32)


def _dot_nt(a, b):
    return lax.dot_general(a, b, (((1,), (1,)), ((), ())), preferred_element_type=F32)


def _dot_hi(a, b):
    return jnp.dot(a, b, preferred_element_type=F32, precision=lax.Precision.HIGHEST)


def _sigmoid(x):
    return 1.0 / (1.0 + jnp.exp(-x))


def _mod_kernel(c_ref, w_ref, b_ref, o_ref):
    c = c_ref[...]
    s = c * _sigmoid(c)
    o_ref[0] = _dot(s.astype(BF16), w_ref[0]) + b_ref[0]


def _modulation(c, w_mod_bf, b_mod):
    depth, d, n6 = w_mod_bf.shape
    bsz = c.shape[0]
    tn = 1536
    return pl.pallas_call(
        _mod_kernel,
        grid=(depth, n6 // tn),
        in_specs=[pl.BlockSpec((bsz, d), lambda l, j: (0, 0)),
                  pl.BlockSpec((1, d, tn), lambda l, j: (l, 0, j)),
                  pl.BlockSpec((1, 1, tn), lambda l, j: (l, 0, j))],
        out_specs=pl.BlockSpec((1, bsz, tn), lambda l, j: (l, 0, j)),
        out_shape=jax.ShapeDtypeStruct((depth, bsz, n6), F32),
        compiler_params=_cparams(("arbitrary", "arbitrary")),
        name="adaln_mod",
    )(c, w_mod_bf, b_mod.reshape(depth, 1, n6))


def _norm_mod(x, gain, shift, scale):
    y = x * lax.rsqrt(jnp.mean(x * x, axis=-1, keepdims=True) + EPS)
    return (y * gain) * (1.0 + scale) + shift


def _inproj_kernel(x_ref, mod_ref, g_ref, w_ref, hy_ref, q_ref, k_ref, v_ref, ga_ref, gb_ref, *, splits):
    m = mod_ref[0]
    h = _norm_mod(x_ref[...], g_ref[...], m[0:1], m[1:2])
    z = _dot(h.astype(BF16), w_ref[...])
    outs = (hy_ref, q_ref, k_ref, v_ref, ga_ref, gb_ref)
    lo = 0
    for ref, hi in zip(outs, splits):
        ref[...] = z[:, lo:hi]
        lo = hi


def _inproj(x2, mod3, gain, w_in_bf, seq, widths, tm=256):
    t, d = x2.shape
    ncols = w_in_bf.shape[1]
    per = seq // tm
    splits = tuple(int(s) for s in np.cumsum(widths))
    return pl.pallas_call(
        functools.partial(_inproj_kernel, splits=splits),
        grid=(t // tm,),
        in_specs=[pl.BlockSpec((tm, d), lambda i: (i, 0)),
                  pl.BlockSpec((1, 6, d), lambda i: (i // per, 0, 0)),
                  pl.BlockSpec((1, d), lambda i: (0, 0)),
                  pl.BlockSpec((d, ncols), lambda i: (0, 0))],
        out_specs=[pl.BlockSpec((tm, w), lambda i: (i, 0)) for w in widths],
        out_shape=[jax.ShapeDtypeStruct((t, w), F32) for w in widths],
        compiler_params=_cparams(("arbitrary",)),
        name="inproj",
    )(x2, mod3, gain.reshape(1, d), w_in_bf)


def _filter_kernel(z_ref, w1_ref, b1_ref, fr_ref, w2_ref, b2_ref, w3_ref, ad_ref, o_ref):
    z = z_ref[0]
    fr = fr_ref[...]
    a = jnp.sin(fr * (_dot_hi(z, w1_ref[...]) + b1_ref[...]))
    a = jnp.sin(fr * (_dot_hi(a, w2_ref[...]) + b2_ref[...]))
    h = _dot_hi(a, w3_ref[0])
    h = h * jnp.exp(-z[:, 0:1] * ad_ref[...])
    row = lax.broadcasted_iota(jnp.int32, h.shape, 0)
    dead = (pl.program_id(0) == 1) & (pl.program_id(1) == 0) & (row == 0)
    o_ref[0] = jnp.where(dead, 0.0, h)


def _filters_time(seq, f_w1, f_b1, f_freq, f_w2, f_b2, f_w3, tm=512):
    hidden = f_w1.shape[1]
    cw = f_w3.shape[1] // (HY_ORDER * N_DIR)
    t = jnp.linspace(0.0, 1.0, seq, dtype=F32)[:, None]
    w = 2.0 * math.pi * jnp.arange(seq, dtype=F32)[:, None] / seq
    bands = jnp.linspace(1e-4, FILT_BANDS - 1, FILT_BANDS, dtype=F32)[None, :]
    z = jnp.concatenate([t, jnp.cos(bands * w), -jnp.sin(bands * w)], axis=-1)
    emb = z.shape[1]
    z = jnp.pad(z, ((0, 0), (0, LANES - emb)))
    zcat = jnp.stack([z, jnp.concatenate([z[:1], z[:0:-1]], axis=0)], axis=0)
    w1p = jnp.pad(f_w1, ((0, LANES - emb), (0, 0)))
    w3d = f_w3.reshape(hidden, HY_ORDER, N_DIR, cw).transpose(2, 0, 1, 3).reshape(N_DIR, hidden, HY_ORDER * cw)
    max_decay = math.log(DECAY_TARGET) / FAST_DECAY_PCT
    min_decay = math.log(DECAY_TARGET) / SLOW_DECAY_PCT
    ad = jnp.abs(jnp.linspace(min_decay, max_decay, cw, dtype=F32))
    ad = jnp.tile(ad, HY_ORDER)[None, :]
    oc = HY_ORDER * cw
    return pl.pallas_call(
        _filter_kernel,
        grid=(N_DIR, seq // tm),
        in_specs=[pl.BlockSpec((1, tm, LANES), lambda g, r: (g, r, 0)),
                  pl.BlockSpec((LANES, hidden), lambda g, r: (0, 0)),
                  pl.BlockSpec((1, hidden), lambda g, r: (0, 0)),
                  pl.BlockSpec((1, hidden), lambda g, r: (0, 0)),
                  pl.BlockSpec((hidden, hidden), lambda g, r: (0, 0)),
                  pl.BlockSpec((1, hidden), lambda g, r: (0, 0)),
                  pl.BlockSpec((1, hidden, oc), lambda g, r: (g, 0, 0)),
                  pl.BlockSpec((1, oc), lambda g, r: (0, 0))],
        out_specs=pl.BlockSpec((1, tm, oc), lambda g, r: (g, r, 0)),
        out_shape=jax.ShapeDtypeStruct((N_DIR, seq, oc), F32),
        compiler_params=_cparams(("arbitrary", "arbitrary")),
        name="hyena_filter_mlp",
    )(zcat, w1p, f_b1[None, :], f_freq[None, :], f_w2, f_b2[None, :], w3d, ad)


def _dft_tables(seq):
    n2 = 2 * seq
    f = jnp.arange(seq, dtype=jnp.int32)
    ft = (f[:, None] * f[None, :]) % n2
    ang = ft.astype(F32) * (2.0 * math.pi / n2)
    return jnp.cos(ang).astype(BF16), jnp.sin(ang).astype(BF16)


def _spec_kernel(fc_ref, fs_ref, h_ref, a_ref, b_ref, nyq_ref, *, seq, tf):
    hlo = h_ref[0]
    hhi = h_ref[1]
    hlo_b = hlo.astype(BF16)
    hhi_b = hhi.astype(BF16)
    f = pl.program_id(1) * tf + lax.broadcasted_iota(jnp.int32, (tf, 1), 0)
    sgn = jnp.where(f % 2 == 0, 1.0, -1.0)
    fc = fc_ref[...]
    fs = fs_ref[...]
    hr = _dot(fc, hlo_b) + sgn * _dot(fc, hhi_b)
    hs = _dot(fs, hlo_b) + sgn * _dot(fs, hhi_b)
    w = jnp.where(f == 0, 1.0, 2.0) * (1.0 / (2 * seq))
    a_ref[...] = w * hr
    b_ref[...] = -(w * hs)
    t = lax.broadcasted_iota(jnp.int32, (seq, 1), 0)
    alt = jnp.where(t % 2 == 0, 1.0, -1.0)
    nyq = jnp.sum(alt * (hlo + hhi), axis=0, keepdims=True) * (1.0 / (2 * seq))
    nyq_ref[...] = jnp.broadcast_to(nyq, nyq_ref.shape)


def _filter_spectra(fc, fs, hcat, tf=512, tcol=512):
    seq = fc.shape[0]
    oc = hcat.shape[2]
    return pl.pallas_call(
        functools.partial(_spec_kernel, seq=seq, tf=tf),
        grid=(oc // tcol, seq // tf),
        in_specs=[pl.BlockSpec((tf, seq), lambda j, i: (i, 0)),
                  pl.BlockSpec((tf, seq), lambda j, i: (i, 0)),
                  pl.BlockSpec((2, seq, tcol), lambda j, i: (0, 0, j))],
        out_specs=[pl.BlockSpec((tf, tcol), lambda j, i: (i, j)),
                   pl.BlockSpec((tf, tcol), lambda j, i: (i, j)),
                   pl.BlockSpec((SUBLANES, tcol), lambda j, i: (0, j))],
        out_shape=[jax.ShapeDtypeStruct((seq, oc), F32),
                   jax.ShapeDtypeStruct((seq, oc), F32),
                   jax.ShapeDtypeStruct((SUBLANES, oc), F32)],
        compiler_params=_cparams(("arbitrary", "arbitrary")),
        name="hyena_filter_spectra",
    )(fc, fs, hcat)


def _shortconv(x, w_ref, b_ref, seq):
    row = lax.broadcasted_iota(jnp.int32, (seq, 1), 0)
    xm = jnp.where(row == 0, 0.0, pltpu.roll(x, 1, 0))
    xp = jnp.where(row == seq - 1, 0.0, pltpu.roll(x, seq - 1, 0))
    return xm * w_ref[0:1, :] + x * w_ref[1:2, :] + xp * w_ref[2:3, :] + b_ref[...]


def _conv_kernel(fc_ref, fs_ref, u_ref, g_ref, a_ref, b_ref, nyq_ref, bias_ref,
                 cwu_ref, cbu_ref, cwg_ref, cbg_ref, o_ref, acc_ref, *, seq, fb, conv_u):
    u = u_ref[...]
    if conv_u:
        u = _shortconv(u, cwu_ref, cbu_ref, seq)
    gate = _shortconv(g_ref[...], cwg_ref, cbg_ref, seq)
    ub = u.astype(BF16)
    for c in range(seq // fb):
        rows = slice(c * fb, (c + 1) * fb)
        ur = _dot(fc_ref[rows, :], ub)
        us = _dot(fs_ref[rows, :], ub)
        a = a_ref[rows, :]
        b = b_ref[rows, :]
        qr = (ur * a + us * b).astype(BF16)
        qi = (us * a - ur * b).astype(BF16)
        part = _dot(fc_ref[:, rows], qr) + _dot(fs_ref[:, rows], qi)
        if c == 0:
            acc_ref[...] = part
        else:
            acc_ref[...] += part
    t = lax.broadcasted_iota(jnp.int32, (seq, 1), 0)
    alt = jnp.where(t % 2 == 0, 1.0, -1.0)
    unyq = jnp.sum(alt * u, axis=0, keepdims=True)
    y = acc_ref[...] + alt * (unyq * nyq_ref[0:1, :]) + bias_ref[...] * u
    o_ref[...] = gate * y


def _long_conv(fc, fs, u_src, u_blk0, g_src, g_blk0, spec_a, spec_b, nyq, s_blk0, bias, cw, cb,
               cu_blk0, cg_blk0, bsz, seq, conv_u, tc=256, fb=512):
    nct = 512 // tc
    t = bsz * seq
    const = lambda j, b: (0, 0)
    return pl.pallas_call(
        functools.partial(_conv_kernel, seq=seq, fb=fb, conv_u=conv_u),
        grid=(nct, bsz),
        in_specs=[pl.BlockSpec(memory_space=pltpu.VMEM),
                  pl.BlockSpec(memory_space=pltpu.VMEM),
                  pl.BlockSpec((seq, tc), lambda j, b: (b, u_blk0 + j)),
                  pl.BlockSpec((seq, tc), lambda j, b: (b, g_blk0 + j)),
                  pl.BlockSpec((seq, tc), lambda j, b: (0, s_blk0 + j)),
                  pl.BlockSpec((seq, tc), lambda j, b: (0, s_blk0 + j)),
                  pl.BlockSpec((SUBLANES, tc), lambda j, b: (0, s_blk0 + j)),
                  pl.BlockSpec((1, tc), lambda j, b: (0, j)),
                  pl.BlockSpec((3, tc), lambda j, b: (0, cu_blk0 + j)),
                  pl.BlockSpec((1, tc), lambda j, b: (0, cu_blk0 + j)),
                  pl.BlockSpec((3, tc), lambda j, b: (0, cg_blk0 + j)),
                  pl.BlockSpec((1, tc), lambda j, b: (0, cg_blk0 + j))],
        out_specs=pl.BlockSpec((seq, tc), lambda j, b: (b, j)),
        out_shape=jax.ShapeDtypeStruct((t, 512), F32),
        scratch_shapes=[pltpu.VMEM((seq, tc), F32)],
        compiler_params=_cparams(("arbitrary", "arbitrary")),
        name="hyena_long_conv_u" if conv_u else "hyena_long_conv",
    )(fc, fs, u_src, g_src, spec_a, spec_b, nyq, bias, cw, cb, cw, cb)


def _attn_kernel(q_ref, k_ref, v_ref, cos_ref, sin_ref, qg_ref, kg_ref, sink_ref, o_ref,
                 qn_ref, km_ref, vm_ref, *, seq):
    lane = lax.broadcasted_iota(jnp.int32, (1, LANES), 1)
    r = lax.broadcasted_iota(jnp.int32, (LANES, LANES), 0) // HEAD_DIM
    c = lax.broadcasted_iota(jnp.int32, (LANES, LANES), 1) // HEAD_DIM
    bd = jnp.where(r == c, 1.0, 0.0).astype(BF16)
    first_half = (lane % HEAD_DIM) < (HEAD_DIM // 2)
    cos = cos_ref[...]
    sin = sin_ref[...]

    def norm_rope(x, gain):
        sq = x * x
        hi = sq.astype(BF16)
        lo = (sq - hi.astype(F32)).astype(BF16)
        ss = _dot(hi, bd) + _dot(lo, bd)
        y = (x * lax.rsqrt(ss * (1.0 / HEAD_DIM) + EPS)) * gain
        partner = jnp.where(first_half, pltpu.roll(y, LANES - HEAD_DIM // 2, 1), pltpu.roll(y, HEAD_DIM // 2, 1))
        return y * cos + partner * sin

    for p in range(N_Q_HEADS // 2):
        cols = slice(p * LANES, (p + 1) * LANES)
        qn_ref[:, cols] = norm_rope(q_ref[:, cols], qg_ref[...]).astype(BF16)
    kn = norm_rope(k_ref[...], kg_ref[...])
    left = lane < HEAD_DIM
    for src_ref, dst_ref in ((None, km_ref), (v_ref, vm_ref)):
        val = kn if src_ref is None else src_ref[...]
        rolled = pltpu.roll(val, HEAD_DIM, 1)
        dst_ref[0] = jnp.where(left, val, 0.0).astype(BF16)
        dst_ref[1] = jnp.where(left, 0.0, rolled).astype(BF16)
        dst_ref[2] = jnp.where(left, rolled, 0.0).astype(BF16)
        dst_ref[3] = jnp.where(left, 0.0, val).astype(BF16)

    span = 3 * BLOCK
    scale = HEAD_DIM ** -0.5
    ii = lax.broadcasted_iota(jnp.int32, (BLOCK, span), 0)
    jj = lax.broadcasted_iota(jnp.int32, (BLOCK, span), 1)

    def block(n, carry):
        q0 = pl.multiple_of(n * BLOCK, BLOCK)
        start = pl.multiple_of(jnp.clip((n - 1) * BLOCK, 0, seq - span), BLOCK)
        valid = jnp.abs((start - q0) + jj - ii) <= WINDOW
        for p in range(N_Q_HEADS // 2):
            cols = slice(p * LANES, (p + 1) * LANES)
            kv = (2 * p) // (N_Q_HEADS // N_KV_HEADS)
            qp = qn_ref[pl.ds(q0, BLOCK), cols]
            o = jnp.zeros((BLOCK, LANES), F32)
            for a in range(2):
                h = 2 * p + a
                kb = km_ref[2 * kv + a, pl.ds(start, span), :]
                s = _dot_nt(qp, kb) * scale
                s = jnp.where(valid, s, -jnp.inf)
                sk = sink_ref[h:h + 1, 0:1]
                m = jnp.maximum(jnp.max(s, axis=-1, keepdims=True), sk)
                e = jnp.exp(s - m)
                den = jnp.sum(e, axis=-1, keepdims=True) + jnp.exp(sk - m)
                pn = (e / den).astype(BF16)
                o = o + _dot(pn, vm_ref[2 * kv + a, pl.ds(start, span), :])
            o_ref[pl.ds(q0, BLOCK), cols] = o
        return carry

    lax.fori_loop(0, seq // BLOCK, block, 0)


def _attention(q, k, v, q_gain, k_gain, sink, bsz, seq):
    inv = ROPE_THETA ** (-jnp.arange(0, HEAD_DIM, 2, dtype=F32) / HEAD_DIM)
    ang = jnp.arange(seq, dtype=F32)[:, None] * inv[None, :]
    cos = jnp.tile(jnp.cos(ang), (1, LANES // (HEAD_DIM // 2)))
    sn = jnp.sin(ang)
    sin = jnp.tile(jnp.concatenate([-sn, sn], axis=1), (1, LANES // HEAD_DIM))
    qg = jnp.tile(q_gain, LANES // HEAD_DIM)[None, :]
    kg = jnp.tile(k_gain, LANES // HEAD_DIM)[None, :]
    sinkb = jnp.broadcast_to(sink[:, None], (N_Q_HEADS, LANES))
    aw = N_Q_HEADS * HEAD_DIM
    kw = N_KV_HEADS * HEAD_DIM
    const = lambda b: (0, 0)
    return pl.pallas_call(
        functools.partial(_attn_kernel, seq=seq),
        grid=(bsz,),
        in_specs=[pl.BlockSpec((seq, aw), lambda b: (b, 0)),
                  pl.BlockSpec((seq, kw), lambda b: (b, 0)),
                  pl.BlockSpec((seq, kw), lambda b: (b, 0)),
                  pl.BlockSpec((seq, LANES), const),
                  pl.BlockSpec((seq, LANES), const),
                  pl.BlockSpec((1, LANES), const),
                  pl.BlockSpec((1, LANES), const),
                  pl.BlockSpec((N_Q_HEADS, LANES), const)],
        out_specs=pl.BlockSpec((seq, aw), lambda b: (b, 0)),
        out_shape=jax.ShapeDtypeStruct((bsz * seq, aw), F32),
        scratch_shapes=[pltpu.VMEM((seq, aw), BF16),
                        pltpu.VMEM((4, seq, LANES), BF16),
                        pltpu.VMEM((4, seq, LANES), BF16)],
        compiler_params=_cparams(("arbitrary",)),
        name="window_attention",
    )(q, k, v, cos, sin, qg, kg, sinkb)


def _merge_kernel(x_ref, mod_ref, ya_ref, yb_ref, ga_ref, gb_ref, wpa_ref, wpb_ref, wo_ref, o_ref):
    pa = _dot(ya_ref[...].astype(BF16), wpa_ref[...])
    pb = _dot(yb_ref[...].astype(BF16), wpb_ref[...])
    merged = _sigmoid(ga_ref[...]) * pa + _sigmoid(gb_ref[...]) * pb
    out = _dot(merged.astype(BF16), wo_ref[...])
    o_ref[...] = x_ref[...] + mod_ref[0][2:3] * out


def _merge(x2, mod3, ya, yb, ga, gb, w_pa_bf, w_pb_bf, w_out_bf, seq, tm=512):
    t, d = x2.shape
    per = seq // tm
    row = lambda w: pl.BlockSpec((tm, w), lambda i: (i, 0))
    full = lambda a: pl.BlockSpec(a.shape, lambda i: (0, 0))
    return pl.pallas_call(
        _merge_kernel,
        grid=(t // tm,),
        in_specs=[row(d), pl.BlockSpec((1, 6, d), lambda i: (i // per, 0, 0)),
                  row(ya.shape[1]), row(yb.shape[1]), row(d), row(d),
                  full(w_pa_bf), full(w_pb_bf), full(w_out_bf)],
        out_specs=row(d),
        out_shape=jax.ShapeDtypeStruct((t, d), F32),
        compiler_params=_cparams(("arbitrary",)),
        name="merge_outproj",
    )(x2, mod3, ya, yb, ga, gb, w_pa_bf, w_pb_bf, w_out_bf)


def _oddeven_merge(lo, hi, r):
    step = r * 2
    if step < hi - lo:
        yield from _oddeven_merge(lo, hi, step)
        yield from _oddeven_merge(lo + r, hi, step)
        yield from ((i, i + r) for i in range(lo + r, hi - r, step))
    else:
        yield (lo, lo + r)


def _oddeven_merge_sort(lo, hi):
    if hi - lo >= 1:
        mid = lo + (hi - lo) // 2
        yield from _oddeven_merge_sort(lo, mid)
        yield from _oddeven_merge_sort(mid + 1, hi)
        yield from _oddeven_merge(lo, hi, 1)


def _topk_rows(s, k, val_ref, idx_ref):
    n, tm = s.shape
    ng = n // SUBLANES
    assert ng & (ng - 1) == 0 and k <= ng
    sub = lax.broadcasted_iota(jnp.int32, (SUBLANES, tm), 0)
    vals = [s[g * SUBLANES:(g + 1) * SUBLANES, :] for g in range(ng)]
    rows = [sub + g * SUBLANES for g in range(ng)]
    for i, j in _oddeven_merge_sort(0, ng - 1):
        keep = (vals[i] > vals[j]) | ((vals[i] == vals[j]) & (rows[i] < rows[j]))
        vals[i], vals[j] = jnp.where(keep, vals[i], vals[j]), jnp.where(keep, vals[j], vals[i])
        rows[i], rows[j] = jnp.where(keep, rows[i], rows[j]), jnp.where(keep, rows[j], rows[i])
    for t in range(k):
        m = jnp.max(vals[0], axis=0, keepdims=True)
        first = jnp.min(jnp.where(vals[0] == m, rows[0], n), axis=0, keepdims=True)
        val_ref[t:t + 1, :] = m
        idx_ref[t:t + 1, :] = first
        if t + 1 < k:
            popped = rows[0] == first
            for g in range(k - 1 - t):
                vals[g] = jnp.where(popped, vals[g + 1], vals[g])
                rows[g] = jnp.where(popped, rows[g + 1], rows[g])


def _peer_score_kernel(x_ref, mod_ref, g_ref, wq_ref, k1_ref, k2_ref, h_ref, idx_ref, gate_ref,
                       v1_ref, i1_ref, v2_ref, i2_ref, cv_ref, ce_ref, it_ref, gt_ref, *, tm):
    m = mod_ref[0]
    h = _norm_mod(x_ref[...], g_ref[...], m[3:4], m[4:5])
    h_ref[...] = h
    q = _dot(h.astype(BF16), wq_ref[...]).astype(BF16)
    half = PEER_QDIM // 2
    neg = jnp.full((1, tm), -jnp.inf, F32)
    for hd in range(PEER_HEADS):
        q1 = q[:, hd * PEER_QDIM: hd * PEER_QDIM + half]
        q2 = q[:, hd * PEER_QDIM + half: (hd + 1) * PEER_QDIM]
        _topk_rows(_dot_nt(k1_ref[...], q1), PEER_TOPK, v1_ref, i1_ref)
        _topk_rows(_dot_nt(k2_ref[...], q2), PEER_TOPK, v2_ref, i2_ref)
        for r, (i, j) in enumerate(_STAIR):
            cv_ref[r:r + 1, :] = v1_ref[i:i + 1, :] + v2_ref[j:j + 1, :]
            ce_ref[r:r + 1, :] = i1_ref[i:i + 1, :] * N_KEYS + i2_ref[j:j + 1, :]
        for r in range(len(_STAIR), _STAIR_ROWS):
            cv_ref[r:r + 1, :] = neg
            ce_ref[r:r + 1, :] = jnp.zeros((1, tm), jnp.int32)
        cand = cv_ref[...]
        ce = ce_ref[...]
        row = lax.broadcasted_iota(jnp.int32, cand.shape, 0)
        vals = []
        for kk in range(PEER_TOPK):
            mx = jnp.max(cand, axis=0, keepdims=True)
            first = jnp.min(jnp.where(cand == mx, row, _STAIR_ROWS), axis=0, keepdims=True)
            sel = row == first
            it_ref[hd * PEER_TOPK + kk: hd * PEER_TOPK + kk + 1, :] = jnp.sum(jnp.where(sel, ce, 0), axis=0, keepdims=True)
            vals.append(mx)
            cand = jnp.where(sel, -jnp.inf, cand)
        ex = [jnp.exp(v - vals[0]) for v in vals]
        tot = ex[0]
        for e in ex[1:]:
            tot = tot + e
        for kk in range(PEER_TOPK):
            gt_ref[hd * PEER_TOPK + kk: hd * PEER_TOPK + kk + 1, :] = ex[kk] / tot
    idx_ref[...] = it_ref[...].T
    gate_ref[...] = gt_ref[...].T


def _peer_score(x2, mod3, gain, wq_bf, k1_bf, k2_bf, seq, tm=256):
    t, d = x2.shape
    per = seq // tm
    slots = PEER_HEADS * PEER_TOPK
    full = lambda a: pl.BlockSpec(a.shape, lambda i: (0, 0))
    return pl.pallas_call(
        functools.partial(_peer_score_kernel, tm=tm),
        grid=(t // tm,),
        in_specs=[pl.BlockSpec((tm, d), lambda i: (i, 0)),
                  pl.BlockSpec((1, 6, d), lambda i: (i // per, 0, 0)),
                  pl.BlockSpec((1, d), lambda i: (0, 0)),
                  full(wq_bf), full(k1_bf), full(k2_bf)],
        out_specs=[pl.BlockSpec((tm, d), lambda i: (i, 0)),
                   pl.BlockSpec((tm, slots), lambda i: (i, 0)),
                   pl.BlockSpec((tm, slots), lambda i: (i, 0))],
        out_shape=[jax.ShapeDtypeStruct((t, d), F32),
                   jax.ShapeDtypeStruct((t, slots), jnp.int32),
                   jax.ShapeDtypeStruct((t, slots), F32)],
        scratch_shapes=[pltpu.VMEM((PEER_TOPK, tm), F32), pltpu.VMEM((PEER_TOPK, tm), jnp.int32),
                        pltpu.VMEM((PEER_TOPK, tm), F32), pltpu.VMEM((PEER_TOPK, tm), jnp.int32),
                        pltpu.VMEM((_STAIR_ROWS, tm), F32), pltpu.VMEM((_STAIR_ROWS, tm), jnp.int32),
                        pltpu.VMEM((slots, tm), jnp.int32), pltpu.VMEM((slots, tm), F32)],
        compiler_params=_cparams(("arbitrary",)),
        name="peer_score_topk",
    )(x2, mod3, gain.reshape(1, d), wq_bf, k1_bf, k2_bf)


def _peer_gather_kernel(idx_ref, idxn_ref, tab_ref, h_ref, gate_ref, x_ref, mod_ref, after_ref, o_ref,
                        buf_ref, sem_ref, *, tc, d, nsteps):
    slots = PEER_HEADS * PEER_TOPK
    step = pl.program_id(0)

    def issue_token(chunk, t):
        ids_ref, c = (idx_ref, chunk) if chunk < GATHER_RING else (idxn_ref, chunk - GATHER_RING)
        for j in range(slots):
            pltpu.make_async_copy(tab_ref.at[ids_ref[c * tc + t, j]],
                                  buf_ref.at[c, t, pl.ds(j, 1), :],
                                  sem_ref.at[c]).start(priority=j % 2)

    def wait(slot):
        pltpu.make_async_copy(buf_ref.at[slot], buf_ref.at[slot], sem_ref.at[slot]).wait()

    eye = (lax.broadcasted_iota(jnp.int32, (slots, slots), 0) ==
           lax.broadcasted_iota(jnp.int32, (slots, slots), 1))
    gt2 = mod_ref[0][5:6]

    def mix_token(slot, t):
        r = slot * tc + t
        word = buf_ref[slot, t]
        u = lax.bitcast_convert_type(word << 16, F32)
        prod = u * h_ref[r:r + 1, :]
        part = prod[:, 0:LANES]
        for c in range(1, d // LANES):
            part = part + prod[:, c * LANES:(c + 1) * LANES]
        a = jnp.sum(part, axis=1, keepdims=True)
        grow = jnp.broadcast_to(gate_ref[r:r + 1, :], (slots, slots))
        gcol = jnp.sum(jnp.where(eye, grow, 0.0), axis=1, keepdims=True)
        w = gcol * (0.5 * a * (1.0 + lax.erf(a * (2.0 ** -0.5))))
        v = lax.bitcast_convert_type(buf_ref[slot, t] & jnp.uint32(0xFFFF0000), F32)
        y = jnp.sum(v * w, axis=0, keepdims=True)
        o_ref[r:r + 1, :] = x_ref[r:r + 1, :] + gt2 * y

    @pl.when(step == 0)
    def _():
        for c in range(GATHER_AHEAD):
            for t in range(tc):
                issue_token(c, t)

    for p in range(GATHER_RING):
        wait(p)
        for t in range(tc):
            issue_token(p + GATHER_AHEAD, t)
            mix_token(p, t)

    @pl.when(step == nsteps - 1)
    def _():
        for c in range(GATHER_AHEAD):
            wait(c)


def _peer_gather(x2, mod3, h2, idx, gate, table, seq, tok0, ntok, after, tc=8):
    d = x2.shape[1]
    slots = idx.shape[1]
    tb = GATHER_RING * tc
    nsteps = ntok // tb
    off = tok0 // tb
    per = seq // tb
    row = lambda w: pl.BlockSpec((tb, w), lambda i: (i + off, 0))
    return pl.pallas_call(
        functools.partial(_peer_gather_kernel, tc=tc, d=d, nsteps=nsteps),
        grid=(nsteps,),
        in_specs=[pl.BlockSpec((tb, slots), lambda i: (i + off, 0), memory_space=pltpu.SMEM),
                  pl.BlockSpec((tb, slots), lambda i: (jnp.minimum(i + 1, nsteps - 1) + off, 0),
                               memory_space=pltpu.SMEM),
                  pl.BlockSpec(memory_space=pl.ANY),
                  row(d), row(slots), row(d),
                  pl.BlockSpec((1, 6, d), lambda i: ((i + off) // per, 0, 0)),
                  pl.BlockSpec(memory_space=pl.ANY)],
        out_specs=pl.BlockSpec((tb, d), lambda i: (i, 0)),
        out_shape=jax.ShapeDtypeStruct((ntok, d), F32),
        scratch_shapes=[pltpu.VMEM((GATHER_RING, tc, slots, d), jnp.uint32),
                        pltpu.SemaphoreType.DMA((GATHER_RING,))],
        compiler_params=_cparams(("arbitrary",)),
        name="peer_gather_mix",
    )(idx, idx, table.reshape(table.shape[0], 1, d), h2, gate, x2, mod3, after)


def _pack_expert_tables(u_tab, v_tab):
    ub = lax.bitcast_convert_type(u_tab.astype(BF16), jnp.uint16).astype(jnp.uint32)
    vb = lax.bitcast_convert_type(v_tab.astype(BF16), jnp.uint16).astype(jnp.uint32)
    return (vb << 16) | ub


SC_LANES = 16
SC_CORES = 2
SC_SUBCORES = 16
SC_GROUP = 8
SC_HALF = 64
SC_SEQS = 12
TC_FIRST_SEQS = 7


def _pack_pairs(tab):
    e, d = tab.shape
    bits = lax.bitcast_convert_type(tab.astype(BF16), jnp.uint16).astype(jnp.uint32)
    bits = bits.reshape(e, d // (2 * SC_LANES), 2, SC_LANES)
    return ((bits[:, :, 1, :] << 16) | bits[:, :, 0, :]).reshape(e, d // 2)


def _sc_worker_base(n_per):
    return (lax.axis_index("s") * SC_CORES + lax.axis_index("c")) * n_per


def _sc_halves(tab_hbm, idx_v, rows_v, sems, tt, half):
    ids = idx_v.at[tt, pl.ds(half * SC_HALF, SC_HALF)]
    return pltpu.make_async_copy(tab_hbm.at[ids], rows_v.at[half], sems.at[half])


def _sc_token_loop(tab_hbm, idx_v, rows_v, sems, compute_half):
    _sc_halves(tab_hbm, idx_v, rows_v, sems, 0, 0).start()

    def tok(tt, carry):
        _sc_halves(tab_hbm, idx_v, rows_v, sems, tt, 1).start()
        _sc_halves(tab_hbm, idx_v, rows_v, sems, tt, 0).wait()
        compute_half(tt, 0)

        @pl.when(tt + 1 < SC_GROUP)
        def _():
            _sc_halves(tab_hbm, idx_v, rows_v, sems, tt + 1, 0).start()

        _sc_halves(tab_hbm, idx_v, rows_v, sems, tt, 1).wait()
        compute_half(tt, 1)
        return carry

    lax.fori_loop(0, SC_GROUP, tok, 0)


def _unpack_pair(word):
    lo = lax.bitcast_convert_type(word << 16, F32)
    hi = lax.bitcast_convert_type(word & jnp.uint32(0xFFFF0000), F32)
    return lo, hi


def _sc_dot_body(tab_hbm, idx_hbm, h_hbm, a_hbm, idx_v, h_v, rows_v, a_v, tr_v, sems, *, tok0, n_per, d):
    base = _sc_worker_base(n_per)
    lane = lax.iota(jnp.int32, SC_LANES)
    nk = d // (2 * SC_LANES)

    def compute_half(tt, half):
        def rows16(gi, carry):
            accs = [jnp.zeros((SC_LANES,), F32) for _ in range(SC_LANES)]
            for k in range(nk):
                hlo = h_v[tt, pl.ds(2 * SC_LANES * k, SC_LANES)]
                hhi = h_v[tt, pl.ds(2 * SC_LANES * k + SC_LANES, SC_LANES)]
                for q in range(SC_LANES):
                    lo, hi = _unpack_pair(rows_v[half, gi * SC_LANES + q, pl.ds(SC_LANES * k, SC_LANES)])
                    accs[q] = accs[q] + lo * hlo + hi * hhi
            for q in range(SC_LANES):
                tr_v[q, :] = accs[q]
            tot = plsc.load_gather(tr_v, [lane, jnp.zeros((SC_LANES,), jnp.int32)])
            for c in range(1, SC_LANES):
                tot = tot + plsc.load_gather(tr_v, [lane, jnp.full((SC_LANES,), c, jnp.int32)])
            a_v[tt, pl.ds(half * SC_HALF + gi * SC_LANES, SC_LANES)] = tot
            return carry
        lax.fori_loop(0, SC_HALF // SC_LANES, rows16, 0)

    def group(g, carry):
        loc = base + g * SC_GROUP
        pltpu.sync_copy(idx_hbm.at[pl.ds(tok0 + loc, SC_GROUP)], idx_v)
        pltpu.sync_copy(h_hbm.at[pl.ds(tok0 + loc, SC_GROUP)], h_v)
        _sc_token_loop(tab_hbm, idx_v, rows_v, sems, compute_half)
        pltpu.sync_copy(a_v, a_hbm.at[pl.ds(loc, SC_GROUP)])
        return carry

    lax.fori_loop(0, n_per // SC_GROUP, group, 0)


def _sc_mix_body(tab_hbm, idx_hbm, w_hbm, y_hbm, idx_v, w_v, rows_v, y_v, sems, *, tok0, n_per, d):
    base = _sc_worker_base(n_per)
    nk = d // (2 * SC_LANES)
    nq = 2
    kq = nk // nq

    def compute_half(tt, half):
        ttv = jnp.full((SC_LANES,), tt, jnp.int32)
        for piece in range(nq):
            col0 = piece * kq * 2 * SC_LANES
            if half == 0:
                accs = tuple(jnp.zeros((SC_LANES,), F32) for _ in range(2 * kq))
            else:
                accs = tuple(y_v[tt, pl.ds(col0 + SC_LANES * m, SC_LANES)] for m in range(2 * kq))

            def row(r, accs):
                wsplat = plsc.load_gather(w_v, [ttv, jnp.full((SC_LANES,), half * SC_HALF + r, jnp.int32)])
                out = []
                for k in range(kq):
                    lo, hi = _unpack_pair(rows_v[half, r, pl.ds(SC_LANES * (piece * kq + k), SC_LANES)])
                    out.append(accs[2 * k] + wsplat * lo)
                    out.append(accs[2 * k + 1] + wsplat * hi)
                return tuple(out)

            accs = lax.fori_loop(0, SC_HALF, row, accs)
            for m in range(2 * kq):
                y_v[tt, pl.ds(col0 + SC_LANES * m, SC_LANES)] = accs[m]

    def group(g, carry):
        loc = base + g * SC_GROUP
        pltpu.sync_copy(idx_hbm.at[pl.ds(tok0 + loc, SC_GROUP)], idx_v)
        pltpu.sync_copy(w_hbm.at[pl.ds(loc, SC_GROUP)], w_v)
        _sc_token_loop(tab_hbm, idx_v, rows_v, sems, compute_half)
        pltpu.sync_copy(y_v, y_hbm.at[pl.ds(loc, SC_GROUP)])
        return carry

    lax.fori_loop(0, n_per // SC_GROUP, group, 0)


def _sc_mesh():
    return plsc.VectorSubcoreMesh(core_axis_name="c", subcore_axis_name="s")


def _sc_dot(u_words, idx, h2, tok0, ntok):
    d = h2.shape[1]
    slots = idx.shape[1]
    n_per = ntok // (SC_CORES * SC_SUBCORES)
    return pl.kernel(
        functools.partial(_sc_dot_body, tok0=tok0, n_per=n_per, d=d),
        out_type=jax.ShapeDtypeStruct((ntok, slots), F32),
        mesh=_sc_mesh(),
        scratch_types=[pltpu.VMEM((SC_GROUP, slots), jnp.int32),
                       pltpu.VMEM((SC_GROUP, d), F32),
                       pltpu.VMEM((2, SC_HALF, d // 2), jnp.uint32),
                       pltpu.VMEM((SC_GROUP, slots), F32),
                       pltpu.VMEM((SC_LANES, SC_LANES), F32),
                       pltpu.SemaphoreType.DMA((2,))],
        compiler_params=pltpu.CompilerParams(needs_layout_passes=False),
        name="peer_sc_dot",
    )(u_words, idx, h2)


def _sc_mix(v_words, idx, w, tok0, ntok, d):
    slots = idx.shape[1]
    n_per = ntok // (SC_CORES * SC_SUBCORES)
    return pl.kernel(
        functools.partial(_sc_mix_body, tok0=tok0, n_per=n_per, d=d),
        out_type=jax.ShapeDtypeStruct((ntok, d), F32),
        mesh=_sc_mesh(),
        scratch_types=[pltpu.VMEM((SC_GROUP, slots), jnp.int32),
                       pltpu.VMEM((SC_GROUP, slots), F32),
                       pltpu.VMEM((2, SC_HALF, d // 2), jnp.uint32),
                       pltpu.VMEM((SC_GROUP, d), F32),
                       pltpu.SemaphoreType.DMA((2,))],
        compiler_params=pltpu.CompilerParams(needs_layout_passes=False),
        name="peer_sc_mix",
    )(v_words, idx, w)


def _gelu_gate_kernel(a_ref, g_ref, after_ref, o_ref):
    a = a_ref[...]
    o_ref[...] = g_ref[...] * (0.5 * a * (1.0 + lax.erf(a * (2.0 ** -0.5))))


def _gelu_gate(a, gate, tok0, after, tm=2048):
    n, slots = a.shape
    off = tok0 // tm
    return pl.pallas_call(
        _gelu_gate_kernel,
        grid=(n // tm,),
        in_specs=[pl.BlockSpec((tm, slots), lambda i: (i, 0)),
                  pl.BlockSpec((tm, slots), lambda i: (i + off, 0)),
                  pl.BlockSpec(memory_space=pl.ANY)],
        out_specs=pl.BlockSpec((tm, slots), lambda i: (i, 0)),
        out_shape=jax.ShapeDtypeStruct((n, slots), F32),
        compiler_params=_cparams(("arbitrary",)),
        name="peer_gelu_gate",
    )(a, gate, after)


def _residual_kernel(x_ref, mod_ref, y_ref, o_ref):
    o_ref[...] = x_ref[...] + mod_ref[0][5:6] * y_ref[...]


def _residual(x2, mod3, y, tok0, seq, tm=512):
    n, d = y.shape
    off = tok0 // tm
    per = seq // tm
    return pl.pallas_call(
        _residual_kernel,
        grid=(n // tm,),
        in_specs=[pl.BlockSpec((tm, d), lambda i: (i + off, 0)),
                  pl.BlockSpec((1, 6, d), lambda i: ((i + off) // per, 0, 0)),
                  pl.BlockSpec((tm, d), lambda i: (i, 0))],
        out_specs=pl.BlockSpec((tm, d), lambda i: (i, 0)),
        out_shape=jax.ShapeDtypeStruct((n, d), F32),
        compiler_params=_cparams(("arbitrary",)),
        name="peer_residual",
    )(x2, mod3, y)


def kernel(x_prompt, x_sample, c_prompt, c_sample, w_mod, b_mod, g_norm1, g_norm2, w_in, conv_w, conv_b, f_w1, f_b1, f_freq, f_w2, f_b2, f_w3, f_bias, q_gain, k_gain, sink, w_pa, w_pb, w_out, peer_wq, peer_k1, peer_k2, peer_u, peer_v):
    depth = w_mod.shape[0]
    bp, seq, d = x_prompt.shape
    bs = x_sample.shape[0]
    assert x_sample.shape[1] == seq
    bsz = bp + bs
    x = jnp.concatenate([x_prompt, x_sample], axis=0).reshape(bsz * seq, d)
    c = jnp.concatenate([c_prompt, c_sample], axis=0)

    hyw = w_pa.shape[1]
    aw = w_pb.shape[1]
    kw = N_KV_HEADS * HEAD_DIM
    widths = (HY_ORDER + 1) * hyw, aw, kw, kw, d, d
    cblk = hyw // 256

    mod = _modulation(c, w_mod.astype(BF16), b_mod)
    fc, fs = _dft_tables(seq)

    for l in range(depth):
        mod3 = mod[l].reshape(bsz, 6, d)
        hy, q, k, v, ga, gb = _inproj(x, mod3, g_norm1[l], w_in[l].astype(BF16), seq, widths)
        hcat = _filters_time(seq, f_w1[l], f_b1[l], f_freq[l], f_w2[l], f_b2[l], f_w3[l])
        spec_a, spec_b, nyq = _filter_spectra(fc, fs, hcat)
        cw, cb = conv_w[l], conv_b[l][None, :]
        zz = _long_conv(fc, fs, hy, 0, hy, cblk, spec_a, spec_b, nyq, 0, f_bias[l][0:1], cw, cb,
                        0, cblk, bsz, seq, conv_u=True)
        ya = _long_conv(fc, fs, zz, 0, hy, 2 * cblk, spec_a, spec_b, nyq, cblk, f_bias[l][1:2], cw, cb,
                        0, 2 * cblk, bsz, seq, conv_u=False)
        yb = _attention(q, k, v, q_gain[l], k_gain[l], sink[l], bsz, seq)
        x = _merge(x, mod3, ya, yb, ga, gb, w_pa[l].astype(BF16), w_pb[l].astype(BF16), w_out[l].astype(BF16), seq)
        h2, idx, gate = _peer_score(x, mod3, g_norm2[l], peer_wq[l].astype(BF16),
                                    peer_k1[l].astype(BF16), peer_k2[l].astype(BF16), seq)
        table = _pack_expert_tables(peer_u[l], peer_v[l])
        t_tc = (bsz - SC_SEQS) * seq
        t_sc = SC_SEQS * seq
        t_g1 = TC_FIRST_SEQS * seq
        a_sc = _sc_dot(_pack_pairs(peer_u[l]), idx, h2, t_tc, t_sc)
        x_g1 = _peer_gather(x, mod3, h2, idx, gate, table, seq, 0, t_g1, gate)
        w_sc = _gelu_gate(a_sc, gate, t_tc, x_g1)
        y_sc = _sc_mix(_pack_pairs(peer_v[l]), idx, w_sc, t_tc, t_sc, d)
        x_g2 = _peer_gather(x, mod3, h2, idx, gate, table, seq, t_g1, t_tc - t_g1, w_sc)
        x = jnp.concatenate([x_g1, x_g2, _residual(x, mod3, y_sc, t_tc, seq)], axis=0)

    x = x.reshape(bsz, seq, d)
    return x[:bp], x[bp:]
```

```python
import functools
import math

import jax
import jax.numpy as jnp
import numpy as np
from jax import lax
from jax.experimental import pallas as pl
from jax.experimental.pallas import tpu as pltpu
from jax.experimental.pallas import tpu_sc as plsc

F32 = jnp.float32
BF16 = jnp.bfloat16

EPS = 1e-6
HEAD_DIM = 64
N_Q_HEADS = 8
N_KV_HEADS = 2
WINDOW = 128
BLOCK = 128
ROPE_THETA = 10000.0
HY_ORDER = 2
N_DIR = 2
FILT_BANDS = 16
DECAY_TARGET = 1e-2
FAST_DECAY_PCT = 0.3
SLOW_DECAY_PCT = 1.5
PEER_HEADS = 8
N_KEYS = 128
PEER_TOPK = 16
PEER_QDIM = 256
LANES = 128
SUBLANES = 8
VMEM_LIMIT = 56 * 1024 * 1024
GATHER_RING = 4
GATHER_AHEAD = 2

_STAIR = [(i, j) for i in range(PEER_TOPK) for j in range(PEER_TOPK) if (i + 1) * (j + 1) <= PEER_TOPK]
_STAIR_ROWS = -(-len(_STAIR) // SUBLANES) * SUBLANES


def _cparams(sem, vmem=VMEM_LIMIT):
    return pltpu.CompilerParams(dimension_semantics=sem, vmem_limit_bytes=vmem)


def _dot(a, b):
    return jnp.dot(a, b, preferred_element_type=F32)


def _dot_nt(a, b):
    return lax.dot_general(a, b, (((1,), (1,)), ((), ())), preferred_element_type=F32)


def _dot_hi(a, b):
    return jnp.dot(a, b, preferred_element_type=F32, precision=lax.Precision.HIGHEST)


def _sigmoid(x):
    return 1.0 / (1.0 + jnp.exp(-x))


def _mod_kernel(c_ref, w_ref, b_ref, o_ref):
    c = c_ref[...]
    s = c * _sigmoid(c)
    o_ref[0] = _dot(s.astype(BF16), w_ref[0]) + b_ref[0]


def _modulation(c, w_mod_bf, b_mod):
    depth, d, n6 = w_mod_bf.shape
    bsz = c.shape[0]
    tn = 1536
    return pl.pallas_call(
        _mod_kernel,
        grid=(depth, n6 // tn),
        in_specs=[pl.BlockSpec((bsz, d), lambda l, j: (0, 0)),
                  pl.BlockSpec((1, d, tn), lambda l, j: (l, 0, j)),
                  pl.BlockSpec((1, 1, tn), lambda l, j: (l, 0, j))],
        out_specs=pl.BlockSpec((1, bsz, tn), lambda l, j: (l, 0, j)),
        out_shape=jax.ShapeDtypeStruct((depth, bsz, n6), F32),
        compiler_params=_cparams(("arbitrary", "arbitrary")),
        name="adaln_mod",
    )(c, w_mod_bf, b_mod.reshape(depth, 1, n6))


def _norm_mod(x, gain, shift, scale):
    y = x * lax.rsqrt(jnp.mean(x * x, axis=-1, keepdims=True) + EPS)
    return (y * gain) * (1.0 + scale) + shift


def _inproj_kernel(x_ref, mod_ref, g_ref, w_ref, hy_ref, q_ref, k_ref, v_ref, ga_ref, gb_ref, *, splits):
    m = mod_ref[0]
    h = _norm_mod(x_ref[...], g_ref[...], m[0:1], m[1:2])
    z = _dot(h.astype(BF16), w_ref[...])
    outs = (hy_ref, q_ref, k_ref, v_ref, ga_ref, gb_ref)
    lo = 0
    for ref, hi in zip(outs, splits):
        ref[...] = z[:, lo:hi]
        lo = hi


def _inproj(x2, mod3, gain, w_in_bf, seq, widths, tm=256):
    t, d = x2.shape
    ncols = w_in_bf.shape[1]
    per = seq // tm
    splits = tuple(int(s) for s in np.cumsum(widths))
    return pl.pallas_call(
        functools.partial(_inproj_kernel, splits=splits),
        grid=(t // tm,),
        in_specs=[pl.BlockSpec((tm, d), lambda i: (i, 0)),
                  pl.BlockSpec((1, 6, d), lambda i: (i // per, 0, 0)),
                  pl.BlockSpec((1, d), lambda i: (0, 0)),
                  pl.BlockSpec((d, ncols), lambda i: (0, 0))],
        out_specs=[pl.BlockSpec((tm, w), lambda i: (i, 0)) for w in widths],
        out_shape=[jax.ShapeDtypeStruct((t, w), F32) for w in widths],
        compiler_params=_cparams(("arbitrary",)),
        name="inproj",
    )(x2, mod3, gain.reshape(1, d), w_in_bf)


def _filter_kernel(z_ref, w1_ref, b1_ref, fr_ref, w2_ref, b2_ref, w3_ref, ad_ref, o_ref):
    z = z_ref[0]
    fr = fr_ref[...]
    a = jnp.sin(fr * (_dot_hi(z, w1_ref[...]) + b1_ref[...]))
    a = jnp.sin(fr * (_dot_hi(a, w2_ref[...]) + b2_ref[...]))
    h = _dot_hi(a, w3_ref[0])
    h = h * jnp.exp(-z[:, 0:1] * ad_ref[...])
    row = lax.broadcasted_iota(jnp.int32, h.shape, 0)
    dead = (pl.program_id(0) == 1) & (pl.program_id(1) == 0) & (row == 0)
    o_ref[0] = jnp.where(dead, 0.0, h)


def _filters_time(seq, f_w1, f_b1, f_freq, f_w2, f_b2, f_w3, tm=512):
    hidden = f_w1.shape[1]
    cw = f_w3.shape[1] // (HY_ORDER * N_DIR)
    t = jnp.linspace(0.0, 1.0, seq, dtype=F32)[:, None]
    w = 2.0 * math.pi * jnp.arange(seq, dtype=F32)[:, None] / seq
    bands = jnp.linspace(1e-4, FILT_BANDS - 1, FILT_BANDS, dtype=F32)[None, :]
    z = jnp.concatenate([t, jnp.cos(bands * w), -jnp.sin(bands * w)], axis=-1)
    emb = z.shape[1]
    z = jnp.pad(z, ((0, 0), (0, LANES - emb)))
    zcat = jnp.stack([z, jnp.concatenate([z[:1], z[:0:-1]], axis=0)], axis=0)
    w1p = jnp.pad(f_w1, ((0, LANES - emb), (0, 0)))
    w3d = f_w3.reshape(hidden, HY_ORDER, N_DIR, cw).transpose(2, 0, 1, 3).reshape(N_DIR, hidden, HY_ORDER * cw)
    max_decay = math.log(DECAY_TARGET) / FAST_DECAY_PCT
    min_decay = math.log(DECAY_TARGET) / SLOW_DECAY_PCT
    ad = jnp.abs(jnp.linspace(min_decay, max_decay, cw, dtype=F32))
    ad = jnp.tile(ad, HY_ORDER)[None, :]
    oc = HY_ORDER * cw
    return pl.pallas_call(
        _filter_kernel,
        grid=(N_DIR, seq // tm),
        in_specs=[pl.BlockSpec((1, tm, LANES), lambda g, r: (g, r, 0)),
                  pl.BlockSpec((LANES, hidden), lambda g, r: (0, 0)),
                  pl.BlockSpec((1, hidden), lambda g, r: (0, 0)),
                  pl.BlockSpec((1, hidden), lambda g, r: (0, 0)),
                  pl.BlockSpec((hidden, hidden), lambda g, r: (0, 0)),
                  pl.BlockSpec((1, hidden), lambda g, r: (0, 0)),
                  pl.BlockSpec((1, hidden, oc), lambda g, r: (g, 0, 0)),
                  pl.BlockSpec((1, oc), lambda g, r: (0, 0))],
        out_specs=pl.BlockSpec((1, tm, oc), lambda g, r: (g, r, 0)),
        out_shape=jax.ShapeDtypeStruct((N_DIR, seq, oc), F32),
        compiler_params=_cparams(("arbitrary", "arbitrary")),
        name="hyena_filter_mlp",
    )(zcat, w1p, f_b1[None, :], f_freq[None, :], f_w2, f_b2[None, :], w3d, ad)


def _dft_tables(seq):
    n2 = 2 * seq
    f = jnp.arange(seq, dtype=jnp.int32)
    ft = (f[:, None] * f[None, :]) % n2
    ang = ft.astype(F32) * (2.0 * math.pi / n2)
    return jnp.cos(ang).astype(BF16), jnp.sin(ang).astype(BF16)


def _spec_kernel(fc_ref, fs_ref, h_ref, a_ref, b_ref, nyq_ref, *, seq, tf):
    hlo = h_ref[0]
    hhi = h_ref[1]
    hlo_b = hlo.astype(BF16)
    hhi_b = hhi.astype(BF16)
    f = pl.program_id(1) * tf + lax.broadcasted_iota(jnp.int32, (tf, 1), 0)
    sgn = jnp.where(f % 2 == 0, 1.0, -1.0)
    fc = fc_ref[...]
    fs = fs_ref[...]
    hr = _dot(fc, hlo_b) + sgn * _dot(fc, hhi_b)
    hs = _dot(fs, hlo_b) + sgn * _dot(fs, hhi_b)
    w = jnp.where(f == 0, 1.0, 2.0) * (1.0 / (2 * seq))
    a_ref[...] = w * hr
    b_ref[...] = -(w * hs)
    t = lax.broadcasted_iota(jnp.int32, (seq, 1), 0)
    alt = jnp.where(t % 2 == 0, 1.0, -1.0)
    nyq = jnp.sum(alt * (hlo + hhi), axis=0, keepdims=True) * (1.0 / (2 * seq))
    nyq_ref[...] = jnp.broadcast_to(nyq, nyq_ref.shape)


def _filter_spectra(fc, fs, hcat, tf=512, tcol=512):
    seq = fc.shape[0]
    oc = hcat.shape[2]
    return pl.pallas_call(
        functools.partial(_spec_kernel, seq=seq, tf=tf),
        grid=(oc // tcol, seq // tf),
        in_specs=[pl.BlockSpec((tf, seq), lambda j, i: (i, 0)),
                  pl.BlockSpec((tf, seq), lambda j, i: (i, 0)),
                  pl.BlockSpec((2, seq, tcol), lambda j, i: (0, 0, j))],
        out_specs=[pl.BlockSpec((tf, tcol), lambda j, i: (i, j)),
                   pl.BlockSpec((tf, tcol), lambda j, i: (i, j)),
                   pl.BlockSpec((SUBLANES, tcol), lambda j, i: (0, j))],
        out_shape=[jax.ShapeDtypeStruct((seq, oc), F32),
                   jax.ShapeDtypeStruct((seq, oc), F32),
                   jax.ShapeDtypeStruct((SUBLANES, oc), F32)],
        compiler_params=_cparams(("arbitrary", "arbitrary")),
        name="hyena_filter_spectra",
    )(fc, fs, hcat)


def _shortconv(x, w_ref, b_ref, seq):
    row = lax.broadcasted_iota(jnp.int32, (seq, 1), 0)
    xm = jnp.where(row == 0, 0.0, pltpu.roll(x, 1, 0))
    xp = jnp.where(row == seq - 1, 0.0, pltpu.roll(x, seq - 1, 0))
    return xm * w_ref[0:1, :] + x * w_ref[1:2, :] + xp * w_ref[2:3, :] + b_ref[...]


def _conv_kernel(fc_ref, fs_ref, u_ref, g_ref, a_ref, b_ref, nyq_ref, bias_ref,
                 cwu_ref, cbu_ref, cwg_ref, cbg_ref, o_ref, acc_ref, *, seq, fb, conv_u):
    u = u_ref[...]
    if conv_u:
        u = _shortconv(u, cwu_ref, cbu_ref, seq)
    gate = _shortconv(g_ref[...], cwg_ref, cbg_ref, seq)
    ub = u.astype(BF16)
    for c in range(seq // fb):
        rows = slice(c * fb, (c + 1) * fb)
        ur = _dot(fc_ref[rows, :], ub)
        us = _dot(fs_ref[rows, :], ub)
        a = a_ref[rows, :]
        b = b_ref[rows, :]
        qr = (ur * a + us * b).astype(BF16)
        qi = (us * a - ur * b).astype(BF16)
        part = _dot(fc_ref[:, rows], qr) + _dot(fs_ref[:, rows], qi)
        if c == 0:
            acc_ref[...] = part
        else:
            acc_ref[...] += part
    t = lax.broadcasted_iota(jnp.int32, (seq, 1), 0)
    alt = jnp.where(t % 2 == 0, 1.0, -1.0)
    unyq = jnp.sum(alt * u, axis=0, keepdims=True)
    y = acc_ref[...] + alt * (unyq * nyq_ref[0:1, :]) + bias_ref[...] * u
    o_ref[...] = gate * y


def _long_conv(fc, fs, u_src, u_blk0, g_src, g_blk0, spec_a, spec_b, nyq, s_blk0, bias, cw, cb,
               cu_blk0, cg_blk0, bsz, seq, conv_u, tc=256, fb=512):
    nct = 512 // tc
    t = bsz * seq
    const = lambda j, b: (0, 0)
    return pl.pallas_call(
        functools.partial(_conv_kernel, seq=seq, fb=fb, conv_u=conv_u),
        grid=(nct, bsz),
        in_specs=[pl.BlockSpec(memory_space=pltpu.VMEM),
                  pl.BlockSpec(memory_space=pltpu.VMEM),
                  pl.BlockSpec((seq, tc), lambda j, b: (b, u_blk0 + j)),
                  pl.BlockSpec((seq, tc), lambda j, b: (b, g_blk0 + j)),
                  pl.BlockSpec((seq, tc), lambda j, b: (0, s_blk0 + j)),
                  pl.BlockSpec((seq, tc), lambda j, b: (0, s_blk0 + j)),
                  pl.BlockSpec((SUBLANES, tc), lambda j, b: (0, s_blk0 + j)),
                  pl.BlockSpec((1, tc), lambda j, b: (0, j)),
                  pl.BlockSpec((3, tc), lambda j, b: (0, cu_blk0 + j)),
                  pl.BlockSpec((1, tc), lambda j, b: (0, cu_blk0 + j)),
                  pl.BlockSpec((3, tc), lambda j, b: (0, cg_blk0 + j)),
                  pl.BlockSpec((1, tc), lambda j, b: (0, cg_blk0 + j))],
        out_specs=pl.BlockSpec((seq, tc), lambda j, b: (b, j)),
        out_shape=jax.ShapeDtypeStruct((t, 512), F32),
        scratch_shapes=[pltpu.VMEM((seq, tc), F32)],
        compiler_params=_cparams(("arbitrary", "arbitrary")),
        name="hyena_long_conv_u" if conv_u else "hyena_long_conv",
    )(fc, fs, u_src, g_src, spec_a, spec_b, nyq, bias, cw, cb, cw, cb)


def _attn_kernel(q_ref, k_ref, v_ref, cos_ref, sin_ref, qg_ref, kg_ref, sink_ref, o_ref,
                 qn_ref, km_ref, vm_ref, *, seq):
    lane = lax.broadcasted_iota(jnp.int32, (1, LANES), 1)
    r = lax.broadcasted_iota(jnp.int32, (LANES, LANES), 0) // HEAD_DIM
    c = lax.broadcasted_iota(jnp.int32, (LANES, LANES), 1) // HEAD_DIM
    bd = jnp.where(r == c, 1.0, 0.0).astype(BF16)
    first_half = (lane % HEAD_DIM) < (HEAD_DIM // 2)
    cos = cos_ref[...]
    sin = sin_ref[...]

    def norm_rope(x, gain):
        sq = x * x
        hi = sq.astype(BF16)
        lo = (sq - hi.astype(F32)).astype(BF16)
        ss = _dot(hi, bd) + _dot(lo, bd)
        y = (x * lax.rsqrt(ss * (1.0 / HEAD_DIM) + EPS)) * gain
        partner = jnp.where(first_half, pltpu.roll(y, LANES - HEAD_DIM // 2, 1), pltpu.roll(y, HEAD_DIM // 2, 1))
        return y * cos + partner * sin

    for p in range(N_Q_HEADS // 2):
        cols = slice(p * LANES, (p + 1) * LANES)
        qn_ref[:, cols] = norm_rope(q_ref[:, cols], qg_ref[...]).astype(BF16)
    kn = norm_rope(k_ref[...], kg_ref[...])
    left = lane < HEAD_DIM
    for src_ref, dst_ref in ((None, km_ref), (v_ref, vm_ref)):
        val = kn if src_ref is None else src_ref[...]
        rolled = pltpu.roll(val, HEAD_DIM, 1)
        dst_ref[0] = jnp.where(left, val, 0.0).astype(BF16)
        dst_ref[1] = jnp.where(left, 0.0, rolled).astype(BF16)
        dst_ref[2] = jnp.where(left, rolled, 0.0).astype(BF16)
        dst_ref[3] = jnp.where(left, 0.0, val).astype(BF16)

    span = 3 * BLOCK
    scale = HEAD_DIM ** -0.5
    ii = lax.broadcasted_iota(jnp.int32, (BLOCK, span), 0)
    jj = lax.broadcasted_iota(jnp.int32, (BLOCK, span), 1)

    def block(n, carry):
        q0 = pl.multiple_of(n * BLOCK, BLOCK)
        start = pl.multiple_of(jnp.clip((n - 1) * BLOCK, 0, seq - span), BLOCK)
        valid = jnp.abs((start - q0) + jj - ii) <= WINDOW
        for p in range(N_Q_HEADS // 2):
            cols = slice(p * LANES, (p + 1) * LANES)
            kv = (2 * p) // (N_Q_HEADS // N_KV_HEADS)
            qp = qn_ref[pl.ds(q0, BLOCK), cols]
            o = jnp.zeros((BLOCK, LANES), F32)
            for a in range(2):
                h = 2 * p + a
                kb = km_ref[2 * kv + a, pl.ds(start, span), :]
                s = _dot_nt(qp, kb) * scale
                s = jnp.where(valid, s, -jnp.inf)
                sk = sink_ref[h:h + 1, 0:1]
                m = jnp.maximum(jnp.max(s, axis=-1, keepdims=True), sk)
                e = jnp.exp(s - m)
                den = jnp.sum(e, axis=-1, keepdims=True) + jnp.exp(sk - m)
                pn = (e / den).astype(BF16)
                o = o + _dot(pn, vm_ref[2 * kv + a, pl.ds(start, span), :])
            o_ref[pl.ds(q0, BLOCK), cols] = o
        return carry

    lax.fori_loop(0, seq // BLOCK, block, 0)


def _attention(q, k, v, q_gain, k_gain, sink, bsz, seq):
    inv = ROPE_THETA ** (-jnp.arange(0, HEAD_DIM, 2, dtype=F32) / HEAD_DIM)
    ang = jnp.arange(seq, dtype=F32)[:, None] * inv[None, :]
    cos = jnp.tile(jnp.cos(ang), (1, LANES // (HEAD_DIM // 2)))
    sn = jnp.sin(ang)
    sin = jnp.tile(jnp.concatenate([-sn, sn], axis=1), (1, LANES // HEAD_DIM))
    qg = jnp.tile(q_gain, LANES // HEAD_DIM)[None, :]
    kg = jnp.tile(k_gain, LANES // HEAD_DIM)[None, :]
    sinkb = jnp.broadcast_to(sink[:, None], (N_Q_HEADS, LANES))
    aw = N_Q_HEADS * HEAD_DIM
    kw = N_KV_HEADS * HEAD_DIM
    const = lambda b: (0, 0)
    return pl.pallas_call(
        functools.partial(_attn_kernel, seq=seq),
        grid=(bsz,),
        in_specs=[pl.BlockSpec((seq, aw), lambda b: (b, 0)),
                  pl.BlockSpec((seq, kw), lambda b: (b, 0)),
                  pl.BlockSpec((seq, kw), lambda b: (b, 0)),
                  pl.BlockSpec((seq, LANES), const),
                  pl.BlockSpec((seq, LANES), const),
                  pl.BlockSpec((1, LANES), const),
                  pl.BlockSpec((1, LANES), const),
                  pl.BlockSpec((N_Q_HEADS, LANES), const)],
        out_specs=pl.BlockSpec((seq, aw), lambda b: (b, 0)),
        out_shape=jax.ShapeDtypeStruct((bsz * seq, aw), F32),
        scratch_shapes=[pltpu.VMEM((seq, aw), BF16),
                        pltpu.VMEM((4, seq, LANES), BF16),
                        pltpu.VMEM((4, seq, LANES), BF16)],
        compiler_params=_cparams(("arbitrary",)),
        name="window_attention",
    )(q, k, v, cos, sin, qg, kg, sinkb)


def _merge_kernel(x_ref, mod_ref, ya_ref, yb_ref, ga_ref, gb_ref, wpa_ref, wpb_ref, wo_ref, o_ref):
    pa = _dot(ya_ref[...].astype(BF16), wpa_ref[...])
    pb = _dot(yb_ref[...].astype(BF16), wpb_ref[...])
    merged = _sigmoid(ga_ref[...]) * pa + _sigmoid(gb_ref[...]) * pb
    out = _dot(merged.astype(BF16), wo_ref[...])
    o_ref[...] = x_ref[...] + mod_ref[0][2:3] * out


def _merge(x2, mod3, ya, yb, ga, gb, w_pa_bf, w_pb_bf, w_out_bf, seq, tm=512):
    t, d = x2.shape
    per = seq // tm
    row = lambda w: pl.BlockSpec((tm, w), lambda i: (i, 0))
    full = lambda a: pl.BlockSpec(a.shape, lambda i: (0, 0))
    return pl.pallas_call(
        _merge_kernel,
        grid=(t // tm,),
        in_specs=[row(d), pl.BlockSpec((1, 6, d), lambda i: (i // per, 0, 0)),
                  row(ya.shape[1]), row(yb.shape[1]), row(d), row(d),
                  full(w_pa_bf), full(w_pb_bf), full(w_out_bf)],
        out_specs=row(d),
        out_shape=jax.ShapeDtypeStruct((t, d), F32),
        compiler_params=_cparams(("arbitrary",)),
        name="merge_outproj",
    )(x2, mod3, ya, yb, ga, gb, w_pa_bf, w_pb_bf, w_out_bf)


def _oddeven_merge(lo, hi, r):
    step = r * 2
    if step < hi - lo:
        yield from _oddeven_merge(lo, hi, step)
        yield from _oddeven_merge(lo + r, hi, step)
        yield from ((i, i + r) for i in range(lo + r, hi - r, step))
    else:
        yield (lo, lo + r)


def _oddeven_merge_sort(lo, hi):
    if hi - lo >= 1:
        mid = lo + (hi - lo) // 2
        yield from _oddeven_merge_sort(lo, mid)
        yield from _oddeven_merge_sort(mid + 1, hi)
        yield from _oddeven_merge(lo, hi, 1)


def _topk_rows(s, k, val_ref, idx_ref):
    n, tm = s.shape
    ng = n // SUBLANES
    assert ng & (ng - 1) == 0 and k <= ng
    sub = lax.broadcasted_iota(jnp.int32, (SUBLANES, tm), 0)
    vals = [s[g * SUBLANES:(g + 1) * SUBLANES, :] for g in range(ng)]
    rows = [sub + g * SUBLANES for g in range(ng)]
    for i, j in _oddeven_merge_sort(0, ng - 1):
        keep = (vals[i] > vals[j]) | ((vals[i] == vals[j]) & (rows[i] < rows[j]))
        vals[i], vals[j] = jnp.where(keep, vals[i], vals[j]), jnp.where(keep, vals[j], vals[i])
        rows[i], rows[j] = jnp.where(keep, rows[i], rows[j]), jnp.where(keep, rows[j], rows[i])
    for t in range(k):
        m = jnp.max(vals[0], axis=0, keepdims=True)
        first = jnp.min(jnp.where(vals[0] == m, rows[0], n), axis=0, keepdims=True)
        val_ref[t:t + 1, :] = m
        idx_ref[t:t + 1, :] = first
        if t + 1 < k:
            popped = rows[0] == first
            for g in range(k - 1 - t):
                vals[g] = jnp.where(popped, vals[g + 1], vals[g])
                rows[g] = jnp.where(popped, rows[g + 1], rows[g])


def _peer_score_kernel(x_ref, mod_ref, g_ref, wq_ref, k1_ref, k2_ref, h_ref, idx_ref, gate_ref,
                       v1_ref, i1_ref, v2_ref, i2_ref, cv_ref, ce_ref, it_ref, gt_ref, *, tm):
    m = mod_ref[0]
    h = _norm_mod(x_ref[...], g_ref[...], m[3:4], m[4:5])
    h_ref[...] = h
    q = _dot(h.astype(BF16), wq_ref[...]).astype(BF16)
    half = PEER_QDIM // 2
    neg = jnp.full((1, tm), -jnp.inf, F32)
    for hd in range(PEER_HEADS):
        q1 = q[:, hd * PEER_QDIM: hd * PEER_QDIM + half]
        q2 = q[:, hd * PEER_QDIM + half: (hd + 1) * PEER_QDIM]
        _topk_rows(_dot_nt(k1_ref[...], q1), PEER_TOPK, v1_ref, i1_ref)
        _topk_rows(_dot_nt(k2_ref[...], q2), PEER_TOPK, v2_ref, i2_ref)
        for r, (i, j) in enumerate(_STAIR):
            cv_ref[r:r + 1, :] = v1_ref[i:i + 1, :] + v2_ref[j:j + 1, :]
            ce_ref[r:r + 1, :] = i1_ref[i:i + 1, :] * N_KEYS + i2_ref[j:j + 1, :]
        for r in range(len(_STAIR), _STAIR_ROWS):
            cv_ref[r:r + 1, :] = neg
            ce_ref[r:r + 1, :] = jnp.zeros((1, tm), jnp.int32)
        cand = cv_ref[...]
        ce = ce_ref[...]
        row = lax.broadcasted_iota(jnp.int32, cand.shape, 0)
        vals = []
        for kk in range(PEER_TOPK):
            mx = jnp.max(cand, axis=0, keepdims=True)
            first = jnp.min(jnp.where(cand == mx, row, _STAIR_ROWS), axis=0, keepdims=True)
            sel = row == first
            it_ref[hd * PEER_TOPK + kk: hd * PEER_TOPK + kk + 1, :] = jnp.sum(jnp.where(sel, ce, 0), axis=0, keepdims=True)
            vals.append(mx)
            cand = jnp.where(sel, -jnp.inf, cand)
        ex = [jnp.exp(v - vals[0]) for v in vals]
        tot = ex[0]
        for e in ex[1:]:
            tot = tot + e
        for kk in range(PEER_TOPK):
            gt_ref[hd * PEER_TOPK + kk: hd * PEER_TOPK + kk + 1, :] = ex[kk] / tot
    idx_ref[...] = it_ref[...].T
    gate_ref[...] = gt_ref[...].T


def _peer_score(x2, mod3, gain, wq_bf, k1_bf, k2_bf, seq, tm=256):
    t, d = x2.shape
    per = seq // tm
    slots = PEER_HEADS * PEER_TOPK
    full = lambda a: pl.BlockSpec(a.shape, lambda i: (0, 0))
    return pl.pallas_call(
        functools.partial(_peer_score_kernel, tm=tm),
        grid=(t // tm,),
        in_specs=[pl.BlockSpec((tm, d), lambda i: (i, 0)),
                  pl.BlockSpec((1, 6, d), lambda i: (i // per, 0, 0)),
                  pl.BlockSpec((1, d), lambda i: (0, 0)),
                  full(wq_bf), full(k1_bf), full(k2_bf)],
        out_specs=[pl.BlockSpec((tm, d), lambda i: (i, 0)),
                   pl.BlockSpec((tm, slots), lambda i: (i, 0)),
                   pl.BlockSpec((tm, slots), lambda i: (i, 0))],
        out_shape=[jax.ShapeDtypeStruct((t, d), F32),
                   jax.ShapeDtypeStruct((t, slots), jnp.int32),
                   jax.ShapeDtypeStruct((t, slots), F32)],
        scratch_shapes=[pltpu.VMEM((PEER_TOPK, tm), F32), pltpu.VMEM((PEER_TOPK, tm), jnp.int32),
                        pltpu.VMEM((PEER_TOPK, tm), F32), pltpu.VMEM((PEER_TOPK, tm), jnp.int32),
                        pltpu.VMEM((_STAIR_ROWS, tm), F32), pltpu.VMEM((_STAIR_ROWS, tm), jnp.int32),
                        pltpu.VMEM((slots, tm), jnp.int32), pltpu.VMEM((slots, tm), F32)],
        compiler_params=_cparams(("arbitrary",)),
        name="peer_score_topk",
    )(x2, mod3, gain.reshape(1, d), wq_bf, k1_bf, k2_bf)


def _peer_gather_kernel(idx_ref, idxn_ref, tab_ref, h_ref, gate_ref, x_ref, mod_ref, after_ref, o_ref,
                        buf_ref, sem_ref, *, tc, d, nsteps):
    slots = PEER_HEADS * PEER_TOPK
    step = pl.program_id(0)

    def issue_token(chunk, t):
        ids_ref, c = (idx_ref, chunk) if chunk < GATHER_RING else (idxn_ref, chunk - GATHER_RING)
        for j in range(slots):
            pltpu.make_async_copy(tab_ref.at[ids_ref[c * tc + t, j]],
                                  buf_ref.at[c, t, pl.ds(j, 1), :],
                                  sem_ref.at[c]).start(priority=j % 2)

    def wait(slot):
        pltpu.make_async_copy(buf_ref.at[slot], buf_ref.at[slot], sem_ref.at[slot]).wait()

    eye = (lax.broadcasted_iota(jnp.int32, (slots, slots), 0) ==
           lax.broadcasted_iota(jnp.int32, (slots, slots), 1))
    gt2 = mod_ref[0][5:6]

    def mix_token(slot, t):
        r = slot * tc + t
        word = buf_ref[slot, t]
        u = lax.bitcast_convert_type(word << 16, F32)
        prod = u * h_ref[r:r + 1, :]
        part = prod[:, 0:LANES]
        for c in range(1, d // LANES):
            part = part + prod[:, c * LANES:(c + 1) * LANES]
        a = jnp.sum(part, axis=1, keepdims=True)
        grow = jnp.broadcast_to(gate_ref[r:r + 1, :], (slots, slots))
        gcol = jnp.sum(jnp.where(eye, grow, 0.0), axis=1, keepdims=True)
        w = gcol * (0.5 * a * (1.0 + lax.erf(a * (2.0 ** -0.5))))
        v = lax.bitcast_convert_type(buf_ref[slot, t] & jnp.uint32(0xFFFF0000), F32)
        y = jnp.sum(v * w, axis=0, keepdims=True)
        o_ref[r:r + 1, :] = x_ref[r:r + 1, :] + gt2 * y

    @pl.when(step == 0)
    def _():
        for c in range(GATHER_AHEAD):
            for t in range(tc):
                issue_token(c, t)

    for p in range(GATHER_RING):
        wait(p)
        for t in range(tc):
            issue_token(p + GATHER_AHEAD, t)
            mix_token(p, t)

    @pl.when(step == nsteps - 1)
    def _():
        for c in range(GATHER_AHEAD):
            wait(c)


def _peer_gather(x2, mod3, h2, idx, gate, table, seq, tok0, ntok, after, tc=8):
    d = x2.shape[1]
    slots = idx.shape[1]
    tb = GATHER_RING * tc
    nsteps = ntok // tb
    off = tok0 // tb
    per = seq // tb
    row = lambda w: pl.BlockSpec((tb, w), lambda i: (i + off, 0))
    return pl.pallas_call(
        functools.partial(_peer_gather_kernel, tc=tc, d=d, nsteps=nsteps),
        grid=(nsteps,),
        in_specs=[pl.BlockSpec((tb, slots), lambda i: (i + off, 0), memory_space=pltpu.SMEM),
                  pl.BlockSpec((tb, slots), lambda i: (jnp.minimum(i + 1, nsteps - 1) + off, 0),
                               memory_space=pltpu.SMEM),
                  pl.BlockSpec(memory_space=pl.ANY),
                  row(d), row(slots), row(d),
                  pl.BlockSpec((1, 6, d), lambda i: ((i + off) // per, 0, 0)),
                  pl.BlockSpec(memory_space=pl.ANY)],
        out_specs=pl.BlockSpec((tb, d), lambda i: (i, 0)),
        out_shape=jax.ShapeDtypeStruct((ntok, d), F32),
        scratch_shapes=[pltpu.VMEM((GATHER_RING, tc, slots, d), jnp.uint32),
                        pltpu.SemaphoreType.DMA((GATHER_RING,))],
        compiler_params=_cparams(("arbitrary",)),
        name="peer_gather_mix",
    )(idx, idx, table.reshape(table.shape[0], 1, d), h2, gate, x2, mod3, after)


def _pack_expert_tables(u_tab, v_tab):
    ub = lax.bitcast_convert_type(u_tab.astype(BF16), jnp.uint16).astype(jnp.uint32)
    vb = lax.bitcast_convert_type(v_tab.astype(BF16), jnp.uint16).astype(jnp.uint32)
    return (vb << 16) | ub


SC_LANES = 16
SC_CORES = 2
SC_SUBCORES = 16
SC_GROUP = 32
SC_HALF = 64
SC_SEQS = 12
TC_FIRST_SEQS = 7


def _pack_pairs(tab):
    e, d = tab.shape
    bits = lax.bitcast_convert_type(tab.astype(BF16), jnp.uint16).astype(jnp.uint32)
    bits = bits.reshape(e, d // (2 * SC_LANES), 2, SC_LANES)
    return ((bits[:, :, 1, :] << 16) | bits[:, :, 0, :]).reshape(e, d // 2)


def _sc_worker_base(n_per):
    return (lax.axis_index("s") * SC_CORES + lax.axis_index("c")) * n_per


def _sc_halves(tab_hbm, idx_v, rows_v, sems, tt, half):
    ids = idx_v.at[tt, pl.ds(half * SC_HALF, SC_HALF)]
    return pltpu.make_async_copy(tab_hbm.at[ids], rows_v.at[half, :, pl.ds(0, tab_hbm.shape[1])], sems.at[half])


def _sc_token_loop(tab_hbm, idx_v, rows_v, sems, compute_half):
    _sc_halves(tab_hbm, idx_v, rows_v, sems, 0, 0).start()

    def tok(tt, carry):
        _sc_halves(tab_hbm, idx_v, rows_v, sems, tt, 1).start()
        _sc_halves(tab_hbm, idx_v, rows_v, sems, tt, 0).wait()
        compute_half(tt, 0)

        @pl.when(tt + 1 < SC_GROUP)
        def _():
            _sc_halves(tab_hbm, idx_v, rows_v, sems, tt + 1, 0).start()

        _sc_halves(tab_hbm, idx_v, rows_v, sems, tt, 1).wait()
        compute_half(tt, 1)
        return carry

    lax.fori_loop(0, SC_GROUP, tok, 0)


def _unpack_pair(word):
    lo = lax.bitcast_convert_type(word << 16, F32)
    hi = lax.bitcast_convert_type(word & jnp.uint32(0xFFFF0000), F32)
    return lo, hi


def _sc_dot_body(tab_hbm, idx_hbm, h_hbm, a_hbm, idx_v, h_v, rows_v, a_v, tr_v, sems, *, tok0, n_per, d):
    base = _sc_worker_base(n_per)
    lane = lax.iota(jnp.int32, SC_LANES)
    nk = d // (2 * SC_LANES)

    def compute_half(tt, half):
        def rows16(gi, carry):
            accs = [jnp.zeros((SC_LANES,), F32) for _ in range(SC_LANES)]
            for k in range(nk):
                hlo = h_v[tt, pl.ds(2 * SC_LANES * k, SC_LANES)]
                hhi = h_v[tt, pl.ds(2 * SC_LANES * k + SC_LANES, SC_LANES)]
                for q in range(SC_LANES):
                    lo, hi = _unpack_pair(rows_v[half, gi * SC_LANES + q, pl.ds(SC_LANES * k, SC_LANES)])
                    accs[q] = accs[q] + lo * hlo + hi * hhi
            for q in range(SC_LANES):
                tr_v[q, :] = accs[q]
            tot = plsc.load_gather(tr_v, [lane, jnp.zeros((SC_LANES,), jnp.int32)])
            for c in range(1, SC_LANES):
                tot = tot + plsc.load_gather(tr_v, [lane, jnp.full((SC_LANES,), c, jnp.int32)])
            a_v[tt, pl.ds(half * SC_HALF + gi * SC_LANES, SC_LANES)] = tot
            return carry
        lax.fori_loop(0, SC_HALF // SC_LANES, rows16, 0)

    def group(g, carry):
        loc = base + g * SC_GROUP
        pltpu.sync_copy(idx_hbm.at[pl.ds(tok0 + loc, SC_GROUP)], idx_v)
        pltpu.sync_copy(h_hbm.at[pl.ds(tok0 + loc, SC_GROUP)], h_v)
        _sc_token_loop(tab_hbm, idx_v, rows_v, sems, compute_half)
        pltpu.sync_copy(a_v, a_hbm.at[pl.ds(loc, SC_GROUP)])
        return carry

    lax.fori_loop(0, n_per // SC_GROUP, group, 0)


def _sc_mix_body(tab_hbm, idx_hbm, w_hbm, y_hbm, idx_v, w_v, rows_v, y_v, sems, *, tok0, n_per, d):
    base = _sc_worker_base(n_per)
    nk = d // (2 * SC_LANES)
    nq = 2
    kq = nk // nq

    def compute_half(tt, half):
        ttv = jnp.full((SC_LANES,), tt, jnp.int32)
        for piece in range(nq):
            col0 = piece * kq * 2 * SC_LANES
            if half == 0:
                accs = tuple(jnp.zeros((SC_LANES,), F32) for _ in range(2 * kq))
            else:
                accs = tuple(y_v[tt, pl.ds(col0 + SC_LANES * m, SC_LANES)] for m in range(2 * kq))

            def row(r, accs):
                wsplat = plsc.load_gather(w_v, [ttv, jnp.full((SC_LANES,), half * SC_HALF + r, jnp.int32)])
                out = []
                for k in range(kq):
                    lo, hi = _unpack_pair(rows_v[half, r, pl.ds(SC_LANES * (piece * kq + k), SC_LANES)])
                    out.append(accs[2 * k] + wsplat * lo)
                    out.append(accs[2 * k + 1] + wsplat * hi)
                return tuple(out)

            accs = lax.fori_loop(0, SC_HALF, row, accs)
            for m in range(2 * kq):
                y_v[tt, pl.ds(col0 + SC_LANES * m, SC_LANES)] = accs[m]

    def group(g, carry):
        loc = base + g * SC_GROUP
        pltpu.sync_copy(idx_hbm.at[pl.ds(tok0 + loc, SC_GROUP)], idx_v)
        pltpu.sync_copy(w_hbm.at[pl.ds(loc, SC_GROUP)], w_v)
        _sc_token_loop(tab_hbm, idx_v, rows_v, sems, compute_half)
        pltpu.sync_copy(y_v, y_hbm.at[pl.ds(loc, SC_GROUP)])
        return carry

    lax.fori_loop(0, n_per // SC_GROUP, group, 0)


def _sc_mesh():
    return plsc.VectorSubcoreMesh(core_axis_name="c", subcore_axis_name="s")


def _sc_dot(u_words, idx, h2, tok0, ntok):
    d = h2.shape[1]
    slots = idx.shape[1]
    n_per = ntok // (SC_CORES * SC_SUBCORES)
    return pl.kernel(
        functools.partial(_sc_dot_body, tok0=tok0, n_per=n_per, d=d),
        out_type=jax.ShapeDtypeStruct((ntok, slots), F32),
        mesh=_sc_mesh(),
        scratch_types=[pltpu.VMEM((SC_GROUP, slots), jnp.int32),
                       pltpu.VMEM((SC_GROUP, d), F32),
                       pltpu.VMEM((2, SC_HALF, d // 2 + SC_LANES), jnp.uint32),
                       pltpu.VMEM((SC_GROUP, slots), F32),
                       pltpu.VMEM((SC_LANES, SC_LANES), F32),
                       pltpu.SemaphoreType.DMA((2,))],
        compiler_params=pltpu.CompilerParams(needs_layout_passes=False),
        name="peer_sc_dot",
    )(u_words, idx, h2)


def _sc_mix(v_words, idx, w, tok0, ntok, d):
    slots = idx.shape[1]
    n_per = ntok // (SC_CORES * SC_SUBCORES)
    return pl.kernel(
        functools.partial(_sc_mix_body, tok0=tok0, n_per=n_per, d=d),
        out_type=jax.ShapeDtypeStruct((ntok, d), F32),
        mesh=_sc_mesh(),
        scratch_types=[pltpu.VMEM((SC_GROUP, slots), jnp.int32),
                       pltpu.VMEM((SC_GROUP, slots), F32),
                       pltpu.VMEM((2, SC_HALF, d // 2), jnp.uint32),
                       pltpu.VMEM((SC_GROUP, d), F32),
                       pltpu.SemaphoreType.DMA((2,))],
        compiler_params=pltpu.CompilerParams(needs_layout_passes=False),
        name="peer_sc_mix",
    )(v_words, idx, w)


def _gelu_gate_kernel(a_ref, g_ref, after_ref, o_ref):
    a = a_ref[...]
    o_ref[...] = g_ref[...] * (0.5 * a * (1.0 + lax.erf(a * (2.0 ** -0.5))))


def _gelu_gate(a, gate, tok0, after, tm=2048):
    n, slots = a.shape
    off = tok0 // tm
    return pl.pallas_call(
        _gelu_gate_kernel,
        grid=(n // tm,),
        in_specs=[pl.BlockSpec((tm, slots), lambda i: (i, 0)),
                  pl.BlockSpec((tm, slots), lambda i: (i + off, 0)),
                  pl.BlockSpec(memory_space=pl.ANY)],
        out_specs=pl.BlockSpec((tm, slots), lambda i: (i, 0)),
        out_shape=jax.ShapeDtypeStruct((n, slots), F32),
        compiler_params=_cparams(("arbitrary",)),
        name="peer_gelu_gate",
    )(a, gate, after)


def _residual_kernel(x_ref, mod_ref, y_ref, o_ref):
    o_ref[...] = x_ref[...] + mod_ref[0][5:6] * y_ref[...]


def _residual(x2, mod3, y, tok0, seq, tm=512):
    n, d = y.shape
    off = tok0 // tm
    per = seq // tm
    return pl.pallas_call(
        _residual_kernel,
        grid=(n // tm,),
        in_specs=[pl.BlockSpec((tm, d), lambda i: (i + off, 0)),
                  pl.BlockSpec((1, 6, d), lambda i: ((i + off) // per, 0, 0)),
                  pl.BlockSpec((tm, d), lambda i: (i, 0))],
        out_specs=pl.BlockSpec((tm, d), lambda i: (i, 0)),
        out_shape=jax.ShapeDtypeStruct((n, d), F32),
        compiler_params=_cparams(("arbitrary",)),
        name="peer_residual",
    )(x2, mod3, y)


def kernel(x_prompt, x_sample, c_prompt, c_sample, w_mod, b_mod, g_norm1, g_norm2, w_in, conv_w, conv_b, f_w1, f_b1, f_freq, f_w2, f_b2, f_w3, f_bias, q_gain, k_gain, sink, w_pa, w_pb, w_out, peer_wq, peer_k1, peer_k2, peer_u, peer_v):
    depth = w_mod.shape[0]
    bp, seq, d = x_prompt.shape
    bs = x_sample.shape[0]
    assert x_sample.shape[1] == seq
    bsz = bp + bs
    x = jnp.concatenate([x_prompt, x_sample], axis=0).reshape(bsz * seq, d)
    c = jnp.concatenate([c_prompt, c_sample], axis=0)

    hyw = w_pa.shape[1]
    aw = w_pb.shape[1]
    kw = N_KV_HEADS * HEAD_DIM
    widths = (HY_ORDER + 1) * hyw, aw, kw, kw, d, d
    cblk = hyw // 256

    mod = _modulation(c, w_mod.astype(BF16), b_mod)
    fc, fs = _dft_tables(seq)

    for l in range(depth):
        mod3 = mod[l].reshape(bsz, 6, d)
        hy, q, k, v, ga, gb = _inproj(x, mod3, g_norm1[l], w_in[l].astype(BF16), seq, widths)
        hcat = _filters_time(seq, f_w1[l], f_b1[l], f_freq[l], f_w2[l], f_b2[l], f_w3[l])
        spec_a, spec_b, nyq = _filter_spectra(fc, fs, hcat)
        cw, cb = conv_w[l], conv_b[l][None, :]
        zz = _long_conv(fc, fs, hy, 0, hy, cblk, spec_a, spec_b, nyq, 0, f_bias[l][0:1], cw, cb,
                        0, cblk, bsz, seq, conv_u=True)
        ya = _long_conv(fc, fs, zz, 0, hy, 2 * cblk, spec_a, spec_b, nyq, cblk, f_bias[l][1:2], cw, cb,
                        0, 2 * cblk, bsz, seq, conv_u=False)
        yb = _attention(q, k, v, q_gain[l], k_gain[l], sink[l], bsz, seq)
        x = _merge(x, mod3, ya, yb, ga, gb, w_pa[l].astype(BF16), w_pb[l].astype(BF16), w_out[l].astype(BF16), seq)
        h2, idx, gate = _peer_score(x, mod3, g_norm2[l], peer_wq[l].astype(BF16),
                                    peer_k1[l].astype(BF16), peer_k2[l].astype(BF16), seq)
        table = _pack_expert_tables(peer_u[l], peer_v[l])
        t_tc = (bsz - SC_SEQS) * seq
        t_sc = SC_SEQS * seq
        t_g1 = TC_FIRST_SEQS * seq
        a_sc = _sc_dot(_pack_pairs(peer_u[l]), idx, h2, t_tc, t_sc)
        x_g1 = _peer_gather(x, mod3, h2, idx, gate, table, seq, 0, t_g1, gate)
        w_sc = _gelu_gate(a_sc, gate, t_tc, x_g1)
        y_sc = _sc_mix(_pack_pairs(peer_v[l]), idx, w_sc, t_tc, t_sc, d)
        x_g2 = _peer_gather(x, mod3, h2, idx, gate, table, seq, t_g1, t_tc - t_g1, w_sc)
        x = jnp.concatenate([x_g1, x_g2, _residual(x, mod3, y_sc, t_tc, seq)], axis=0)

    x = x.reshape(bsz, seq, d)
    return x[:bp], x[bp:]
```

```python
import functools
import math

import jax
import jax.numpy as jnp
import numpy as np
from jax import lax
from jax.experimental import pallas as pl
from jax.experimental.pallas import tpu as pltpu
from jax.experimental.pallas import tpu_sc as plsc

F32 = jnp.float32
BF16 = jnp.bfloat16

EPS = 1e-6
HEAD_DIM = 64
N_Q_HEADS = 8
N_KV_HEADS = 2
WINDOW = 128
BLOCK = 128
ROPE_THETA = 10000.0
HY_ORDER = 2
N_DIR = 2
FILT_BANDS = 16
DECAY_TARGET = 1e-2
FAST_DECAY_PCT = 0.3
SLOW_DECAY_PCT = 1.5
PEER_HEADS = 8
N_KEYS = 128
PEER_TOPK = 16
PEER_QDIM = 256
LANES = 128
SUBLANES = 8
VMEM_LIMIT = 56 * 1024 * 1024
GATHER_RING = 4
GATHER_AHEAD = 2

_STAIR = [(i, j) for i in range(PEER_TOPK) for j in range(PEER_TOPK) if (i + 1) * (j + 1) <= PEER_TOPK]
_STAIR_ROWS = -(-len(_STAIR) // SUBLANES) * SUBLANES


def _cparams(sem, vmem=VMEM_LIMIT):
    return pltpu.CompilerParams(dimension_semantics=sem, vmem_limit_bytes=vmem)


def _dot(a, b):
    return jnp.dot(a, b, preferred_element_type=F32)


def _dot_nt(a, b):
    return lax.dot_general(a, b, (((1,), (1,)), ((), ())), preferred_element_type=F32)


def _dot_hi(a, b):
    return jnp.dot(a, b, preferred_element_type=F32, precision=lax.Precision.HIGHEST)


def _sigmoid(x):
    return 1.0 / (1.0 + jnp.exp(-x))


def _mod_kernel(c_ref, w_ref, b_ref, o_ref):
    c = c_ref[...]
    s = c * _sigmoid(c)
    o_ref[0] = _dot(s.astype(BF16), w_ref[0]) + b_ref[0]


def _modulation(c, w_mod_bf, b_mod):
    depth, d, n6 = w_mod_bf.shape
    bsz = c.shape[0]
    tn = 1536
    return pl.pallas_call(
        _mod_kernel,
        grid=(depth, n6 // tn),
        in_specs=[pl.BlockSpec((bsz, d), lambda l, j: (0, 0)),
                  pl.BlockSpec((1, d, tn), lambda l, j: (l, 0, j)),
                  pl.BlockSpec((1, 1, tn), lambda l, j: (l, 0, j))],
        out_specs=pl.BlockSpec((1, bsz, tn), lambda l, j: (l, 0, j)),
        out_shape=jax.ShapeDtypeStruct((depth, bsz, n6), F32),
        compiler_params=_cparams(("arbitrary", "arbitrary")),
        name="adaln_mod",
    )(c, w_mod_bf, b_mod.reshape(depth, 1, n6))


def _norm_mod(x, gain, shift, scale):
    y = x * lax.rsqrt(jnp.mean(x * x, axis=-1, keepdims=True) + EPS)
    return (y * gain) * (1.0 + scale) + shift


def _inproj_kernel(x_ref, mod_ref, g_ref, w_ref, hy_ref, q_ref, k_ref, v_ref, ga_ref, gb_ref, *, splits):
    m = mod_ref[0]
    h = _norm_mod(x_ref[...], g_ref[...], m[0:1], m[1:2])
    z = _dot(h.astype(BF16), w_ref[...])
    outs = (hy_ref, q_ref, k_ref, v_ref, ga_ref, gb_ref)
    lo = 0
    for ref, hi in zip(outs, splits):
        ref[...] = z[:, lo:hi]
        lo = hi


def _inproj(x2, mod3, gain, w_in_bf, seq, widths, tm=256):
    t, d = x2.shape
    ncols = w_in_bf.shape[1]
    per = seq // tm
    splits = tuple(int(s) for s in np.cumsum(widths))
    return pl.pallas_call(
        functools.partial(_inproj_kernel, splits=splits),
        grid=(t // tm,),
        in_specs=[pl.BlockSpec((tm, d), lambda i: (i, 0)),
                  pl.BlockSpec((1, 6, d), lambda i: (i // per, 0, 0)),
                  pl.BlockSpec((1, d), lambda i: (0, 0)),
                  pl.BlockSpec((d, ncols), lambda i: (0, 0))],
        out_specs=[pl.BlockSpec((tm, w), lambda i: (i, 0)) for w in widths],
        out_shape=[jax.ShapeDtypeStruct((t, w), F32) for w in widths],
        compiler_params=_cparams(("arbitrary",)),
        name="inproj",
    )(x2, mod3, gain.reshape(1, d), w_in_bf)


def _filter_kernel(z_ref, w1_ref, b1_ref, fr_ref, w2_ref, b2_ref, w3_ref, ad_ref, o_ref):
    z = z_ref[0]
    fr = fr_ref[...]
    a = jnp.sin(fr * (_dot_hi(z, w1_ref[...]) + b1_ref[...]))
    a = jnp.sin(fr * (_dot_hi(a, w2_ref[...]) + b2_ref[...]))
    h = _dot_hi(a, w3_ref[0])
    h = h * jnp.exp(-z[:, 0:1] * ad_ref[...])
    row = lax.broadcasted_iota(jnp.int32, h.shape, 0)
    dead = (pl.program_id(0) == 1) & (pl.program_id(1) == 0) & (row == 0)
    o_ref[0] = jnp.where(dead, 0.0, h)


def _filters_time(seq, f_w1, f_b1, f_freq, f_w2, f_b2, f_w3, tm=512):
    hidden = f_w1.shape[1]
    cw = f_w3.shape[1] // (HY_ORDER * N_DIR)
    t = jnp.linspace(0.0, 1.0, seq, dtype=F32)[:, None]
    w = 2.0 * math.pi * jnp.arange(seq, dtype=F32)[:, None] / seq
    bands = jnp.linspace(1e-4, FILT_BANDS - 1, FILT_BANDS, dtype=F32)[None, :]
    z = jnp.concatenate([t, jnp.cos(bands * w), -jnp.sin(bands * w)], axis=-1)
    emb = z.shape[1]
    z = jnp.pad(z, ((0, 0), (0, LANES - emb)))
    zcat = jnp.stack([z, jnp.concatenate([z[:1], z[:0:-1]], axis=0)], axis=0)
    w1p = jnp.pad(f_w1, ((0, LANES - emb), (0, 0)))
    w3d = f_w3.reshape(hidden, HY_ORDER, N_DIR, cw).transpose(2, 0, 1, 3).reshape(N_DIR, hidden, HY_ORDER * cw)
    max_decay = math.log(DECAY_TARGET) / FAST_DECAY_PCT
    min_decay = math.log(DECAY_TARGET) / SLOW_DECAY_PCT
    ad = jnp.abs(jnp.linspace(min_decay, max_decay, cw, dtype=F32))
    ad = jnp.tile(ad, HY_ORDER)[None, :]
    oc = HY_ORDER * cw
    return pl.pallas_call(
        _filter_kernel,
        grid=(N_DIR, seq // tm),
        in_specs=[pl.BlockSpec((1, tm, LANES), lambda g, r: (g, r, 0)),
                  pl.BlockSpec((LANES, hidden), lambda g, r: (0, 0)),
                  pl.BlockSpec((1, hidden), lambda g, r: (0, 0)),
                  pl.BlockSpec((1, hidden), lambda g, r: (0, 0)),
                  pl.BlockSpec((hidden, hidden), lambda g, r: (0, 0)),
                  pl.BlockSpec((1, hidden), lambda g, r: (0, 0)),
                  pl.BlockSpec((1, hidden, oc), lambda g, r: (g, 0, 0)),
                  pl.BlockSpec((1, oc), lambda g, r: (0, 0))],
        out_specs=pl.BlockSpec((1, tm, oc), lambda g, r: (g, r, 0)),
        out_shape=jax.ShapeDtypeStruct((N_DIR, seq, oc), F32),
        compiler_params=_cparams(("arbitrary", "arbitrary")),
        name="hyena_filter_mlp",
    )(zcat, w1p, f_b1[None, :], f_freq[None, :], f_w2, f_b2[None, :], w3d, ad)


def _dft_tables(seq):
    n2 = 2 * seq
    f = jnp.arange(seq, dtype=jnp.int32)
    ft = (f[:, None] * f[None, :]) % n2
    ang = ft.astype(F32) * (2.0 * math.pi / n2)
    return jnp.cos(ang).astype(BF16), jnp.sin(ang).astype(BF16)


def _spec_kernel(fc_ref, fs_ref, h_ref, a_ref, b_ref, nyq_ref, *, seq, tf):
    hlo = h_ref[0]
    hhi = h_ref[1]
    hlo_b = hlo.astype(BF16)
    hhi_b = hhi.astype(BF16)
    f = pl.program_id(1) * tf + lax.broadcasted_iota(jnp.int32, (tf, 1), 0)
    sgn = jnp.where(f % 2 == 0, 1.0, -1.0)
    fc = fc_ref[...]
    fs = fs_ref[...]
    hr = _dot(fc, hlo_b) + sgn * _dot(fc, hhi_b)
    hs = _dot(fs, hlo_b) + sgn * _dot(fs, hhi_b)
    w = jnp.where(f == 0, 1.0, 2.0) * (1.0 / (2 * seq))
    a_ref[...] = w * hr
    b_ref[...] = -(w * hs)
    t = lax.broadcasted_iota(jnp.int32, (seq, 1), 0)
    alt = jnp.where(t % 2 == 0, 1.0, -1.0)
    nyq = jnp.sum(alt * (hlo + hhi), axis=0, keepdims=True) * (1.0 / (2 * seq))
    nyq_ref[...] = jnp.broadcast_to(nyq, nyq_ref.shape)


def _filter_spectra(fc, fs, hcat, tf=512, tcol=512):
    seq = fc.shape[0]
    oc = hcat.shape[2]
    return pl.pallas_call(
        functools.partial(_spec_kernel, seq=seq, tf=tf),
        grid=(oc // tcol, seq // tf),
        in_specs=[pl.BlockSpec((tf, seq), lambda j, i: (i, 0)),
                  pl.BlockSpec((tf, seq), lambda j, i: (i, 0)),
                  pl.BlockSpec((2, seq, tcol), lambda j, i: (0, 0, j))],
        out_specs=[pl.BlockSpec((tf, tcol), lambda j, i: (i, j)),
                   pl.BlockSpec((tf, tcol), lambda j, i: (i, j)),
                   pl.BlockSpec((SUBLANES, tcol), lambda j, i: (0, j))],
        out_shape=[jax.ShapeDtypeStruct((seq, oc), F32),
                   jax.ShapeDtypeStruct((seq, oc), F32),
                   jax.ShapeDtypeStruct((SUBLANES, oc), F32)],
        compiler_params=_cparams(("arbitrary", "arbitrary")),
        name="hyena_filter_spectra",
    )(fc, fs, hcat)


def _shortconv(x, w_ref, b_ref, seq):
    row = lax.broadcasted_iota(jnp.int32, (seq, 1), 0)
    xm = jnp.where(row == 0, 0.0, pltpu.roll(x, 1, 0))
    xp = jnp.where(row == seq - 1, 0.0, pltpu.roll(x, seq - 1, 0))
    return xm * w_ref[0:1, :] + x * w_ref[1:2, :] + xp * w_ref[2:3, :] + b_ref[...]


def _conv_kernel(fc_ref, fs_ref, u_ref, g_ref, a_ref, b_ref, nyq_ref, bias_ref,
                 cwu_ref, cbu_ref, cwg_ref, cbg_ref, o_ref, acc_ref, *, seq, fb, conv_u):
    u = u_ref[...]
    if conv_u:
        u = _shortconv(u, cwu_ref, cbu_ref, seq)
    gate = _shortconv(g_ref[...], cwg_ref, cbg_ref, seq)
    ub = u.astype(BF16)
    for c in range(seq // fb):
        rows = slice(c * fb, (c + 1) * fb)
        ur = _dot(fc_ref[rows, :], ub)
        us = _dot(fs_ref[rows, :], ub)
        a = a_ref[rows, :]
        b = b_ref[rows, :]
        qr = (ur * a + us * b).astype(BF16)
        qi = (us * a - ur * b).astype(BF16)
        part = _dot(fc_ref[:, rows], qr) + _dot(fs_ref[:, rows], qi)
        if c == 0:
            acc_ref[...] = part
        else:
            acc_ref[...] += part
    t = lax.broadcasted_iota(jnp.int32, (seq, 1), 0)
    alt = jnp.where(t % 2 == 0, 1.0, -1.0)
    unyq = jnp.sum(alt * u, axis=0, keepdims=True)
    y = acc_ref[...] + alt * (unyq * nyq_ref[0:1, :]) + bias_ref[...] * u
    o_ref[...] = gate * y


def _long_conv(fc, fs, u_src, u_blk0, g_src, g_blk0, spec_a, spec_b, nyq, s_blk0, bias, cw, cb,
               cu_blk0, cg_blk0, bsz, seq, conv_u, tc=256, fb=512):
    nct = 512 // tc
    t = bsz * seq
    const = lambda j, b: (0, 0)
    return pl.pallas_call(
        functools.partial(_conv_kernel, seq=seq, fb=fb, conv_u=conv_u),
        grid=(nct, bsz),
        in_specs=[pl.BlockSpec(memory_space=pltpu.VMEM),
                  pl.BlockSpec(memory_space=pltpu.VMEM),
                  pl.BlockSpec((seq, tc), lambda j, b: (b, u_blk0 + j)),
                  pl.BlockSpec((seq, tc), lambda j, b: (b, g_blk0 + j)),
                  pl.BlockSpec((seq, tc), lambda j, b: (0, s_blk0 + j)),
                  pl.BlockSpec((seq, tc), lambda j, b: (0, s_blk0 + j)),
                  pl.BlockSpec((SUBLANES, tc), lambda j, b: (0, s_blk0 + j)),
                  pl.BlockSpec((1, tc), lambda j, b: (0, j)),
                  pl.BlockSpec((3, tc), lambda j, b: (0, cu_blk0 + j)),
                  pl.BlockSpec((1, tc), lambda j, b: (0, cu_blk0 + j)),
                  pl.BlockSpec((3, tc), lambda j, b: (0, cg_blk0 + j)),
                  pl.BlockSpec((1, tc), lambda j, b: (0, cg_blk0 + j))],
        out_specs=pl.BlockSpec((seq, tc), lambda j, b: (b, j)),
        out_shape=jax.ShapeDtypeStruct((t, 512), F32),
        scratch_shapes=[pltpu.VMEM((seq, tc), F32)],
        compiler_params=_cparams(("arbitrary", "arbitrary")),
        name="hyena_long_conv_u" if conv_u else "hyena_long_conv",
    )(fc, fs, u_src, g_src, spec_a, spec_b, nyq, bias, cw, cb, cw, cb)


def _attn_kernel(q_ref, k_ref, v_ref, cos_ref, sin_ref, qg_ref, kg_ref, sink_ref, o_ref,
                 qn_ref, km_ref, vm_ref, *, seq):
    lane = lax.broadcasted_iota(jnp.int32, (1, LANES), 1)
    r = lax.broadcasted_iota(jnp.int32, (LANES, LANES), 0) // HEAD_DIM
    c = lax.broadcasted_iota(jnp.int32, (LANES, LANES), 1) // HEAD_DIM
    bd = jnp.where(r == c, 1.0, 0.0).astype(BF16)
    first_half = (lane % HEAD_DIM) < (HEAD_DIM // 2)
    cos = cos_ref[...]
    sin = sin_ref[...]

    def norm_rope(x, gain):
        sq = x * x
        hi = sq.astype(BF16)
        lo = (sq - hi.astype(F32)).astype(BF16)
        ss = _dot(hi, bd) + _dot(lo, bd)
        y = (x * lax.rsqrt(ss * (1.0 / HEAD_DIM) + EPS)) * gain
        partner = jnp.where(first_half, pltpu.roll(y, LANES - HEAD_DIM // 2, 1), pltpu.roll(y, HEAD_DIM // 2, 1))
        return y * cos + partner * sin

    for p in range(N_Q_HEADS // 2):
        cols = slice(p * LANES, (p + 1) * LANES)
        qn_ref[:, cols] = norm_rope(q_ref[:, cols], qg_ref[...]).astype(BF16)
    kn = norm_rope(k_ref[...], kg_ref[...])
    left = lane < HEAD_DIM
    for src_ref, dst_ref in ((None, km_ref), (v_ref, vm_ref)):
        val = kn if src_ref is None else src_ref[...]
        rolled = pltpu.roll(val, HEAD_DIM, 1)
        dst_ref[0] = jnp.where(left, val, 0.0).astype(BF16)
        dst_ref[1] = jnp.where(left, 0.0, rolled).astype(BF16)
        dst_ref[2] = jnp.where(left, rolled, 0.0).astype(BF16)
        dst_ref[3] = jnp.where(left, 0.0, val).astype(BF16)

    span = 3 * BLOCK
    scale = HEAD_DIM ** -0.5
    ii = lax.broadcasted_iota(jnp.int32, (BLOCK, span), 0)
    jj = lax.broadcasted_iota(jnp.int32, (BLOCK, span), 1)

    def block(n, carry):
        q0 = pl.multiple_of(n * BLOCK, BLOCK)
        start = pl.multiple_of(jnp.clip((n - 1) * BLOCK, 0, seq - span), BLOCK)
        valid = jnp.abs((start - q0) + jj - ii) <= WINDOW
        for p in range(N_Q_HEADS // 2):
            cols = slice(p * LANES, (p + 1) * LANES)
            kv = (2 * p) // (N_Q_HEADS // N_KV_HEADS)
            qp = qn_ref[pl.ds(q0, BLOCK), cols]
            o = jnp.zeros((BLOCK, LANES), F32)
            for a in range(2):
                h = 2 * p + a
                kb = km_ref[2 * kv + a, pl.ds(start, span), :]
                s = _dot_nt(qp, kb) * scale
                s = jnp.where(valid, s, -jnp.inf)
                sk = sink_ref[h:h + 1, 0:1]
                m = jnp.maximum(jnp.max(s, axis=-1, keepdims=True), sk)
                e = jnp.exp(s - m)
                den = jnp.sum(e, axis=-1, keepdims=True) + jnp.exp(sk - m)
                pn = (e / den).astype(BF16)
                o = o + _dot(pn, vm_ref[2 * kv + a, pl.ds(start, span), :])
            o_ref[pl.ds(q0, BLOCK), cols] = o
        return carry

    lax.fori_loop(0, seq // BLOCK, block, 0)


def _attention(q, k, v, q_gain, k_gain, sink, bsz, seq):
    inv = ROPE_THETA ** (-jnp.arange(0, HEAD_DIM, 2, dtype=F32) / HEAD_DIM)
    ang = jnp.arange(seq, dtype=F32)[:, None] * inv[None, :]
    cos = jnp.tile(jnp.cos(ang), (1, LANES // (HEAD_DIM // 2)))
    sn = jnp.sin(ang)
    sin = jnp.tile(jnp.concatenate([-sn, sn], axis=1), (1, LANES // HEAD_DIM))
    qg = jnp.tile(q_gain, LANES // HEAD_DIM)[None, :]
    kg = jnp.tile(k_gain, LANES // HEAD_DIM)[None, :]
    sinkb = jnp.broadcast_to(sink[:, None], (N_Q_HEADS, LANES))
    aw = N_Q_HEADS * HEAD_DIM
    kw = N_KV_HEADS * HEAD_DIM
    const = lambda b: (0, 0)
    return pl.pallas_call(
        functools.partial(_attn_kernel, seq=seq),
        grid=(bsz,),
        in_specs=[pl.BlockSpec((seq, aw), lambda b: (b, 0)),
                  pl.BlockSpec((seq, kw), lambda b: (b, 0)),
                  pl.BlockSpec((seq, kw), lambda b: (b, 0)),
                  pl.BlockSpec((seq, LANES), const),
                  pl.BlockSpec((seq, LANES), const),
                  pl.BlockSpec((1, LANES), const),
                  pl.BlockSpec((1, LANES), const),
                  pl.BlockSpec((N_Q_HEADS, LANES), const)],
        out_specs=pl.BlockSpec((seq, aw), lambda b: (b, 0)),
        out_shape=jax.ShapeDtypeStruct((bsz * seq, aw), F32),
        scratch_shapes=[pltpu.VMEM((seq, aw), BF16),
                        pltpu.VMEM((4, seq, LANES), BF16),
                        pltpu.VMEM((4, seq, LANES), BF16)],
        compiler_params=_cparams(("arbitrary",)),
        name="window_attention",
    )(q, k, v, cos, sin, qg, kg, sinkb)


def _merge_kernel(x_ref, mod_ref, ya_ref, yb_ref, ga_ref, gb_ref, wpa_ref, wpb_ref, wo_ref, o_ref):
    pa = _dot(ya_ref[...].astype(BF16), wpa_ref[...])
    pb = _dot(yb_ref[...].astype(BF16), wpb_ref[...])
    merged = _sigmoid(ga_ref[...]) * pa + _sigmoid(gb_ref[...]) * pb
    out = _dot(merged.astype(BF16), wo_ref[...])
    o_ref[...] = x_ref[...] + mod_ref[0][2:3] * out


def _merge(x2, mod3, ya, yb, ga, gb, w_pa_bf, w_pb_bf, w_out_bf, seq, tm=512):
    t, d = x2.shape
    per = seq // tm
    row = lambda w: pl.BlockSpec((tm, w), lambda i: (i, 0))
    full = lambda a: pl.BlockSpec(a.shape, lambda i: (0, 0))
    return pl.pallas_call(
        _merge_kernel,
        grid=(t // tm,),
        in_specs=[row(d), pl.BlockSpec((1, 6, d), lambda i: (i // per, 0, 0)),
                  row(ya.shape[1]), row(yb.shape[1]), row(d), row(d),
                  full(w_pa_bf), full(w_pb_bf), full(w_out_bf)],
        out_specs=row(d),
        out_shape=jax.ShapeDtypeStruct((t, d), F32),
        compiler_params=_cparams(("arbitrary",)),
        name="merge_outproj",
    )(x2, mod3, ya, yb, ga, gb, w_pa_bf, w_pb_bf, w_out_bf)


def _oddeven_merge(lo, hi, r):
    step = r * 2
    if step < hi - lo:
        yield from _oddeven_merge(lo, hi, step)
        yield from _oddeven_merge(lo + r, hi, step)
        yield from ((i, i + r) for i in range(lo + r, hi - r, step))
    else:
        yield (lo, lo + r)


def _oddeven_merge_sort(lo, hi):
    if hi - lo >= 1:
        mid = lo + (hi - lo) // 2
        yield from _oddeven_merge_sort(lo, mid)
        yield from _oddeven_merge_sort(mid + 1, hi)
        yield from _oddeven_merge(lo, hi, 1)


def _topk_rows(s, k, val_ref, idx_ref):
    n, tm = s.shape
    ng = n // SUBLANES
    assert ng & (ng - 1) == 0 and k <= ng
    sub = lax.broadcasted_iota(jnp.int32, (SUBLANES, tm), 0)
    vals = [s[g * SUBLANES:(g + 1) * SUBLANES, :] for g in range(ng)]
    rows = [sub + g * SUBLANES for g in range(ng)]
    for i, j in _oddeven_merge_sort(0, ng - 1):
        keep = (vals[i] > vals[j]) | ((vals[i] == vals[j]) & (rows[i] < rows[j]))
        vals[i], vals[j] = jnp.where(keep, vals[i], vals[j]), jnp.where(keep, vals[j], vals[i])
        rows[i], rows[j] = jnp.where(keep, rows[i], rows[j]), jnp.where(keep, rows[j], rows[i])
    for t in range(k):
        m = jnp.max(vals[0], axis=0, keepdims=True)
        first = jnp.min(jnp.where(vals[0] == m, rows[0], n), axis=0, keepdims=True)
        val_ref[t:t + 1, :] = m
        idx_ref[t:t + 1, :] = first
        if t + 1 < k:
            popped = rows[0] == first
            for g in range(k - 1 - t):
                vals[g] = jnp.where(popped, vals[g + 1], vals[g])
                rows[g] = jnp.where(popped, rows[g + 1], rows[g])


def _peer_score_kernel(x_ref, mod_ref, g_ref, wq_ref, k1_ref, k2_ref, h_ref, idx_ref, gate_ref,
                       v1_ref, i1_ref, v2_ref, i2_ref, cv_ref, ce_ref, it_ref, gt_ref, *, tm):
    m = mod_ref[0]
    h = _norm_mod(x_ref[...], g_ref[...], m[3:4], m[4:5])
    h_ref[...] = h
    q = _dot(h.astype(BF16), wq_ref[...]).astype(BF16)
    half = PEER_QDIM // 2
    neg = jnp.full((1, tm), -jnp.inf, F32)
    for hd in range(PEER_HEADS):
        q1 = q[:, hd * PEER_QDIM: hd * PEER_QDIM + half]
        q2 = q[:, hd * PEER_QDIM + half: (hd + 1) * PEER_QDIM]
        _topk_rows(_dot_nt(k1_ref[...], q1), PEER_TOPK, v1_ref, i1_ref)
        _topk_rows(_dot_nt(k2_ref[...], q2), PEER_TOPK, v2_ref, i2_ref)
        for r, (i, j) in enumerate(_STAIR):
            cv_ref[r:r + 1, :] = v1_ref[i:i + 1, :] + v2_ref[j:j + 1, :]
            ce_ref[r:r + 1, :] = i1_ref[i:i + 1, :] * N_KEYS + i2_ref[j:j + 1, :]
        for r in range(len(_STAIR), _STAIR_ROWS):
            cv_ref[r:r + 1, :] = neg
            ce_ref[r:r + 1, :] = jnp.zeros((1, tm), jnp.int32)
        cand = cv_ref[...]
        ce = ce_ref[...]
        row = lax.broadcasted_iota(jnp.int32, cand.shape, 0)
        vals = []
        for kk in range(PEER_TOPK):
            mx = jnp.max(cand, axis=0, keepdims=True)
            first = jnp.min(jnp.where(cand == mx, row, _STAIR_ROWS), axis=0, keepdims=True)
            sel = row == first
            it_ref[hd * PEER_TOPK + kk: hd * PEER_TOPK + kk + 1, :] = jnp.sum(jnp.where(sel, ce, 0), axis=0, keepdims=True)
            vals.append(mx)
            cand = jnp.where(sel, -jnp.inf, cand)
        ex = [jnp.exp(v - vals[0]) for v in vals]
        tot = ex[0]
        for e in ex[1:]:
            tot = tot + e
        for kk in range(PEER_TOPK):
            gt_ref[hd * PEER_TOPK + kk: hd * PEER_TOPK + kk + 1, :] = ex[kk] / tot
    idx_ref[...] = it_ref[...].T
    gate_ref[...] = gt_ref[...].T


def _peer_score(x2, mod3, gain, wq_bf, k1_bf, k2_bf, seq, tm=256):
    t, d = x2.shape
    per = seq // tm
    slots = PEER_HEADS * PEER_TOPK
    full = lambda a: pl.BlockSpec(a.shape, lambda i: (0, 0))
    return pl.pallas_call(
        functools.partial(_peer_score_kernel, tm=tm),
        grid=(t // tm,),
        in_specs=[pl.BlockSpec((tm, d), lambda i: (i, 0)),
                  pl.BlockSpec((1, 6, d), lambda i: (i // per, 0, 0)),
                  pl.BlockSpec((1, d), lambda i: (0, 0)),
                  full(wq_bf), full(k1_bf), full(k2_bf)],
        out_specs=[pl.BlockSpec((tm, d), lambda i: (i, 0)),
                   pl.BlockSpec((tm, slots), lambda i: (i, 0)),
                   pl.BlockSpec((tm, slots), lambda i: (i, 0))],
        out_shape=[jax.ShapeDtypeStruct((t, d), F32),
                   jax.ShapeDtypeStruct((t, slots), jnp.int32),
                   jax.ShapeDtypeStruct((t, slots), F32)],
        scratch_shapes=[pltpu.VMEM((PEER_TOPK, tm), F32), pltpu.VMEM((PEER_TOPK, tm), jnp.int32),
                        pltpu.VMEM((PEER_TOPK, tm), F32), pltpu.VMEM((PEER_TOPK, tm), jnp.int32),
                        pltpu.VMEM((_STAIR_ROWS, tm), F32), pltpu.VMEM((_STAIR_ROWS, tm), jnp.int32),
                        pltpu.VMEM((slots, tm), jnp.int32), pltpu.VMEM((slots, tm), F32)],
        compiler_params=_cparams(("arbitrary",)),
        name="peer_score_topk",
    )(x2, mod3, gain.reshape(1, d), wq_bf, k1_bf, k2_bf)


def _peer_gather_kernel(idx_ref, idxn_ref, tab_ref, h_ref, gate_ref, x_ref, mod_ref, after_ref, o_ref,
                        buf_ref, sem_ref, *, tc, d, nsteps):
    slots = PEER_HEADS * PEER_TOPK
    step = pl.program_id(0)

    def issue_token(chunk, t):
        ids_ref, c = (idx_ref, chunk) if chunk < GATHER_RING else (idxn_ref, chunk - GATHER_RING)
        for j in range(slots):
            pltpu.make_async_copy(tab_ref.at[ids_ref[c * tc + t, j]],
                                  buf_ref.at[c, t, pl.ds(j, 1), :],
                                  sem_ref.at[c]).start(priority=j % 2)

    def wait(slot):
        pltpu.make_async_copy(buf_ref.at[slot], buf_ref.at[slot], sem_ref.at[slot]).wait()

    eye = (lax.broadcasted_iota(jnp.int32, (slots, slots), 0) ==
           lax.broadcasted_iota(jnp.int32, (slots, slots), 1))
    gt2 = mod_ref[0][5:6]

    def mix_token(slot, t):
        r = slot * tc + t
        word = buf_ref[slot, t]
        u = lax.bitcast_convert_type(word << 16, F32)
        prod = u * h_ref[r:r + 1, :]
        part = prod[:, 0:LANES]
        for c in range(1, d // LANES):
            part = part + prod[:, c * LANES:(c + 1) * LANES]
        a = jnp.sum(part, axis=1, keepdims=True)
        grow = jnp.broadcast_to(gate_ref[r:r + 1, :], (slots, slots))
        gcol = jnp.sum(jnp.where(eye, grow, 0.0), axis=1, keepdims=True)
        w = gcol * (0.5 * a * (1.0 + lax.erf(a * (2.0 ** -0.5))))
        v = lax.bitcast_convert_type(buf_ref[slot, t] & jnp.uint32(0xFFFF0000), F32)
        y = jnp.sum(v * w, axis=0, keepdims=True)
        o_ref[r:r + 1, :] = x_ref[r:r + 1, :] + gt2 * y

    @pl.when(step == 0)
    def _():
        for c in range(GATHER_AHEAD):
            for t in range(tc):
                issue_token(c, t)

    for p in range(GATHER_RING):
        wait(p)
        for t in range(tc):
            issue_token(p + GATHER_AHEAD, t)
            mix_token(p, t)

    @pl.when(step == nsteps - 1)
    def _():
        for c in range(GATHER_AHEAD):
            wait(c)


def _peer_gather(x2, mod3, h2, idx, gate, table, seq, tok0, ntok, after, tc=8):
    d = x2.shape[1]
    slots = idx.shape[1]
    tb = GATHER_RING * tc
    nsteps = ntok // tb
    off = tok0 // tb
    per = seq // tb
    row = lambda w: pl.BlockSpec((tb, w), lambda i: (i + off, 0))
    return pl.pallas_call(
        functools.partial(_peer_gather_kernel, tc=tc, d=d, nsteps=nsteps),
        grid=(nsteps,),
        in_specs=[pl.BlockSpec((tb, slots), lambda i: (i + off, 0), memory_space=pltpu.SMEM),
                  pl.BlockSpec((tb, slots), lambda i: (jnp.minimum(i + 1, nsteps - 1) + off, 0),
                               memory_space=pltpu.SMEM),
                  pl.BlockSpec(memory_space=pl.ANY),
                  row(d), row(slots), row(d),
                  pl.BlockSpec((1, 6, d), lambda i: ((i + off) // per, 0, 0)),
                  pl.BlockSpec(memory_space=pl.ANY)],
        out_specs=pl.BlockSpec((tb, d), lambda i: (i, 0)),
        out_shape=jax.ShapeDtypeStruct((ntok, d), F32),
        scratch_shapes=[pltpu.VMEM((GATHER_RING, tc, slots, d), jnp.uint32),
                        pltpu.SemaphoreType.DMA((GATHER_RING,))],
        compiler_params=_cparams(("arbitrary",)),
        name="peer_gather_mix",
    )(idx, idx, table.reshape(table.shape[0], 1, d), h2, gate, x2, mod3, after)


SC_LANES = 16
SC_CORES = 2
SC_SUBCORES = 16
SC_GROUP = 32
SC_HALF = 64
SC_SEQS = 12
TC_FIRST_SEQS = 7


def _pack_kernel(u_ref, v_ref, uv_ref, uw_ref, vw_ref):
    half = u_ref.shape[1] // 2

    def hi_bits(x):
        return lax.bitcast_convert_type(x.astype(BF16).astype(F32), jnp.uint32)

    ub = hi_bits(u_ref[...])
    vb = hi_bits(v_ref[...])
    uv_ref[...] = vb | (ub >> 16)
    uw_ref[...] = ub[:, half:] | (ub[:, :half] >> 16)
    vw_ref[...] = vb[:, half:] | (vb[:, :half] >> 16)


def _pack_tables(u_tab, v_tab, tm=1024):
    e, d = u_tab.shape
    blk = lambda w: pl.BlockSpec((tm, w), lambda i: (i, 0))
    return pl.pallas_call(
        _pack_kernel,
        grid=(e // tm,),
        in_specs=[blk(d), blk(d)],
        out_specs=[blk(d), blk(d // 2), blk(d // 2)],
        out_shape=[jax.ShapeDtypeStruct((e, d), jnp.uint32),
                   jax.ShapeDtypeStruct((e, d // 2), jnp.uint32),
                   jax.ShapeDtypeStruct((e, d // 2), jnp.uint32)],
        compiler_params=_cparams(("arbitrary",)),
        name="peer_pack_tables",
    )(u_tab, v_tab)


def _sc_worker_base(n_per):
    return (lax.axis_index("s") * SC_CORES + lax.axis_index("c")) * n_per


def _sc_halves(tab_hbm, idx_v, rows_v, sems, tt, half):
    ids = idx_v.at[tt, pl.ds(half * SC_HALF, SC_HALF)]
    return pltpu.make_async_copy(tab_hbm.at[ids], rows_v.at[half, :, pl.ds(0, tab_hbm.shape[1])], sems.at[half])


def _sc_token_loop(tab_hbm, idx_v, rows_v, sems, compute_half):
    _sc_halves(tab_hbm, idx_v, rows_v, sems, 0, 0).start()

    def tok(tt, carry):
        _sc_halves(tab_hbm, idx_v, rows_v, sems, tt, 1).start()
        _sc_halves(tab_hbm, idx_v, rows_v, sems, tt, 0).wait()
        compute_half(tt, 0)

        @pl.when(tt + 1 < SC_GROUP)
        def _():
            _sc_halves(tab_hbm, idx_v, rows_v, sems, tt + 1, 0).start()

        _sc_halves(tab_hbm, idx_v, rows_v, sems, tt, 1).wait()
        compute_half(tt, 1)
        return carry

    lax.fori_loop(0, SC_GROUP, tok, 0)


def _unpack_pair(word):
    lo = lax.bitcast_convert_type(word << 16, F32)
    hi = lax.bitcast_convert_type(word & jnp.uint32(0xFFFF0000), F32)
    return lo, hi


def _sc_dot_body(tab_hbm, idx_hbm, h_hbm, a_hbm, idx_v, h_v, rows_v, a_v, tr_v, sems, *, tok0, n_per, d):
    base = _sc_worker_base(n_per)
    lane = lax.iota(jnp.int32, SC_LANES)
    nk = d // (2 * SC_LANES)

    def compute_half(tt, half):
        def rows16(gi, carry):
            accs = [jnp.zeros((SC_LANES,), F32) for _ in range(SC_LANES)]
            for k in range(nk):
                hlo = h_v[tt, pl.ds(SC_LANES * k, SC_LANES)]
                hhi = h_v[tt, pl.ds(d // 2 + SC_LANES * k, SC_LANES)]
                for q in range(SC_LANES):
                    lo, hi = _unpack_pair(rows_v[half, gi * SC_LANES + q, pl.ds(SC_LANES * k, SC_LANES)])
                    accs[q] = accs[q] + lo * hlo + hi * hhi
            for q in range(SC_LANES):
                tr_v[q, :] = accs[q]
            tot = plsc.load_gather(tr_v, [lane, jnp.zeros((SC_LANES,), jnp.int32)])
            for c in range(1, SC_LANES):
                tot = tot + plsc.load_gather(tr_v, [lane, jnp.full((SC_LANES,), c, jnp.int32)])
            a_v[tt, pl.ds(half * SC_HALF + gi * SC_LANES, SC_LANES)] = tot
            return carry
        lax.fori_loop(0, SC_HALF // SC_LANES, rows16, 0)

    def group(g, carry):
        loc = base + g * SC_GROUP
        pltpu.sync_copy(idx_hbm.at[pl.ds(tok0 + loc, SC_GROUP)], idx_v)
        pltpu.sync_copy(h_hbm.at[pl.ds(tok0 + loc, SC_GROUP)], h_v)
        _sc_token_loop(tab_hbm, idx_v, rows_v, sems, compute_half)
        pltpu.sync_copy(a_v, a_hbm.at[pl.ds(loc, SC_GROUP)])
        return carry

    lax.fori_loop(0, n_per // SC_GROUP, group, 0)


def _sc_mix_body(tab_hbm, idx_hbm, w_hbm, y_hbm, idx_v, w_v, rows_v, y_v, sems, *, tok0, n_per, d):
    base = _sc_worker_base(n_per)
    nk = d // (2 * SC_LANES)
    nq = 2
    kq = nk // nq

    def compute_half(tt, half):
        ttv = jnp.full((SC_LANES,), tt, jnp.int32)
        for piece in range(nq):
            cols = [(m % 2) * (d // 2) + SC_LANES * (piece * kq + m // 2) for m in range(2 * kq)]
            if half == 0:
                accs = tuple(jnp.zeros((SC_LANES,), F32) for _ in range(2 * kq))
            else:
                accs = tuple(y_v[tt, pl.ds(c, SC_LANES)] for c in cols)

            def row(r, accs):
                wsplat = plsc.load_gather(w_v, [ttv, jnp.full((SC_LANES,), half * SC_HALF + r, jnp.int32)])
                out = []
                for k in range(kq):
                    lo, hi = _unpack_pair(rows_v[half, r, pl.ds(SC_LANES * (piece * kq + k), SC_LANES)])
                    out.append(accs[2 * k] + wsplat * lo)
                    out.append(accs[2 * k + 1] + wsplat * hi)
                return tuple(out)

            accs = lax.fori_loop(0, SC_HALF, row, accs)
            for m in range(2 * kq):
                y_v[tt, pl.ds(cols[m], SC_LANES)] = accs[m]

    def group(g, carry):
        loc = base + g * SC_GROUP
        pltpu.sync_copy(idx_hbm.at[pl.ds(tok0 + loc, SC_GROUP)], idx_v)
        pltpu.sync_copy(w_hbm.at[pl.ds(loc, SC_GROUP)], w_v)
        _sc_token_loop(tab_hbm, idx_v, rows_v, sems, compute_half)
        pltpu.sync_copy(y_v, y_hbm.at[pl.ds(loc, SC_GROUP)])
        return carry

    lax.fori_loop(0, n_per // SC_GROUP, group, 0)


def _sc_mesh():
    return plsc.VectorSubcoreMesh(core_axis_name="c", subcore_axis_name="s")


def _sc_dot(u_words, idx, h2, tok0, ntok):
    d = h2.shape[1]
    slots = idx.shape[1]
    n_per = ntok // (SC_CORES * SC_SUBCORES)
    return pl.kernel(
        functools.partial(_sc_dot_body, tok0=tok0, n_per=n_per, d=d),
        out_type=jax.ShapeDtypeStruct((ntok, slots), F32),
        mesh=_sc_mesh(),
        scratch_types=[pltpu.VMEM((SC_GROUP, slots), jnp.int32),
                       pltpu.VMEM((SC_GROUP, d), F32),
                       pltpu.VMEM((2, SC_HALF, d // 2 + SC_LANES), jnp.uint32),
                       pltpu.VMEM((SC_GROUP, slots), F32),
                       pltpu.VMEM((SC_LANES, SC_LANES), F32),
                       pltpu.SemaphoreType.DMA((2,))],
        compiler_params=pltpu.CompilerParams(needs_layout_passes=False),
        name="peer_sc_dot",
    )(u_words, idx, h2)


def _sc_mix(v_words, idx, w, tok0, ntok, d):
    slots = idx.shape[1]
    n_per = ntok // (SC_CORES * SC_SUBCORES)
    return pl.kernel(
        functools.partial(_sc_mix_body, tok0=tok0, n_per=n_per, d=d),
        out_type=jax.ShapeDtypeStruct((ntok, d), F32),
        mesh=_sc_mesh(),
        scratch_types=[pltpu.VMEM((SC_GROUP, slots), jnp.int32),
                       pltpu.VMEM((SC_GROUP, slots), F32),
                       pltpu.VMEM((2, SC_HALF, d // 2), jnp.uint32),
                       pltpu.VMEM((SC_GROUP, d), F32),
                       pltpu.SemaphoreType.DMA((2,))],
        compiler_params=pltpu.CompilerParams(needs_layout_passes=False),
        name="peer_sc_mix",
    )(v_words, idx, w)


def _gelu_gate_kernel(a_ref, g_ref, after_ref, o_ref):
    a = a_ref[...]
    o_ref[...] = g_ref[...] * (0.5 * a * (1.0 + lax.erf(a * (2.0 ** -0.5))))


def _gelu_gate(a, gate, tok0, after, tm=2048):
    n, slots = a.shape
    off = tok0 // tm
    return pl.pallas_call(
        _gelu_gate_kernel,
        grid=(n // tm,),
        in_specs=[pl.BlockSpec((tm, slots), lambda i: (i, 0)),
                  pl.BlockSpec((tm, slots), lambda i: (i + off, 0)),
                  pl.BlockSpec(memory_space=pl.ANY)],
        out_specs=pl.BlockSpec((tm, slots), lambda i: (i, 0)),
        out_shape=jax.ShapeDtypeStruct((n, slots), F32),
        compiler_params=_cparams(("arbitrary",)),
        name="peer_gelu_gate",
    )(a, gate, after)


def _residual_kernel(x_ref, mod_ref, y_ref, o_ref):
    o_ref[...] = x_ref[...] + mod_ref[0][5:6] * y_ref[...]


def _residual(x2, mod3, y, tok0, seq, tm=512):
    n, d = y.shape
    off = tok0 // tm
    per = seq // tm
    return pl.pallas_call(
        _residual_kernel,
        grid=(n // tm,),
        in_specs=[pl.BlockSpec((tm, d), lambda i: (i + off, 0)),
                  pl.BlockSpec((1, 6, d), lambda i: ((i + off) // per, 0, 0)),
                  pl.BlockSpec((tm, d), lambda i: (i, 0))],
        out_specs=pl.BlockSpec((tm, d), lambda i: (i, 0)),
        out_shape=jax.ShapeDtypeStruct((n, d), F32),
        compiler_params=_cparams(("arbitrary",)),
        name="peer_residual",
    )(x2, mod3, y)


def kernel(x_prompt, x_sample, c_prompt, c_sample, w_mod, b_mod, g_norm1, g_norm2, w_in, conv_w, conv_b, f_w1, f_b1, f_freq, f_w2, f_b2, f_w3, f_bias, q_gain, k_gain, sink, w_pa, w_pb, w_out, peer_wq, peer_k1, peer_k2, peer_u, peer_v):
    depth = w_mod.shape[0]
    bp, seq, d = x_prompt.shape
    bs = x_sample.shape[0]
    assert x_sample.shape[1] == seq
    bsz = bp + bs
    x = jnp.concatenate([x_prompt, x_sample], axis=0).reshape(bsz * seq, d)
    c = jnp.concatenate([c_prompt, c_sample], axis=0)

    hyw = w_pa.shape[1]
    aw = w_pb.shape[1]
    kw = N_KV_HEADS * HEAD_DIM
    widths = (HY_ORDER + 1) * hyw, aw, kw, kw, d, d
    cblk = hyw // 256

    mod = _modulation(c, w_mod.astype(BF16), b_mod)
    fc, fs = _dft_tables(seq)

    for l in range(depth):
        mod3 = mod[l].reshape(bsz, 6, d)
        hy, q, k, v, ga, gb = _inproj(x, mod3, g_norm1[l], w_in[l].astype(BF16), seq, widths)
        hcat = _filters_time(seq, f_w1[l], f_b1[l], f_freq[l], f_w2[l], f_b2[l], f_w3[l])
        spec_a, spec_b, nyq = _filter_spectra(fc, fs, hcat)
        cw, cb = conv_w[l], conv_b[l][None, :]
        zz = _long_conv(fc, fs, hy, 0, hy, cblk, spec_a, spec_b, nyq, 0, f_bias[l][0:1], cw, cb,
                        0, cblk, bsz, seq, conv_u=True)
        ya = _long_conv(fc, fs, zz, 0, hy, 2 * cblk, spec_a, spec_b, nyq, cblk, f_bias[l][1:2], cw, cb,
                        0, 2 * cblk, bsz, seq, conv_u=False)
        yb = _attention(q, k, v, q_gain[l], k_gain[l], sink[l], bsz, seq)
        x = _merge(x, mod3, ya, yb, ga, gb, w_pa[l].astype(BF16), w_pb[l].astype(BF16), w_out[l].astype(BF16), seq)
        h2, idx, gate = _peer_score(x, mod3, g_norm2[l], peer_wq[l].astype(BF16),
                                    peer_k1[l].astype(BF16), peer_k2[l].astype(BF16), seq)
        table, u_words, v_words = _pack_tables(peer_u[l], peer_v[l])
        t_tc = (bsz - SC_SEQS) * seq
        t_sc = SC_SEQS * seq
        t_g1 = TC_FIRST_SEQS * seq
        a_sc = _sc_dot(u_words, idx, h2, t_tc, t_sc)
        x_g1 = _peer_gather(x, mod3, h2, idx, gate, table, seq, 0, t_g1, gate)
        w_sc = _gelu_gate(a_sc, gate, t_tc, x_g1)
        y_sc = _sc_mix(v_words, idx, w_sc, t_tc, t_sc, d)
        x_g2 = _peer_gather(x, mod3, h2, idx, gate, table, seq, t_g1, t_tc - t_g1, w_sc)
        x = jnp.concatenate([x_g1, x_g2, _residual(x, mod3, y_sc, t_tc, seq)], axis=0)

    x = x.reshape(bsz, seq, d)
    return x[:bp], x[bp:]
```

```python
import functools
import math

import jax
import jax.numpy as jnp
import numpy as np
from jax import lax
from jax.experimental import pallas as pl
from jax.experimental.pallas import tpu as pltpu
from jax.experimental.pallas import tpu_sc as plsc

F32 = jnp.float32
BF16 = jnp.bfloat16

EPS = 1e-6
HEAD_DIM = 64
N_Q_HEADS = 8
N_KV_HEADS = 2
WINDOW = 128
BLOCK = 128
ROPE_THETA = 10000.0
HY_ORDER = 2
N_DIR = 2
FILT_BANDS = 16
DECAY_TARGET = 1e-2
FAST_DECAY_PCT = 0.3
SLOW_DECAY_PCT = 1.5
PEER_HEADS = 8
N_KEYS = 128
PEER_TOPK = 16
PEER_QDIM = 256
LANES = 128
SUBLANES = 8
VMEM_LIMIT = 56 * 1024 * 1024
GATHER_RING = 4
GATHER_AHEAD = 2

_STAIR = [(i, j) for i in range(PEER_TOPK) for j in range(PEER_TOPK) if (i + 1) * (j + 1) <= PEER_TOPK]
_STAIR_ROWS = -(-len(_STAIR) // SUBLANES) * SUBLANES


def _cparams(sem, vmem=VMEM_LIMIT):
    return pltpu.CompilerParams(dimension_semantics=sem, vmem_limit_bytes=vmem)


def _dot(a, b):
    return jnp.dot(a, b, preferred_element_type=F32)


def _dot_nt(a, b):
    return lax.dot_general(a, b, (((1,), (1,)), ((), ())), preferred_element_type=F32)


def _dot_hi(a, b):
    return jnp.dot(a, b, preferred_element_type=F32, precision=lax.Precision.HIGHEST)


def _sigmoid(x):
    return 1.0 / (1.0 + jnp.exp(-x))


def _mod_kernel(c_ref, w_ref, b_ref, o_ref):
    c = c_ref[...]
    s = c * _sigmoid(c)
    o_ref[0] = _dot(s.astype(BF16), w_ref[0]) + b_ref[0]


def _modulation(c, w_mod_bf, b_mod):
    depth, d, n6 = w_mod_bf.shape
    bsz = c.shape[0]
    tn = 1536
    return pl.pallas_call(
        _mod_kernel,
        grid=(depth, n6 // tn),
        in_specs=[pl.BlockSpec((bsz, d), lambda l, j: (0, 0)),
                  pl.BlockSpec((1, d, tn), lambda l, j: (l, 0, j)),
                  pl.BlockSpec((1, 1, tn), lambda l, j: (l, 0, j))],
        out_specs=pl.BlockSpec((1, bsz, tn), lambda l, j: (l, 0, j)),
        out_shape=jax.ShapeDtypeStruct((depth, bsz, n6), F32),
        compiler_params=_cparams(("arbitrary", "arbitrary")),
        name="adaln_mod",
    )(c, w_mod_bf, b_mod.reshape(depth, 1, n6))


def _norm_mod(x, gain, shift, scale):
    y = x * lax.rsqrt(jnp.mean(x * x, axis=-1, keepdims=True) + EPS)
    return (y * gain) * (1.0 + scale) + shift


def _inproj_kernel(x_ref, mod_ref, g_ref, w_ref, hy_ref, q_ref, k_ref, v_ref, ga_ref, gb_ref, *, splits):
    m = mod_ref[0]
    h = _norm_mod(x_ref[...], g_ref[...], m[0:1], m[1:2])
    z = _dot(h.astype(BF16), w_ref[...])
    outs = (hy_ref, q_ref, k_ref, v_ref, ga_ref, gb_ref)
    lo = 0
    for ref, hi in zip(outs, splits):
        ref[...] = z[:, lo:hi]
        lo = hi


def _inproj(x2, mod3, gain, w_in_bf, seq, widths, tm=256):
    t, d = x2.shape
    ncols = w_in_bf.shape[1]
    per = seq // tm
    splits = tuple(int(s) for s in np.cumsum(widths))
    return pl.pallas_call(
        functools.partial(_inproj_kernel, splits=splits),
        grid=(t // tm,),
        in_specs=[pl.BlockSpec((tm, d), lambda i: (i, 0)),
                  pl.BlockSpec((1, 6, d), lambda i: (i // per, 0, 0)),
                  pl.BlockSpec((1, d), lambda i: (0, 0)),
                  pl.BlockSpec((d, ncols), lambda i: (0, 0))],
        out_specs=[pl.BlockSpec((tm, w), lambda i: (i, 0)) for w in widths],
        out_shape=[jax.ShapeDtypeStruct((t, w), F32) for w in widths],
        compiler_params=_cparams(("arbitrary",)),
        name="inproj",
    )(x2, mod3, gain.reshape(1, d), w_in_bf)


def _filter_kernel(z_ref, w1_ref, b1_ref, fr_ref, w2_ref, b2_ref, w3_ref, ad_ref, o_ref):
    z = z_ref[0]
    fr = fr_ref[...]
    a = jnp.sin(fr * (_dot_hi(z, w1_ref[...]) + b1_ref[...]))
    a = jnp.sin(fr * (_dot_hi(a, w2_ref[...]) + b2_ref[...]))
    h = _dot_hi(a, w3_ref[0])
    h = h * jnp.exp(-z[:, 0:1] * ad_ref[...])
    row = lax.broadcasted_iota(jnp.int32, h.shape, 0)
    dead = (pl.program_id(0) == 1) & (pl.program_id(1) == 0) & (row == 0)
    o_ref[0] = jnp.where(dead, 0.0, h)


def _filters_time(seq, f_w1, f_b1, f_freq, f_w2, f_b2, f_w3, tm=512):
    hidden = f_w1.shape[1]
    cw = f_w3.shape[1] // (HY_ORDER * N_DIR)
    t = jnp.linspace(0.0, 1.0, seq, dtype=F32)[:, None]
    w = 2.0 * math.pi * jnp.arange(seq, dtype=F32)[:, None] / seq
    bands = jnp.linspace(1e-4, FILT_BANDS - 1, FILT_BANDS, dtype=F32)[None, :]
    z = jnp.concatenate([t, jnp.cos(bands * w), -jnp.sin(bands * w)], axis=-1)
    emb = z.shape[1]
    z = jnp.pad(z, ((0, 0), (0, LANES - emb)))
    zcat = jnp.stack([z, jnp.concatenate([z[:1], z[:0:-1]], axis=0)], axis=0)
    w1p = jnp.pad(f_w1, ((0, LANES - emb), (0, 0)))
    w3d = f_w3.reshape(hidden, HY_ORDER, N_DIR, cw).transpose(2, 0, 1, 3).reshape(N_DIR, hidden, HY_ORDER * cw)
    max_decay = math.log(DECAY_TARGET) / FAST_DECAY_PCT
    min_decay = math.log(DECAY_TARGET) / SLOW_DECAY_PCT
    ad = jnp.abs(jnp.linspace(min_decay, max_decay, cw, dtype=F32))
    ad = jnp.tile(ad, HY_ORDER)[None, :]
    oc = HY_ORDER * cw
    return pl.pallas_call(
        _filter_kernel,
        grid=(N_DIR, seq // tm),
        in_specs=[pl.BlockSpec((1, tm, LANES), lambda g, r: (g, r, 0)),
                  pl.BlockSpec((LANES, hidden), lambda g, r: (0, 0)),
                  pl.BlockSpec((1, hidden), lambda g, r: (0, 0)),
                  pl.BlockSpec((1, hidden), lambda g, r: (0, 0)),
                  pl.BlockSpec((hidden, hidden), lambda g, r: (0, 0)),
                  pl.BlockSpec((1, hidden), lambda g, r: (0, 0)),
                  pl.BlockSpec((1, hidden, oc), lambda g, r: (g, 0, 0)),
                  pl.BlockSpec((1, oc), lambda g, r: (0, 0))],
        out_specs=pl.BlockSpec((1, tm, oc), lambda g, r: (g, r, 0)),
        out_shape=jax.ShapeDtypeStruct((N_DIR, seq, oc), F32),
        compiler_params=_cparams(("arbitrary", "arbitrary")),
        name="hyena_filter_mlp",
    )(zcat, w1p, f_b1[None, :], f_freq[None, :], f_w2, f_b2[None, :], w3d, ad)


def _dft_tables(seq):
    n2 = 2 * seq
    f = jnp.arange(seq, dtype=jnp.int32)
    ft = (f[:, None] * f[None, :]) % n2
    ang = ft.astype(F32) * (2.0 * math.pi / n2)
    return jnp.cos(ang).astype(BF16), jnp.sin(ang).astype(BF16)


def _spec_kernel(fc_ref, fs_ref, h_ref, a_ref, b_ref, nyq_ref, *, seq, tf):
    hlo = h_ref[0]
    hhi = h_ref[1]
    hlo_b = hlo.astype(BF16)
    hhi_b = hhi.astype(BF16)
    f = pl.program_id(1) * tf + lax.broadcasted_iota(jnp.int32, (tf, 1), 0)
    sgn = jnp.where(f % 2 == 0, 1.0, -1.0)
    fc = fc_ref[...]
    fs = fs_ref[...]
    hr = _dot(fc, hlo_b) + sgn * _dot(fc, hhi_b)
    hs = _dot(fs, hlo_b) + sgn * _dot(fs, hhi_b)
    w = jnp.where(f == 0, 1.0, 2.0) * (1.0 / (2 * seq))
    a_ref[...] = w * hr
    b_ref[...] = -(w * hs)
    t = lax.broadcasted_iota(jnp.int32, (seq, 1), 0)
    alt = jnp.where(t % 2 == 0, 1.0, -1.0)
    nyq = jnp.sum(alt * (hlo + hhi), axis=0, keepdims=True) * (1.0 / (2 * seq))
    nyq_ref[...] = jnp.broadcast_to(nyq, nyq_ref.shape)


def _filter_spectra(fc, fs, hcat, tf=512, tcol=512):
    seq = fc.shape[0]
    oc = hcat.shape[2]
    return pl.pallas_call(
        functools.partial(_spec_kernel, seq=seq, tf=tf),
        grid=(oc // tcol, seq // tf),
        in_specs=[pl.BlockSpec((tf, seq), lambda j, i: (i, 0)),
                  pl.BlockSpec((tf, seq), lambda j, i: (i, 0)),
                  pl.BlockSpec((2, seq, tcol), lambda j, i: (0, 0, j))],
        out_specs=[pl.BlockSpec((tf, tcol), lambda j, i: (i, j)),
                   pl.BlockSpec((tf, tcol), lambda j, i: (i, j)),
                   pl.BlockSpec((SUBLANES, tcol), lambda j, i: (0, j))],
        out_shape=[jax.ShapeDtypeStruct((seq, oc), F32),
                   jax.ShapeDtypeStruct((seq, oc), F32),
                   jax.ShapeDtypeStruct((SUBLANES, oc), F32)],
        compiler_params=_cparams(("arbitrary", "arbitrary")),
        name="hyena_filter_spectra",
    )(fc, fs, hcat)


def _shortconv(x, w_ref, b_ref, seq):
    row = lax.broadcasted_iota(jnp.int32, (seq, 1), 0)
    xm = jnp.where(row == 0, 0.0, pltpu.roll(x, 1, 0))
    xp = jnp.where(row == seq - 1, 0.0, pltpu.roll(x, seq - 1, 0))
    return xm * w_ref[0:1, :] + x * w_ref[1:2, :] + xp * w_ref[2:3, :] + b_ref[...]


def _conv_kernel(fc_ref, fs_ref, u_ref, g_ref, a_ref, b_ref, nyq_ref, bias_ref,
                 cwu_ref, cbu_ref, cwg_ref, cbg_ref, o_ref, acc_ref, *, seq, fb, conv_u):
    u = u_ref[...]
    if conv_u:
        u = _shortconv(u, cwu_ref, cbu_ref, seq)
    gate = _shortconv(g_ref[...], cwg_ref, cbg_ref, seq)
    ub = u.astype(BF16)
    for c in range(seq // fb):
        rows = slice(c * fb, (c + 1) * fb)
        ur = _dot(fc_ref[rows, :], ub)
        us = _dot(fs_ref[rows, :], ub)
        a = a_ref[rows, :]
        b = b_ref[rows, :]
        qr = (ur * a + us * b).astype(BF16)
        qi = (us * a - ur * b).astype(BF16)
        part = _dot(fc_ref[:, rows], qr) + _dot(fs_ref[:, rows], qi)
        if c == 0:
            acc_ref[...] = part
        else:
            acc_ref[...] += part
    t = lax.broadcasted_iota(jnp.int32, (seq, 1), 0)
    alt = jnp.where(t % 2 == 0, 1.0, -1.0)
    unyq = jnp.sum(alt * u, axis=0, keepdims=True)
    y = acc_ref[...] + alt * (unyq * nyq_ref[0:1, :]) + bias_ref[...] * u
    o_ref[...] = gate * y


def _long_conv(fc, fs, u_src, u_blk0, g_src, g_blk0, spec_a, spec_b, nyq, s_blk0, bias, cw, cb,
               cu_blk0, cg_blk0, bsz, seq, conv_u, tc=256, fb=512):
    nct = 512 // tc
    t = bsz * seq
    const = lambda j, b: (0, 0)
    return pl.pallas_call(
        functools.partial(_conv_kernel, seq=seq, fb=fb, conv_u=conv_u),
        grid=(nct, bsz),
        in_specs=[pl.BlockSpec(memory_space=pltpu.VMEM),
                  pl.BlockSpec(memory_space=pltpu.VMEM),
                  pl.BlockSpec((seq, tc), lambda j, b: (b, u_blk0 + j)),
                  pl.BlockSpec((seq, tc), lambda j, b: (b, g_blk0 + j)),
                  pl.BlockSpec((seq, tc), lambda j, b: (0, s_blk0 + j)),
                  pl.BlockSpec((seq, tc), lambda j, b: (0, s_blk0 + j)),
                  pl.BlockSpec((SUBLANES, tc), lambda j, b: (0, s_blk0 + j)),
                  pl.BlockSpec((1, tc), lambda j, b: (0, j)),
                  pl.BlockSpec((3, tc), lambda j, b: (0, cu_blk0 + j)),
                  pl.BlockSpec((1, tc), lambda j, b: (0, cu_blk0 + j)),
                  pl.BlockSpec((3, tc), lambda j, b: (0, cg_blk0 + j)),
                  pl.BlockSpec((1, tc), lambda j, b: (0, cg_blk0 + j))],
        out_specs=pl.BlockSpec((seq, tc), lambda j, b: (b, j)),
        out_shape=jax.ShapeDtypeStruct((t, 512), F32),
        scratch_shapes=[pltpu.VMEM((seq, tc), F32)],
        compiler_params=_cparams(("arbitrary", "arbitrary")),
        name="hyena_long_conv_u" if conv_u else "hyena_long_conv",
    )(fc, fs, u_src, g_src, spec_a, spec_b, nyq, bias, cw, cb, cw, cb)


def _attn_kernel(q_ref, k_ref, v_ref, cos_ref, sin_ref, qg_ref, kg_ref, sink_ref, o_ref,
                 qn_ref, km_ref, vm_ref, *, seq):
    lane = lax.broadcasted_iota(jnp.int32, (1, LANES), 1)
    r = lax.broadcasted_iota(jnp.int32, (LANES, LANES), 0) // HEAD_DIM
    c = lax.broadcasted_iota(jnp.int32, (LANES, LANES), 1) // HEAD_DIM
    bd = jnp.where(r == c, 1.0, 0.0).astype(BF16)
    first_half = (lane % HEAD_DIM) < (HEAD_DIM // 2)
    cos = cos_ref[...]
    sin = sin_ref[...]

    def norm_rope(x, gain):
        sq = x * x
        hi = sq.astype(BF16)
        lo = (sq - hi.astype(F32)).astype(BF16)
        ss = _dot(hi, bd) + _dot(lo, bd)
        y = (x * lax.rsqrt(ss * (1.0 / HEAD_DIM) + EPS)) * gain
        partner = jnp.where(first_half, pltpu.roll(y, LANES - HEAD_DIM // 2, 1), pltpu.roll(y, HEAD_DIM // 2, 1))
        return y * cos + partner * sin

    for p in range(N_Q_HEADS // 2):
        cols = slice(p * LANES, (p + 1) * LANES)
        qn_ref[:, cols] = norm_rope(q_ref[:, cols], qg_ref[...]).astype(BF16)
    kn = norm_rope(k_ref[...], kg_ref[...])
    left = lane < HEAD_DIM
    for src_ref, dst_ref in ((None, km_ref), (v_ref, vm_ref)):
        val = kn if src_ref is None else src_ref[...]
        rolled = pltpu.roll(val, HEAD_DIM, 1)
        dst_ref[0] = jnp.where(left, val, 0.0).astype(BF16)
        dst_ref[1] = jnp.where(left, 0.0, rolled).astype(BF16)
        dst_ref[2] = jnp.where(left, rolled, 0.0).astype(BF16)
        dst_ref[3] = jnp.where(left, 0.0, val).astype(BF16)

    span = 3 * BLOCK
    scale = HEAD_DIM ** -0.5
    ii = lax.broadcasted_iota(jnp.int32, (BLOCK, span), 0)
    jj = lax.broadcasted_iota(jnp.int32, (BLOCK, span), 1)

    def block(n, carry):
        q0 = pl.multiple_of(n * BLOCK, BLOCK)
        start = pl.multiple_of(jnp.clip((n - 1) * BLOCK, 0, seq - span), BLOCK)
        valid = jnp.abs((start - q0) + jj - ii) <= WINDOW
        for p in range(N_Q_HEADS // 2):
            cols = slice(p * LANES, (p + 1) * LANES)
            kv = (2 * p) // (N_Q_HEADS // N_KV_HEADS)
            qp = qn_ref[pl.ds(q0, BLOCK), cols]
            o = jnp.zeros((BLOCK, LANES), F32)
            for a in range(2):
                h = 2 * p + a
                kb = km_ref[2 * kv + a, pl.ds(start, span), :]
                s = _dot_nt(qp, kb) * scale
                s = jnp.where(valid, s, -jnp.inf)
                sk = sink_ref[h:h + 1, 0:1]
                m = jnp.maximum(jnp.max(s, axis=-1, keepdims=True), sk)
                e = jnp.exp(s - m)
                den = jnp.sum(e, axis=-1, keepdims=True) + jnp.exp(sk - m)
                pn = (e / den).astype(BF16)
                o = o + _dot(pn, vm_ref[2 * kv + a, pl.ds(start, span), :])
            o_ref[pl.ds(q0, BLOCK), cols] = o
        return carry

    lax.fori_loop(0, seq // BLOCK, block, 0)


def _attention(q, k, v, q_gain, k_gain, sink, bsz, seq):
    inv = ROPE_THETA ** (-jnp.arange(0, HEAD_DIM, 2, dtype=F32) / HEAD_DIM)
    ang = jnp.arange(seq, dtype=F32)[:, None] * inv[None, :]
    cos = jnp.tile(jnp.cos(ang), (1, LANES // (HEAD_DIM // 2)))
    sn = jnp.sin(ang)
    sin = jnp.tile(jnp.concatenate([-sn, sn], axis=1), (1, LANES // HEAD_DIM))
    qg = jnp.tile(q_gain, LANES // HEAD_DIM)[None, :]
    kg = jnp.tile(k_gain, LANES // HEAD_DIM)[None, :]
    sinkb = jnp.broadcast_to(sink[:, None], (N_Q_HEADS, LANES))
    aw = N_Q_HEADS * HEAD_DIM
    kw = N_KV_HEADS * HEAD_DIM
    const = lambda b: (0, 0)
    return pl.pallas_call(
        functools.partial(_attn_kernel, seq=seq),
        grid=(bsz,),
        in_specs=[pl.BlockSpec((seq, aw), lambda b: (b, 0)),
                  pl.BlockSpec((seq, kw), lambda b: (b, 0)),
                  pl.BlockSpec((seq, kw), lambda b: (b, 0)),
                  pl.BlockSpec((seq, LANES), const),
                  pl.BlockSpec((seq, LANES), const),
                  pl.BlockSpec((1, LANES), const),
                  pl.BlockSpec((1, LANES), const),
                  pl.BlockSpec((N_Q_HEADS, LANES), const)],
        out_specs=pl.BlockSpec((seq, aw), lambda b: (b, 0)),
        out_shape=jax.ShapeDtypeStruct((bsz * seq, aw), F32),
        scratch_shapes=[pltpu.VMEM((seq, aw), BF16),
                        pltpu.VMEM((4, seq, LANES), BF16),
                        pltpu.VMEM((4, seq, LANES), BF16)],
        compiler_params=_cparams(("arbitrary",)),
        name="window_attention",
    )(q, k, v, cos, sin, qg, kg, sinkb)


def _merge_kernel(x_ref, mod_ref, ya_ref, yb_ref, ga_ref, gb_ref, wpa_ref, wpb_ref, wo_ref, o_ref):
    pa = _dot(ya_ref[...].astype(BF16), wpa_ref[...])
    pb = _dot(yb_ref[...].astype(BF16), wpb_ref[...])
    merged = _sigmoid(ga_ref[...]) * pa + _sigmoid(gb_ref[...]) * pb
    out = _dot(merged.astype(BF16), wo_ref[...])
    o_ref[...] = x_ref[...] + mod_ref[0][2:3] * out


def _merge(x2, mod3, ya, yb, ga, gb, w_pa_bf, w_pb_bf, w_out_bf, seq, tm=512):
    t, d = x2.shape
    per = seq // tm
    row = lambda w: pl.BlockSpec((tm, w), lambda i: (i, 0))
    full = lambda a: pl.BlockSpec(a.shape, lambda i: (0, 0))
    return pl.pallas_call(
        _merge_kernel,
        grid=(t // tm,),
        in_specs=[row(d), pl.BlockSpec((1, 6, d), lambda i: (i // per, 0, 0)),
                  row(ya.shape[1]), row(yb.shape[1]), row(d), row(d),
                  full(w_pa_bf), full(w_pb_bf), full(w_out_bf)],
        out_specs=row(d),
        out_shape=jax.ShapeDtypeStruct((t, d), F32),
        compiler_params=_cparams(("arbitrary",)),
        name="merge_outproj",
    )(x2, mod3, ya, yb, ga, gb, w_pa_bf, w_pb_bf, w_out_bf)


def _oddeven_merge(lo, hi, r):
    step = r * 2
    if step < hi - lo:
        yield from _oddeven_merge(lo, hi, step)
        yield from _oddeven_merge(lo + r, hi, step)
        yield from ((i, i + r) for i in range(lo + r, hi - r, step))
    else:
        yield (lo, lo + r)


def _oddeven_merge_sort(lo, hi):
    if hi - lo >= 1:
        mid = lo + (hi - lo) // 2
        yield from _oddeven_merge_sort(lo, mid)
        yield from _oddeven_merge_sort(mid + 1, hi)
        yield from _oddeven_merge(lo, hi, 1)


def _topk_rows(s, k, val_ref, idx_ref):
    n, tm = s.shape
    ng = n // SUBLANES
    assert ng & (ng - 1) == 0 and k <= ng
    sub = lax.broadcasted_iota(jnp.int32, (SUBLANES, tm), 0)
    vals = [s[g * SUBLANES:(g + 1) * SUBLANES, :] for g in range(ng)]
    rows = [sub + g * SUBLANES for g in range(ng)]
    for i, j in _oddeven_merge_sort(0, ng - 1):
        keep = (vals[i] > vals[j]) | ((vals[i] == vals[j]) & (rows[i] < rows[j]))
        vals[i], vals[j] = jnp.where(keep, vals[i], vals[j]), jnp.where(keep, vals[j], vals[i])
        rows[i], rows[j] = jnp.where(keep, rows[i], rows[j]), jnp.where(keep, rows[j], rows[i])
    for t in range(k):
        m = jnp.max(vals[0], axis=0, keepdims=True)
        first = jnp.min(jnp.where(vals[0] == m, rows[0], n), axis=0, keepdims=True)
        val_ref[t:t + 1, :] = m
        idx_ref[t:t + 1, :] = first
        if t + 1 < k:
            popped = rows[0] == first
            for g in range(k - 1 - t):
                vals[g] = jnp.where(popped, vals[g + 1], vals[g])
                rows[g] = jnp.where(popped, rows[g + 1], rows[g])


def _peer_score_kernel(x_ref, mod_ref, g_ref, wq_ref, k1_ref, k2_ref, h_ref, idx_ref, gate_ref,
                       v1_ref, i1_ref, v2_ref, i2_ref, cv_ref, ce_ref, it_ref, gt_ref, *, tm):
    m = mod_ref[0]
    h = _norm_mod(x_ref[...], g_ref[...], m[3:4], m[4:5])
    h_ref[...] = h
    q = _dot(h.astype(BF16), wq_ref[...]).astype(BF16)
    half = PEER_QDIM // 2
    neg = jnp.full((1, tm), -jnp.inf, F32)
    for hd in range(PEER_HEADS):
        q1 = q[:, hd * PEER_QDIM: hd * PEER_QDIM + half]
        q2 = q[:, hd * PEER_QDIM + half: (hd + 1) * PEER_QDIM]
        _topk_rows(_dot_nt(k1_ref[...], q1), PEER_TOPK, v1_ref, i1_ref)
        _topk_rows(_dot_nt(k2_ref[...], q2), PEER_TOPK, v2_ref, i2_ref)
        for r, (i, j) in enumerate(_STAIR):
            cv_ref[r:r + 1, :] = v1_ref[i:i + 1, :] + v2_ref[j:j + 1, :]
            ce_ref[r:r + 1, :] = i1_ref[i:i + 1, :] * N_KEYS + i2_ref[j:j + 1, :]
        for r in range(len(_STAIR), _STAIR_ROWS):
            cv_ref[r:r + 1, :] = neg
            ce_ref[r:r + 1, :] = jnp.zeros((1, tm), jnp.int32)
        cand = cv_ref[...]
        ce = ce_ref[...]
        row = lax.broadcasted_iota(jnp.int32, cand.shape, 0)
        vals = []
        for kk in range(PEER_TOPK):
            mx = jnp.max(cand, axis=0, keepdims=True)
            first = jnp.min(jnp.where(cand == mx, row, _STAIR_ROWS), axis=0, keepdims=True)
            sel = row == first
            it_ref[hd * PEER_TOPK + kk: hd * PEER_TOPK + kk + 1, :] = jnp.sum(jnp.where(sel, ce, 0), axis=0, keepdims=True)
            vals.append(mx)
            cand = jnp.where(sel, -jnp.inf, cand)
        ex = [jnp.exp(v - vals[0]) for v in vals]
        tot = ex[0]
        for e in ex[1:]:
            tot = tot + e
        for kk in range(PEER_TOPK):
            gt_ref[hd * PEER_TOPK + kk: hd * PEER_TOPK + kk + 1, :] = ex[kk] / tot
    idx_ref[...] = it_ref[...].T
    gate_ref[...] = gt_ref[...].T


def _peer_score(x2, mod3, gain, wq_bf, k1_bf, k2_bf, seq, tm=256):
    t, d = x2.shape
    per = seq // tm
    slots = PEER_HEADS * PEER_TOPK
    full = lambda a: pl.BlockSpec(a.shape, lambda i: (0, 0))
    return pl.pallas_call(
        functools.partial(_peer_score_kernel, tm=tm),
        grid=(t // tm,),
        in_specs=[pl.BlockSpec((tm, d), lambda i: (i, 0)),
                  pl.BlockSpec((1, 6, d), lambda i: (i // per, 0, 0)),
                  pl.BlockSpec((1, d), lambda i: (0, 0)),
                  full(wq_bf), full(k1_bf), full(k2_bf)],
        out_specs=[pl.BlockSpec((tm, d), lambda i: (i, 0)),
                   pl.BlockSpec((tm, slots), lambda i: (i, 0)),
                   pl.BlockSpec((tm, slots), lambda i: (i, 0))],
        out_shape=[jax.ShapeDtypeStruct((t, d), F32),
                   jax.ShapeDtypeStruct((t, slots), jnp.int32),
                   jax.ShapeDtypeStruct((t, slots), F32)],
        scratch_shapes=[pltpu.VMEM((PEER_TOPK, tm), F32), pltpu.VMEM((PEER_TOPK, tm), jnp.int32),
                        pltpu.VMEM((PEER_TOPK, tm), F32), pltpu.VMEM((PEER_TOPK, tm), jnp.int32),
                        pltpu.VMEM((_STAIR_ROWS, tm), F32), pltpu.VMEM((_STAIR_ROWS, tm), jnp.int32),
                        pltpu.VMEM((slots, tm), jnp.int32), pltpu.VMEM((slots, tm), F32)],
        compiler_params=_cparams(("arbitrary",)),
        name="peer_score_topk",
    )(x2, mod3, gain.reshape(1, d), wq_bf, k1_bf, k2_bf)


def _peer_gather_kernel(idx_ref, idxn_ref, tab_ref, h_ref, gate_ref, x_ref, mod_ref, after_ref, *rest,
                        tc, d, nsteps):
    o_ref, buf_ref, sem_ref = rest[-3:]
    slots = PEER_HEADS * PEER_TOPK
    step = pl.program_id(0)

    def issue_token(chunk, t):
        ids_ref, c = (idx_ref, chunk) if chunk < GATHER_RING else (idxn_ref, chunk - GATHER_RING)
        for j in range(slots):
            pltpu.make_async_copy(tab_ref.at[ids_ref[c * tc + t, j]],
                                  buf_ref.at[c, t, pl.ds(j, 1), :],
                                  sem_ref.at[c]).start(priority=j % 2)

    def wait(slot):
        pltpu.make_async_copy(buf_ref.at[slot], buf_ref.at[slot], sem_ref.at[slot]).wait()

    eye = (lax.broadcasted_iota(jnp.int32, (slots, slots), 0) ==
           lax.broadcasted_iota(jnp.int32, (slots, slots), 1))
    gt2 = mod_ref[0][5:6]

    def mix_token(slot, t):
        r = slot * tc + t
        word = buf_ref[slot, t]
        u = lax.bitcast_convert_type(word << 16, F32)
        prod = u * h_ref[r:r + 1, :]
        part = prod[:, 0:LANES]
        for c in range(1, d // LANES):
            part = part + prod[:, c * LANES:(c + 1) * LANES]
        a = jnp.sum(part, axis=1, keepdims=True)
        grow = jnp.broadcast_to(gate_ref[r:r + 1, :], (slots, slots))
        gcol = jnp.sum(jnp.where(eye, grow, 0.0), axis=1, keepdims=True)
        w = gcol * (0.5 * a * (1.0 + lax.erf(a * (2.0 ** -0.5))))
        v = lax.bitcast_convert_type(buf_ref[slot, t] & jnp.uint32(0xFFFF0000), F32)
        y = jnp.sum(v * w, axis=0, keepdims=True)
        o_ref[r:r + 1, :] = x_ref[r:r + 1, :] + gt2 * y

    @pl.when(step == 0)
    def _():
        for c in range(GATHER_AHEAD):
            for t in range(tc):
                issue_token(c, t)

    for p in range(GATHER_RING):
        wait(p)
        for t in range(tc):
            issue_token(p + GATHER_AHEAD, t)
            mix_token(p, t)

    @pl.when(step == nsteps - 1)
    def _():
        for c in range(GATHER_AHEAD):
            wait(c)


def _peer_gather(x2, mod3, h2, idx, gate, table, seq, tok0, ntok, after, dst=None, tc=8):
    t, d = x2.shape
    slots = idx.shape[1]
    tb = GATHER_RING * tc
    nsteps = ntok // tb
    off = tok0 // tb
    per = seq // tb
    row = lambda w: pl.BlockSpec((tb, w), lambda i: (i + off, 0))
    operands = [idx, idx, table.reshape(table.shape[0], 1, d), h2, gate, x2, mod3, after]
    in_specs = [pl.BlockSpec((tb, slots), lambda i: (i + off, 0), memory_space=pltpu.SMEM),
                pl.BlockSpec((tb, slots), lambda i: (jnp.minimum(i + 1, nsteps - 1) + off, 0),
                             memory_space=pltpu.SMEM),
                pl.BlockSpec(memory_space=pl.ANY),
                row(d), row(slots), row(d),
                pl.BlockSpec((1, 6, d), lambda i: ((i + off) // per, 0, 0)),
                pl.BlockSpec(memory_space=pl.ANY)]
    aliases = {}
    if dst is not None:
        aliases = {len(operands): 0}
        operands.append(dst)
        in_specs.append(pl.BlockSpec(memory_space=pl.ANY))
    return pl.pallas_call(
        functools.partial(_peer_gather_kernel, tc=tc, d=d, nsteps=nsteps),
        grid=(nsteps,),
        in_specs=in_specs,
        out_specs=row(d),
        out_shape=jax.ShapeDtypeStruct((t, d), F32),
        input_output_aliases=aliases,
        scratch_shapes=[pltpu.VMEM((GATHER_RING, tc, slots, d), jnp.uint32),
                        pltpu.SemaphoreType.DMA((GATHER_RING,))],
        compiler_params=_cparams(("arbitrary",)),
        name="peer_gather_mix",
    )(*operands)


SC_LANES = 16
SC_CORES = 2
SC_SUBCORES = 16
SC_GROUP = 32
SC_HALF = 64
SC_SEQS = 12
TC_FIRST_SEQS = 7


def _pack_kernel(u_ref, v_ref, uv_ref, uw_ref, vw_ref):
    half = u_ref.shape[2] // 2

    def hi_bits(x):
        return lax.bitcast_convert_type(x.astype(BF16).astype(F32), jnp.uint32)

    ub = hi_bits(u_ref[0])
    vb = hi_bits(v_ref[0])
    uv_ref[...] = vb | (ub >> 16)
    uw_ref[...] = ub[:, half:] | (ub[:, :half] >> 16)
    vw_ref[...] = vb[:, half:] | (vb[:, :half] >> 16)


def _pack_tables(u_tabs, v_tabs, layer, tm=1024):
    _, e, d = u_tabs.shape
    blk = lambda w: pl.BlockSpec((tm, w), lambda i: (i, 0))
    src = pl.BlockSpec((1, tm, d), lambda i: (layer, i, 0))
    return pl.pallas_call(
        _pack_kernel,
        grid=(e // tm,),
        in_specs=[src, src],
        out_specs=[blk(d), blk(d // 2), blk(d // 2)],
        out_shape=[jax.ShapeDtypeStruct((e, d), jnp.uint32),
                   jax.ShapeDtypeStruct((e, d // 2), jnp.uint32),
                   jax.ShapeDtypeStruct((e, d // 2), jnp.uint32)],
        compiler_params=_cparams(("arbitrary",)),
        name="peer_pack_tables",
    )(u_tabs, v_tabs)


def _sc_worker_base(n_per):
    return (lax.axis_index("s") * SC_CORES + lax.axis_index("c")) * n_per


def _sc_halves(tab_hbm, idx_v, rows_v, sems, tt, half):
    ids = idx_v.at[tt, pl.ds(half * SC_HALF, SC_HALF)]
    return pltpu.make_async_copy(tab_hbm.at[ids], rows_v.at[half, :, pl.ds(0, tab_hbm.shape[1])], sems.at[half])


def _sc_token_loop(tab_hbm, idx_v, rows_v, sems, compute_half):
    _sc_halves(tab_hbm, idx_v, rows_v, sems, 0, 0).start()

    def tok(tt, carry):
        _sc_halves(tab_hbm, idx_v, rows_v, sems, tt, 1).start()
        _sc_halves(tab_hbm, idx_v, rows_v, sems, tt, 0).wait()
        compute_half(tt, 0)

        @pl.when(tt + 1 < SC_GROUP)
        def _():
            _sc_halves(tab_hbm, idx_v, rows_v, sems, tt + 1, 0).start()

        _sc_halves(tab_hbm, idx_v, rows_v, sems, tt, 1).wait()
        compute_half(tt, 1)
        return carry

    lax.fori_loop(0, SC_GROUP, tok, 0)


def _unpack_pair(word):
    lo = lax.bitcast_convert_type(word << 16, F32)
    hi = lax.bitcast_convert_type(word & jnp.uint32(0xFFFF0000), F32)
    return lo, hi


def _sc_dot_body(tab_hbm, idx_hbm, h_hbm, a_hbm, idx_v, h_v, rows_v, a_v, tr_v, sems, *, tok0, n_per, d):
    base = _sc_worker_base(n_per)
    lane = lax.iota(jnp.int32, SC_LANES)
    nk = d // (2 * SC_LANES)

    def compute_half(tt, half):
        def rows16(gi, carry):
            accs = [jnp.zeros((SC_LANES,), F32) for _ in range(SC_LANES)]
            for k in range(nk):
                hlo = h_v[tt, pl.ds(SC_LANES * k, SC_LANES)]
                hhi = h_v[tt, pl.ds(d // 2 + SC_LANES * k, SC_LANES)]
                for q in range(SC_LANES):
                    lo, hi = _unpack_pair(rows_v[half, gi * SC_LANES + q, pl.ds(SC_LANES * k, SC_LANES)])
                    accs[q] = accs[q] + lo * hlo + hi * hhi
            for q in range(SC_LANES):
                tr_v[q, :] = accs[q]
            tot = plsc.load_gather(tr_v, [lane, jnp.zeros((SC_LANES,), jnp.int32)])
            for c in range(1, SC_LANES):
                tot = tot + plsc.load_gather(tr_v, [lane, jnp.full((SC_LANES,), c, jnp.int32)])
            a_v[tt, pl.ds(half * SC_HALF + gi * SC_LANES, SC_LANES)] = tot
            return carry
        lax.fori_loop(0, SC_HALF // SC_LANES, rows16, 0)

    def group(g, carry):
        loc = base + g * SC_GROUP
        pltpu.sync_copy(idx_hbm.at[pl.ds(tok0 + loc, SC_GROUP)], idx_v)
        pltpu.sync_copy(h_hbm.at[pl.ds(tok0 + loc, SC_GROUP)], h_v)
        _sc_token_loop(tab_hbm, idx_v, rows_v, sems, compute_half)
        pltpu.sync_copy(a_v, a_hbm.at[pl.ds(loc, SC_GROUP)])
        return carry

    lax.fori_loop(0, n_per // SC_GROUP, group, 0)


def _sc_mix_body(tab_hbm, idx_hbm, w_hbm, y_hbm, idx_v, w_v, rows_v, y_v, sems, *, tok0, n_per, d):
    base = _sc_worker_base(n_per)
    nk = d // (2 * SC_LANES)
    nq = 2
    kq = nk // nq

    def compute_half(tt, half):
        ttv = jnp.full((SC_LANES,), tt, jnp.int32)
        for piece in range(nq):
            cols = [(m % 2) * (d // 2) + SC_LANES * (piece * kq + m // 2) for m in range(2 * kq)]
            if half == 0:
                accs = tuple(jnp.zeros((SC_LANES,), F32) for _ in range(2 * kq))
            else:
                accs = tuple(y_v[tt, pl.ds(c, SC_LANES)] for c in cols)

            def row(r, accs):
                wsplat = plsc.load_gather(w_v, [ttv, jnp.full((SC_LANES,), half * SC_HALF + r, jnp.int32)])
                out = []
                for k in range(kq):
                    lo, hi = _unpack_pair(rows_v[half, r, pl.ds(SC_LANES * (piece * kq + k), SC_LANES)])
                    out.append(accs[2 * k] + wsplat * lo)
                    out.append(accs[2 * k + 1] + wsplat * hi)
                return tuple(out)

            accs = lax.fori_loop(0, SC_HALF, row, accs)
            for m in range(2 * kq):
                y_v[tt, pl.ds(cols[m], SC_LANES)] = accs[m]

    def group(g, carry):
        loc = base + g * SC_GROUP
        pltpu.sync_copy(idx_hbm.at[pl.ds(tok0 + loc, SC_GROUP)], idx_v)
        pltpu.sync_copy(w_hbm.at[pl.ds(loc, SC_GROUP)], w_v)
        _sc_token_loop(tab_hbm, idx_v, rows_v, sems, compute_half)
        pltpu.sync_copy(y_v, y_hbm.at[pl.ds(loc, SC_GROUP)])
        return carry

    lax.fori_loop(0, n_per // SC_GROUP, group, 0)


def _sc_mesh():
    return plsc.VectorSubcoreMesh(core_axis_name="c", subcore_axis_name="s")


def _sc_dot(u_words, idx, h2, tok0, ntok):
    d = h2.shape[1]
    slots = idx.shape[1]
    n_per = ntok // (SC_CORES * SC_SUBCORES)
    return pl.kernel(
        functools.partial(_sc_dot_body, tok0=tok0, n_per=n_per, d=d),
        out_type=jax.ShapeDtypeStruct((ntok, slots), F32),
        mesh=_sc_mesh(),
        scratch_types=[pltpu.VMEM((SC_GROUP, slots), jnp.int32),
                       pltpu.VMEM((SC_GROUP, d), F32),
                       pltpu.VMEM((2, SC_HALF, d // 2 + SC_LANES), jnp.uint32),
                       pltpu.VMEM((SC_GROUP, slots), F32),
                       pltpu.VMEM((SC_LANES, SC_LANES), F32),
                       pltpu.SemaphoreType.DMA((2,))],
        compiler_params=pltpu.CompilerParams(needs_layout_passes=False),
        name="peer_sc_dot",
    )(u_words, idx, h2)


def _sc_mix(v_words, idx, w, tok0, ntok, d):
    slots = idx.shape[1]
    n_per = ntok // (SC_CORES * SC_SUBCORES)
    return pl.kernel(
        functools.partial(_sc_mix_body, tok0=tok0, n_per=n_per, d=d),
        out_type=jax.ShapeDtypeStruct((ntok, d), F32),
        mesh=_sc_mesh(),
        scratch_types=[pltpu.VMEM((SC_GROUP, slots), jnp.int32),
                       pltpu.VMEM((SC_GROUP, slots), F32),
                       pltpu.VMEM((2, SC_HALF, d // 2), jnp.uint32),
                       pltpu.VMEM((SC_GROUP, d), F32),
                       pltpu.SemaphoreType.DMA((2,))],
        compiler_params=pltpu.CompilerParams(needs_layout_passes=False),
        name="peer_sc_mix",
    )(v_words, idx, w)


def _gelu_gate_kernel(a_ref, g_ref, after_ref, o_ref):
    a = a_ref[...]
    o_ref[...] = g_ref[...] * (0.5 * a * (1.0 + lax.erf(a * (2.0 ** -0.5))))


def _gelu_gate(a, gate, tok0, after, tm=2048):
    n, slots = a.shape
    off = tok0 // tm
    return pl.pallas_call(
        _gelu_gate_kernel,
        grid=(n // tm,),
        in_specs=[pl.BlockSpec((tm, slots), lambda i: (i, 0)),
                  pl.BlockSpec((tm, slots), lambda i: (i + off, 0)),
                  pl.BlockSpec(memory_space=pl.ANY)],
        out_specs=pl.BlockSpec((tm, slots), lambda i: (i, 0)),
        out_shape=jax.ShapeDtypeStruct((n, slots), F32),
        compiler_params=_cparams(("arbitrary",)),
        name="peer_gelu_gate",
    )(a, gate, after)


def _residual_kernel(x_ref, mod_ref, y_ref, dst_ref, o_ref):
    o_ref[...] = x_ref[...] + mod_ref[0][5:6] * y_ref[...]


def _residual(x2, mod3, y, tok0, seq, dst, tm=512):
    n, d = y.shape
    off = tok0 // tm
    per = seq // tm
    return pl.pallas_call(
        _residual_kernel,
        grid=(n // tm,),
        in_specs=[pl.BlockSpec((tm, d), lambda i: (i + off, 0)),
                  pl.BlockSpec((1, 6, d), lambda i: ((i + off) // per, 0, 0)),
                  pl.BlockSpec((tm, d), lambda i: (i, 0)),
                  pl.BlockSpec(memory_space=pl.ANY)],
        out_specs=pl.BlockSpec((tm, d), lambda i: (i + off, 0)),
        out_shape=jax.ShapeDtypeStruct(x2.shape, F32),
        input_output_aliases={3: 0},
        compiler_params=_cparams(("arbitrary",)),
        name="peer_residual",
    )(x2, mod3, y, dst)


def kernel(x_prompt, x_sample, c_prompt, c_sample, w_mod, b_mod, g_norm1, g_norm2, w_in, conv_w, conv_b, f_w1, f_b1, f_freq, f_w2, f_b2, f_w3, f_bias, q_gain, k_gain, sink, w_pa, w_pb, w_out, peer_wq, peer_k1, peer_k2, peer_u, peer_v):
    depth = w_mod.shape[0]
    bp, seq, d = x_prompt.shape
    bs = x_sample.shape[0]
    assert x_sample.shape[1] == seq
    bsz = bp + bs
    x = jnp.concatenate([x_prompt, x_sample], axis=0).reshape(bsz * seq, d)
    c = jnp.concatenate([c_prompt, c_sample], axis=0)

    hyw = w_pa.shape[1]
    aw = w_pb.shape[1]
    kw = N_KV_HEADS * HEAD_DIM
    widths = (HY_ORDER + 1) * hyw, aw, kw, kw, d, d
    cblk = hyw // 256

    mod = _modulation(c, w_mod.astype(BF16), b_mod)
    fc, fs = _dft_tables(seq)

    for l in range(depth):
        mod3 = mod[l].reshape(bsz, 6, d)
        hy, q, k, v, ga, gb = _inproj(x, mod3, g_norm1[l], w_in[l].astype(BF16), seq, widths)
        hcat = _filters_time(seq, f_w1[l], f_b1[l], f_freq[l], f_w2[l], f_b2[l], f_w3[l])
        spec_a, spec_b, nyq = _filter_spectra(fc, fs, hcat)
        cw, cb = conv_w[l], conv_b[l][None, :]
        zz = _long_conv(fc, fs, hy, 0, hy, cblk, spec_a, spec_b, nyq, 0, f_bias[l][0:1], cw, cb,
                        0, cblk, bsz, seq, conv_u=True)
        ya = _long_conv(fc, fs, zz, 0, hy, 2 * cblk, spec_a, spec_b, nyq, cblk, f_bias[l][1:2], cw, cb,
                        0, 2 * cblk, bsz, seq, conv_u=False)
        yb = _attention(q, k, v, q_gain[l], k_gain[l], sink[l], bsz, seq)
        x = _merge(x, mod3, ya, yb, ga, gb, w_pa[l].astype(BF16), w_pb[l].astype(BF16), w_out[l].astype(BF16), seq)
        h2, idx, gate = _peer_score(x, mod3, g_norm2[l], peer_wq[l].astype(BF16),
                                    peer_k1[l].astype(BF16), peer_k2[l].astype(BF16), seq)
        table, u_words, v_words = _pack_tables(peer_u, peer_v, l)
        t_tc = (bsz - SC_SEQS) * seq
        t_sc = SC_SEQS * seq
        t_g1 = TC_FIRST_SEQS * seq
        a_sc = _sc_dot(u_words, idx, h2, t_tc, t_sc)
        x_g1 = _peer_gather(x, mod3, h2, idx, gate, table, seq, 0, t_g1, gate)
        w_sc = _gelu_gate(a_sc, gate, t_tc, x_g1)
        y_sc = _sc_mix(v_words, idx, w_sc, t_tc, t_sc, d)
        x_g2 = _peer_gather(x, mod3, h2, idx, gate, table, seq, t_g1, t_tc - t_g1, w_sc, dst=x_g1)
        x = _residual(x, mod3, y_sc, t_tc, seq, x_g2)

    x = x.reshape(bsz, seq, d)
    return x[:bp], x[bp:]
```

```python
import functools
import math

import jax
import jax.numpy as jnp
import numpy as np
from jax import lax
from jax.experimental import pallas as pl
from jax.experimental.pallas import tpu as pltpu
from jax.experimental.pallas import tpu_sc as plsc

F32 = jnp.float32
BF16 = jnp.bfloat16

EPS = 1e-6
HEAD_DIM = 64
N_Q_HEADS = 8
N_KV_HEADS = 2
WINDOW = 128
BLOCK = 128
ROPE_THETA = 10000.0
HY_ORDER = 2
N_DIR = 2
FILT_BANDS = 16
DECAY_TARGET = 1e-2
FAST_DECAY_PCT = 0.3
SLOW_DECAY_PCT = 1.5
PEER_HEADS = 8
N_KEYS = 128
PEER_TOPK = 16
PEER_QDIM = 256
LANES = 128
SUBLANES = 8
VMEM_LIMIT = 56 * 1024 * 1024
GATHER_RING = 4
GATHER_AHEAD = 2

_STAIR = [(i, j) for i in range(PEER_TOPK) for j in range(PEER_TOPK) if (i + 1) * (j + 1) <= PEER_TOPK]
_STAIR_ROWS = -(-len(_STAIR) // SUBLANES) * SUBLANES


def _cparams(sem, vmem=VMEM_LIMIT):
    return pltpu.CompilerParams(dimension_semantics=sem, vmem_limit_bytes=vmem)


def _dot(a, b):
    return jnp.dot(a, b, preferred_element_type=F32)


def _dot_nt(a, b):
    return lax.dot_general(a, b, (((1,), (1,)), ((), ())), preferred_element_type=F32)


def _dot_hi(a, b):
    return jnp.dot(a, b, preferred_element_type=F32, precision=lax.Precision.HIGHEST)


def _sigmoid(x):
    return 1.0 / (1.0 + jnp.exp(-x))


def _mod_kernel(c_ref, w_ref, b_ref, o_ref):
    c = c_ref[...]
    s = c * _sigmoid(c)
    o_ref[0] = _dot(s.astype(BF16), w_ref[0]) + b_ref[0]


def _modulation(c, w_mod_bf, b_mod):
    depth, d, n6 = w_mod_bf.shape
    bsz = c.shape[0]
    tn = 1536
    return pl.pallas_call(
        _mod_kernel,
        grid=(depth, n6 // tn),
        in_specs=[pl.BlockSpec((bsz, d), lambda l, j: (0, 0)),
                  pl.BlockSpec((1, d, tn), lambda l, j: (l, 0, j)),
                  pl.BlockSpec((1, 1, tn), lambda l, j: (l, 0, j))],
        out_specs=pl.BlockSpec((1, bsz, tn), lambda l, j: (l, 0, j)),
        out_shape=jax.ShapeDtypeStruct((depth, bsz, n6), F32),
        compiler_params=_cparams(("arbitrary", "arbitrary")),
        name="adaln_mod",
    )(c, w_mod_bf, b_mod.reshape(depth, 1, n6))


def _norm_mod(x, gain, shift, scale):
    y = x * lax.rsqrt(jnp.mean(x * x, axis=-1, keepdims=True) + EPS)
    return (y * gain) * (1.0 + scale) + shift


def _inproj_kernel(x_ref, mod_ref, g_ref, w_ref, hy_ref, q_ref, k_ref, v_ref, ga_ref, gb_ref, *, splits):
    m = mod_ref[0]
    h = _norm_mod(x_ref[...], g_ref[...], m[0:1], m[1:2])
    z = _dot(h.astype(BF16), w_ref[...])
    outs = (hy_ref, q_ref, k_ref, v_ref, ga_ref, gb_ref)
    lo = 0
    for ref, hi in zip(outs, splits):
        ref[...] = z[:, lo:hi]
        lo = hi


def _inproj(x2, mod3, gain, w_in_bf, seq, widths, tm=256):
    t, d = x2.shape
    ncols = w_in_bf.shape[1]
    per = seq // tm
    splits = tuple(int(s) for s in np.cumsum(widths))
    return pl.pallas_call(
        functools.partial(_inproj_kernel, splits=splits),
        grid=(t // tm,),
        in_specs=[pl.BlockSpec((tm, d), lambda i: (i, 0)),
                  pl.BlockSpec((1, 6, d), lambda i: (i // per, 0, 0)),
                  pl.BlockSpec((1, d), lambda i: (0, 0)),
                  pl.BlockSpec((d, ncols), lambda i: (0, 0))],
        out_specs=[pl.BlockSpec((tm, w), lambda i: (i, 0)) for w in widths],
        out_shape=[jax.ShapeDtypeStruct((t, w), F32) for w in widths],
        compiler_params=_cparams(("arbitrary",)),
        name="inproj",
    )(x2, mod3, gain.reshape(1, d), w_in_bf)


def _filter_kernel(z_ref, w1_ref, b1_ref, fr_ref, w2_ref, b2_ref, w3_ref, ad_ref, o_ref):
    z = z_ref[0]
    fr = fr_ref[...]
    a = jnp.sin(fr * (_dot_hi(z, w1_ref[...]) + b1_ref[...]))
    a = jnp.sin(fr * (_dot_hi(a, w2_ref[...]) + b2_ref[...]))
    h = _dot_hi(a, w3_ref[0])
    h = h * jnp.exp(-z[:, 0:1] * ad_ref[...])
    row = lax.broadcasted_iota(jnp.int32, h.shape, 0)
    dead = (pl.program_id(0) == 1) & (pl.program_id(1) == 0) & (row == 0)
    o_ref[0] = jnp.where(dead, 0.0, h)


def _filters_time(seq, f_w1, f_b1, f_freq, f_w2, f_b2, f_w3, tm=512):
    hidden = f_w1.shape[1]
    cw = f_w3.shape[1] // (HY_ORDER * N_DIR)
    t = jnp.linspace(0.0, 1.0, seq, dtype=F32)[:, None]
    w = 2.0 * math.pi * jnp.arange(seq, dtype=F32)[:, None] / seq
    bands = jnp.linspace(1e-4, FILT_BANDS - 1, FILT_BANDS, dtype=F32)[None, :]
    z = jnp.concatenate([t, jnp.cos(bands * w), -jnp.sin(bands * w)], axis=-1)
    emb = z.shape[1]
    z = jnp.pad(z, ((0, 0), (0, LANES - emb)))
    zcat = jnp.stack([z, jnp.concatenate([z[:1], z[:0:-1]], axis=0)], axis=0)
    w1p = jnp.pad(f_w1, ((0, LANES - emb), (0, 0)))
    w3d = f_w3.reshape(hidden, HY_ORDER, N_DIR, cw).transpose(2, 0, 1, 3).reshape(N_DIR, hidden, HY_ORDER * cw)
    max_decay = math.log(DECAY_TARGET) / FAST_DECAY_PCT
    min_decay = math.log(DECAY_TARGET) / SLOW_DECAY_PCT
    ad = jnp.abs(jnp.linspace(min_decay, max_decay, cw, dtype=F32))
    ad = jnp.tile(ad, HY_ORDER)[None, :]
    oc = HY_ORDER * cw
    return pl.pallas_call(
        _filter_kernel,
        grid=(N_DIR, seq // tm),
        in_specs=[pl.BlockSpec((1, tm, LANES), lambda g, r: (g, r, 0)),
                  pl.BlockSpec((LANES, hidden), lambda g, r: (0, 0)),
                  pl.BlockSpec((1, hidden), lambda g, r: (0, 0)),
                  pl.BlockSpec((1, hidden), lambda g, r: (0, 0)),
                  pl.BlockSpec((hidden, hidden), lambda g, r: (0, 0)),
                  pl.BlockSpec((1, hidden), lambda g, r: (0, 0)),
                  pl.BlockSpec((1, hidden, oc), lambda g, r: (g, 0, 0)),
                  pl.BlockSpec((1, oc), lambda g, r: (0, 0))],
        out_specs=pl.BlockSpec((1, tm, oc), lambda g, r: (g, r, 0)),
        out_shape=jax.ShapeDtypeStruct((N_DIR, seq, oc), F32),
        compiler_params=_cparams(("arbitrary", "arbitrary")),
        name="hyena_filter_mlp",
    )(zcat, w1p, f_b1[None, :], f_freq[None, :], f_w2, f_b2[None, :], w3d, ad)


def _dft_tables(seq):
    n2 = 2 * seq
    f = jnp.arange(seq, dtype=jnp.int32)
    ft = (f[:, None] * f[None, :]) % n2
    ang = ft.astype(F32) * (2.0 * math.pi / n2)
    return jnp.cos(ang).astype(BF16), jnp.sin(ang).astype(BF16)


def _spec_kernel(fc_ref, fs_ref, h_ref, a_ref, b_ref, nyq_ref, *, seq, tf):
    hlo = h_ref[0]
    hhi = h_ref[1]
    hlo_b = hlo.astype(BF16)
    hhi_b = hhi.astype(BF16)
    f = pl.program_id(1) * tf + lax.broadcasted_iota(jnp.int32, (tf, 1), 0)
    sgn = jnp.where(f % 2 == 0, 1.0, -1.0)
    fc = fc_ref[...]
    fs = fs_ref[...]
    hr = _dot(fc, hlo_b) + sgn * _dot(fc, hhi_b)
    hs = _dot(fs, hlo_b) + sgn * _dot(fs, hhi_b)
    w = jnp.where(f == 0, 1.0, 2.0) * (1.0 / (2 * seq))
    a_ref[...] = w * hr
    b_ref[...] = -(w * hs)
    t = lax.broadcasted_iota(jnp.int32, (seq, 1), 0)
    alt = jnp.where(t % 2 == 0, 1.0, -1.0)
    nyq = jnp.sum(alt * (hlo + hhi), axis=0, keepdims=True) * (1.0 / (2 * seq))
    nyq_ref[...] = jnp.broadcast_to(nyq, nyq_ref.shape)


def _filter_spectra(fc, fs, hcat, tf=512, tcol=512):
    seq = fc.shape[0]
    oc = hcat.shape[2]
    return pl.pallas_call(
        functools.partial(_spec_kernel, seq=seq, tf=tf),
        grid=(oc // tcol, seq // tf),
        in_specs=[pl.BlockSpec((tf, seq), lambda j, i: (i, 0)),
                  pl.BlockSpec((tf, seq), lambda j, i: (i, 0)),
                  pl.BlockSpec((2, seq, tcol), lambda j, i: (0, 0, j))],
        out_specs=[pl.BlockSpec((tf, tcol), lambda j, i: (i, j)),
                   pl.BlockSpec((tf, tcol), lambda j, i: (i, j)),
                   pl.BlockSpec((SUBLANES, tcol), lambda j, i: (0, j))],
        out_shape=[jax.ShapeDtypeStruct((seq, oc), F32),
                   jax.ShapeDtypeStruct((seq, oc), F32),
                   jax.ShapeDtypeStruct((SUBLANES, oc), F32)],
        compiler_params=_cparams(("arbitrary", "arbitrary")),
        name="hyena_filter_spectra",
    )(fc, fs, hcat)


def _shortconv(x, w_ref, b_ref, seq):
    row = lax.broadcasted_iota(jnp.int32, (seq, 1), 0)
    xm = jnp.where(row == 0, 0.0, pltpu.roll(x, 1, 0))
    xp = jnp.where(row == seq - 1, 0.0, pltpu.roll(x, seq - 1, 0))
    return xm * w_ref[0:1, :] + x * w_ref[1:2, :] + xp * w_ref[2:3, :] + b_ref[...]


def _conv_kernel(fc_ref, fs_ref, u_ref, g_ref, a_ref, b_ref, nyq_ref, bias_ref,
                 cwu_ref, cbu_ref, cwg_ref, cbg_ref, o_ref, acc_ref, *, seq, fb, conv_u):
    u = u_ref[...]
    if conv_u:
        u = _shortconv(u, cwu_ref, cbu_ref, seq)
    gate = _shortconv(g_ref[...], cwg_ref, cbg_ref, seq)
    ub = u.astype(BF16)
    for c in range(seq // fb):
        rows = slice(c * fb, (c + 1) * fb)
        ur = _dot(fc_ref[rows, :], ub)
        us = _dot(fs_ref[rows, :], ub)
        a = a_ref[rows, :]
        b = b_ref[rows, :]
        qr = (ur * a + us * b).astype(BF16)
        qi = (us * a - ur * b).astype(BF16)
        part = _dot(fc_ref[:, rows], qr) + _dot(fs_ref[:, rows], qi)
        if c == 0:
            acc_ref[...] = part
        else:
            acc_ref[...] += part
    t = lax.broadcasted_iota(jnp.int32, (seq, 1), 0)
    alt = jnp.where(t % 2 == 0, 1.0, -1.0)
    unyq = jnp.sum(alt * u, axis=0, keepdims=True)
    y = acc_ref[...] + alt * (unyq * nyq_ref[0:1, :]) + bias_ref[...] * u
    o_ref[...] = gate * y


def _long_conv(fc, fs, u_src, u_blk0, g_src, g_blk0, spec_a, spec_b, nyq, s_blk0, bias, cw, cb,
               cu_blk0, cg_blk0, bsz, seq, conv_u, tc=256, fb=256):
    nct = 512 // tc
    t = bsz * seq
    const = lambda j, b: (0, 0)
    return pl.pallas_call(
        functools.partial(_conv_kernel, seq=seq, fb=fb, conv_u=conv_u),
        grid=(nct, bsz),
        in_specs=[pl.BlockSpec(memory_space=pltpu.VMEM),
                  pl.BlockSpec(memory_space=pltpu.VMEM),
                  pl.BlockSpec((seq, tc), lambda j, b: (b, u_blk0 + j)),
                  pl.BlockSpec((seq, tc), lambda j, b: (b, g_blk0 + j)),
                  pl.BlockSpec((seq, tc), lambda j, b: (0, s_blk0 + j)),
                  pl.BlockSpec((seq, tc), lambda j, b: (0, s_blk0 + j)),
                  pl.BlockSpec((SUBLANES, tc), lambda j, b: (0, s_blk0 + j)),
                  pl.BlockSpec((1, tc), lambda j, b: (0, j)),
                  pl.BlockSpec((3, tc), lambda j, b: (0, cu_blk0 + j)),
                  pl.BlockSpec((1, tc), lambda j, b: (0, cu_blk0 + j)),
                  pl.BlockSpec((3, tc), lambda j, b: (0, cg_blk0 + j)),
                  pl.BlockSpec((1, tc), lambda j, b: (0, cg_blk0 + j))],
        out_specs=pl.BlockSpec((seq, tc), lambda j, b: (b, j)),
        out_shape=jax.ShapeDtypeStruct((t, 512), F32),
        scratch_shapes=[pltpu.VMEM((seq, tc), F32)],
        compiler_params=_cparams(("arbitrary", "arbitrary")),
        name="hyena_long_conv_u" if conv_u else "hyena_long_conv",
    )(fc, fs, u_src, g_src, spec_a, spec_b, nyq, bias, cw, cb, cw, cb)


def _attn_kernel(q_ref, k_ref, v_ref, cos_ref, sin_ref, qg_ref, kg_ref, sink_ref, o_ref,
                 qn_ref, km_ref, vm_ref, *, seq):
    lane = lax.broadcasted_iota(jnp.int32, (1, LANES), 1)
    r = lax.broadcasted_iota(jnp.int32, (LANES, LANES), 0) // HEAD_DIM
    c = lax.broadcasted_iota(jnp.int32, (LANES, LANES), 1) // HEAD_DIM
    bd = jnp.where(r == c, 1.0, 0.0).astype(BF16)
    first_half = (lane % HEAD_DIM) < (HEAD_DIM // 2)
    cos = cos_ref[...]
    sin = sin_ref[...]

    def norm_rope(x, gain):
        sq = x * x
        hi = sq.astype(BF16)
        lo = (sq - hi.astype(F32)).astype(BF16)
        ss = _dot(hi, bd) + _dot(lo, bd)
        y = (x * lax.rsqrt(ss * (1.0 / HEAD_DIM) + EPS)) * gain
        partner = jnp.where(first_half, pltpu.roll(y, LANES - HEAD_DIM // 2, 1), pltpu.roll(y, HEAD_DIM // 2, 1))
        return y * cos + partner * sin

    for p in range(N_Q_HEADS // 2):
        cols = slice(p * LANES, (p + 1) * LANES)
        qn_ref[:, cols] = norm_rope(q_ref[:, cols], qg_ref[...]).astype(BF16)
    kn = norm_rope(k_ref[...], kg_ref[...])
    left = lane < HEAD_DIM
    for src_ref, dst_ref in ((None, km_ref), (v_ref, vm_ref)):
        val = kn if src_ref is None else src_ref[...]
        rolled = pltpu.roll(val, HEAD_DIM, 1)
        dst_ref[0] = jnp.where(left, val, 0.0).astype(BF16)
        dst_ref[1] = jnp.where(left, 0.0, rolled).astype(BF16)
        dst_ref[2] = jnp.where(left, rolled, 0.0).astype(BF16)
        dst_ref[3] = jnp.where(left, 0.0, val).astype(BF16)

    span = 3 * BLOCK
    scale = HEAD_DIM ** -0.5
    ii = lax.broadcasted_iota(jnp.int32, (BLOCK, span), 0)
    jj = lax.broadcasted_iota(jnp.int32, (BLOCK, span), 1)

    def block(n, carry):
        q0 = pl.multiple_of(n * BLOCK, BLOCK)
        start = pl.multiple_of(jnp.clip((n - 1) * BLOCK, 0, seq - span), BLOCK)
        valid = jnp.abs((start - q0) + jj - ii) <= WINDOW
        for p in range(N_Q_HEADS // 2):
            cols = slice(p * LANES, (p + 1) * LANES)
            kv = (2 * p) // (N_Q_HEADS // N_KV_HEADS)
            qp = qn_ref[pl.ds(q0, BLOCK), cols]
            o = jnp.zeros((BLOCK, LANES), F32)
            for a in range(2):
                h = 2 * p + a
                kb = km_ref[2 * kv + a, pl.ds(start, span), :]
                s = _dot_nt(qp, kb) * scale
                s = jnp.where(valid, s, -jnp.inf)
                sk = sink_ref[h:h + 1, 0:1]
                m = jnp.maximum(jnp.max(s, axis=-1, keepdims=True), sk)
                e = jnp.exp(s - m)
                den = jnp.sum(e, axis=-1, keepdims=True) + jnp.exp(sk - m)
                pn = (e / den).astype(BF16)
                o = o + _dot(pn, vm_ref[2 * kv + a, pl.ds(start, span), :])
            o_ref[pl.ds(q0, BLOCK), cols] = o
        return carry

    lax.fori_loop(0, seq // BLOCK, block, 0)


def _attention(q, k, v, q_gain, k_gain, sink, bsz, seq):
    inv = ROPE_THETA ** (-jnp.arange(0, HEAD_DIM, 2, dtype=F32) / HEAD_DIM)
    ang = jnp.arange(seq, dtype=F32)[:, None] * inv[None, :]
    cos = jnp.tile(jnp.cos(ang), (1, LANES // (HEAD_DIM // 2)))
    sn = jnp.sin(ang)
    sin = jnp.tile(jnp.concatenate([-sn, sn], axis=1), (1, LANES // HEAD_DIM))
    qg = jnp.tile(q_gain, LANES // HEAD_DIM)[None, :]
    kg = jnp.tile(k_gain, LANES // HEAD_DIM)[None, :]
    sinkb = jnp.broadcast_to(sink[:, None], (N_Q_HEADS, LANES))
    aw = N_Q_HEADS * HEAD_DIM
    kw = N_KV_HEADS * HEAD_DIM
    const = lambda b: (0, 0)
    return pl.pallas_call(
        functools.partial(_attn_kernel, seq=seq),
        grid=(bsz,),
        in_specs=[pl.BlockSpec((seq, aw), lambda b: (b, 0)),
                  pl.BlockSpec((seq, kw), lambda b: (b, 0)),
                  pl.BlockSpec((seq, kw), lambda b: (b, 0)),
                  pl.BlockSpec((seq, LANES), const),
                  pl.BlockSpec((seq, LANES), const),
                  pl.BlockSpec((1, LANES), const),
                  pl.BlockSpec((1, LANES), const),
                  pl.BlockSpec((N_Q_HEADS, LANES), const)],
        out_specs=pl.BlockSpec((seq, aw), lambda b: (b, 0)),
        out_shape=jax.ShapeDtypeStruct((bsz * seq, aw), F32),
        scratch_shapes=[pltpu.VMEM((seq, aw), BF16),
                        pltpu.VMEM((4, seq, LANES), BF16),
                        pltpu.VMEM((4, seq, LANES), BF16)],
        compiler_params=_cparams(("arbitrary",)),
        name="window_attention",
    )(q, k, v, cos, sin, qg, kg, sinkb)


def _merge_kernel(x_ref, mod_ref, ya_ref, yb_ref, ga_ref, gb_ref, wpa_ref, wpb_ref, wo_ref, o_ref):
    pa = _dot(ya_ref[...].astype(BF16), wpa_ref[...])
    pb = _dot(yb_ref[...].astype(BF16), wpb_ref[...])
    merged = _sigmoid(ga_ref[...]) * pa + _sigmoid(gb_ref[...]) * pb
    out = _dot(merged.astype(BF16), wo_ref[...])
    o_ref[...] = x_ref[...] + mod_ref[0][2:3] * out


def _merge(x2, mod3, ya, yb, ga, gb, w_pa_bf, w_pb_bf, w_out_bf, seq, tm=512):
    t, d = x2.shape
    per = seq // tm
    row = lambda w: pl.BlockSpec((tm, w), lambda i: (i, 0))
    full = lambda a: pl.BlockSpec(a.shape, lambda i: (0, 0))
    return pl.pallas_call(
        _merge_kernel,
        grid=(t // tm,),
        in_specs=[row(d), pl.BlockSpec((1, 6, d), lambda i: (i // per, 0, 0)),
                  row(ya.shape[1]), row(yb.shape[1]), row(d), row(d),
                  full(w_pa_bf), full(w_pb_bf), full(w_out_bf)],
        out_specs=row(d),
        out_shape=jax.ShapeDtypeStruct((t, d), F32),
        compiler_params=_cparams(("arbitrary",)),
        name="merge_outproj",
    )(x2, mod3, ya, yb, ga, gb, w_pa_bf, w_pb_bf, w_out_bf)


def _oddeven_merge(lo, hi, r):
    step = r * 2
    if step < hi - lo:
        yield from _oddeven_merge(lo, hi, step)
        yield from _oddeven_merge(lo + r, hi, step)
        yield from ((i, i + r) for i in range(lo + r, hi - r, step))
    else:
        yield (lo, lo + r)


def _oddeven_merge_sort(lo, hi):
    if hi - lo >= 1:
        mid = lo + (hi - lo) // 2
        yield from _oddeven_merge_sort(lo, mid)
        yield from _oddeven_merge_sort(mid + 1, hi)
        yield from _oddeven_merge(lo, hi, 1)


def _topk_rows(s, k, val_ref, idx_ref):
    n, tm = s.shape
    ng = n // SUBLANES
    assert ng & (ng - 1) == 0 and k <= ng
    sub = lax.broadcasted_iota(jnp.int32, (SUBLANES, tm), 0)
    vals = [s[g * SUBLANES:(g + 1) * SUBLANES, :] for g in range(ng)]
    rows = [sub + g * SUBLANES for g in range(ng)]
    for i, j in _oddeven_merge_sort(0, ng - 1):
        keep = (vals[i] > vals[j]) | ((vals[i] == vals[j]) & (rows[i] < rows[j]))
        vals[i], vals[j] = jnp.where(keep, vals[i], vals[j]), jnp.where(keep, vals[j], vals[i])
        rows[i], rows[j] = jnp.where(keep, rows[i], rows[j]), jnp.where(keep, rows[j], rows[i])
    for t in range(k):
        m = jnp.max(vals[0], axis=0, keepdims=True)
        first = jnp.min(jnp.where(vals[0] == m, rows[0], n), axis=0, keepdims=True)
        val_ref[t:t + 1, :] = m
        idx_ref[t:t + 1, :] = first
        if t + 1 < k:
            popped = rows[0] == first
            for g in range(k - 1 - t):
                vals[g] = jnp.where(popped, vals[g + 1], vals[g])
                rows[g] = jnp.where(popped, rows[g + 1], rows[g])


def _peer_score_kernel(x_ref, mod_ref, g_ref, wq_ref, k1_ref, k2_ref, h_ref, idx_ref, gate_ref,
                       v1_ref, i1_ref, v2_ref, i2_ref, cv_ref, ce_ref, it_ref, gt_ref, *, tm):
    m = mod_ref[0]
    h = _norm_mod(x_ref[...], g_ref[...], m[3:4], m[4:5])
    h_ref[...] = h
    q = _dot(h.astype(BF16), wq_ref[...]).astype(BF16)
    half = PEER_QDIM // 2
    neg = jnp.full((1, tm), -jnp.inf, F32)
    for hd in range(PEER_HEADS):
        q1 = q[:, hd * PEER_QDIM: hd * PEER_QDIM + half]
        q2 = q[:, hd * PEER_QDIM + half: (hd + 1) * PEER_QDIM]
        _topk_rows(_dot_nt(k1_ref[...], q1), PEER_TOPK, v1_ref, i1_ref)
        _topk_rows(_dot_nt(k2_ref[...], q2), PEER_TOPK, v2_ref, i2_ref)
        for r, (i, j) in enumerate(_STAIR):
            cv_ref[r:r + 1, :] = v1_ref[i:i + 1, :] + v2_ref[j:j + 1, :]
            ce_ref[r:r + 1, :] = i1_ref[i:i + 1, :] * N_KEYS + i2_ref[j:j + 1, :]
        for r in range(len(_STAIR), _STAIR_ROWS):
            cv_ref[r:r + 1, :] = neg
            ce_ref[r:r + 1, :] = jnp.zeros((1, tm), jnp.int32)
        cand = cv_ref[...]
        ce = ce_ref[...]
        row = lax.broadcasted_iota(jnp.int32, cand.shape, 0)
        vals = []
        for kk in range(PEER_TOPK):
            mx = jnp.max(cand, axis=0, keepdims=True)
            first = jnp.min(jnp.where(cand == mx, row, _STAIR_ROWS), axis=0, keepdims=True)
            sel = row == first
            it_ref[hd * PEER_TOPK + kk: hd * PEER_TOPK + kk + 1, :] = jnp.sum(jnp.where(sel, ce, 0), axis=0, keepdims=True)
            vals.append(mx)
            cand = jnp.where(sel, -jnp.inf, cand)
        ex = [jnp.exp(v - vals[0]) for v in vals]
        tot = ex[0]
        for e in ex[1:]:
            tot = tot + e
        for kk in range(PEER_TOPK):
            gt_ref[hd * PEER_TOPK + kk: hd * PEER_TOPK + kk + 1, :] = ex[kk] / tot
    idx_ref[...] = it_ref[...].T
    gate_ref[...] = gt_ref[...].T


def _peer_score(x2, mod3, gain, wq_bf, k1_bf, k2_bf, seq, tm=256):
    t, d = x2.shape
    per = seq // tm
    slots = PEER_HEADS * PEER_TOPK
    full = lambda a: pl.BlockSpec(a.shape, lambda i: (0, 0))
    return pl.pallas_call(
        functools.partial(_peer_score_kernel, tm=tm),
        grid=(t // tm,),
        in_specs=[pl.BlockSpec((tm, d), lambda i: (i, 0)),
                  pl.BlockSpec((1, 6, d), lambda i: (i // per, 0, 0)),
                  pl.BlockSpec((1, d), lambda i: (0, 0)),
                  full(wq_bf), full(k1_bf), full(k2_bf)],
        out_specs=[pl.BlockSpec((tm, d), lambda i: (i, 0)),
                   pl.BlockSpec((tm, slots), lambda i: (i, 0)),
                   pl.BlockSpec((tm, slots), lambda i: (i, 0))],
        out_shape=[jax.ShapeDtypeStruct((t, d), F32),
                   jax.ShapeDtypeStruct((t, slots), jnp.int32),
                   jax.ShapeDtypeStruct((t, slots), F32)],
        scratch_shapes=[pltpu.VMEM((PEER_TOPK, tm), F32), pltpu.VMEM((PEER_TOPK, tm), jnp.int32),
                        pltpu.VMEM((PEER_TOPK, tm), F32), pltpu.VMEM((PEER_TOPK, tm), jnp.int32),
                        pltpu.VMEM((_STAIR_ROWS, tm), F32), pltpu.VMEM((_STAIR_ROWS, tm), jnp.int32),
                        pltpu.VMEM((slots, tm), jnp.int32), pltpu.VMEM((slots, tm), F32)],
        compiler_params=_cparams(("arbitrary",)),
        name="peer_score_topk",
    )(x2, mod3, gain.reshape(1, d), wq_bf, k1_bf, k2_bf)


def _peer_gather_kernel(idx_ref, idxn_ref, tab_ref, h_ref, gate_ref, x_ref, mod_ref, after_ref, *rest,
                        tc, d, nsteps):
    o_ref, buf_ref, sem_ref = rest[-3:]
    slots = PEER_HEADS * PEER_TOPK
    step = pl.program_id(0)

    def issue_token(chunk, t):
        ids_ref, c = (idx_ref, chunk) if chunk < GATHER_RING else (idxn_ref, chunk - GATHER_RING)
        for j in range(slots):
            pltpu.make_async_copy(tab_ref.at[ids_ref[c * tc + t, j]],
                                  buf_ref.at[c, t, pl.ds(j, 1), :],
                                  sem_ref.at[c]).start(priority=j % 2)

    def wait(slot):
        pltpu.make_async_copy(buf_ref.at[slot], buf_ref.at[slot], sem_ref.at[slot]).wait()

    eye = (lax.broadcasted_iota(jnp.int32, (slots, slots), 0) ==
           lax.broadcasted_iota(jnp.int32, (slots, slots), 1))
    gt2 = mod_ref[0][5:6]

    def mix_token(slot, t):
        r = slot * tc + t
        word = buf_ref[slot, t]
        u = lax.bitcast_convert_type(word << 16, F32)
        prod = u * h_ref[r:r + 1, :]
        part = prod[:, 0:LANES]
        for c in range(1, d // LANES):
            part = part + prod[:, c * LANES:(c + 1) * LANES]
        a = jnp.sum(part, axis=1, keepdims=True)
        grow = jnp.broadcast_to(gate_ref[r:r + 1, :], (slots, slots))
        gcol = jnp.sum(jnp.where(eye, grow, 0.0), axis=1, keepdims=True)
        w = gcol * (0.5 * a * (1.0 + lax.erf(a * (2.0 ** -0.5))))
        v = lax.bitcast_convert_type(buf_ref[slot, t] & jnp.uint32(0xFFFF0000), F32)
        y = jnp.sum(v * w, axis=0, keepdims=True)
        o_ref[r:r + 1, :] = x_ref[r:r + 1, :] + gt2 * y

    @pl.when(step == 0)
    def _():
        for c in range(GATHER_AHEAD):
            for t in range(tc):
                issue_token(c, t)

    for p in range(GATHER_RING):
        wait(p)
        for t in range(tc):
            issue_token(p + GATHER_AHEAD, t)
            mix_token(p, t)

    @pl.when(step == nsteps - 1)
    def _():
        for c in range(GATHER_AHEAD):
            wait(c)


def _peer_gather(x2, mod3, h2, idx, gate, table, seq, tok0, ntok, after, dst=None, tc=8):
    t, d = x2.shape
    slots = idx.shape[1]
    tb = GATHER_RING * tc
    nsteps = ntok // tb
    off = tok0 // tb
    per = seq // tb
    row = lambda w: pl.BlockSpec((tb, w), lambda i: (i + off, 0))
    operands = [idx, idx, table.reshape(table.shape[0], 1, d), h2, gate, x2, mod3, after]
    in_specs = [pl.BlockSpec((tb, slots), lambda i: (i + off, 0), memory_space=pltpu.SMEM),
                pl.BlockSpec((tb, slots), lambda i: (jnp.minimum(i + 1, nsteps - 1) + off, 0),
                             memory_space=pltpu.SMEM),
                pl.BlockSpec(memory_space=pl.ANY),
                row(d), row(slots), row(d),
                pl.BlockSpec((1, 6, d), lambda i: ((i + off) // per, 0, 0)),
                pl.BlockSpec(memory_space=pl.ANY)]
    aliases = {}
    if dst is not None:
        aliases = {len(operands): 0}
        operands.append(dst)
        in_specs.append(pl.BlockSpec(memory_space=pl.ANY))
    return pl.pallas_call(
        functools.partial(_peer_gather_kernel, tc=tc, d=d, nsteps=nsteps),
        grid=(nsteps,),
        in_specs=in_specs,
        out_specs=row(d),
        out_shape=jax.ShapeDtypeStruct((t, d), F32),
        input_output_aliases=aliases,
        scratch_shapes=[pltpu.VMEM((GATHER_RING, tc, slots, d), jnp.uint32),
                        pltpu.SemaphoreType.DMA((GATHER_RING,))],
        compiler_params=_cparams(("arbitrary",)),
        name="peer_gather_mix",
    )(*operands)


SC_LANES = 16
SC_CORES = 2
SC_SUBCORES = 16
SC_GROUP = 32
SC_HALF = 64
SC_SEQS = 12
TC_FIRST_SEQS = 7


def _pack_kernel(u_ref, v_ref, uv_ref, uw_ref, vw_ref):
    half = u_ref.shape[2] // 2

    def hi_bits(x):
        return lax.bitcast_convert_type(x.astype(BF16).astype(F32), jnp.uint32)

    ub = hi_bits(u_ref[0])
    vb = hi_bits(v_ref[0])
    uv_ref[...] = vb | (ub >> 16)
    uw_ref[...] = ub[:, half:] | (ub[:, :half] >> 16)
    vw_ref[...] = vb[:, half:] | (vb[:, :half] >> 16)


def _pack_tables(u_tabs, v_tabs, layer, tm=1024):
    _, e, d = u_tabs.shape
    blk = lambda w: pl.BlockSpec((tm, w), lambda i: (i, 0))
    src = pl.BlockSpec((1, tm, d), lambda i: (layer, i, 0))
    return pl.pallas_call(
        _pack_kernel,
        grid=(e // tm,),
        in_specs=[src, src],
        out_specs=[blk(d), blk(d // 2), blk(d // 2)],
        out_shape=[jax.ShapeDtypeStruct((e, d), jnp.uint32),
                   jax.ShapeDtypeStruct((e, d // 2), jnp.uint32),
                   jax.ShapeDtypeStruct((e, d // 2), jnp.uint32)],
        compiler_params=_cparams(("arbitrary",)),
        name="peer_pack_tables",
    )(u_tabs, v_tabs)


def _sc_worker_base(n_per):
    return (lax.axis_index("s") * SC_CORES + lax.axis_index("c")) * n_per


def _sc_halves(tab_hbm, idx_v, rows_v, sems, tt, half):
    ids = idx_v.at[tt, pl.ds(half * SC_HALF, SC_HALF)]
    return pltpu.make_async_copy(tab_hbm.at[ids], rows_v.at[half], sems.at[half])


def _sc_token_loop(tab_hbm, idx_v, rows_v, sems, compute_half):
    _sc_halves(tab_hbm, idx_v, rows_v, sems, 0, 0).start()

    def tok(tt, carry):
        _sc_halves(tab_hbm, idx_v, rows_v, sems, tt, 1).start()
        _sc_halves(tab_hbm, idx_v, rows_v, sems, tt, 0).wait()
        compute_half(tt, 0)

        @pl.when(tt + 1 < SC_GROUP)
        def _():
            _sc_halves(tab_hbm, idx_v, rows_v, sems, tt + 1, 0).start()

        _sc_halves(tab_hbm, idx_v, rows_v, sems, tt, 1).wait()
        compute_half(tt, 1)
        return carry

    lax.fori_loop(0, SC_GROUP, tok, 0)


def _unpack_pair(word):
    lo = lax.bitcast_convert_type(word << 16, F32)
    hi = lax.bitcast_convert_type(word & jnp.uint32(0xFFFF0000), F32)
    return lo, hi


def _sc_dot_body(tab_hbm, idx_hbm, h_hbm, a_hbm, idx_v, h_v, rows_v, a_v, tr_v, sems, *, tok0, n_per, d):
    base = _sc_worker_base(n_per)
    lane = lax.iota(jnp.int32, SC_LANES)
    nk = d // (2 * SC_LANES)

    def compute_half(tt, half):
        def rows16(gi, carry):
            accs = [jnp.zeros((SC_LANES,), F32) for _ in range(SC_LANES)]
            for k in range(nk):
                hlo = h_v[tt, pl.ds(SC_LANES * k, SC_LANES)]
                hhi = h_v[tt, pl.ds(d // 2 + SC_LANES * k, SC_LANES)]
                for q in range(SC_LANES):
                    lo, hi = _unpack_pair(rows_v[half, gi * SC_LANES + q, pl.ds(SC_LANES * k, SC_LANES)])
                    accs[q] = accs[q] + lo * hlo + hi * hhi
            for q in range(SC_LANES):
                tr_v[q, :] = accs[q]
            tot = plsc.load_gather(tr_v, [lane, jnp.zeros((SC_LANES,), jnp.int32)])
            for c in range(1, SC_LANES):
                tot = tot + plsc.load_gather(tr_v, [lane, jnp.full((SC_LANES,), c, jnp.int32)])
            a_v[tt, pl.ds(half * SC_HALF + gi * SC_LANES, SC_LANES)] = tot
            return carry
        lax.fori_loop(0, SC_HALF // SC_LANES, rows16, 0)

    def group(g, carry):
        loc = base + g * SC_GROUP
        pltpu.sync_copy(idx_hbm.at[pl.ds(tok0 + loc, SC_GROUP)], idx_v)
        pltpu.sync_copy(h_hbm.at[pl.ds(tok0 + loc, SC_GROUP)], h_v)
        _sc_token_loop(tab_hbm, idx_v, rows_v, sems, compute_half)
        pltpu.sync_copy(a_v, a_hbm.at[pl.ds(loc, SC_GROUP)])
        return carry

    lax.fori_loop(0, n_per // SC_GROUP, group, 0)


def _sc_mix_body(tab_hbm, idx_hbm, w_hbm, y_hbm, idx_v, w_v, rows_v, y_v, sems, *, tok0, n_per, d):
    base = _sc_worker_base(n_per)
    nk = d // (2 * SC_LANES)
    nq = 2
    kq = nk // nq

    def compute_half(tt, half):
        ttv = jnp.full((SC_LANES,), tt, jnp.int32)
        for piece in range(nq):
            cols = [(m % 2) * (d // 2) + SC_LANES * (piece * kq + m // 2) for m in range(2 * kq)]
            if half == 0:
                accs = tuple(jnp.zeros((SC_LANES,), F32) for _ in range(2 * kq))
            else:
                accs = tuple(y_v[tt, pl.ds(c, SC_LANES)] for c in cols)

            def row(r, accs):
                wsplat = plsc.load_gather(w_v, [ttv, jnp.full((SC_LANES,), half * SC_HALF + r, jnp.int32)])
                out = []
                for k in range(kq):
                    lo, hi = _unpack_pair(rows_v[half, r, pl.ds(SC_LANES * (piece * kq + k), SC_LANES)])
                    out.append(accs[2 * k] + wsplat * lo)
                    out.append(accs[2 * k + 1] + wsplat * hi)
                return tuple(out)

            accs = lax.fori_loop(0, SC_HALF, row, accs)
            for m in range(2 * kq):
                y_v[tt, pl.ds(cols[m], SC_LANES)] = accs[m]

    def group(g, carry):
        loc = base + g * SC_GROUP
        pltpu.sync_copy(idx_hbm.at[pl.ds(tok0 + loc, SC_GROUP)], idx_v)
        pltpu.sync_copy(w_hbm.at[pl.ds(loc, SC_GROUP)], w_v)
        _sc_token_loop(tab_hbm, idx_v, rows_v, sems, compute_half)
        pltpu.sync_copy(y_v, y_hbm.at[pl.ds(loc, SC_GROUP)])
        return carry

    lax.fori_loop(0, n_per // SC_GROUP, group, 0)


def _sc_mesh():
    return plsc.VectorSubcoreMesh(core_axis_name="c", subcore_axis_name="s")


def _sc_dot(u_words, idx, h2, tok0, ntok):
    d = h2.shape[1]
    slots = idx.shape[1]
    n_per = ntok // (SC_CORES * SC_SUBCORES)
    return pl.kernel(
        functools.partial(_sc_dot_body, tok0=tok0, n_per=n_per, d=d),
        out_type=jax.ShapeDtypeStruct((ntok, slots), F32),
        mesh=_sc_mesh(),
        scratch_types=[pltpu.VMEM((SC_GROUP, slots), jnp.int32),
                       pltpu.VMEM((SC_GROUP, d), F32),
                       pltpu.VMEM((2, SC_HALF, d // 2), jnp.uint32),
                       pltpu.VMEM((SC_GROUP, slots), F32),
                       pltpu.VMEM((SC_LANES, SC_LANES), F32),
                       pltpu.SemaphoreType.DMA((2,))],
        compiler_params=pltpu.CompilerParams(needs_layout_passes=False),
        name="peer_sc_dot",
    )(u_words, idx, h2)


def _sc_mix(v_words, idx, w, tok0, ntok, d):
    slots = idx.shape[1]
    n_per = ntok // (SC_CORES * SC_SUBCORES)
    return pl.kernel(
        functools.partial(_sc_mix_body, tok0=tok0, n_per=n_per, d=d),
        out_type=jax.ShapeDtypeStruct((ntok, d), F32),
        mesh=_sc_mesh(),
        scratch_types=[pltpu.VMEM((SC_GROUP, slots), jnp.int32),
                       pltpu.VMEM((SC_GROUP, slots), F32),
                       pltpu.VMEM((2, SC_HALF, d // 2), jnp.uint32),
                       pltpu.VMEM((SC_GROUP, d), F32),
                       pltpu.SemaphoreType.DMA((2,))],
        compiler_params=pltpu.CompilerParams(needs_layout_passes=False),
        name="peer_sc_mix",
    )(v_words, idx, w)


def _gelu_gate_kernel(a_ref, g_ref, after_ref, o_ref):
    a = a_ref[...]
    o_ref[...] = g_ref[...] * (0.5 * a * (1.0 + lax.erf(a * (2.0 ** -0.5))))


def _gelu_gate(a, gate, tok0, after, tm=2048):
    n, slots = a.shape
    off = tok0 // tm
    return pl.pallas_call(
        _gelu_gate_kernel,
        grid=(n // tm,),
        in_specs=[pl.BlockSpec((tm, slots), lambda i: (i, 0)),
                  pl.BlockSpec((tm, slots), lambda i: (i + off, 0)),
                  pl.BlockSpec(memory_space=pl.ANY)],
        out_specs=pl.BlockSpec((tm, slots), lambda i: (i, 0)),
        out_shape=jax.ShapeDtypeStruct((n, slots), F32),
        compiler_params=_cparams(("arbitrary",)),
        name="peer_gelu_gate",
    )(a, gate, after)


def _residual_kernel(x_ref, mod_ref, y_ref, dst_ref, o_ref):
    o_ref[...] = x_ref[...] + mod_ref[0][5:6] * y_ref[...]


def _residual(x2, mod3, y, tok0, seq, dst, tm=512):
    n, d = y.shape
    off = tok0 // tm
    per = seq // tm
    return pl.pallas_call(
        _residual_kernel,
        grid=(n // tm,),
        in_specs=[pl.BlockSpec((tm, d), lambda i: (i + off, 0)),
                  pl.BlockSpec((1, 6, d), lambda i: ((i + off) // per, 0, 0)),
                  pl.BlockSpec((tm, d), lambda i: (i, 0)),
                  pl.BlockSpec(memory_space=pl.ANY)],
        out_specs=pl.BlockSpec((tm, d), lambda i: (i + off, 0)),
        out_shape=jax.ShapeDtypeStruct(x2.shape, F32),
        input_output_aliases={3: 0},
        compiler_params=_cparams(("arbitrary",)),
        name="peer_residual",
    )(x2, mod3, y, dst)


def kernel(x_prompt, x_sample, c_prompt, c_sample, w_mod, b_mod, g_norm1, g_norm2, w_in, conv_w, conv_b, f_w1, f_b1, f_freq, f_w2, f_b2, f_w3, f_bias, q_gain, k_gain, sink, w_pa, w_pb, w_out, peer_wq, peer_k1, peer_k2, peer_u, peer_v):
    depth = w_mod.shape[0]
    bp, seq, d = x_prompt.shape
    bs = x_sample.shape[0]
    assert x_sample.shape[1] == seq
    bsz = bp + bs
    x = jnp.concatenate([x_prompt, x_sample], axis=0).reshape(bsz * seq, d)
    c = jnp.concatenate([c_prompt, c_sample], axis=0)

    hyw = w_pa.shape[1]
    aw = w_pb.shape[1]
    kw = N_KV_HEADS * HEAD_DIM
    widths = (HY_ORDER + 1) * hyw, aw, kw, kw, d, d
    cblk = hyw // 256

    mod = _modulation(c, w_mod.astype(BF16), b_mod)
    fc, fs = _dft_tables(seq)

    for l in range(depth):
        mod3 = mod[l].reshape(bsz, 6, d)
        hy, q, k, v, ga, gb = _inproj(x, mod3, g_norm1[l], w_in[l].astype(BF16), seq, widths)
        hcat = _filters_time(seq, f_w1[l], f_b1[l], f_freq[l], f_w2[l], f_b2[l], f_w3[l])
        spec_a, spec_b, nyq = _filter_spectra(fc, fs, hcat)
        cw, cb = conv_w[l], conv_b[l][None, :]
        zz = _long_conv(fc, fs, hy, 0, hy, cblk, spec_a, spec_b, nyq, 0, f_bias[l][0:1], cw, cb,
                        0, cblk, bsz, seq, conv_u=True)
        ya = _long_conv(fc, fs, zz, 0, hy, 2 * cblk, spec_a, spec_b, nyq, cblk, f_bias[l][1:2], cw, cb,
                        0, 2 * cblk, bsz, seq, conv_u=False)
        yb = _attention(q, k, v, q_gain[l], k_gain[l], sink[l], bsz, seq)
        x = _merge(x, mod3, ya, yb, ga, gb, w_pa[l].astype(BF16), w_pb[l].astype(BF16), w_out[l].astype(BF16), seq)
        h2, idx, gate = _peer_score(x, mod3, g_norm2[l], peer_wq[l].astype(BF16),
                                    peer_k1[l].astype(BF16), peer_k2[l].astype(BF16), seq)
        table, u_words, v_words = _pack_tables(peer_u, peer_v, l)
        t_tc = (bsz - SC_SEQS) * seq
        t_sc = SC_SEQS * seq
        t_g1 = TC_FIRST_SEQS * seq
        a_sc = _sc_dot(u_words, idx, h2, t_tc, t_sc)
        x_g1 = _peer_gather(x, mod3, h2, idx, gate, table, seq, 0, t_g1, gate)
        w_sc = _gelu_gate(a_sc, gate, t_tc, x_g1)
        y_sc = _sc_mix(v_words, idx, w_sc, t_tc, t_sc, d)
        x_g2 = _peer_gather(x, mod3, h2, idx, gate, table, seq, t_g1, t_tc - t_g1, w_sc, dst=x_g1)
        x = _residual(x, mod3, y_sc, t_tc, seq, x_g2)

    x = x.reshape(bsz, seq, d)
    return x[:bp], x[bp:]
```

```python
import functools
import math

import jax
import jax.numpy as jnp
import numpy as np
from jax import lax
from jax.experimental import pallas as pl
from jax.experimental.pallas import tpu as pltpu
from jax.experimental.pallas import tpu_sc as plsc

F32 = jnp.float32
BF16 = jnp.bfloat16

EPS = 1e-6
HEAD_DIM = 64
N_Q_HEADS = 8
N_KV_HEADS = 2
WINDOW = 128
BLOCK = 128
ROPE_THETA = 10000.0
HY_ORDER = 2
N_DIR = 2
FILT_BANDS = 16
DECAY_TARGET = 1e-2
FAST_DECAY_PCT = 0.3
SLOW_DECAY_PCT = 1.5
PEER_HEADS = 8
N_KEYS = 128
PEER_TOPK = 16
PEER_QDIM = 256
LANES = 128
SUBLANES = 8
VMEM_LIMIT = 56 * 1024 * 1024
GATHER_RING = 4
GATHER_AHEAD = 2

_STAIR = [(i, j) for i in range(PEER_TOPK) for j in range(PEER_TOPK) if (i + 1) * (j + 1) <= PEER_TOPK]
_STAIR_ROWS = -(-len(_STAIR) // SUBLANES) * SUBLANES


def _cparams(sem, vmem=VMEM_LIMIT):
    return pltpu.CompilerParams(dimension_semantics=sem, vmem_limit_bytes=vmem)


def _dot(a, b):
    return jnp.dot(a, b, preferred_element_type=F32)


def _dot_nt(a, b):
    return lax.dot_general(a, b, (((1,), (1,)), ((), ())), preferred_element_type=F32)


def _dot_hi(a, b):
    return jnp.dot(a, b, preferred_element_type=F32, precision=lax.Precision.HIGHEST)


def _sigmoid(x):
    return 1.0 / (1.0 + jnp.exp(-x))


def _mod_kernel(c_ref, w_ref, b_ref, o_ref):
    c = c_ref[...]
    s = c * _sigmoid(c)
    o_ref[0] = _dot(s.astype(BF16), w_ref[0]) + b_ref[0]


def _modulation(c, w_mod_bf, b_mod):
    depth, d, n6 = w_mod_bf.shape
    bsz = c.shape[0]
    tn = 1536
    return pl.pallas_call(
        _mod_kernel,
        grid=(depth, n6 // tn),
        in_specs=[pl.BlockSpec((bsz, d), lambda l, j: (0, 0)),
                  pl.BlockSpec((1, d, tn), lambda l, j: (l, 0, j)),
                  pl.BlockSpec((1, 1, tn), lambda l, j: (l, 0, j))],
        out_specs=pl.BlockSpec((1, bsz, tn), lambda l, j: (l, 0, j)),
        out_shape=jax.ShapeDtypeStruct((depth, bsz, n6), F32),
        compiler_params=_cparams(("arbitrary", "arbitrary")),
        name="adaln_mod",
    )(c, w_mod_bf, b_mod.reshape(depth, 1, n6))


def _norm_mod(x, gain, shift, scale):
    y = x * lax.rsqrt(jnp.mean(x * x, axis=-1, keepdims=True) + EPS)
    return (y * gain) * (1.0 + scale) + shift


def _inproj_kernel(x_ref, mod_ref, g_ref, w_ref, hy_ref, q_ref, k_ref, v_ref, ga_ref, gb_ref, *, splits):
    m = mod_ref[0]
    h = _norm_mod(x_ref[...], g_ref[...], m[0:1], m[1:2])
    z = _dot(h.astype(BF16), w_ref[...])
    outs = (hy_ref, q_ref, k_ref, v_ref, ga_ref, gb_ref)
    lo = 0
    for ref, hi in zip(outs, splits):
        ref[...] = z[:, lo:hi]
        lo = hi


def _inproj(x2, mod3, gain, w_in_bf, seq, widths, tm=256):
    t, d = x2.shape
    ncols = w_in_bf.shape[1]
    per = seq // tm
    splits = tuple(int(s) for s in np.cumsum(widths))
    return pl.pallas_call(
        functools.partial(_inproj_kernel, splits=splits),
        grid=(t // tm,),
        in_specs=[pl.BlockSpec((tm, d), lambda i: (i, 0)),
                  pl.BlockSpec((1, 6, d), lambda i: (i // per, 0, 0)),
                  pl.BlockSpec((1, d), lambda i: (0, 0)),
                  pl.BlockSpec((d, ncols), lambda i: (0, 0))],
        out_specs=[pl.BlockSpec((tm, w), lambda i: (i, 0)) for w in widths],
        out_shape=[jax.ShapeDtypeStruct((t, w), F32) for w in widths],
        compiler_params=_cparams(("arbitrary",)),
        name="inproj",
    )(x2, mod3, gain.reshape(1, d), w_in_bf)


def _filter_kernel(z_ref, w1_ref, b1_ref, fr_ref, w2_ref, b2_ref, w3_ref, ad_ref, o_ref):
    z = z_ref[0]
    fr = fr_ref[...]
    a = jnp.sin(fr * (_dot_hi(z, w1_ref[...]) + b1_ref[...]))
    a = jnp.sin(fr * (_dot_hi(a, w2_ref[...]) + b2_ref[...]))
    h = _dot_hi(a, w3_ref[0])
    h = h * jnp.exp(-z[:, 0:1] * ad_ref[...])
    row = lax.broadcasted_iota(jnp.int32, h.shape, 0)
    dead = (pl.program_id(0) == 1) & (pl.program_id(1) == 0) & (row == 0)
    o_ref[0] = jnp.where(dead, 0.0, h)


def _filters_time(seq, f_w1, f_b1, f_freq, f_w2, f_b2, f_w3, tm=512):
    hidden = f_w1.shape[1]
    cw = f_w3.shape[1] // (HY_ORDER * N_DIR)
    t = jnp.linspace(0.0, 1.0, seq, dtype=F32)[:, None]
    w = 2.0 * math.pi * jnp.arange(seq, dtype=F32)[:, None] / seq
    bands = jnp.linspace(1e-4, FILT_BANDS - 1, FILT_BANDS, dtype=F32)[None, :]
    z = jnp.concatenate([t, jnp.cos(bands * w), -jnp.sin(bands * w)], axis=-1)
    emb = z.shape[1]
    z = jnp.pad(z, ((0, 0), (0, LANES - emb)))
    zcat = jnp.stack([z, jnp.concatenate([z[:1], z[:0:-1]], axis=0)], axis=0)
    w1p = jnp.pad(f_w1, ((0, LANES - emb), (0, 0)))
    w3d = f_w3.reshape(hidden, HY_ORDER, N_DIR, cw).transpose(2, 0, 1, 3).reshape(N_DIR, hidden, HY_ORDER * cw)
    max_decay = math.log(DECAY_TARGET) / FAST_DECAY_PCT
    min_decay = math.log(DECAY_TARGET) / SLOW_DECAY_PCT
    ad = jnp.abs(jnp.linspace(min_decay, max_decay, cw, dtype=F32))
    ad = jnp.tile(ad, HY_ORDER)[None, :]
    oc = HY_ORDER * cw
    return pl.pallas_call(
        _filter_kernel,
        grid=(N_DIR, seq // tm),
        in_specs=[pl.BlockSpec((1, tm, LANES), lambda g, r: (g, r, 0)),
                  pl.BlockSpec((LANES, hidden), lambda g, r: (0, 0)),
                  pl.BlockSpec((1, hidden), lambda g, r: (0, 0)),
                  pl.BlockSpec((1, hidden), lambda g, r: (0, 0)),
                  pl.BlockSpec((hidden, hidden), lambda g, r: (0, 0)),
                  pl.BlockSpec((1, hidden), lambda g, r: (0, 0)),
                  pl.BlockSpec((1, hidden, oc), lambda g, r: (g, 0, 0)),
                  pl.BlockSpec((1, oc), lambda g, r: (0, 0))],
        out_specs=pl.BlockSpec((1, tm, oc), lambda g, r: (g, r, 0)),
        out_shape=jax.ShapeDtypeStruct((N_DIR, seq, oc), F32),
        compiler_params=_cparams(("arbitrary", "arbitrary")),
        name="hyena_filter_mlp",
    )(zcat, w1p, f_b1[None, :], f_freq[None, :], f_w2, f_b2[None, :], w3d, ad)


def _dft_tables(seq):
    n2 = 2 * seq
    f = jnp.arange(seq, dtype=jnp.int32)
    ft = (f[:, None] * f[None, :]) % n2
    ang = ft.astype(F32) * (2.0 * math.pi / n2)
    return jnp.cos(ang).astype(BF16), jnp.sin(ang).astype(BF16)


def _spec_kernel(fc_ref, fs_ref, h_ref, a_ref, b_ref, nyq_ref, *, seq, tf):
    hlo = h_ref[0]
    hhi = h_ref[1]
    hlo_b = hlo.astype(BF16)
    hhi_b = hhi.astype(BF16)
    f = pl.program_id(1) * tf + lax.broadcasted_iota(jnp.int32, (tf, 1), 0)
    sgn = jnp.where(f % 2 == 0, 1.0, -1.0)
    fc = fc_ref[...]
    fs = fs_ref[...]
    hr = _dot(fc, hlo_b) + sgn * _dot(fc, hhi_b)
    hs = _dot(fs, hlo_b) + sgn * _dot(fs, hhi_b)
    w = jnp.where(f == 0, 1.0, 2.0) * (1.0 / (2 * seq))
    a_ref[...] = w * hr
    b_ref[...] = -(w * hs)
    t = lax.broadcasted_iota(jnp.int32, (seq, 1), 0)
    alt = jnp.where(t % 2 == 0, 1.0, -1.0)
    nyq = jnp.sum(alt * (hlo + hhi), axis=0, keepdims=True) * (1.0 / (2 * seq))
    nyq_ref[...] = jnp.broadcast_to(nyq, nyq_ref.shape)


def _filter_spectra(fc, fs, hcat, tf=512, tcol=512):
    seq = fc.shape[0]
    oc = hcat.shape[2]
    return pl.pallas_call(
        functools.partial(_spec_kernel, seq=seq, tf=tf),
        grid=(oc // tcol, seq // tf),
        in_specs=[pl.BlockSpec((tf, seq), lambda j, i: (i, 0)),
                  pl.BlockSpec((tf, seq), lambda j, i: (i, 0)),
                  pl.BlockSpec((2, seq, tcol), lambda j, i: (0, 0, j))],
        out_specs=[pl.BlockSpec((tf, tcol), lambda j, i: (i, j)),
                   pl.BlockSpec((tf, tcol), lambda j, i: (i, j)),
                   pl.BlockSpec((SUBLANES, tcol), lambda j, i: (0, j))],
        out_shape=[jax.ShapeDtypeStruct((seq, oc), F32),
                   jax.ShapeDtypeStruct((seq, oc), F32),
                   jax.ShapeDtypeStruct((SUBLANES, oc), F32)],
        compiler_params=_cparams(("arbitrary", "arbitrary")),
        name="hyena_filter_spectra",
    )(fc, fs, hcat)


def _shortconv(x, w_ref, b_ref, seq):
    row = lax.broadcasted_iota(jnp.int32, (seq, 1), 0)
    xm = jnp.where(row == 0, 0.0, pltpu.roll(x, 1, 0))
    xp = jnp.where(row == seq - 1, 0.0, pltpu.roll(x, seq - 1, 0))
    return xm * w_ref[0:1, :] + x * w_ref[1:2, :] + xp * w_ref[2:3, :] + b_ref[...]


def _conv_kernel(fc_ref, fs_ref, u_ref, g_ref, a_ref, b_ref, nyq_ref, bias_ref,
                 cwu_ref, cbu_ref, cwg_ref, cbg_ref, o_ref, acc_ref, *, seq, fb, conv_u):
    u = u_ref[...]
    if conv_u:
        u = _shortconv(u, cwu_ref, cbu_ref, seq)
    gate = _shortconv(g_ref[...], cwg_ref, cbg_ref, seq)
    ub = u.astype(BF16)
    for c in range(seq // fb):
        rows = slice(c * fb, (c + 1) * fb)
        ur = _dot(fc_ref[rows, :], ub)
        us = _dot(fs_ref[rows, :], ub)
        a = a_ref[rows, :]
        b = b_ref[rows, :]
        qr = (ur * a + us * b).astype(BF16)
        qi = (us * a - ur * b).astype(BF16)
        part = _dot(fc_ref[:, rows], qr) + _dot(fs_ref[:, rows], qi)
        if c == 0:
            acc_ref[...] = part
        else:
            acc_ref[...] += part
    t = lax.broadcasted_iota(jnp.int32, (seq, 1), 0)
    alt = jnp.where(t % 2 == 0, 1.0, -1.0)
    unyq = jnp.sum(alt * u, axis=0, keepdims=True)
    y = acc_ref[...] + alt * (unyq * nyq_ref[0:1, :]) + bias_ref[...] * u
    o_ref[...] = gate * y


def _long_conv(fc, fs, u_src, u_blk0, g_src, g_blk0, spec_a, spec_b, nyq, s_blk0, bias, cw, cb,
               cu_blk0, cg_blk0, bsz, seq, conv_u, tc=256, fb=256):
    nct = 512 // tc
    t = bsz * seq
    const = lambda j, b: (0, 0)
    return pl.pallas_call(
        functools.partial(_conv_kernel, seq=seq, fb=fb, conv_u=conv_u),
        grid=(nct, bsz),
        in_specs=[pl.BlockSpec(memory_space=pltpu.VMEM),
                  pl.BlockSpec(memory_space=pltpu.VMEM),
                  pl.BlockSpec((seq, tc), lambda j, b: (b, u_blk0 + j)),
                  pl.BlockSpec((seq, tc), lambda j, b: (b, g_blk0 + j)),
                  pl.BlockSpec((seq, tc), lambda j, b: (0, s_blk0 + j)),
                  pl.BlockSpec((seq, tc), lambda j, b: (0, s_blk0 + j)),
                  pl.BlockSpec((SUBLANES, tc), lambda j, b: (0, s_blk0 + j)),
                  pl.BlockSpec((1, tc), lambda j, b: (0, j)),
                  pl.BlockSpec((3, tc), lambda j, b: (0, cu_blk0 + j)),
                  pl.BlockSpec((1, tc), lambda j, b: (0, cu_blk0 + j)),
                  pl.BlockSpec((3, tc), lambda j, b: (0, cg_blk0 + j)),
                  pl.BlockSpec((1, tc), lambda j, b: (0, cg_blk0 + j))],
        out_specs=pl.BlockSpec((seq, tc), lambda j, b: (b, j)),
        out_shape=jax.ShapeDtypeStruct((t, 512), F32),
        scratch_shapes=[pltpu.VMEM((seq, tc), F32)],
        compiler_params=_cparams(("arbitrary", "arbitrary")),
        name="hyena_long_conv_u" if conv_u else "hyena_long_conv",
    )(fc, fs, u_src, g_src, spec_a, spec_b, nyq, bias, cw, cb, cw, cb)


def _attn_kernel(q_ref, k_ref, v_ref, cos_ref, sin_ref, qg_ref, kg_ref, sink_ref, o_ref,
                 qn_ref, km_ref, vm_ref, s_ref, *, seq):
    lane = lax.broadcasted_iota(jnp.int32, (1, LANES), 1)
    r = lax.broadcasted_iota(jnp.int32, (LANES, LANES), 0) // HEAD_DIM
    c = lax.broadcasted_iota(jnp.int32, (LANES, LANES), 1) // HEAD_DIM
    bd = jnp.where(r == c, 1.0, 0.0).astype(BF16)
    first_half = (lane % HEAD_DIM) < (HEAD_DIM // 2)
    cos = cos_ref[...]
    sin = sin_ref[...]

    def norm_rope(x, gain):
        sq = x * x
        hi = sq.astype(BF16)
        lo = (sq - hi.astype(F32)).astype(BF16)
        ss = _dot(hi, bd) + _dot(lo, bd)
        y = (x * lax.rsqrt(ss * (1.0 / HEAD_DIM) + EPS)) * gain
        partner = jnp.where(first_half, pltpu.roll(y, LANES - HEAD_DIM // 2, 1), pltpu.roll(y, HEAD_DIM // 2, 1))
        return y * cos + partner * sin

    for p in range(N_Q_HEADS // 2):
        cols = slice(p * LANES, (p + 1) * LANES)
        qn_ref[:, cols] = norm_rope(q_ref[:, cols], qg_ref[...]).astype(BF16)
    kn = norm_rope(k_ref[...], kg_ref[...])
    left = lane < HEAD_DIM
    for src_ref, dst_ref in ((None, km_ref), (v_ref, vm_ref)):
        val = kn if src_ref is None else src_ref[...]
        rolled = pltpu.roll(val, HEAD_DIM, 1)
        dst_ref[0] = jnp.where(left, val, 0.0).astype(BF16)
        dst_ref[1] = jnp.where(left, 0.0, rolled).astype(BF16)
        dst_ref[2] = jnp.where(left, rolled, 0.0).astype(BF16)
        dst_ref[3] = jnp.where(left, 0.0, val).astype(BF16)

    span = 3 * BLOCK
    scale = HEAD_DIM ** -0.5
    ii = lax.broadcasted_iota(jnp.int32, (BLOCK, span), 0)
    jj = lax.broadcasted_iota(jnp.int32, (BLOCK, span), 1)

    def block(n, carry):
        q0 = pl.multiple_of(n * BLOCK, BLOCK)
        start = pl.multiple_of(jnp.clip((n - 1) * BLOCK, 0, seq - span), BLOCK)
        valid = jnp.abs((start - q0) + jj - ii) <= WINDOW
        for p in range(N_Q_HEADS // 2):
            kv = (2 * p) // (N_Q_HEADS // N_KV_HEADS)
            qp = qn_ref[pl.ds(q0, BLOCK), p * LANES:(p + 1) * LANES]
            for a in range(2):
                s_ref[2 * p + a] = _dot_nt(qp, km_ref[2 * kv + a, pl.ds(start, span), :])
        for p in range(N_Q_HEADS // 2):
            cols = slice(p * LANES, (p + 1) * LANES)
            kv = (2 * p) // (N_Q_HEADS // N_KV_HEADS)
            o = jnp.zeros((BLOCK, LANES), F32)
            for a in range(2):
                h = 2 * p + a
                s = s_ref[h] * scale
                s = jnp.where(valid, s, -jnp.inf)
                sk = sink_ref[h:h + 1, 0:1]
                m = jnp.maximum(jnp.max(s, axis=-1, keepdims=True), sk)
                e = jnp.exp(s - m)
                den = jnp.sum(e, axis=-1, keepdims=True) + jnp.exp(sk - m)
                pn = (e / den).astype(BF16)
                o = o + _dot(pn, vm_ref[2 * kv + a, pl.ds(start, span), :])
            o_ref[pl.ds(q0, BLOCK), cols] = o
        return carry

    lax.fori_loop(0, seq // BLOCK, block, 0)


def _attention(q, k, v, q_gain, k_gain, sink, bsz, seq):
    inv = ROPE_THETA ** (-jnp.arange(0, HEAD_DIM, 2, dtype=F32) / HEAD_DIM)
    ang = jnp.arange(seq, dtype=F32)[:, None] * inv[None, :]
    cos = jnp.tile(jnp.cos(ang), (1, LANES // (HEAD_DIM // 2)))
    sn = jnp.sin(ang)
    sin = jnp.tile(jnp.concatenate([-sn, sn], axis=1), (1, LANES // HEAD_DIM))
    qg = jnp.tile(q_gain, LANES // HEAD_DIM)[None, :]
    kg = jnp.tile(k_gain, LANES // HEAD_DIM)[None, :]
    sinkb = jnp.broadcast_to(sink[:, None], (N_Q_HEADS, LANES))
    aw = N_Q_HEADS * HEAD_DIM
    kw = N_KV_HEADS * HEAD_DIM
    const = lambda b: (0, 0)
    return pl.pallas_call(
        functools.partial(_attn_kernel, seq=seq),
        grid=(bsz,),
        in_specs=[pl.BlockSpec((seq, aw), lambda b: (b, 0)),
                  pl.BlockSpec((seq, kw), lambda b: (b, 0)),
                  pl.BlockSpec((seq, kw), lambda b: (b, 0)),
                  pl.BlockSpec((seq, LANES), const),
                  pl.BlockSpec((seq, LANES), const),
                  pl.BlockSpec((1, LANES), const),
                  pl.BlockSpec((1, LANES), const),
                  pl.BlockSpec((N_Q_HEADS, LANES), const)],
        out_specs=pl.BlockSpec((seq, aw), lambda b: (b, 0)),
        out_shape=jax.ShapeDtypeStruct((bsz * seq, aw), F32),
        scratch_shapes=[pltpu.VMEM((seq, aw), BF16),
                        pltpu.VMEM((4, seq, LANES), BF16),
                        pltpu.VMEM((4, seq, LANES), BF16),
                        pltpu.VMEM((N_Q_HEADS, BLOCK, 3 * BLOCK), F32)],
        compiler_params=_cparams(("arbitrary",)),
        name="window_attention",
    )(q, k, v, cos, sin, qg, kg, sinkb)


def _merge_kernel(x_ref, mod_ref, ya_ref, yb_ref, ga_ref, gb_ref, wpa_ref, wpb_ref, wo_ref, o_ref):
    pa = _dot(ya_ref[...].astype(BF16), wpa_ref[...])
    pb = _dot(yb_ref[...].astype(BF16), wpb_ref[...])
    merged = _sigmoid(ga_ref[...]) * pa + _sigmoid(gb_ref[...]) * pb
    out = _dot(merged.astype(BF16), wo_ref[...])
    o_ref[...] = x_ref[...] + mod_ref[0][2:3] * out


def _merge(x2, mod3, ya, yb, ga, gb, w_pa_bf, w_pb_bf, w_out_bf, seq, tm=512):
    t, d = x2.shape
    per = seq // tm
    row = lambda w: pl.BlockSpec((tm, w), lambda i: (i, 0))
    full = lambda a: pl.BlockSpec(a.shape, lambda i: (0, 0))
    return pl.pallas_call(
        _merge_kernel,
        grid=(t // tm,),
        in_specs=[row(d), pl.BlockSpec((1, 6, d), lambda i: (i // per, 0, 0)),
                  row(ya.shape[1]), row(yb.shape[1]), row(d), row(d),
                  full(w_pa_bf), full(w_pb_bf), full(w_out_bf)],
        out_specs=row(d),
        out_shape=jax.ShapeDtypeStruct((t, d), F32),
        compiler_params=_cparams(("arbitrary",)),
        name="merge_outproj",
    )(x2, mod3, ya, yb, ga, gb, w_pa_bf, w_pb_bf, w_out_bf)


def _oddeven_merge(lo, hi, r):
    step = r * 2
    if step < hi - lo:
        yield from _oddeven_merge(lo, hi, step)
        yield from _oddeven_merge(lo + r, hi, step)
        yield from ((i, i + r) for i in range(lo + r, hi - r, step))
    else:
        yield (lo, lo + r)


def _oddeven_merge_sort(lo, hi):
    if hi - lo >= 1:
        mid = lo + (hi - lo) // 2
        yield from _oddeven_merge_sort(lo, mid)
        yield from _oddeven_merge_sort(mid + 1, hi)
        yield from _oddeven_merge(lo, hi, 1)


def _topk_rows(s, k, val_ref, idx_ref):
    n, tm = s.shape
    ng = n // SUBLANES
    assert ng & (ng - 1) == 0 and k <= ng
    sub = lax.broadcasted_iota(jnp.int32, (SUBLANES, tm), 0)
    vals = [s[g * SUBLANES:(g + 1) * SUBLANES, :] for g in range(ng)]
    rows = [sub + g * SUBLANES for g in range(ng)]
    for i, j in _oddeven_merge_sort(0, ng - 1):
        keep = (vals[i] > vals[j]) | ((vals[i] == vals[j]) & (rows[i] < rows[j]))
        vals[i], vals[j] = jnp.where(keep, vals[i], vals[j]), jnp.where(keep, vals[j], vals[i])
        rows[i], rows[j] = jnp.where(keep, rows[i], rows[j]), jnp.where(keep, rows[j], rows[i])
    for t in range(k):
        m = jnp.max(vals[0], axis=0, keepdims=True)
        first = jnp.min(jnp.where(vals[0] == m, rows[0], n), axis=0, keepdims=True)
        val_ref[t:t + 1, :] = m
        idx_ref[t:t + 1, :] = first
        if t + 1 < k:
            popped = rows[0] == first
            for g in range(k - 1 - t):
                vals[g] = jnp.where(popped, vals[g + 1], vals[g])
                rows[g] = jnp.where(popped, rows[g + 1], rows[g])


def _peer_score_kernel(x_ref, mod_ref, g_ref, wq_ref, k1_ref, k2_ref, h_ref, idx_ref, gate_ref,
                       v1_ref, i1_ref, v2_ref, i2_ref, cv_ref, ce_ref, it_ref, gt_ref, *, tm):
    m = mod_ref[0]
    h = _norm_mod(x_ref[...], g_ref[...], m[3:4], m[4:5])
    h_ref[...] = h
    q = _dot(h.astype(BF16), wq_ref[...]).astype(BF16)
    half = PEER_QDIM // 2
    neg = jnp.full((1, tm), -jnp.inf, F32)
    for hd in range(PEER_HEADS):
        q1 = q[:, hd * PEER_QDIM: hd * PEER_QDIM + half]
        q2 = q[:, hd * PEER_QDIM + half: (hd + 1) * PEER_QDIM]
        _topk_rows(_dot_nt(k1_ref[...], q1), PEER_TOPK, v1_ref, i1_ref)
        _topk_rows(_dot_nt(k2_ref[...], q2), PEER_TOPK, v2_ref, i2_ref)
        for r, (i, j) in enumerate(_STAIR):
            cv_ref[r:r + 1, :] = v1_ref[i:i + 1, :] + v2_ref[j:j + 1, :]
            ce_ref[r:r + 1, :] = i1_ref[i:i + 1, :] * N_KEYS + i2_ref[j:j + 1, :]
        for r in range(len(_STAIR), _STAIR_ROWS):
            cv_ref[r:r + 1, :] = neg
            ce_ref[r:r + 1, :] = jnp.zeros((1, tm), jnp.int32)
        cand = cv_ref[...]
        ce = ce_ref[...]
        row = lax.broadcasted_iota(jnp.int32, cand.shape, 0)
        vals = []
        for kk in range(PEER_TOPK):
            mx = jnp.max(cand, axis=0, keepdims=True)
            first = jnp.min(jnp.where(cand == mx, row, _STAIR_ROWS), axis=0, keepdims=True)
            sel = row == first
            it_ref[hd * PEER_TOPK + kk: hd * PEER_TOPK + kk + 1, :] = jnp.sum(jnp.where(sel, ce, 0), axis=0, keepdims=True)
            vals.append(mx)
            cand = jnp.where(sel, -jnp.inf, cand)
        ex = [jnp.exp(v - vals[0]) for v in vals]
        tot = ex[0]
        for e in ex[1:]:
            tot = tot + e
        for kk in range(PEER_TOPK):
            gt_ref[hd * PEER_TOPK + kk: hd * PEER_TOPK + kk + 1, :] = ex[kk] / tot
    idx_ref[...] = it_ref[...].T
    gate_ref[...] = gt_ref[...].T


def _peer_score(x2, mod3, gain, wq_bf, k1_bf, k2_bf, seq, tm=256):
    t, d = x2.shape
    per = seq // tm
    slots = PEER_HEADS * PEER_TOPK
    full = lambda a: pl.BlockSpec(a.shape, lambda i: (0, 0))
    return pl.pallas_call(
        functools.partial(_peer_score_kernel, tm=tm),
        grid=(t // tm,),
        in_specs=[pl.BlockSpec((tm, d), lambda i: (i, 0)),
                  pl.BlockSpec((1, 6, d), lambda i: (i // per, 0, 0)),
                  pl.BlockSpec((1, d), lambda i: (0, 0)),
                  full(wq_bf), full(k1_bf), full(k2_bf)],
        out_specs=[pl.BlockSpec((tm, d), lambda i: (i, 0)),
                   pl.BlockSpec((tm, slots), lambda i: (i, 0)),
                   pl.BlockSpec((tm, slots), lambda i: (i, 0))],
        out_shape=[jax.ShapeDtypeStruct((t, d), F32),
                   jax.ShapeDtypeStruct((t, slots), jnp.int32),
                   jax.ShapeDtypeStruct((t, slots), F32)],
        scratch_shapes=[pltpu.VMEM((PEER_TOPK, tm), F32), pltpu.VMEM((PEER_TOPK, tm), jnp.int32),
                        pltpu.VMEM((PEER_TOPK, tm), F32), pltpu.VMEM((PEER_TOPK, tm), jnp.int32),
                        pltpu.VMEM((_STAIR_ROWS, tm), F32), pltpu.VMEM((_STAIR_ROWS, tm), jnp.int32),
                        pltpu.VMEM((slots, tm), jnp.int32), pltpu.VMEM((slots, tm), F32)],
        compiler_params=_cparams(("arbitrary",)),
        name="peer_score_topk",
    )(x2, mod3, gain.reshape(1, d), wq_bf, k1_bf, k2_bf)


def _peer_gather_kernel(idx_ref, idxn_ref, tab_ref, h_ref, gate_ref, x_ref, mod_ref, after_ref, *rest,
                        tc, d, nsteps):
    o_ref, buf_ref, sem_ref = rest[-3:]
    slots = PEER_HEADS * PEER_TOPK
    step = pl.program_id(0)

    def issue_token(chunk, t):
        ids_ref, c = (idx_ref, chunk) if chunk < GATHER_RING else (idxn_ref, chunk - GATHER_RING)
        for j in range(slots):
            pltpu.make_async_copy(tab_ref.at[ids_ref[c * tc + t, j]],
                                  buf_ref.at[c, t, pl.ds(j, 1), :],
                                  sem_ref.at[c]).start(priority=j % 2)

    def wait(slot):
        pltpu.make_async_copy(buf_ref.at[slot], buf_ref.at[slot], sem_ref.at[slot]).wait()

    eye = (lax.broadcasted_iota(jnp.int32, (slots, slots), 0) ==
           lax.broadcasted_iota(jnp.int32, (slots, slots), 1))
    gt2 = mod_ref[0][5:6]

    def mix_token(slot, t):
        r = slot * tc + t
        word = buf_ref[slot, t]
        u = lax.bitcast_convert_type(word << 16, F32)
        prod = u * h_ref[r:r + 1, :]
        part = prod[:, 0:LANES]
        for c in range(1, d // LANES):
            part = part + prod[:, c * LANES:(c + 1) * LANES]
        a = jnp.sum(part, axis=1, keepdims=True)
        grow = jnp.broadcast_to(gate_ref[r:r + 1, :], (slots, slots))
        gcol = jnp.sum(jnp.where(eye, grow, 0.0), axis=1, keepdims=True)
        w = gcol * (0.5 * a * (1.0 + lax.erf(a * (2.0 ** -0.5))))
        v = lax.bitcast_convert_type(buf_ref[slot, t] & jnp.uint32(0xFFFF0000), F32)
        y = jnp.sum(v * w, axis=0, keepdims=True)
        o_ref[r:r + 1, :] = x_ref[r:r + 1, :] + gt2 * y

    @pl.when(step == 0)
    def _():
        for c in range(GATHER_AHEAD):
            for t in range(tc):
                issue_token(c, t)

    for p in range(GATHER_RING):
        wait(p)
        for t in range(tc):
            issue_token(p + GATHER_AHEAD, t)
            mix_token(p, t)

    @pl.when(step == nsteps - 1)
    def _():
        for c in range(GATHER_AHEAD):
            wait(c)


def _peer_gather(x2, mod3, h2, idx, gate, table, seq, tok0, ntok, after, dst=None, tc=8):
    t, d = x2.shape
    slots = idx.shape[1]
    tb = GATHER_RING * tc
    nsteps = ntok // tb
    off = tok0 // tb
    per = seq // tb
    row = lambda w: pl.BlockSpec((tb, w), lambda i: (i + off, 0))
    operands = [idx, idx, table.reshape(table.shape[0], 1, d), h2, gate, x2, mod3, after]
    in_specs = [pl.BlockSpec((tb, slots), lambda i: (i + off, 0), memory_space=pltpu.SMEM),
                pl.BlockSpec((tb, slots), lambda i: (jnp.minimum(i + 1, nsteps - 1) + off, 0),
                             memory_space=pltpu.SMEM),
                pl.BlockSpec(memory_space=pl.ANY),
                row(d), row(slots), row(d),
                pl.BlockSpec((1, 6, d), lambda i: ((i + off) // per, 0, 0)),
                pl.BlockSpec(memory_space=pl.ANY)]
    aliases = {}
    if dst is not None:
        aliases = {len(operands): 0}
        operands.append(dst)
        in_specs.append(pl.BlockSpec(memory_space=pl.ANY))
    return pl.pallas_call(
        functools.partial(_peer_gather_kernel, tc=tc, d=d, nsteps=nsteps),
        grid=(nsteps,),
        in_specs=in_specs,
        out_specs=row(d),
        out_shape=jax.ShapeDtypeStruct((t, d), F32),
        input_output_aliases=aliases,
        scratch_shapes=[pltpu.VMEM((GATHER_RING, tc, slots, d), jnp.uint32),
                        pltpu.SemaphoreType.DMA((GATHER_RING,))],
        compiler_params=_cparams(("arbitrary",)),
        name="peer_gather_mix",
    )(*operands)


SC_LANES = 16
SC_CORES = 2
SC_SUBCORES = 16
SC_GROUP = 32
SC_HALF = 64
SC_SEQS = 12
TC_FIRST_SEQS = 7


def _pack_kernel(u_ref, v_ref, uv_ref, uw_ref, vw_ref):
    half = u_ref.shape[2] // 2

    def hi_bits(x):
        return lax.bitcast_convert_type(x.astype(BF16).astype(F32), jnp.uint32)

    ub = hi_bits(u_ref[0])
    vb = hi_bits(v_ref[0])
    uv_ref[...] = vb | (ub >> 16)
    uw_ref[...] = ub[:, half:] | (ub[:, :half] >> 16)
    vw_ref[...] = vb[:, half:] | (vb[:, :half] >> 16)


def _pack_tables(u_tabs, v_tabs, layer, tm=1024):
    _, e, d = u_tabs.shape
    blk = lambda w: pl.BlockSpec((tm, w), lambda i: (i, 0))
    src = pl.BlockSpec((1, tm, d), lambda i: (layer, i, 0))
    return pl.pallas_call(
        _pack_kernel,
        grid=(e // tm,),
        in_specs=[src, src],
        out_specs=[blk(d), blk(d // 2), blk(d // 2)],
        out_shape=[jax.ShapeDtypeStruct((e, d), jnp.uint32),
                   jax.ShapeDtypeStruct((e, d // 2), jnp.uint32),
                   jax.ShapeDtypeStruct((e, d // 2), jnp.uint32)],
        compiler_params=_cparams(("arbitrary",)),
        name="peer_pack_tables",
    )(u_tabs, v_tabs)


def _sc_worker_base(n_per):
    return (lax.axis_index("s") * SC_CORES + lax.axis_index("c")) * n_per


def _sc_halves(tab_hbm, idx_v, rows_v, sems, tt, half):
    ids = idx_v.at[tt, pl.ds(half * SC_HALF, SC_HALF)]
    return pltpu.make_async_copy(tab_hbm.at[ids], rows_v.at[half], sems.at[half])


def _sc_token_loop(tab_hbm, idx_v, rows_v, sems, compute_half):
    _sc_halves(tab_hbm, idx_v, rows_v, sems, 0, 0).start()

    def tok(tt, carry):
        _sc_halves(tab_hbm, idx_v, rows_v, sems, tt, 1).start()
        _sc_halves(tab_hbm, idx_v, rows_v, sems, tt, 0).wait()
        compute_half(tt, 0)

        @pl.when(tt + 1 < SC_GROUP)
        def _():
            _sc_halves(tab_hbm, idx_v, rows_v, sems, tt + 1, 0).start()

        _sc_halves(tab_hbm, idx_v, rows_v, sems, tt, 1).wait()
        compute_half(tt, 1)
        return carry

    lax.fori_loop(0, SC_GROUP, tok, 0)


def _unpack_pair(word):
    lo = lax.bitcast_convert_type(word << 16, F32)
    hi = lax.bitcast_convert_type(word & jnp.uint32(0xFFFF0000), F32)
    return lo, hi


def _sc_dot_body(tab_hbm, idx_hbm, h_hbm, a_hbm, idx_v, h_v, rows_v, a_v, tr_v, sems, *, tok0, n_per, d):
    base = _sc_worker_base(n_per)
    lane = lax.iota(jnp.int32, SC_LANES)
    nk = d // (2 * SC_LANES)

    def compute_half(tt, half):
        def rows16(gi, carry):
            accs = [jnp.zeros((SC_LANES,), F32) for _ in range(SC_LANES)]
            for k in range(nk):
                hlo = h_v[tt, pl.ds(SC_LANES * k, SC_LANES)]
                hhi = h_v[tt, pl.ds(d // 2 + SC_LANES * k, SC_LANES)]
                for q in range(SC_LANES):
                    lo, hi = _unpack_pair(rows_v[half, gi * SC_LANES + q, pl.ds(SC_LANES * k, SC_LANES)])
                    accs[q] = accs[q] + lo * hlo + hi * hhi
            for q in range(SC_LANES):
                tr_v[q, :] = accs[q]
            tot = plsc.load_gather(tr_v, [lane, jnp.zeros((SC_LANES,), jnp.int32)])
            for c in range(1, SC_LANES):
                tot = tot + plsc.load_gather(tr_v, [lane, jnp.full((SC_LANES,), c, jnp.int32)])
            a_v[tt, pl.ds(half * SC_HALF + gi * SC_LANES, SC_LANES)] = tot
            return carry
        lax.fori_loop(0, SC_HALF // SC_LANES, rows16, 0)

    def group(g, carry):
        loc = base + g * SC_GROUP
        pltpu.sync_copy(idx_hbm.at[pl.ds(tok0 + loc, SC_GROUP)], idx_v)
        pltpu.sync_copy(h_hbm.at[pl.ds(tok0 + loc, SC_GROUP)], h_v)
        _sc_token_loop(tab_hbm, idx_v, rows_v, sems, compute_half)
        pltpu.sync_copy(a_v, a_hbm.at[pl.ds(loc, SC_GROUP)])
        return carry

    lax.fori_loop(0, n_per // SC_GROUP, group, 0)


def _sc_mix_body(tab_hbm, idx_hbm, w_hbm, y_hbm, idx_v, w_v, rows_v, y_v, sems, *, tok0, n_per, d):
    base = _sc_worker_base(n_per)
    nk = d // (2 * SC_LANES)
    nq = 2
    kq = nk // nq

    def compute_half(tt, half):
        ttv = jnp.full((SC_LANES,), tt, jnp.int32)
        for piece in range(nq):
            cols = [(m % 2) * (d // 2) + SC_LANES * (piece * kq + m // 2) for m in range(2 * kq)]
            if half == 0:
                accs = tuple(jnp.zeros((SC_LANES,), F32) for _ in range(2 * kq))
            else:
                accs = tuple(y_v[tt, pl.ds(c, SC_LANES)] for c in cols)

            def row(r, accs):
                wsplat = plsc.load_gather(w_v, [ttv, jnp.full((SC_LANES,), half * SC_HALF + r, jnp.int32)])
                out = []
                for k in range(kq):
                    lo, hi = _unpack_pair(rows_v[half, r, pl.ds(SC_LANES * (piece * kq + k), SC_LANES)])
                    out.append(accs[2 * k] + wsplat * lo)
                    out.append(accs[2 * k + 1] + wsplat * hi)
                return tuple(out)

            accs = lax.fori_loop(0, SC_HALF, row, accs)
            for m in range(2 * kq):
                y_v[tt, pl.ds(cols[m], SC_LANES)] = accs[m]

    def group(g, carry):
        loc = base + g * SC_GROUP
        pltpu.sync_copy(idx_hbm.at[pl.ds(tok0 + loc, SC_GROUP)], idx_v)
        pltpu.sync_copy(w_hbm.at[pl.ds(loc, SC_GROUP)], w_v)
        _sc_token_loop(tab_hbm, idx_v, rows_v, sems, compute_half)
        pltpu.sync_copy(y_v, y_hbm.at[pl.ds(loc, SC_GROUP)])
        return carry

    lax.fori_loop(0, n_per // SC_GROUP, group, 0)


def _sc_mesh():
    return plsc.VectorSubcoreMesh(core_axis_name="c", subcore_axis_name="s")


def _sc_dot(u_words, idx, h2, tok0, ntok):
    d = h2.shape[1]
    slots = idx.shape[1]
    n_per = ntok // (SC_CORES * SC_SUBCORES)
    return pl.kernel(
        functools.partial(_sc_dot_body, tok0=tok0, n_per=n_per, d=d),
        out_type=jax.ShapeDtypeStruct((ntok, slots), F32),
        mesh=_sc_mesh(),
        scratch_types=[pltpu.VMEM((SC_GROUP, slots), jnp.int32),
                       pltpu.VMEM((SC_GROUP, d), F32),
                       pltpu.VMEM((2, SC_HALF, d // 2), jnp.uint32),
                       pltpu.VMEM((SC_GROUP, slots), F32),
                       pltpu.VMEM((SC_LANES, SC_LANES), F32),
                       pltpu.SemaphoreType.DMA((2,))],
        compiler_params=pltpu.CompilerParams(needs_layout_passes=False),
        name="peer_sc_dot",
    )(u_words, idx, h2)


def _sc_mix(v_words, idx, w, tok0, ntok, d):
    slots = idx.shape[1]
    n_per = ntok // (SC_CORES * SC_SUBCORES)
    return pl.kernel(
        functools.partial(_sc_mix_body, tok0=tok0, n_per=n_per, d=d),
        out_type=jax.ShapeDtypeStruct((ntok, d), F32),
        mesh=_sc_mesh(),
        scratch_types=[pltpu.VMEM((SC_GROUP, slots), jnp.int32),
                       pltpu.VMEM((SC_GROUP, slots), F32),
                       pltpu.VMEM((2, SC_HALF, d // 2), jnp.uint32),
                       pltpu.VMEM((SC_GROUP, d), F32),
                       pltpu.SemaphoreType.DMA((2,))],
        compiler_params=pltpu.CompilerParams(needs_layout_passes=False),
        name="peer_sc_mix",
    )(v_words, idx, w)


def _gelu_gate_kernel(a_ref, g_ref, after_ref, o_ref):
    a = a_ref[...]
    o_ref[...] = g_ref[...] * (0.5 * a * (1.0 + lax.erf(a * (2.0 ** -0.5))))


def _gelu_gate(a, gate, tok0, after, tm=2048):
    n, slots = a.shape
    off = tok0 // tm
    return pl.pallas_call(
        _gelu_gate_kernel,
        grid=(n // tm,),
        in_specs=[pl.BlockSpec((tm, slots), lambda i: (i, 0)),
                  pl.BlockSpec((tm, slots), lambda i: (i + off, 0)),
                  pl.BlockSpec(memory_space=pl.ANY)],
        out_specs=pl.BlockSpec((tm, slots), lambda i: (i, 0)),
        out_shape=jax.ShapeDtypeStruct((n, slots), F32),
        compiler_params=_cparams(("arbitrary",)),
        name="peer_gelu_gate",
    )(a, gate, after)


def _residual_kernel(x_ref, mod_ref, y_ref, dst_ref, o_ref):
    o_ref[...] = x_ref[...] + mod_ref[0][5:6] * y_ref[...]


def _residual(x2, mod3, y, tok0, seq, dst, tm=512):
    n, d = y.shape
    off = tok0 // tm
    per = seq // tm
    return pl.pallas_call(
        _residual_kernel,
        grid=(n // tm,),
        in_specs=[pl.BlockSpec((tm, d), lambda i: (i + off, 0)),
                  pl.BlockSpec((1, 6, d), lambda i: ((i + off) // per, 0, 0)),
                  pl.BlockSpec((tm, d), lambda i: (i, 0)),
                  pl.BlockSpec(memory_space=pl.ANY)],
        out_specs=pl.BlockSpec((tm, d), lambda i: (i + off, 0)),
        out_shape=jax.ShapeDtypeStruct(x2.shape, F32),
        input_output_aliases={3: 0},
        compiler_params=_cparams(("arbitrary",)),
        name="peer_residual",
    )(x2, mod3, y, dst)


def kernel(x_prompt, x_sample, c_prompt, c_sample, w_mod, b_mod, g_norm1, g_norm2, w_in, conv_w, conv_b, f_w1, f_b1, f_freq, f_w2, f_b2, f_w3, f_bias, q_gain, k_gain, sink, w_pa, w_pb, w_out, peer_wq, peer_k1, peer_k2, peer_u, peer_v):
    depth = w_mod.shape[0]
    bp, seq, d = x_prompt.shape
    bs = x_sample.shape[0]
    assert x_sample.shape[1] == seq
    bsz = bp + bs
    x = jnp.concatenate([x_prompt, x_sample], axis=0).reshape(bsz * seq, d)
    c = jnp.concatenate([c_prompt, c_sample], axis=0)

    hyw = w_pa.shape[1]
    aw = w_pb.shape[1]
    kw = N_KV_HEADS * HEAD_DIM
    widths = (HY_ORDER + 1) * hyw, aw, kw, kw, d, d
    cblk = hyw // 256

    mod = _modulation(c, w_mod.astype(BF16), b_mod)
    fc, fs = _dft_tables(seq)

    for l in range(depth):
        mod3 = mod[l].reshape(bsz, 6, d)
        hy, q, k, v, ga, gb = _inproj(x, mod3, g_norm1[l], w_in[l].astype(BF16), seq, widths)
        hcat = _filters_time(seq, f_w1[l], f_b1[l], f_freq[l], f_w2[l], f_b2[l], f_w3[l])
        spec_a, spec_b, nyq = _filter_spectra(fc, fs, hcat)
        cw, cb = conv_w[l], conv_b[l][None, :]
        zz = _long_conv(fc, fs, hy, 0, hy, cblk, spec_a, spec_b, nyq, 0, f_bias[l][0:1], cw, cb,
                        0, cblk, bsz, seq, conv_u=True)
        ya = _long_conv(fc, fs, zz, 0, hy, 2 * cblk, spec_a, spec_b, nyq, cblk, f_bias[l][1:2], cw, cb,
                        0, 2 * cblk, bsz, seq, conv_u=False)
        yb = _attention(q, k, v, q_gain[l], k_gain[l], sink[l], bsz, seq)
        x = _merge(x, mod3, ya, yb, ga, gb, w_pa[l].astype(BF16), w_pb[l].astype(BF16), w_out[l].astype(BF16), seq)
        h2, idx, gate = _peer_score(x, mod3, g_norm2[l], peer_wq[l].astype(BF16),
                                    peer_k1[l].astype(BF16), peer_k2[l].astype(BF16), seq)
        table, u_words, v_words = _pack_tables(peer_u, peer_v, l)
        t_tc = (bsz - SC_SEQS) * seq
        t_sc = SC_SEQS * seq
        t_g1 = TC_FIRST_SEQS * seq
        a_sc = _sc_dot(u_words, idx, h2, t_tc, t_sc)
        x_g1 = _peer_gather(x, mod3, h2, idx, gate, table, seq, 0, t_g1, gate)
        w_sc = _gelu_gate(a_sc, gate, t_tc, x_g1)
        y_sc = _sc_mix(v_words, idx, w_sc, t_tc, t_sc, d)
        x_g2 = _peer_gather(x, mod3, h2, idx, gate, table, seq, t_g1, t_tc - t_g1, w_sc, dst=x_g1)
        x = _residual(x, mod3, y_sc, t_tc, seq, x_g2)

    x = x.reshape(bsz, seq, d)
    return x[:bp], x[bp:]
```
